```python
import jax, jax.numpy as jnp
from jax import lax
import numpy as np

D_MODEL = 1024
BATCH = 4
SEQ = 4096
DEPTH = 1

GRID_W = 64
CTX_LEN = 256
RWKV_HEADS = 8
RWKV_HEAD_DIM = 64
RWKV_WIDTH = RWKV_HEADS * RWKV_HEAD_DIM
DECAY_LORA = 64
ICLR_LORA = 64
GATE_LORA = 128
SGU_WIDTH = 512
SGU_GROUPS = 8
CHUNK = 128
N_EXPERTS = 32
TOP_K = 4
D_EXPERT = D_MODEL
SWIGLU_LIMIT = 7.0
SWIGLU_ALPHA = 1.702
ROW_BLOCK = 128
N_BRANCHES = 2
RMS_EPS = 1e-6
LN_EPS = 1e-5
GN_EPS = 64e-5
RWKV_SPLIT = (RWKV_WIDTH, RWKV_WIDTH, RWKV_WIDTH, DECAY_LORA, DECAY_LORA, ICLR_LORA, ICLR_LORA, GATE_LORA)
RWKV_COLS = 3 * RWKV_WIDTH + 2 * DECAY_LORA + 2 * ICLR_LORA + GATE_LORA
SGU_COLS = 2 * SGU_WIDTH
GATE_COLS = N_BRANCHES * D_MODEL
IN_COLS = RWKV_COLS + SGU_COLS + GATE_COLS

kernel_name = 'hybrid_rwkv7_chunksgu_moe_dit_layer'


def _offsets(sizes):
    out, acc = [], 0
    for s in sizes[:-1]:
        acc += s
        out.append(acc)
    return out


def rmsnorm(x, g):
    xf = x.astype(jnp.float32)
    y = xf * lax.rsqrt(jnp.mean(xf * xf, axis=-1, keepdims=True) + RMS_EPS)
    return (y * g.astype(jnp.float32)).astype(x.dtype)


def layernorm(x, w, b, eps):
    xf = x.astype(jnp.float32)
    mu = jnp.mean(xf, axis=-1, keepdims=True)
    var = jnp.mean(jnp.square(xf - mu), axis=-1, keepdims=True)
    y = (xf - mu) * lax.rsqrt(var + eps)
    return (y * w.astype(jnp.float32) + b.astype(jnp.float32)).astype(x.dtype)


def modulate(h, shift, scale):
    return h * (1.0 + scale) + shift


def q_shift(p):
    B, T, C = p.shape
    rows = T // GRID_W
    g = p.reshape(B, rows, GRID_W, C // 4, 4)
    left = jnp.pad(g[:, :, :-1, :, 0], ((0, 0), (0, 0), (1, 0), (0, 0)))
    right = jnp.pad(g[:, :, 1:, :, 1], ((0, 0), (0, 0), (0, 1), (0, 0)))
    up = jnp.pad(g[:, :-1, :, :, 2], ((0, 0), (1, 0), (0, 0), (0, 0)))
    down = jnp.pad(g[:, 1:, :, :, 3], ((0, 0), (0, 1), (0, 0), (0, 0)))
    return jnp.stack([left, right, up, down], axis=-1).reshape(B, T, C)


def seq_shift(p):
    B, T, C = p.shape
    g = p.reshape(B, T, C // 2, 2)
    prev = jnp.pad(g[:, :-1, :, 0], ((0, 0), (1, 0), (0, 0)))
    nxt = jnp.pad(g[:, 1:, :, 1], ((0, 0), (0, 1), (0, 0)))
    return jnp.stack([prev, nxt], axis=-1).reshape(B, T, C)


def rwkv_inputs(P, shift_fn, lp):
    P = P.astype(jnp.float32)
    P = P + lp['shift_mu'] * (shift_fn(P) - P)
    r, k, v, wd_f, wd_b, ad_f, ad_b, gd = jnp.split(P, _offsets(RWKV_SPLIT), axis=-1)
    B, T, _ = P.shape
    heads = lambda t: t.reshape(B, T, RWKV_HEADS, RWKV_HEAD_DIM)
    kk = heads(k * lp['k_k'])
    kk = kk * lax.rsqrt(jnp.sum(kk * kk, axis=-1, keepdims=True) + 1e-12)
    g = jax.nn.sigmoid(gd) @ lp['gate_lora_b']
    dirs = []
    for d, (wd, ad) in enumerate(((wd_f, ad_f), (wd_b, ad_b))):
        w_log = -jax.nn.softplus(-(lp['decay_w0'][d] + jnp.tanh(wd) @ lp['decay_lora_b'][d])) - 0.5
        decay = jnp.exp(-jnp.exp(w_log))
        a = jax.nn.sigmoid(lp['iclr_a0'][d] + ad @ lp['iclr_lora_b'][d])
        k_d = k * (1.0 + (a - 1.0) * lp['k_a'])
        dirs.append({'w': heads(decay), 'k': heads(k_d), 'a': -kk, 'b': kk * heads(a)})
    return heads(r), heads(v), g, dirs


def wkv_scan(S0, r, w, k, v, a, b, reverse, emit):
    tm = lambda t: jnp.moveaxis(t, 1, 0)

    def step(S, inp):
        r_t, w_t, k_t, v_t, a_t, b_t = inp
        sa = jnp.einsum('bhvk,bhk->bhv', S, a_t)
        S = S * w_t[:, :, None, :] + sa[..., None] * b_t[:, :, None, :] + v_t[..., None] * k_t[:, :, None, :]
        return S, (jnp.einsum('bhvk,bhk->bhv', S, r_t) if emit else None)

    S, y = lax.scan(step, S0, (tm(r), tm(w), tm(k), tm(v), tm(a), tm(b)), reverse=reverse)
    return S, (jnp.moveaxis(y, 0, 1) if emit else None)


def rwkv_readout(y, inputs, lp):
    r, v, g, dirs = inputs
    B, T = y.shape[:2]
    mu = jnp.mean(y, axis=-1, keepdims=True)
    var = jnp.mean(jnp.square(y - mu), axis=-1, keepdims=True)
    yn = ((y - mu) * lax.rsqrt(var + GN_EPS)).reshape(B, T, RWKV_WIDTH) * lp['gn_w'] + lp['gn_b']
    bonus = sum(jnp.sum(r * dd['k'] * lp['r_k'], axis=-1, keepdims=True) for dd in dirs) * v
    return ((yn + bonus.reshape(B, T, RWKV_WIDTH)) * g) @ lp['w_out_rwkv']


def rwkv_branch(P_ctx, P_lat, lp, ctx_out):
    ctx_in = rwkv_inputs(P_ctx, seq_shift, lp)
    lat_in = rwkv_inputs(P_lat, q_shift, lp)
    B = P_lat.shape[0]
    S0 = jnp.zeros((B, RWKV_HEADS, RWKV_HEAD_DIM, RWKV_HEAD_DIM), jnp.float32)
    y_lat, y_ctx = 0.0, 0.0
    for d, rev in enumerate((False, True)):
        rc, vc, _, dc = ctx_in
        S_c, yc = wkv_scan(S0, rc, dc[d]['w'], dc[d]['k'], vc, dc[d]['a'], dc[d]['b'], rev, ctx_out)
        rl, vl, _, dl = lat_in
        _, yl = wkv_scan(S_c, rl, dl[d]['w'], dl[d]['k'], vl, dl[d]['a'], dl[d]['b'], rev, True)
        y_lat = y_lat + yl
        if ctx_out:
            y_ctx = y_ctx + yc
    out_lat = rwkv_readout(y_lat, lat_in, lp).astype(P_lat.dtype)
    out_ctx = rwkv_readout(y_ctx, ctx_in, lp).astype(P_ctx.dtype) if ctx_out else None
    return out_lat, out_ctx


def chunk_sgu(P_sgu, lp):
    u, z = jnp.split(jax.nn.gelu(P_sgu, approximate=False), 2, axis=-1)
    z = layernorm(z, lp['sgu_ln_w'], lp['sgu_ln_b'], LN_EPS)
    B, T, C = z.shape
    zc = z.reshape(B, T // CHUNK, CHUNK, SGU_GROUPS, C // SGU_GROUPS)
    s = jnp.einsum('gpq,bnqgc->bnpgc', lp['sgu_w_spatial'], zc) + lp['sgu_b_spatial'].T[:, :, None]
    return u * s.reshape(B, T, C)


def merge_out(P_rest, y_a, lp):
    P_sgu, P_gate = jnp.split(P_rest, [SGU_COLS], axis=-1)
    y_b = chunk_sgu(P_sgu, lp) @ lp['w_out_sgu']
    gate_a, gate_b = jnp.split(jax.nn.sigmoid(P_gate), N_BRANCHES, axis=-1)
    return (gate_a * y_a + gate_b * y_b) @ lp['w_o']


def clamped_swiglu(gate, up):
    gate = jnp.minimum(gate, SWIGLU_LIMIT)
    up = jnp.clip(up, -SWIGLU_LIMIT, SWIGLU_LIMIT)
    return gate * jax.nn.sigmoid(SWIGLU_ALPHA * gate) * (up + 1.0)


def moe_ffn(h, lp):
    B, T, D = h.shape
    xf = h.reshape(-1, D)
    n_tok = xf.shape[0]
    logits = (xf @ lp['router_w'] + lp['router_b']).astype(jnp.float32)
    top_v, top_e = lax.top_k(logits, TOP_K)
    weights = jax.nn.softmax(top_v, axis=-1)
    n_assign = n_tok * TOP_K
    flat_e = top_e.reshape(-1)
    order = jnp.argsort(flat_e)
    sorted_e = flat_e[order]
    sorted_tok = (order // TOP_K).astype(jnp.int32)
    sorted_w = weights.reshape(-1)[order]
    counts = jnp.bincount(flat_e, length=N_EXPERTS)
    padded = (counts + ROW_BLOCK - 1) // ROW_BLOCK * ROW_BLOCK
    pad_end = jnp.cumsum(padded)
    pad_start = pad_end - padded
    grp_start = jnp.cumsum(counts) - counts
    dest = pad_start[sorted_e] + (jnp.arange(n_assign, dtype=jnp.int32) - grp_start[sorted_e])
    n_rows = (n_assign + N_EXPERTS * (ROW_BLOCK - 1) + ROW_BLOCK - 1) // ROW_BLOCK * ROW_BLOCK
    n_blocks = n_rows // ROW_BLOCK
    row_tok = jnp.zeros((n_rows,), jnp.int32).at[dest].set(sorted_tok)
    block_start = jnp.arange(n_blocks, dtype=jnp.int32) * ROW_BLOCK
    block_e = jnp.minimum(jnp.searchsorted(pad_end, block_start, side='right'), N_EXPERTS - 1)
    xb = xf[row_tok].reshape(n_blocks, ROW_BLOCK, D)

    def expert_block(args):
        xblk, e = args
        gate = xblk @ lp['exp_w_gate'][e] + lp['exp_b_gate'][e]
        up = xblk @ lp['exp_w_up'][e] + lp['exp_b_up'][e]
        return clamped_swiglu(gate, up) @ lp['exp_w_down'][e] + lp['exp_b_down'][e]

    yb = lax.map(expert_block, (xb, block_e)).reshape(n_rows, D)
    contrib = yb[dest] * sorted_w[:, None].astype(yb.dtype)
    out = jax.ops.segment_sum(contrib, sorted_tok, num_segments=n_tok)
    return out.reshape(B, T, D).astype(h.dtype)


def setup_inputs(seed: int = 0) -> dict:
    key = jax.random.key(seed)
    ks = iter(jax.random.split(key, 48))
    nrm = lambda shape, scale: jax.random.normal(next(ks), shape, jnp.float32) * scale
    L, D, W, E, F = DEPTH, D_MODEL, RWKV_WIDTH, N_EXPERTS, D_EXPERT
    return {
        'x': nrm((BATCH, SEQ, D), 1.0),
        'c': nrm((BATCH, D), 1.0),
        'ctx': nrm((BATCH, CTX_LEN, D), 1.0),
        'c_ctx': nrm((D,), 1.0),
        'w_ada': nrm((L, D, 6 * D), 0.5 * D ** -0.5),
        'b_ada': nrm((L, 6 * D), 0.02),
        'norm1_g': 1.0 + nrm((L, D), 0.01),
        'w_in': nrm((L, D, IN_COLS), D ** -0.5),
        'shift_mu': jax.random.uniform(next(ks), (L, RWKV_COLS), jnp.float32),
        'decay_w0': -2.0 + nrm((L, 2, W), 1.0),
        'decay_lora_b': nrm((L, 2, DECAY_LORA, W), 0.5 * DECAY_LORA ** -0.5),
        'iclr_a0': nrm((L, 2, W), 0.5),
        'iclr_lora_b': nrm((L, 2, ICLR_LORA, W), 0.5 * ICLR_LORA ** -0.5),
        'gate_lora_b': nrm((L, GATE_LORA, W), GATE_LORA ** -0.5),
        'k_k': 1.0 + nrm((L, W), 0.1),
        'k_a': 1.0 + nrm((L, W), 0.1),
        'r_k': nrm((L, RWKV_HEADS, RWKV_HEAD_DIM), 0.1),
        'gn_w': 1.0 + nrm((L, W), 0.01),
        'gn_b': nrm((L, W), 0.01),
        'w_out_rwkv': nrm((L, W, D), W ** -0.5),
        'sgu_ln_w': 1.0 + nrm((L, SGU_WIDTH), 0.01),
        'sgu_ln_b': nrm((L, SGU_WIDTH), 0.01),
        'sgu_w_spatial': nrm((L, SGU_GROUPS, CHUNK, CHUNK), CHUNK ** -0.5),
        'sgu_b_spatial': 1.0 + nrm((L, SGU_GROUPS, CHUNK), 0.1),
        'w_out_sgu': nrm((L, SGU_WIDTH, D), SGU_WIDTH ** -0.5),
        'w_o': nrm((L, D, D), D ** -0.5),
        'norm2_g': 1.0 + nrm((L, D), 0.01),
        'router_w': nrm((L, D, E), D ** -0.5),
        'router_b': nrm((L, E), 0.01),
        'exp_w_gate': nrm((L, E, D, F), D ** -0.5),
        'exp_b_gate': nrm((L, E, F), 0.01),
        'exp_w_up': nrm((L, E, D, F), D ** -0.5),
        'exp_b_up': nrm((L, E, F), 0.01),
        'exp_w_down': nrm((L, E, F, D), F ** -0.5),
        'exp_b_down': nrm((L, E, D), 0.01),
        'final_norm_g': 1.0 + nrm((D,), 0.01),
    }


def reference(x, c, ctx, c_ctx, w_ada, b_ada, norm1_g, w_in, shift_mu, decay_w0, decay_lora_b,
              iclr_a0, iclr_lora_b, gate_lora_b, k_k, k_a, r_k, gn_w, gn_b, w_out_rwkv,
              sgu_ln_w, sgu_ln_b, sgu_w_spatial, sgu_b_spatial, w_out_sgu, w_o, norm2_g,
              router_w, router_b, exp_w_gate, exp_b_gate, exp_w_up, exp_b_up, exp_w_down,
              exp_b_down, final_norm_g):
    for l in range(DEPTH):
        last = l == DEPTH - 1
        lp = {
            'shift_mu': shift_mu[l], 'decay_w0': decay_w0[l], 'decay_lora_b': decay_lora_b[l],
            'iclr_a0': iclr_a0[l], 'iclr_lora_b': iclr_lora_b[l], 'gate_lora_b': gate_lora_b[l],
            'k_k': k_k[l], 'k_a': k_a[l], 'r_k': r_k[l], 'gn_w': gn_w[l], 'gn_b': gn_b[l],
            'w_out_rwkv': w_out_rwkv[l], 'sgu_ln_w': sgu_ln_w[l], 'sgu_ln_b': sgu_ln_b[l],
            'sgu_w_spatial': sgu_w_spatial[l], 'sgu_b_spatial': sgu_b_spatial[l],
            'w_out_sgu': w_out_sgu[l], 'w_o': w_o[l], 'router_w': router_w[l],
            'router_b': router_b[l], 'exp_w_gate': exp_w_gate[l], 'exp_b_gate': exp_b_gate[l],
            'exp_w_up': exp_w_up[l], 'exp_b_up': exp_b_up[l], 'exp_w_down': exp_w_down[l],
            'exp_b_down': exp_b_down[l],
        }
        mod = jax.nn.silu(c) @ w_ada[l] + b_ada[l]
        mod_c = jax.nn.silu(c_ctx) @ w_ada[l] + b_ada[l]
        sh1, sc1, ga1, sh2, sc2, ga2 = [m[:, None, :] for m in jnp.split(mod, 6, axis=-1)]
        csh1, csc1, cga1, csh2, csc2, cga2 = jnp.split(mod_c, 6, axis=-1)

        h = modulate(rmsnorm(x, norm1_g[l]), sh1, sc1)
        hc = modulate(rmsnorm(ctx, norm1_g[l]), csh1, csc1)
        P = h @ w_in[l]
        Pc = hc @ (w_in[l, :, :RWKV_COLS] if last else w_in[l])
        y_a, y_a_ctx = rwkv_branch(Pc[..., :RWKV_COLS], P[..., :RWKV_COLS], lp, not last)
        x_new = x + ga1 * merge_out(P[..., RWKV_COLS:], y_a, lp)
        if not last:
            ctx = ctx + cga1 * merge_out(Pc[..., RWKV_COLS:], y_a_ctx, lp)
        x = x_new

        h = modulate(rmsnorm(x, norm2_g[l]), sh2, sc2)
        x = x + ga2 * moe_ffn(h, lp)
        if not last:
            hc = modulate(rmsnorm(ctx, norm2_g[l]), csh2, csc2)
            ctx = ctx + cga2 * moe_ffn(hc, lp)
    return rmsnorm(x, final_norm_g)
```

```python
import functools
import math

import jax
import jax.numpy as jnp
from jax import lax
from jax.experimental import pallas as pl
from jax.experimental.pallas import tpu as pltpu

F32 = jnp.float32
BF16 = jnp.bfloat16
HIGHEST = lax.Precision.HIGHEST

D_MODEL = 1024
GRID_W = 64
RWKV_HEADS = 8
HEAD_DIM = 64
RWKV_WIDTH = RWKV_HEADS * HEAD_DIM
DECAY_LORA = 64
ICLR_LORA = 64
GATE_LORA = 128
RWKV_COLS = 3 * RWKV_WIDTH + 2 * DECAY_LORA + 2 * ICLR_LORA + GATE_LORA
SGU_WIDTH = 512
SGU_GROUPS = 8
SGU_CHUNK = 128
N_EXPERTS = 32
TOP_K = 4
SWIGLU_LIMIT = 7.0
SWIGLU_ALPHA = 1.702
RMS_EPS = 1e-6
LN_EPS = 1e-5
GN_EPS = 64e-5

WKV_CHUNK = 64
EXPERT_ROWS = 256
VMEM_LIMIT = 48 * 1024 * 1024


def _params(*sem):
    return pltpu.CompilerParams(dimension_semantics=sem, vmem_limit_bytes=VMEM_LIMIT)


def _mm(a, b, dims=((1,), (0,)), exact=False):
    dn = (dims, ((), ()))
    if exact:
        return lax.dot_general(a, b, dn, precision=HIGHEST, preferred_element_type=F32)
    return lax.dot_general(a.astype(BF16), b.astype(BF16), dn, preferred_element_type=F32)


def _full(shape):
    n = len(shape)
    return pl.BlockSpec(shape, lambda *_: (0,) * n)


def _norm_mod(x, g, shift, scale):
    y = x * lax.rsqrt(jnp.mean(x * x, axis=-1, keepdims=True) + RMS_EPS) * g
    return y * (1.0 + scale) + shift


def _mods_kernel(c_ref, w_ref, b_ref, o_ref):
    c = c_ref[...]
    s = c * jax.nn.sigmoid(c)
    o_ref[...] = _mm(s, w_ref[...], exact=True) + b_ref[...]


def _stage_mods(cs, w_ada, b_ada):
    rows, d = cs.shape
    n = w_ada.shape[1]
    tn = 1536
    return pl.pallas_call(
        _mods_kernel,
        grid=(n // tn,),
        in_specs=[_full((rows, d)),
                  pl.BlockSpec((d, tn), lambda j: (0, j)),
                  pl.BlockSpec((1, tn), lambda j: (0, j))],
        out_specs=pl.BlockSpec((rows, tn), lambda j: (0, j)),
        out_shape=jax.ShapeDtypeStruct((rows, n), F32),
        compiler_params=_params("arbitrary"),
        name="mods",
    )(cs, w_ada, b_ada)


def _inproj_kernel(x_ref, sh_ref, sc_ref, g_ref, w_ref, o_ref):
    h = _norm_mod(x_ref[0], g_ref[...], sh_ref[0], sc_ref[0])
    o_ref[0] = _mm(h, w_ref[...])


def _stage_inproj(x, shift, scale, g, w_bf16, tm):
    b, t, d = x.shape
    n = w_bf16.shape[1]
    return pl.pallas_call(
        _inproj_kernel,
        grid=(b, t // tm),
        in_specs=[pl.BlockSpec((1, tm, d), lambda i, j: (i, j, 0)),
                  pl.BlockSpec((1, 1, d), lambda i, j: (i, 0, 0)),
                  pl.BlockSpec((1, 1, d), lambda i, j: (i, 0, 0)),
                  _full((1, d)),
                  _full((d, n))],
        out_specs=pl.BlockSpec((1, tm, n), lambda i, j: (i, j, 0)),
        out_shape=jax.ShapeDtypeStruct((b, t, n), F32),
        compiler_params=_params("arbitrary", "arbitrary"),
        name="inproj",
    )(x, shift, scale, g, w_bf16)


def _rwkv_feats(p, mu_kk, mu_ka, r_k, w0, dlb, a0, ilb, glb, headsum):
    w = RWKV_WIDTH
    r = p[:, 0:w]
    k = p[:, w:2 * w]
    v = p[:, 2 * w:3 * w]
    o = 3 * w
    wd = (p[:, o:o + DECAY_LORA], p[:, o + DECAY_LORA:o + 2 * DECAY_LORA])
    o += 2 * DECAY_LORA
    ad = (p[:, o:o + ICLR_LORA], p[:, o + ICLR_LORA:o + 2 * ICLR_LORA])
    o += 2 * ICLR_LORA
    gd = p[:, o:o + GATE_LORA]

    kk = k * mu_kk
    kk = kk * lax.rsqrt(_mm(kk * kk, headsum, exact=True) + 1e-12)
    g = _mm(jax.nn.sigmoid(gd), glb, exact=True)
    ks, bs, lws = [], [], []
    ksum = None
    for d in range(2):
        z = w0[d:d + 1] + _mm(jnp.tanh(wd[d]), dlb[d], exact=True)
        lws.append(-math.exp(-0.5) * jax.nn.sigmoid(z))
        ic = jax.nn.sigmoid(a0[d:d + 1] + _mm(ad[d], ilb[d], exact=True))
        kd = k * (1.0 + (ic - 1.0) * mu_ka)
        ks.append(kd)
        bs.append(kk * ic)
        ksum = kd if ksum is None else ksum + kd
    bonus = _mm(r * ksum * r_k, headsum, exact=True) * v
    return r, v, -kk, g, bonus, ks, bs, lws


def _prep_kernel(pm_ref, pp_ref, pn_ref, mu_ref, kk_ref, ka_ref, rk_ref, w0_ref, dlb_ref,
                 a0_ref, ilb_ref, glb_ref, hs_ref,
                 r_ref, v_ref, a_ref, g_ref, bon_ref, k_ref, b_ref, lw_ref, *, tm, grid_shift):
    main = pm_ref[0]
    lane = lax.broadcasted_iota(jnp.int32, (1, RWKV_COLS), 1)
    if grid_shift:
        ext = jnp.concatenate([pp_ref[0], main, pn_ref[0]], axis=0)
        t = pl.program_id(1) * tm + lax.broadcasted_iota(jnp.int32, (tm, 1), 0)
        col = t & (GRID_W - 1)
        row = t >> (GRID_W.bit_length() - 1)
        n_rows = pl.num_programs(1) * tm // GRID_W
        left = jnp.where(col > 0, ext[GRID_W - 1:GRID_W - 1 + tm], 0.0)
        right = jnp.where(col < GRID_W - 1, ext[GRID_W + 1:GRID_W + 1 + tm], 0.0)
        up = jnp.where(row > 0, ext[0:tm], 0.0)
        down = jnp.where(row < n_rows - 1, ext[2 * GRID_W:2 * GRID_W + tm], 0.0)
        cm = lane & 3
        shifted = jnp.where(cm == 0, left, jnp.where(cm == 1, right, jnp.where(cm == 2, up, down)))
    else:
        zero = jnp.zeros((1, RWKV_COLS), F32)
        prev = jnp.concatenate([zero, main[:tm - 1]], axis=0)
        nxt = jnp.concatenate([main[1:], zero], axis=0)
        shifted = jnp.where((lane & 1) == 0, prev, nxt)
    p = main + mu_ref[...] * (shifted - main)
    r, v, a, g, bonus, ks, bs, lws = _rwkv_feats(
        p, kk_ref[...], ka_ref[...], rk_ref[...], w0_ref[...], dlb_ref, a0_ref[...], ilb_ref,
        glb_ref[...], hs_ref[...])
    r_ref[0] = r
    v_ref[0] = v
    a_ref[0] = a
    g_ref[0] = g
    bon_ref[0] = bonus
    for d in range(2):
        k_ref[d, 0] = ks[d]
        b_ref[d, 0] = bs[d]
        lw_ref[d, 0] = lws[d]


def _stage_prep(p, prm, tm, grid_shift):
    b, t, c = p.shape
    w = RWKV_WIDTH
    hb = GRID_W if grid_shift else 8
    per = tm // hb
    last = t // hb - 1
    tok = pl.BlockSpec((1, tm, w), lambda i, j: (i, j, 0))
    tok2 = pl.BlockSpec((2, 1, tm, w), lambda i, j: (0, i, j, 0))
    in_specs = [
        pl.BlockSpec((1, tm, c), lambda i, j: (i, j, 0)),
        pl.BlockSpec((1, hb, c), lambda i, j: (i, jnp.maximum(j * per - 1, 0), 0)),
        pl.BlockSpec((1, hb, c), lambda i, j: (i, jnp.minimum((j + 1) * per, last), 0)),
    ] + [_full(a.shape) for a in prm]
    return pl.pallas_call(
        functools.partial(_prep_kernel, tm=tm, grid_shift=grid_shift),
        grid=(b, t // tm),
        in_specs=in_specs,
        out_specs=[tok] * 5 + [tok2] * 3,
        out_shape=[jax.ShapeDtypeStruct((b, t, w), F32)] * 5
        + [jax.ShapeDtypeStruct((2, b, t, w), F32)] * 3,
        compiler_params=_params("arbitrary", "arbitrary"),
        name="prep_lat" if grid_shift else "prep_ctx",
    )(p, p, p, *prm)


def _chunk_kernel(r_ref, v_ref, a_ref, k_ref, b_ref, lw_ref, *out_refs, emit):
    c = WKV_CHUNK
    n = HEAD_DIM
    r = r_ref[0]
    v = v_ref[0]
    a = a_ref[0]
    ii = lax.broadcasted_iota(jnp.int32, (c, c), 0)
    jj = lax.broadcasted_iota(jnp.int32, (c, c), 1)
    eye = (ii == jj).astype(F32)
    for d in range(2):
        k = k_ref[d, 0]
        bb = b_ref[d, 0]
        lw = lw_ref[d, 0]
        incl = (jj <= ii) if d == 0 else (jj >= ii)
        strict = (jj < ii) if d == 0 else (jj > ii)
        lc = _mm(incl.astype(F32), lw, exact=True)
        lx = lc - lw
        ltot = lc[c - 1:c] if d == 0 else lc[0:1]
        at = a * jnp.exp(lx)
        rt = r * jnp.exp(lc)
        inv = jnp.exp(-lc)
        kt = k * inv
        bt = bb * inv
        tail = jnp.exp(ltot - lc)
        kh = k * tail
        bh = bb * tail
        etot = jnp.exp(ltot)
        ms, ls, qs, ys = [], [], [], []
        for h in range(RWKV_HEADS):
            sl = slice(h * n, (h + 1) * n)
            gram = _mm(jnp.concatenate([at[:, sl], rt[:, sl]], axis=0),
                       jnp.concatenate([kt[:, sl], bt[:, sl]], axis=0),
                       dims=((1,), (1,)), exact=True)
            a_ak = jnp.where(strict, gram[:c, :c], 0.0)
            a_ab = jnp.where(strict, gram[:c, c:], 0.0)
            a_rk = jnp.where(incl, gram[c:, :c], 0.0)
            a_rb = jnp.where(incl, gram[c:, c:], 0.0)
            x = a_ab
            tinv = eye + x
            for _ in range(5):
                x = _mm(x, x, exact=True)
                tinv = tinv + _mm(tinv, x, exact=True)
            av = _mm(jnp.concatenate([a_ak, a_rk], axis=0), v[:, sl], exact=True)
            pu = _mm(tinv, jnp.concatenate([at[:, sl], av[:c]], axis=1), exact=True)
            bp = _mm(bh[:, sl], pu, dims=((0,), (0,)), exact=True)
            ms.append(eye * etot[:, sl] + bp[:, :n])
            ls.append(bp[:, n:] + _mm(kh[:, sl], v[:, sl], dims=((0,), (0,)), exact=True))
            if emit:
                qy = jnp.concatenate([rt[:, sl], av[c:]], axis=1) + _mm(a_rb, pu, exact=True)
                qs.append(qy[:, :n])
                ys.append(qy[:, n:])
        out_refs[0][d, 0, 0] = jnp.concatenate(ms, axis=1)
        out_refs[1][d, 0, 0] = jnp.concatenate(ls, axis=1)
        if emit:
            out_refs[2][d, 0] = jnp.concatenate(qs, axis=1)
            out_refs[3][d, 0] = jnp.concatenate(ys, axis=1)


def _stage_chunk(r, v, a, k, bb, lw, emit):
    b, t, w = r.shape
    c = WKV_CHUNK
    nch = t // c
    tok = pl.BlockSpec((1, c, w), lambda i, j: (i, j, 0))
    tok2 = pl.BlockSpec((2, 1, c, w), lambda i, j: (0, i, j, 0))
    st = pl.BlockSpec((2, 1, 1, HEAD_DIM, w), lambda i, j: (0, i, j, 0, 0))
    st_shape = jax.ShapeDtypeStruct((2, b, nch, HEAD_DIM, w), F32)
    out_specs = [st, st]
    out_shape = [st_shape, st_shape]
    if emit:
        out_specs += [tok2, tok2]
        out_shape += [jax.ShapeDtypeStruct((2, b, t, w), F32)] * 2
    return pl.pallas_call(
        functools.partial(_chunk_kernel, emit=emit),
        grid=(b, nch),
        in_specs=[tok, tok, tok, tok2, tok2, tok2],
        out_specs=out_specs,
        out_shape=out_shape,
        compiler_params=_params("arbitrary", "arbitrary"),
        name="chunk_lat" if emit else "chunk_ctx",
    )(r, v, a, k, bb, lw)


def _scan_kernel(*refs, emit, has_init):
    refs = list(refs)
    m_ref = refs.pop(0)
    l_ref = refs.pop(0)
    q_ref = refs.pop(0) if emit else None
    yl_ref = refs.pop(0) if emit else None
    z0_ref = refs.pop(0) if has_init else None
    o_ref = refs.pop(0)
    z_scr = refs.pop(0)
    n = HEAD_DIM

    @pl.when(pl.program_id(2) == 0)
    def _():
        if has_init:
            z_scr[...] = z0_ref[0, 0]
        else:
            z_scr[...] = jnp.zeros_like(z_scr)

    z = z_scr[...]
    m = m_ref[0, 0, 0]
    zs, ys = [], []
    for h in range(RWKV_HEADS):
        sl = slice(h * n, (h + 1) * n)
        zh = z[:, sl]
        zs.append(_mm(m[:, sl], zh, exact=True))
        if emit:
            ys.append(_mm(q_ref[0, 0][:, sl], zh, exact=True))
    z_new = jnp.concatenate(zs, axis=1) + l_ref[0, 0, 0]
    z_scr[...] = z_new
    if emit:
        o_ref[0, 0] = jnp.concatenate(ys, axis=1) + yl_ref[0, 0]
    else:
        o_ref[0, 0] = z_new


def _stage_scan(m, l, q=None, yl=None, z0=None):
    _, b, nch, n, w = m.shape
    emit = q is not None
    c = WKV_CHUNK
    order = lambda d, s: jnp.where(d == 0, s, nch - 1 - s)
    st = pl.BlockSpec((1, 1, 1, n, w), lambda i, d, s: (d, i, order(d, s), 0, 0))
    tok = pl.BlockSpec((1, 1, c, w), lambda i, d, s: (d, i, order(d, s), 0))
    zspec = pl.BlockSpec((1, 1, n, w), lambda i, d, s: (d, i, 0, 0))
    args, in_specs = [m, l], [st, st]
    if emit:
        args += [q, yl]
        in_specs += [tok, tok]
    if z0 is not None:
        args.append(z0)
        in_specs.append(zspec)
    if emit:
        out_specs, out_shape = tok, jax.ShapeDtypeStruct((2, b, nch * c, w), F32)
    else:
        out_specs, out_shape = zspec, jax.ShapeDtypeStruct((2, b, n, w), F32)
    return pl.pallas_call(
        functools.partial(_scan_kernel, emit=emit, has_init=z0 is not None),
        grid=(b, 2, nch),
        in_specs=in_specs,
        out_specs=out_specs,
        out_shape=out_shape,
        scratch_shapes=[pltpu.VMEM((n, w), F32)],
        compiler_params=_params("arbitrary", "arbitrary", "arbitrary"),
        name="scan_lat" if emit else "scan_ctx",
    )(*args)


def _merge_kernel(x_ref, y_ref, g_ref, bon_ref, sh_ref, sc_ref, ga_ref, n1_ref, w2_ref,
                  gnw_ref, gnb_ref, hs_ref, wor_ref, lnw_ref, lnb_ref, wsp_ref, bsp_ref,
                  wos_ref, wo_ref, o_ref, *, tm):
    x = x_ref[0]
    h = _norm_mod(x, n1_ref[...], sh_ref[0], sc_ref[0])
    p2 = _mm(h, w2_ref[...])

    ps = p2[:, :2 * SGU_WIDTH]
    ge = 0.5 * ps * (1.0 + lax.erf(ps * (1.0 / math.sqrt(2.0))))
    u = ge[:, :SGU_WIDTH]
    z = ge[:, SGU_WIDTH:]
    mu = jnp.mean(z, axis=-1, keepdims=True)
    zc = z - mu
    var = jnp.mean(zc * zc, axis=-1, keepdims=True)
    z = zc * lax.rsqrt(var + LN_EPS) * lnw_ref[...] + lnb_ref[...]
    gw = SGU_WIDTH // SGU_GROUPS
    rows = []
    for c in range(tm // SGU_CHUNK):
        zc = z[c * SGU_CHUNK:(c + 1) * SGU_CHUNK]
        cols = [_mm(wsp_ref[gi], zc[:, gi * gw:(gi + 1) * gw]) for gi in range(SGU_GROUPS)]
        rows.append(jnp.concatenate(cols, axis=1) + bsp_ref[...])
    s = jnp.concatenate(rows, axis=0)
    y_b = _mm(u * s, wos_ref[...])

    y = y_ref[0, 0] + y_ref[1, 0]
    hs = hs_ref[...]
    ym = _mm(y, hs, exact=True) * (1.0 / HEAD_DIM)
    yc = y - ym
    yv = _mm(yc * yc, hs, exact=True) * (1.0 / HEAD_DIM)
    yn = yc * lax.rsqrt(yv + GN_EPS) * gnw_ref[...] + gnb_ref[...]
    y_a = _mm((yn + bon_ref[0]) * g_ref[0], wor_ref[...])

    gates = jax.nn.sigmoid(p2[:, 2 * SGU_WIDTH:])
    mix = gates[:, :D_MODEL] * y_a + gates[:, D_MODEL:] * y_b
    o_ref[0] = x + ga_ref[0] * _mm(mix, wo_ref[...])


def _stage_merge(x, y, g, bonus, sh, sc, ga, consts, tm):
    b, t, d = x.shape
    w = RWKV_WIDTH
    mod = pl.BlockSpec((1, 1, d), lambda i, j: (i, 0, 0))
    return pl.pallas_call(
        functools.partial(_merge_kernel, tm=tm),
        grid=(b, t // tm),
        in_specs=[pl.BlockSpec((1, tm, d), lambda i, j: (i, j, 0)),
                  pl.BlockSpec((2, 1, tm, w), lambda i, j: (0, i, j, 0)),
                  pl.BlockSpec((1, tm, w), lambda i, j: (i, j, 0)),
                  pl.BlockSpec((1, tm, w), lambda i, j: (i, j, 0)),
                  mod, mod, mod] + [_full(a.shape) for a in consts],
        out_specs=pl.BlockSpec((1, tm, d), lambda i, j: (i, j, 0)),
        out_shape=jax.ShapeDtypeStruct((b, t, d), F32),
        compiler_params=_params("arbitrary", "arbitrary"),
        name="merge",
    )(x, y, g, bonus, sh, sc, ga, *consts)


def _router_kernel(x_ref, sh_ref, sc_ref, n2_ref, rwt_ref, rb_ref,
                   h_ref, e_ref, w_ref, rank_ref, cnt_ref, carry, *, tm):
    @pl.when(pl.program_id(0) == 0)
    def _():
        carry[...] = jnp.zeros_like(carry)

    h = _norm_mod(x_ref[...], n2_ref[...], sh_ref[0], sc_ref[0])
    h_ref[...] = h.astype(BF16)
    logits = _mm(rwt_ref[...], h, dims=((1,), (1,)), exact=True) + rb_ref[...]
    eio = lax.broadcasted_iota(jnp.int32, (N_EXPERTS, tm), 0)
    vals, sels = [], []
    for _ in range(TOP_K):
        m = jnp.max(logits, axis=0, keepdims=True)
        idx = jnp.min(jnp.where(logits == m, eio, N_EXPERTS), axis=0, keepdims=True)
        sel = eio == idx
        logits = jnp.where(sel, -jnp.inf, logits)
        vals.append(m)
        sels.append(sel)
        e_ref[len(vals) - 1:len(vals), :] = idx
    ex = [jnp.exp(vk - vals[0]) for vk in vals]
    tot = ex[0] + ex[1] + ex[2] + ex[3]
    for kk in range(TOP_K):
        w_ref[kk:kk + 1, :] = ex[kk] / tot
    cnt = (sels[0] | sels[1] | sels[2] | sels[3]).astype(F32)
    ti = lax.broadcasted_iota(jnp.int32, (tm, tm), 0)
    tj = lax.broadcasted_iota(jnp.int32, (tm, tm), 1)
    before = _mm(cnt, (ti < tj).astype(F32))
    base = carry[...] + before
    for kk in range(TOP_K):
        rank_ref[kk:kk + 1, :] = jnp.sum(jnp.where(sels[kk], base, 0.0), axis=0,
                                         keepdims=True).astype(jnp.int32)
    new = carry[...] + jnp.sum(cnt, axis=1, keepdims=True)
    carry[...] = new
    cnt_ref[...] = jnp.broadcast_to(new, cnt_ref.shape).astype(jnp.int32)


def _stage_router(x2, sh, sc, n2g, rwt, rb, tokens_per_batch, tm):
    n, d = x2.shape
    per = tokens_per_batch // tm
    mod = pl.BlockSpec((1, 1, d), lambda i: (i // per, 0, 0))
    lane = pl.BlockSpec((TOP_K, tm), lambda i: (0, i))
    return pl.pallas_call(
        functools.partial(_router_kernel, tm=tm),
        grid=(n // tm,),
        in_specs=[pl.BlockSpec((tm, d), lambda i: (i, 0)), mod, mod,
                  _full(n2g.shape), _full(rwt.shape), _full(rb.shape)],
        out_specs=[pl.BlockSpec((tm, d), lambda i: (i, 0)), lane, lane, lane,
                   _full((N_EXPERTS, 128))],
        out_shape=[jax.ShapeDtypeStruct((n, d), BF16),
                   jax.ShapeDtypeStruct((TOP_K, n), jnp.int32),
                   jax.ShapeDtypeStruct((TOP_K, n), F32),
                   jax.ShapeDtypeStruct((TOP_K, n), jnp.int32),
                   jax.ShapeDtypeStruct((N_EXPERTS, 128), jnp.int32)],
        scratch_shapes=[pltpu.VMEM((N_EXPERTS, 1), F32)],
        compiler_params=_params("arbitrary"),
        name="router",
    )(x2, sh, sc, n2g, rwt, rb)


def _expert_kernel(be_ref, nu_ref, x_ref, wg_ref, bg_ref, wu_ref, bu_ref, wd_ref, bd_ref,
                   o_ref, wg_s, wu_s, wd_s):
    i = pl.program_id(0)
    prev = be_ref[jnp.maximum(i - 1, 0)]

    @pl.when((i == 0) | (be_ref[i] != prev))
    def _():
        wg_s[...] = wg_ref[0].astype(BF16)
        wu_s[...] = wu_ref[0].astype(BF16)
        wd_s[...] = wd_ref[0].astype(BF16)

    @pl.when(i < nu_ref[0])
    def _():
        x = x_ref[...]
        gate = jnp.dot(x, wg_s[...], preferred_element_type=F32) + bg_ref[0]
        up = jnp.dot(x, wu_s[...], preferred_element_type=F32) + bu_ref[0]
        gate = jnp.minimum(gate, SWIGLU_LIMIT)
        up = jnp.clip(up, -SWIGLU_LIMIT, SWIGLU_LIMIT)
        act = gate * jax.nn.sigmoid(SWIGLU_ALPHA * gate) * (up + 1.0)
        o_ref[...] = _mm(act, wd_s[...]) + bd_ref[0]

    @pl.when(i >= nu_ref[0])
    def _():
        o_ref[...] = jnp.zeros_like(o_ref)


def _stage_experts(block_e, n_used, xs, wg, bg, wu, bu, wd, bd):
    n_rows, d = xs.shape
    f = wg.shape[2]
    bm = EXPERT_ROWS
    wspec = lambda a, b_: pl.BlockSpec((1, a, b_), lambda i, be, nu: (be[i], 0, 0))
    return pl.pallas_call(
        _expert_kernel,
        grid_spec=pltpu.PrefetchScalarGridSpec(
            num_scalar_prefetch=2,
            grid=(n_rows // bm,),
            in_specs=[pl.BlockSpec((bm, d), lambda i, be, nu: (i, 0)),
                      wspec(d, f), wspec(1, f), wspec(d, f), wspec(1, f),
                      wspec(f, d), wspec(1, d)],
            out_specs=pl.BlockSpec((bm, d), lambda i, be, nu: (i, 0)),
            scratch_shapes=[pltpu.VMEM((d, f), BF16), pltpu.VMEM((d, f), BF16),
                            pltpu.VMEM((f, d), BF16)]),
        out_shape=jax.ShapeDtypeStruct((n_rows, d), F32),
        compiler_params=_params("arbitrary"),
        name="experts",
    )(block_e, n_used, xs, wg, bg, wu, bu, wd, bd)


def _combine_kernel(x_ref, y_ref, w_ref, ga_ref, g_ref, o_ref):
    acc = y_ref[0] * w_ref[0]
    for kk in range(1, TOP_K):
        acc = acc + y_ref[kk] * w_ref[kk]
    x = x_ref[...] + ga_ref[0] * acc
    o_ref[...] = x * lax.rsqrt(jnp.mean(x * x, axis=-1, keepdims=True) + RMS_EPS) * g_ref[...]


def _stage_combine(x2, yg, w, ga, g, tokens_per_batch, tm):
    n, d = x2.shape
    per = tokens_per_batch // tm
    return pl.pallas_call(
        _combine_kernel,
        grid=(n // tm,),
        in_specs=[pl.BlockSpec((tm, d), lambda i: (i, 0)),
                  pl.BlockSpec((TOP_K, tm, d), lambda i: (0, i, 0)),
                  pl.BlockSpec((TOP_K, tm, 1), lambda i: (0, i, 0)),
                  pl.BlockSpec((1, 1, d), lambda i: (i // per, 0, 0)),
                  _full(g.shape)],
        out_specs=pl.BlockSpec((tm, d), lambda i: (i, 0)),
        out_shape=jax.ShapeDtypeStruct((n, d), F32),
        compiler_params=_params("arbitrary"),
        name="combine",
    )(x2, yg, w, ga, g)


def _rwkv_branch(p_lat, p_ctx, prm):
    feats_c = _stage_prep(p_ctx, prm, tm=p_ctx.shape[1], grid_shift=False)
    feats_l = _stage_prep(p_lat, prm, tm=256, grid_shift=True)
    r, v, a, _, _, k, bb, lw = feats_c
    m_c, l_c = _stage_chunk(r, v, a, k, bb, lw, emit=False)
    z_ctx = _stage_scan(m_c, l_c)
    r, v, a, g, bonus, k, bb, lw = feats_l
    m_l, l_l, q_l, y_l = _stage_chunk(r, v, a, k, bb, lw, emit=True)
    y = _stage_scan(m_l, l_l, q_l, y_l, z_ctx)
    return y, g, bonus


def _route(top_e, rank, counts, n_tok):
    bm = EXPERT_ROWS
    n_rows = n_tok * TOP_K + N_EXPERTS * bm
    padded = (counts + bm - 1) // bm * bm
    pad_end = jnp.cumsum(padded)
    pad_start = pad_end - padded
    dest = pad_start[top_e] + rank
    tok = jnp.broadcast_to(jnp.arange(n_tok, dtype=jnp.int32)[None], dest.shape)
    row_tok = jnp.zeros((n_rows,), jnp.int32).at[dest.reshape(-1)].set(tok.reshape(-1))
    block_start = jnp.arange(n_rows // bm, dtype=jnp.int32) * bm
    block_e = jnp.minimum(jnp.searchsorted(pad_end, block_start, side="right"),
                          N_EXPERTS - 1).astype(jnp.int32)
    n_used = (pad_end[-1] // bm).astype(jnp.int32).reshape(1)
    return dest, row_tok, block_e, n_used


def kernel(x, c, ctx, c_ctx, w_ada, b_ada, norm1_g, w_in, shift_mu, decay_w0, decay_lora_b,
           iclr_a0, iclr_lora_b, gate_lora_b, k_k, k_a, r_k, gn_w, gn_b, w_out_rwkv,
           sgu_ln_w, sgu_ln_b, sgu_w_spatial, sgu_b_spatial, w_out_sgu, w_o, norm2_g,
           router_w, router_b, exp_w_gate, exp_b_gate, exp_w_up, exp_b_up, exp_w_down,
           exp_b_down, final_norm_g):
    assert w_ada.shape[0] == 1, "single-layer problem"
    b, t, d = x.shape
    n_tok = b * t
    w = RWKV_WIDTH
    row = lambda a: a.reshape(1, -1)

    cs = jnp.zeros((8, d), F32).at[:b].set(c).at[b].set(c_ctx)
    mod = _stage_mods(cs, w_ada[0], row(b_ada[0]))
    sh1, sc1, ga1, sh2, sc2, ga2 = [m[:b, None, :] for m in jnp.split(mod, 6, axis=-1)]
    csh1, csc1 = [jnp.broadcast_to(m[b][None, None, :], (b, 1, d))
                  for m in jnp.split(mod, 6, axis=-1)[:2]]

    w_in_bf = w_in[0].astype(BF16)
    n1 = row(norm1_g[0])
    p_lat = _stage_inproj(x, sh1, sc1, n1, w_in_bf[:, :RWKV_COLS], tm=512)
    p_ctx = _stage_inproj(ctx, csh1, csc1, n1, w_in_bf[:, :RWKV_COLS], tm=ctx.shape[1])

    head_id = jnp.arange(w, dtype=jnp.int32) // HEAD_DIM
    headsum = (head_id[:, None] == head_id[None, :]).astype(F32)
    prm = [row(shift_mu[0]), row(k_k[0]), row(k_a[0]), row(r_k[0]), decay_w0[0],
           decay_lora_b[0], iclr_a0[0], iclr_lora_b[0], gate_lora_b[0], headsum]
    y, g, bonus = _rwkv_branch(p_lat, p_ctx, prm)

    bsp = jnp.repeat(sgu_b_spatial[0].T, SGU_WIDTH // SGU_GROUPS, axis=1)
    consts = [n1, w_in_bf[:, RWKV_COLS:], row(gn_w[0]), row(gn_b[0]), headsum,
              w_out_rwkv[0].astype(BF16), row(sgu_ln_w[0]), row(sgu_ln_b[0]),
              sgu_w_spatial[0].astype(BF16), bsp, w_out_sgu[0].astype(BF16),
              w_o[0].astype(BF16)]
    x1 = _stage_merge(x, y, g, bonus, sh1, sc1, ga1, consts, tm=256)

    x1f = x1.reshape(n_tok, d)
    h2, top_e, top_w, rank, counts = _stage_router(
        x1f, sh2, sc2, row(norm2_g[0]), router_w[0].T, router_b[0].reshape(-1, 1), t, tm=512)
    dest, row_tok, block_e, n_used = _route(top_e, rank, counts[:, 0], n_tok)
    xs = h2[row_tok]
    e3 = lambda a: a.reshape(N_EXPERTS, 1, -1)
    yb = _stage_experts(block_e, n_used, xs, exp_w_gate[0], e3(exp_b_gate[0]), exp_w_up[0],
                        e3(exp_b_up[0]), exp_w_down[0], e3(exp_b_down[0]))
    yg = yb[dest]
    out = _stage_combine(x1f, yg, top_w[:, :, None], ga2, row(final_norm_g), t, tm=512)
    return out.reshape(b, t, d)
```

```python
import functools
import math

import jax
import jax.numpy as jnp
from jax import lax
from jax.experimental import pallas as pl
from jax.experimental.pallas import tpu as pltpu

F32 = jnp.float32
BF16 = jnp.bfloat16
HIGHEST = lax.Precision.HIGHEST

D_MODEL = 1024
GRID_W = 64
RWKV_HEADS = 8
HEAD_DIM = 64
RWKV_WIDTH = RWKV_HEADS * HEAD_DIM
DECAY_LORA = 64
ICLR_LORA = 64
GATE_LORA = 128
RWKV_COLS = 3 * RWKV_WIDTH + 2 * DECAY_LORA + 2 * ICLR_LORA + GATE_LORA
SGU_WIDTH = 512
SGU_GROUPS = 8
SGU_CHUNK = 128
N_EXPERTS = 32
TOP_K = 4
SWIGLU_LIMIT = 7.0
SWIGLU_ALPHA = 1.702
RMS_EPS = 1e-6
LN_EPS = 1e-5
GN_EPS = 64e-5

WKV_CHUNK = 64
EXPERT_ROWS = 256
VMEM_LIMIT = 48 * 1024 * 1024


def _params(*sem):
    return pltpu.CompilerParams(dimension_semantics=sem, vmem_limit_bytes=VMEM_LIMIT)


def _mm(a, b, dims=((1,), (0,)), exact=False):
    dn = (dims, ((), ()))
    if exact:
        return lax.dot_general(a, b, dn, precision=HIGHEST, preferred_element_type=F32)
    return lax.dot_general(a.astype(BF16), b.astype(BF16), dn, preferred_element_type=F32)


def _split3(x):
    hi = x.astype(BF16)
    r1 = x - hi.astype(F32)
    mid = r1.astype(BF16)
    lo = (r1 - mid.astype(F32)).astype(BF16)
    return hi, mid, lo


def _mm_x_sel(x, sel):
    return sum(_mm(p, sel) for p in _split3(x))


def _mm_sel_x(sel, x):
    return sum(_mm(sel, p) for p in _split3(x))


def _full(shape):
    n = len(shape)
    return pl.BlockSpec(shape, lambda *_: (0,) * n)


def _norm_mod(x, g, shift, scale):
    y = x * lax.rsqrt(jnp.mean(x * x, axis=-1, keepdims=True) + RMS_EPS) * g
    return y * (1.0 + scale) + shift


def _mods_kernel(c_ref, w_ref, b_ref, o_ref):
    c = c_ref[...]
    s = c * jax.nn.sigmoid(c)
    o_ref[...] = _mm(s, w_ref[...], exact=True) + b_ref[...]


def _stage_mods(cs, w_ada, b_ada):
    rows, d = cs.shape
    n = w_ada.shape[1]
    tn = 1536
    return pl.pallas_call(
        _mods_kernel,
        grid=(n // tn,),
        in_specs=[_full((rows, d)),
                  pl.BlockSpec((d, tn), lambda j: (0, j)),
                  pl.BlockSpec((1, tn), lambda j: (0, j))],
        out_specs=pl.BlockSpec((rows, tn), lambda j: (0, j)),
        out_shape=jax.ShapeDtypeStruct((rows, n), F32),
        compiler_params=_params("arbitrary"),
        name="mods",
    )(cs, w_ada, b_ada)


def _inproj_kernel(x_ref, sh_ref, sc_ref, g_ref, w_ref, o_ref):
    h = _norm_mod(x_ref[0], g_ref[...], sh_ref[0], sc_ref[0])
    o_ref[0] = _mm(h, w_ref[...])


def _stage_inproj(x, shift, scale, g, w_bf16, tm):
    b, t, d = x.shape
    n = w_bf16.shape[1]
    return pl.pallas_call(
        _inproj_kernel,
        grid=(b, t // tm),
        in_specs=[pl.BlockSpec((1, tm, d), lambda i, j: (i, j, 0)),
                  pl.BlockSpec((1, 1, d), lambda i, j: (i, 0, 0)),
                  pl.BlockSpec((1, 1, d), lambda i, j: (i, 0, 0)),
                  _full((1, d)),
                  _full((d, n))],
        out_specs=pl.BlockSpec((1, tm, n), lambda i, j: (i, j, 0)),
        out_shape=jax.ShapeDtypeStruct((b, t, n), F32),
        compiler_params=_params("arbitrary", "arbitrary"),
        name="inproj",
    )(x, shift, scale, g, w_bf16)


def _rwkv_feats(p, mu_kk, mu_ka, r_k, w0, dlb, a0, ilb, glb, headsum):
    w = RWKV_WIDTH
    r = p[:, 0:w]
    k = p[:, w:2 * w]
    v = p[:, 2 * w:3 * w]
    o = 3 * w
    wd = (p[:, o:o + DECAY_LORA], p[:, o + DECAY_LORA:o + 2 * DECAY_LORA])
    o += 2 * DECAY_LORA
    ad = (p[:, o:o + ICLR_LORA], p[:, o + ICLR_LORA:o + 2 * ICLR_LORA])
    o += 2 * ICLR_LORA
    gd = p[:, o:o + GATE_LORA]

    kk = k * mu_kk
    kk = kk * lax.rsqrt(_mm_x_sel(kk * kk, headsum) + 1e-12)
    g = _mm(jax.nn.sigmoid(gd), glb, exact=True)
    ks, bs, lws = [], [], []
    ksum = None
    for d in range(2):
        z = w0[d:d + 1] + _mm(jnp.tanh(wd[d]), dlb[d], exact=True)
        lws.append(-math.exp(-0.5) * jax.nn.sigmoid(z))
        ic = jax.nn.sigmoid(a0[d:d + 1] + _mm(ad[d], ilb[d], exact=True))
        kd = k * (1.0 + (ic - 1.0) * mu_ka)
        ks.append(kd)
        bs.append(kk * ic)
        ksum = kd if ksum is None else ksum + kd
    bonus = _mm_x_sel(r * ksum * r_k, headsum) * v
    return r, v, -kk, g, bonus, ks, bs, lws


def _prep_kernel(pm_ref, pp_ref, pn_ref, mu_ref, kk_ref, ka_ref, rk_ref, w0_ref, dlb_ref,
                 a0_ref, ilb_ref, glb_ref, hs_ref,
                 r_ref, v_ref, a_ref, g_ref, bon_ref, k_ref, b_ref, lw_ref, *, tm, grid_shift):
    main = pm_ref[0]
    lane = lax.broadcasted_iota(jnp.int32, (1, RWKV_COLS), 1)
    if grid_shift:
        ext = jnp.concatenate([pp_ref[0], main, pn_ref[0]], axis=0)
        t = pl.program_id(1) * tm + lax.broadcasted_iota(jnp.int32, (tm, 1), 0)
        col = t & (GRID_W - 1)
        row = t >> (GRID_W.bit_length() - 1)
        n_rows = pl.num_programs(1) * tm // GRID_W
        left = jnp.where(col > 0, ext[GRID_W - 1:GRID_W - 1 + tm], 0.0)
        right = jnp.where(col < GRID_W - 1, ext[GRID_W + 1:GRID_W + 1 + tm], 0.0)
        up = jnp.where(row > 0, ext[0:tm], 0.0)
        down = jnp.where(row < n_rows - 1, ext[2 * GRID_W:2 * GRID_W + tm], 0.0)
        cm = lane & 3
        shifted = jnp.where(cm == 0, left, jnp.where(cm == 1, right, jnp.where(cm == 2, up, down)))
    else:
        zero = jnp.zeros((1, RWKV_COLS), F32)
        prev = jnp.concatenate([zero, main[:tm - 1]], axis=0)
        nxt = jnp.concatenate([main[1:], zero], axis=0)
        shifted = jnp.where((lane & 1) == 0, prev, nxt)
    p = main + mu_ref[...] * (shifted - main)
    r, v, a, g, bonus, ks, bs, lws = _rwkv_feats(
        p, kk_ref[...], ka_ref[...], rk_ref[...], w0_ref[...], dlb_ref, a0_ref[...], ilb_ref,
        glb_ref[...], hs_ref[...])
    r_ref[0] = r
    v_ref[0] = v
    a_ref[0] = a
    g_ref[0] = g
    bon_ref[0] = bonus
    for d in range(2):
        k_ref[d, 0] = ks[d]
        b_ref[d, 0] = bs[d]
        lw_ref[d, 0] = lws[d]


def _stage_prep(p, prm, tm, grid_shift):
    b, t, c = p.shape
    w = RWKV_WIDTH
    hb = GRID_W if grid_shift else 8
    per = tm // hb
    last = t // hb - 1
    tok = pl.BlockSpec((1, tm, w), lambda i, j: (i, j, 0))
    tok2 = pl.BlockSpec((2, 1, tm, w), lambda i, j: (0, i, j, 0))
    in_specs = [
        pl.BlockSpec((1, tm, c), lambda i, j: (i, j, 0)),
        pl.BlockSpec((1, hb, c), lambda i, j: (i, jnp.maximum(j * per - 1, 0), 0)),
        pl.BlockSpec((1, hb, c), lambda i, j: (i, jnp.minimum((j + 1) * per, last), 0)),
    ] + [_full(a.shape) for a in prm]
    return pl.pallas_call(
        functools.partial(_prep_kernel, tm=tm, grid_shift=grid_shift),
        grid=(b, t // tm),
        in_specs=in_specs,
        out_specs=[tok] * 5 + [tok2] * 3,
        out_shape=[jax.ShapeDtypeStruct((b, t, w), F32)] * 5
        + [jax.ShapeDtypeStruct((2, b, t, w), F32)] * 3,
        compiler_params=_params("arbitrary", "arbitrary"),
        name="prep_lat" if grid_shift else "prep_ctx",
    )(p, p, p, *prm)


def _wkv_chunk(d, r, v, a, k, bb, lw, z, emit):
    c = WKV_CHUNK
    n = HEAD_DIM
    heads = range(RWKV_HEADS)
    ii = lax.broadcasted_iota(jnp.int32, (c, c), 0)
    jj = lax.broadcasted_iota(jnp.int32, (c, c), 1)
    eye = (ii == jj).astype(F32)
    incl = (jj <= ii) if d == 0 else (jj >= ii)
    strict = (jj < ii) if d == 0 else (jj > ii)
    lc = _mm_sel_x(incl.astype(BF16), lw)
    lx = lc - lw
    ltot = lc[c - 1:c] if d == 0 else lc[0:1]
    inv = jnp.exp(-lc)
    tail = jnp.exp(ltot - lc)
    at = (a * jnp.exp(lx)).astype(BF16)
    rt = (r * jnp.exp(lc)).astype(BF16)
    kt = (k * inv).astype(BF16)
    bt = (bb * inv).astype(BF16)
    kh = (k * tail).astype(BF16)
    bh = (bb * tail).astype(BF16)
    vb = v.astype(BF16)
    zb = z.astype(BF16)
    sl = [slice(h * n, (h + 1) * n) for h in heads]
    nt = ((1,), (1,))
    tn = ((0,), (0,))

    if emit:
        gram = [_mm(jnp.concatenate([at[:, s], rt[:, s]], axis=0),
                    jnp.concatenate([kt[:, s], bt[:, s]], axis=0), nt) for s in sl]
    else:
        gram = [_mm(at[:, s], jnp.concatenate([kt[:, s], bt[:, s]], axis=0), nt) for s in sl]
    a_ab = [jnp.where(strict, g[:c, c:], 0.0) for g in gram]
    x = [m.astype(BF16) for m in a_ab]
    tinv = [eye + m for m in a_ab]
    for _ in range(5):
        x = [_mm(m, m).astype(BF16) for m in x]
        tinv = [t_ + _mm(t_, m) for t_, m in zip(tinv, x)]
    if emit:
        lhs_z = [jnp.concatenate([at[:, s], rt[:, s]], axis=0) for s in sl]
        lhs_v = [jnp.concatenate([jnp.where(strict, g[:c, :c], 0.0),
                                  jnp.where(incl, g[c:, :c], 0.0)], axis=0) for g in gram]
    else:
        lhs_z = [at[:, s] for s in sl]
        lhs_v = [jnp.where(strict, g[:c, :c], 0.0) for g in gram]
    zv = [_mm(lz, zb[:, s]) + _mm(lv, vb[:, s]) for lz, lv, s in zip(lhs_z, lhs_v, sl)]
    u = [_mm(t_, m[:c]).astype(BF16) for t_, m in zip(tinv, zv)]
    z_new = [_mm(jnp.concatenate([kh[:, s], bh[:, s]], axis=0),
                 jnp.concatenate([vb[:, s], uh], axis=0), tn) for s, uh in zip(sl, u)]
    eye_n = (lax.broadcasted_iota(jnp.int32, (n, n), 0)
             == lax.broadcasted_iota(jnp.int32, (n, n), 1)).astype(F32)
    etot = jnp.exp(ltot)
    scale = jnp.concatenate(
        [jnp.broadcast_to(jnp.sum(eye_n * etot[:, s], axis=1, keepdims=True), (n, n))
         for s in sl], axis=1)
    z_cat = jnp.concatenate(z_new, axis=1) + scale * z
    if not emit:
        return None, z_cat
    y = [m[c:] + _mm(jnp.where(incl, g[c:, c:], 0.0), uh) for m, g, uh in zip(zv, gram, u)]
    return jnp.concatenate(y, axis=1), z_cat


def _wkv_kernel(*refs, emit, has_init):
    refs = list(refs)
    ins = [[refs.pop(0) for _ in range(6)] for _ in range(2)]
    z0_ref = refs.pop(0) if has_init else None
    outs = [refs.pop(0) for _ in range(2 if emit else 1)]
    z_scr = refs.pop(0)

    @pl.when(pl.program_id(1) == 0)
    def _():
        if has_init:
            z_scr[...] = z0_ref[:, 0]
        else:
            z_scr[...] = jnp.zeros_like(z_scr)

    for d in range(2):
        r_ref, v_ref, a_ref, k_ref, b_ref, lw_ref = ins[d]
        y, z_new = _wkv_chunk(d, r_ref[0], v_ref[0], a_ref[0], k_ref[0, 0], b_ref[0, 0],
                              lw_ref[0, 0], z_scr[d], emit)
        z_scr[d] = z_new
        if emit:
            outs[d][0] = y
        else:
            outs[0][d, 0] = z_new


def _stage_wkv(r, v, a, k, bb, lw, z0=None, emit=True):
    b, t, w = r.shape
    c = WKV_CHUNK
    nch = t // c
    n = HEAD_DIM
    pos = (lambda s: s, lambda s: nch - 1 - s)
    in_specs, args = [], []
    for d in range(2):
        tok = pl.BlockSpec((1, c, w), lambda i, s, d=d: (i, pos[d](s), 0))
        tok2 = pl.BlockSpec((1, 1, c, w), lambda i, s, d=d: (d, i, pos[d](s), 0))
        in_specs += [tok, tok, tok, tok2, tok2, tok2]
        args += [r, v, a, k, bb, lw]
    zspec = pl.BlockSpec((2, 1, n, w), lambda i, s: (0, i, 0, 0))
    if z0 is not None:
        in_specs.append(zspec)
        args.append(z0)
    if emit:
        out_specs = [pl.BlockSpec((1, c, w), lambda i, s, d=d: (i, pos[d](s), 0)) for d in range(2)]
        out_shape = [jax.ShapeDtypeStruct((b, t, w), F32)] * 2
    else:
        out_specs = [zspec]
        out_shape = [jax.ShapeDtypeStruct((2, b, n, w), F32)]
    return pl.pallas_call(
        functools.partial(_wkv_kernel, emit=emit, has_init=z0 is not None),
        grid=(b, nch),
        in_specs=in_specs,
        out_specs=out_specs,
        out_shape=out_shape,
        scratch_shapes=[pltpu.VMEM((2, n, w), F32)],
        compiler_params=_params("arbitrary", "arbitrary"),
        name="wkv_lat" if emit else "wkv_ctx",
    )(*args)


def _merge_kernel(x_ref, yf_ref, yb_ref, g_ref, bon_ref, sh_ref, sc_ref, ga_ref, n1_ref, w2_ref,
                  gnw_ref, gnb_ref, hs_ref, wor_ref, lnw_ref, lnb_ref, wsp_ref, bsp_ref,
                  wos_ref, wo_ref, o_ref, *, tm):
    x = x_ref[0]
    h = _norm_mod(x, n1_ref[...], sh_ref[0], sc_ref[0])
    p2 = _mm(h, w2_ref[...])

    ps = p2[:, :2 * SGU_WIDTH]
    ge = 0.5 * ps * (1.0 + lax.erf(ps * (1.0 / math.sqrt(2.0))))
    u = ge[:, :SGU_WIDTH]
    z = ge[:, SGU_WIDTH:]
    mu = jnp.mean(z, axis=-1, keepdims=True)
    zc = z - mu
    var = jnp.mean(zc * zc, axis=-1, keepdims=True)
    z = zc * lax.rsqrt(var + LN_EPS) * lnw_ref[...] + lnb_ref[...]
    gw = SGU_WIDTH // SGU_GROUPS
    rows = []
    for c in range(tm // SGU_CHUNK):
        zc = z[c * SGU_CHUNK:(c + 1) * SGU_CHUNK]
        cols = [_mm(wsp_ref[gi], zc[:, gi * gw:(gi + 1) * gw]) for gi in range(SGU_GROUPS)]
        rows.append(jnp.concatenate(cols, axis=1) + bsp_ref[...])
    s = jnp.concatenate(rows, axis=0)
    y_b = _mm(u * s, wos_ref[...])

    y = yf_ref[0] + yb_ref[0]
    hs = hs_ref[...]
    ym = _mm_x_sel(y, hs) * (1.0 / HEAD_DIM)
    yc = y - ym
    yv = _mm_x_sel(yc * yc, hs) * (1.0 / HEAD_DIM)
    yn = yc * lax.rsqrt(yv + GN_EPS) * gnw_ref[...] + gnb_ref[...]
    y_a = _mm((yn + bon_ref[0]) * g_ref[0], wor_ref[...])

    gates = jax.nn.sigmoid(p2[:, 2 * SGU_WIDTH:])
    mix = gates[:, :D_MODEL] * y_a + gates[:, D_MODEL:] * y_b
    o_ref[0] = x + ga_ref[0] * _mm(mix, wo_ref[...])


def _stage_merge(x, yf, yb, g, bonus, sh, sc, ga, consts, tm):
    b, t, d = x.shape
    w = RWKV_WIDTH
    mod = pl.BlockSpec((1, 1, d), lambda i, j: (i, 0, 0))
    tok = pl.BlockSpec((1, tm, w), lambda i, j: (i, j, 0))
    return pl.pallas_call(
        functools.partial(_merge_kernel, tm=tm),
        grid=(b, t // tm),
        in_specs=[pl.BlockSpec((1, tm, d), lambda i, j: (i, j, 0)), tok, tok, tok, tok,
                  mod, mod, mod] + [_full(a.shape) for a in consts],
        out_specs=pl.BlockSpec((1, tm, d), lambda i, j: (i, j, 0)),
        out_shape=jax.ShapeDtypeStruct((b, t, d), F32),
        compiler_params=_params("arbitrary", "arbitrary"),
        name="merge",
    )(x, yf, yb, g, bonus, sh, sc, ga, *consts)


def _to_token_tiles(ref, val, rows):
    for cc in range(val.shape[1] // 128):
        ref[pl.ds(cc, rows, stride=8), :] = val[:, cc * 128:(cc + 1) * 128]


def _from_token_tiles(ref, rows):
    return jnp.concatenate([ref[pl.ds(cc, rows, stride=8), :] for cc in range(8)], axis=1)


def _router_kernel(x_ref, sh_ref, sc_ref, n2_ref, rwt_ref, rb_ref,
                   h_ref, xt_ref, e_ref, w_ref, rank_ref, cnt_ref, carry, *, tm):
    @pl.when(pl.program_id(0) == 0)
    def _():
        carry[...] = jnp.zeros_like(carry)

    x = x_ref[...]
    h = _norm_mod(x, n2_ref[...], sh_ref[0], sc_ref[0])
    _to_token_tiles(h_ref, h, tm)
    _to_token_tiles(xt_ref, x, tm)
    logits = _mm(rwt_ref[...], h, dims=((1,), (1,)), exact=True) + rb_ref[...]
    eio = lax.broadcasted_iota(jnp.int32, (N_EXPERTS, tm), 0)
    vals, sels = [], []
    for _ in range(TOP_K):
        m = jnp.max(logits, axis=0, keepdims=True)
        idx = jnp.min(jnp.where(logits == m, eio, N_EXPERTS), axis=0, keepdims=True)
        sel = eio == idx
        logits = jnp.where(sel, -jnp.inf, logits)
        vals.append(m)
        sels.append(sel)
        e_ref[len(vals) - 1:len(vals), :] = idx
    ex = [jnp.exp(vk - vals[0]) for vk in vals]
    tot = ex[0] + ex[1] + ex[2] + ex[3]
    for kk in range(TOP_K):
        w_ref[kk:kk + 1, :] = ex[kk] / tot
    cnt = (sels[0] | sels[1] | sels[2] | sels[3]).astype(F32)
    ti = lax.broadcasted_iota(jnp.int32, (tm, tm), 0)
    tj = lax.broadcasted_iota(jnp.int32, (tm, tm), 1)
    before = _mm(cnt, (ti < tj).astype(F32))
    base = carry[...] + before
    for kk in range(TOP_K):
        rank_ref[kk:kk + 1, :] = jnp.sum(jnp.where(sels[kk], base, 0.0), axis=0,
                                         keepdims=True).astype(jnp.int32)
    new = carry[...] + jnp.sum(cnt, axis=1, keepdims=True)
    carry[...] = new
    cnt_ref[...] = jnp.broadcast_to(new, cnt_ref.shape).astype(jnp.int32)


def _stage_router(x2, sh, sc, n2g, rwt, rb, tokens_per_batch, tm):
    n, d = x2.shape
    per = tokens_per_batch // tm
    mod = pl.BlockSpec((1, 1, d), lambda i: (i // per, 0, 0))
    lane = pl.BlockSpec((TOP_K, tm), lambda i: (0, i))
    assert d == 8 * 128, "token-tile layout stores one (8, 128) tile per token"
    tiles = pl.BlockSpec((tm * 8, 128), lambda i: (i, 0))
    return pl.pallas_call(
        functools.partial(_router_kernel, tm=tm),
        grid=(n // tm,),
        in_specs=[pl.BlockSpec((tm, d), lambda i: (i, 0)), mod, mod,
                  _full(n2g.shape), _full(rwt.shape), _full(rb.shape)],
        out_specs=[tiles, tiles, lane, lane, lane, _full((N_EXPERTS, 128))],
        out_shape=[jax.ShapeDtypeStruct((n * 8, 128), F32),
                   jax.ShapeDtypeStruct((n * 8, 128), F32),
                   jax.ShapeDtypeStruct((TOP_K, n), jnp.int32),
                   jax.ShapeDtypeStruct((TOP_K, n), F32),
                   jax.ShapeDtypeStruct((TOP_K, n), jnp.int32),
                   jax.ShapeDtypeStruct((N_EXPERTS, 128), jnp.int32)],
        scratch_shapes=[pltpu.VMEM((N_EXPERTS, 1), F32)],
        compiler_params=_params("arbitrary"),
        name="router",
    )(x2, sh, sc, n2g, rwt, rb)


def _expert_kernel(be_ref, nu_ref, x_ref, wg_ref, bg_ref, wu_ref, bu_ref, wd_ref, bd_ref,
                   o_ref, wg_s, wu_s, wd_s):
    i = pl.program_id(0)
    prev = be_ref[jnp.maximum(i - 1, 0)]

    @pl.when((i == 0) | (be_ref[i] != prev))
    def _():
        wg_s[...] = wg_ref[0].astype(BF16)
        wu_s[...] = wu_ref[0].astype(BF16)
        wd_s[...] = wd_ref[0].astype(BF16)

    @pl.when(i < nu_ref[0])
    def _():
        x = _from_token_tiles(x_ref, EXPERT_ROWS)
        gate = _mm(x, wg_s[...]) + bg_ref[0]
        up = _mm(x, wu_s[...]) + bu_ref[0]
        gate = jnp.minimum(gate, SWIGLU_LIMIT)
        up = jnp.clip(up, -SWIGLU_LIMIT, SWIGLU_LIMIT)
        act = gate * jax.nn.sigmoid(SWIGLU_ALPHA * gate) * (up + 1.0)
        _to_token_tiles(o_ref, _mm(act, wd_s[...]) + bd_ref[0], EXPERT_ROWS)

    @pl.when(i >= nu_ref[0])
    def _():
        o_ref[...] = jnp.zeros_like(o_ref)


def _stage_experts(block_e, n_used, xs, wg, bg, wu, bu, wd, bd):
    n_rows = xs.shape[0] // 8
    d, f = wg.shape[1:]
    bm = EXPERT_ROWS
    wspec = lambda a, b_: pl.BlockSpec((1, a, b_), lambda i, be, nu: (be[i], 0, 0))
    return pl.pallas_call(
        _expert_kernel,
        grid_spec=pltpu.PrefetchScalarGridSpec(
            num_scalar_prefetch=2,
            grid=(n_rows // bm,),
            in_specs=[pl.BlockSpec((bm * 8, 128),
                                   lambda i, be, nu: (jnp.minimum(i, nu[0] - 1), 0)),
                      wspec(d, f), wspec(1, f), wspec(d, f), wspec(1, f),
                      wspec(f, d), wspec(1, d)],
            out_specs=pl.BlockSpec((bm * 8, 128), lambda i, be, nu: (i, 0)),
            scratch_shapes=[pltpu.VMEM((d, f), BF16), pltpu.VMEM((d, f), BF16),
                            pltpu.VMEM((f, d), BF16)]),
        out_shape=jax.ShapeDtypeStruct((n_rows * 8, 128), F32),
        compiler_params=_params("arbitrary"),
        name="experts",
    )(block_e, n_used, xs, wg, bg, wu, bu, wd, bd)


def _tile_copy(src_ref, src_row, dst_ref, dst_row, sem):
    src = src_ref.at[pl.ds(pl.multiple_of(src_row * 8, 8), 8)]
    dst = dst_ref.at[pl.ds(pl.multiple_of(dst_row * 8, 8), 8)]
    return pltpu.make_async_copy(src, dst, sem)


def _dispatch_kernel(dest_ref, zstart_ref, zon_ref, h_ref, xs_ref, zeros, sem, zsem, *, tm, n_tok):
    i = pl.program_id(0)
    bm8 = EXPERT_ROWS * 8

    def zero_fill(e):
        dst = xs_ref.at[pl.ds(pl.multiple_of(zstart_ref[e] * 8, 8), bm8)]
        return pltpu.make_async_copy(zeros, dst, zsem)

    @pl.when(i == 0)
    def _():
        zeros[...] = jnp.zeros_like(zeros)
        for e in range(zon_ref.shape[0]):
            @pl.when(zon_ref[e] > 0)
            def _():
                zero_fill(e).start()
        for e in range(zon_ref.shape[0]):
            @pl.when(zon_ref[e] > 0)
            def _():
                zero_fill(e).wait()

    def body(t, carry):
        for kk in range(TOP_K):
            _tile_copy(h_ref, t, xs_ref, dest_ref[kk * n_tok + i * tm + t], sem).start()
        return carry

    lax.fori_loop(0, tm, body, 0)
    for _ in range(TOP_K):
        pltpu.make_async_copy(h_ref, xs_ref.at[pl.ds(0, tm * 8)], sem).wait()


def _stage_dispatch(dest_flat, zstart, zon, h2t, n_rows, tm):
    n = h2t.shape[0] // 8
    return pl.pallas_call(
        functools.partial(_dispatch_kernel, tm=tm, n_tok=n),
        grid_spec=pltpu.PrefetchScalarGridSpec(
            num_scalar_prefetch=3,
            grid=(n // tm,),
            in_specs=[pl.BlockSpec((tm * 8, 128), lambda i, *_: (i, 0))],
            out_specs=pl.BlockSpec(memory_space=pl.ANY),
            scratch_shapes=[pltpu.VMEM((EXPERT_ROWS * 8, 128), F32),
                            pltpu.SemaphoreType.DMA(()), pltpu.SemaphoreType.DMA(())]),
        out_shape=jax.ShapeDtypeStruct((n_rows * 8, 128), F32),
        compiler_params=_params("arbitrary"),
        name="dispatch",
    )(dest_flat, zstart, zon, h2t)


def _combine_kernel(dest_ref, w_ref, xt_ref, ga_ref, g_ref, yb_ref, o_ref, buf, res, sem,
                    *, tm, n_tok):
    i = pl.program_id(0)

    def issue(tile, slot):
        def body(t, carry):
            for kk in range(TOP_K):
                _tile_copy(yb_ref, dest_ref[kk * n_tok + tile * tm + t], buf.at[slot, kk], t,
                           sem.at[slot]).start()
            return carry
        lax.fori_loop(0, tm, body, 0)

    @pl.when(i == 0)
    def _():
        issue(0, 0)

    @pl.when(i + 1 < pl.num_programs(0))
    def _():
        issue(i + 1, (i + 1) % 2)

    slot = i % 2
    for kk in range(TOP_K):
        pltpu.make_async_copy(buf.at[slot, kk], buf.at[slot, kk], sem.at[slot]).wait()
    ga = ga_ref[0]
    g = g_ref[...]
    d = 8 * 128

    def token(t, carry):
        rows = pl.ds(pl.multiple_of(t * 8, 8), 8)
        acc = buf[slot, 0, rows, :] * w_ref[i * tm + t]
        for kk in range(1, TOP_K):
            acc = acc + buf[slot, kk, rows, :] * w_ref[kk * n_tok + i * tm + t]
        xv = xt_ref[rows, :] + ga * acc
        ss = jnp.sum(jnp.sum(xv * xv, axis=1, keepdims=True), axis=0, keepdims=True)
        res[rows, :] = xv * lax.rsqrt(ss * (1.0 / d) + RMS_EPS) * g
        return carry

    lax.fori_loop(0, tm, token, 0, unroll=8)
    o_ref[...] = _from_token_tiles(res, tm)


def _stage_combine(dest_flat, w_flat, x1t, ga_t, g_t, yb, tokens_per_batch, tm):
    n = x1t.shape[0] // 8
    d = 8 * 128
    per = tokens_per_batch // tm
    return pl.pallas_call(
        functools.partial(_combine_kernel, tm=tm, n_tok=n),
        grid_spec=pltpu.PrefetchScalarGridSpec(
            num_scalar_prefetch=2,
            grid=(n // tm,),
            in_specs=[pl.BlockSpec((tm * 8, 128), lambda i, *_: (i, 0)),
                      pl.BlockSpec((1, 8, 128), lambda i, *_: (i // per, 0, 0)),
                      pl.BlockSpec((8, 128), lambda i, *_: (0, 0)),
                      pl.BlockSpec(memory_space=pl.ANY)],
            out_specs=pl.BlockSpec((tm, d), lambda i, *_: (i, 0)),
            scratch_shapes=[pltpu.VMEM((2, TOP_K, tm * 8, 128), F32),
                            pltpu.VMEM((tm * 8, 128), F32),
                            pltpu.SemaphoreType.DMA((2,))]),
        out_shape=jax.ShapeDtypeStruct((n, d), F32),
        compiler_params=_params("arbitrary"),
        name="combine",
    )(dest_flat, w_flat, x1t, ga_t, g_t, yb)


def _rwkv_branch(p_lat, p_ctx, prm):
    r, v, a, _, _, k, bb, lw = _stage_prep(p_ctx, prm, tm=p_ctx.shape[1], grid_shift=False)
    (z_ctx,) = _stage_wkv(r, v, a, k, bb, lw, emit=False)
    r, v, a, g, bonus, k, bb, lw = _stage_prep(p_lat, prm, tm=256, grid_shift=True)
    y_f, y_b = _stage_wkv(r, v, a, k, bb, lw, z0=z_ctx, emit=True)
    return y_f, y_b, g, bonus


def _route(top_e, rank, counts, n_tok):
    bm = EXPERT_ROWS
    n_rows = n_tok * TOP_K + N_EXPERTS * bm
    padded = (counts + bm - 1) // bm * bm
    pad_end = jnp.cumsum(padded)
    pad_start = pad_end - padded
    experts = jnp.arange(N_EXPERTS, dtype=jnp.int32)
    start_of = jnp.sum(jnp.where(top_e[..., None] == experts, pad_start, 0), axis=-1)
    dest = (start_of + rank).astype(jnp.int32).reshape(-1)
    block_start = jnp.arange(n_rows // bm, dtype=jnp.int32) * bm
    block_e = jnp.minimum(jnp.sum(pad_end[None, :] <= block_start[:, None], axis=1),
                          N_EXPERTS - 1).astype(jnp.int32)
    n_used = (pad_end[-1] // bm).astype(jnp.int32).reshape(1)
    tail = n_used[0] + experts
    zstart = jnp.concatenate([jnp.maximum(pad_end - bm, 0),
                              jnp.minimum(tail, n_rows // bm - 1) * bm]).astype(jnp.int32)
    zon = jnp.concatenate([padded > 0, tail < n_rows // bm]).astype(jnp.int32)
    return dest, block_e, n_used, zstart, zon, n_rows


def kernel(x, c, ctx, c_ctx, w_ada, b_ada, norm1_g, w_in, shift_mu, decay_w0, decay_lora_b,
           iclr_a0, iclr_lora_b, gate_lora_b, k_k, k_a, r_k, gn_w, gn_b, w_out_rwkv,
           sgu_ln_w, sgu_ln_b, sgu_w_spatial, sgu_b_spatial, w_out_sgu, w_o, norm2_g,
           router_w, router_b, exp_w_gate, exp_b_gate, exp_w_up, exp_b_up, exp_w_down,
           exp_b_down, final_norm_g):
    assert w_ada.shape[0] == 1, "single-layer problem"
    b, t, d = x.shape
    n_tok = b * t
    w = RWKV_WIDTH
    row = lambda a: a.reshape(1, -1)

    cs = jnp.zeros((8, d), F32).at[:b].set(c).at[b].set(c_ctx)
    mod = _stage_mods(cs, w_ada[0], row(b_ada[0]))
    sh1, sc1, ga1, sh2, sc2, ga2 = [m[:b, None, :] for m in jnp.split(mod, 6, axis=-1)]
    csh1, csc1 = [jnp.broadcast_to(m[b][None, None, :], (b, 1, d))
                  for m in jnp.split(mod, 6, axis=-1)[:2]]

    w_in_bf = w_in[0].astype(BF16)
    n1 = row(norm1_g[0])
    p_lat = _stage_inproj(x, sh1, sc1, n1, w_in_bf[:, :RWKV_COLS], tm=512)
    p_ctx = _stage_inproj(ctx, csh1, csc1, n1, w_in_bf[:, :RWKV_COLS], tm=ctx.shape[1])

    head_id = jnp.arange(w, dtype=jnp.int32) // HEAD_DIM
    headsum = (head_id[:, None] == head_id[None, :]).astype(BF16)
    prm = [row(shift_mu[0]), row(k_k[0]), row(k_a[0]), row(r_k[0]), decay_w0[0],
           decay_lora_b[0], iclr_a0[0], iclr_lora_b[0], gate_lora_b[0], headsum]
    y_f, y_b, g, bonus = _rwkv_branch(p_lat, p_ctx, prm)

    bsp = jnp.repeat(sgu_b_spatial[0].T, SGU_WIDTH // SGU_GROUPS, axis=1)
    consts = [n1, w_in_bf[:, RWKV_COLS:], row(gn_w[0]), row(gn_b[0]), headsum,
              w_out_rwkv[0].astype(BF16), row(sgu_ln_w[0]), row(sgu_ln_b[0]),
              sgu_w_spatial[0].astype(BF16), bsp, w_out_sgu[0].astype(BF16),
              w_o[0].astype(BF16)]
    x1 = _stage_merge(x, y_f, y_b, g, bonus, sh1, sc1, ga1, consts, tm=256)

    x1f = x1.reshape(n_tok, d)
    h2t, x1t, top_e, top_w, rank, counts = _stage_router(
        x1f, sh2, sc2, row(norm2_g[0]), router_w[0].T, router_b[0].reshape(-1, 1), t, tm=512)
    dest, block_e, n_used, zstart, zon, n_rows = _route(top_e, rank, counts[:, 0], n_tok)
    xs = _stage_dispatch(dest, zstart, zon, h2t, n_rows, tm=512)
    e3 = lambda a: a.reshape(N_EXPERTS, 1, -1)
    yb = _stage_experts(block_e, n_used, xs, exp_w_gate[0], e3(exp_b_gate[0]), exp_w_up[0],
                        e3(exp_b_up[0]), exp_w_down[0], e3(exp_b_down[0]))
    out = _stage_combine(dest, top_w.reshape(-1), x1t, ga2.reshape(b, 8, 128),
                         final_norm_g.reshape(8, 128), yb, t, tm=256)
    return out.reshape(b, t, d)
```

```python
import functools
import math

import jax
import jax.numpy as jnp
from jax import lax
from jax.experimental import pallas as pl
from jax.experimental.pallas import tpu as pltpu

F32 = jnp.float32
BF16 = jnp.bfloat16
HIGHEST = lax.Precision.HIGHEST

D_MODEL = 1024
GRID_W = 64
RWKV_HEADS = 8
HEAD_DIM = 64
RWKV_WIDTH = RWKV_HEADS * HEAD_DIM
DECAY_LORA = 64
ICLR_LORA = 64
GATE_LORA = 128
RWKV_COLS = 3 * RWKV_WIDTH + 2 * DECAY_LORA + 2 * ICLR_LORA + GATE_LORA
SGU_WIDTH = 512
SGU_GROUPS = 8
SGU_CHUNK = 128
N_EXPERTS = 32
TOP_K = 4
SWIGLU_LIMIT = 7.0
SWIGLU_ALPHA = 1.702
RMS_EPS = 1e-6
LN_EPS = 1e-5
GN_EPS = 64e-5

WKV_CHUNK = 64
WKV_GROUP = 4
EXPERT_ROWS = 256
VMEM_LIMIT = 48 * 1024 * 1024


def _params(*sem):
    return pltpu.CompilerParams(dimension_semantics=sem, vmem_limit_bytes=VMEM_LIMIT)


def _mm(a, b, dims=((1,), (0,)), exact=False):
    dn = (dims, ((), ()))
    if exact:
        return lax.dot_general(a, b, dn, precision=HIGHEST, preferred_element_type=F32)
    return lax.dot_general(a.astype(BF16), b.astype(BF16), dn, preferred_element_type=F32)


def _split3(x):
    hi = x.astype(BF16)
    r1 = x - hi.astype(F32)
    mid = r1.astype(BF16)
    lo = (r1 - mid.astype(F32)).astype(BF16)
    return hi, mid, lo


def _mm_sel_x(sel, x):
    return sum(_mm(sel, p) for p in _split3(x))


def _full(shape):
    n = len(shape)
    return pl.BlockSpec(shape, lambda *_: (0,) * n)


def _norm_mod(x, g, shift, scale):
    y = x * lax.rsqrt(jnp.mean(x * x, axis=-1, keepdims=True) + RMS_EPS) * g
    return y * (1.0 + scale) + shift


def _mods_kernel(c_ref, w_ref, b_ref, o_ref):
    c = c_ref[...]
    s = c * jax.nn.sigmoid(c)
    o_ref[...] = _mm(s, w_ref[...], exact=True) + b_ref[...]


def _stage_mods(cs, w_ada, b_ada):
    rows, d = cs.shape
    n = w_ada.shape[1]
    tn = 1536
    return pl.pallas_call(
        _mods_kernel,
        grid=(n // tn,),
        in_specs=[_full((rows, d)),
                  pl.BlockSpec((d, tn), lambda j: (0, j)),
                  pl.BlockSpec((1, tn), lambda j: (0, j))],
        out_specs=pl.BlockSpec((rows, tn), lambda j: (0, j)),
        out_shape=jax.ShapeDtypeStruct((rows, n), F32),
        compiler_params=_params("arbitrary"),
        name="mods",
    )(cs, w_ada, b_ada)


def _inproj_kernel(x_ref, sh_ref, sc_ref, g_ref, w_ref, o_ref):
    h = _norm_mod(x_ref[0], g_ref[...], sh_ref[0], sc_ref[0])
    o_ref[0] = _mm(h, w_ref[...])


def _stage_inproj(x, shift, scale, g, w_bf16, tm):
    b, t, d = x.shape
    n = w_bf16.shape[1]
    return pl.pallas_call(
        _inproj_kernel,
        grid=(b, t // tm),
        in_specs=[pl.BlockSpec((1, tm, d), lambda i, j: (i, j, 0)),
                  pl.BlockSpec((1, 1, d), lambda i, j: (i, 0, 0)),
                  pl.BlockSpec((1, 1, d), lambda i, j: (i, 0, 0)),
                  _full((1, d)),
                  _full((d, n))],
        out_specs=pl.BlockSpec((1, tm, n), lambda i, j: (i, j, 0)),
        out_shape=jax.ShapeDtypeStruct((b, t, n), F32),
        compiler_params=_params("arbitrary", "arbitrary"),
        name="inproj",
    )(x, shift, scale, g, w_bf16)


def _rwkv_feats(p, mu_kk, mu_ka, r_k, w0, dlb, a0, ilb, glb, headsum):
    w = RWKV_WIDTH
    r = p[:, 0:w]
    k = p[:, w:2 * w]
    v = p[:, 2 * w:3 * w]
    o = 3 * w
    wd = (p[:, o:o + DECAY_LORA], p[:, o + DECAY_LORA:o + 2 * DECAY_LORA])
    o += 2 * DECAY_LORA
    ad = (p[:, o:o + ICLR_LORA], p[:, o + ICLR_LORA:o + 2 * ICLR_LORA])
    o += 2 * ICLR_LORA
    gd = p[:, o:o + GATE_LORA]

    kk = k * mu_kk
    kk = kk * lax.rsqrt(_mm(kk * kk, headsum) + 1e-12)
    g = _mm(jax.nn.sigmoid(gd), glb)
    ks, bs, lws = [], [], []
    ksum = None
    for d in range(2):
        z = w0[d:d + 1] + _mm(jnp.tanh(wd[d]), dlb[d])
        lws.append(-math.exp(-0.5) * jax.nn.sigmoid(z))
        ic = jax.nn.sigmoid(a0[d:d + 1] + _mm(ad[d], ilb[d]))
        kd = k * (1.0 + (ic - 1.0) * mu_ka)
        ks.append(kd)
        bs.append(kk * ic)
        ksum = kd if ksum is None else ksum + kd
    bonus = _mm(r * ksum * r_k, headsum) * v
    return r, v, -kk, g, bonus, ks, bs, lws


def _prep_kernel(pm_ref, pp_ref, pn_ref, mu_ref, kk_ref, ka_ref, rk_ref, w0_ref, dlb_ref,
                 a0_ref, ilb_ref, glb_ref, hs_ref,
                 r_ref, v_ref, a_ref, g_ref, bon_ref, k_ref, b_ref, lw_ref, *, tm, grid_shift):
    main = pm_ref[0]
    lane = lax.broadcasted_iota(jnp.int32, (1, RWKV_COLS), 1)
    if grid_shift:
        ext = jnp.concatenate([pp_ref[0], main, pn_ref[0]], axis=0)
        t = pl.program_id(1) * tm + lax.broadcasted_iota(jnp.int32, (tm, 1), 0)
        col = t & (GRID_W - 1)
        row = t >> (GRID_W.bit_length() - 1)
        n_rows = pl.num_programs(1) * tm // GRID_W
        left = jnp.where(col > 0, ext[GRID_W - 1:GRID_W - 1 + tm], 0.0)
        right = jnp.where(col < GRID_W - 1, ext[GRID_W + 1:GRID_W + 1 + tm], 0.0)
        up = jnp.where(row > 0, ext[0:tm], 0.0)
        down = jnp.where(row < n_rows - 1, ext[2 * GRID_W:2 * GRID_W + tm], 0.0)
        cm = lane & 3
        shifted = jnp.where(cm == 0, left, jnp.where(cm == 1, right, jnp.where(cm == 2, up, down)))
    else:
        zero = jnp.zeros((1, RWKV_COLS), F32)
        prev = jnp.concatenate([zero, main[:tm - 1]], axis=0)
        nxt = jnp.concatenate([main[1:], zero], axis=0)
        shifted = jnp.where((lane & 1) == 0, prev, nxt)
    p = main + mu_ref[...] * (shifted - main)
    r, v, a, g, bonus, ks, bs, lws = _rwkv_feats(
        p, kk_ref[...], ka_ref[...], rk_ref[...], w0_ref[...], dlb_ref, a0_ref[...], ilb_ref,
        glb_ref[...], hs_ref[...])
    r_ref[0] = r
    v_ref[0] = v
    a_ref[0] = a
    g_ref[0] = g
    bon_ref[0] = bonus
    for d in range(2):
        k_ref[d, 0] = ks[d]
        b_ref[d, 0] = bs[d]
        lw_ref[d, 0] = lws[d]


def _stage_prep(p, prm, tm, grid_shift):
    b, t, c = p.shape
    w = RWKV_WIDTH
    hb = GRID_W if grid_shift else 8
    per = tm // hb
    last = t // hb - 1
    tok = pl.BlockSpec((1, tm, w), lambda i, j: (i, j, 0))
    tok2 = pl.BlockSpec((2, 1, tm, w), lambda i, j: (0, i, j, 0))
    in_specs = [
        pl.BlockSpec((1, tm, c), lambda i, j: (i, j, 0)),
        pl.BlockSpec((1, hb, c), lambda i, j: (i, jnp.maximum(j * per - 1, 0), 0)),
        pl.BlockSpec((1, hb, c), lambda i, j: (i, jnp.minimum((j + 1) * per, last), 0)),
    ] + [_full(a.shape) for a in prm]
    return pl.pallas_call(
        functools.partial(_prep_kernel, tm=tm, grid_shift=grid_shift),
        grid=(b, t // tm),
        in_specs=in_specs,
        out_specs=[tok] * 5 + [tok2] * 3,
        out_shape=[jax.ShapeDtypeStruct((b, t, w), F32)] * 5
        + [jax.ShapeDtypeStruct((2, b, t, w), F32)] * 3,
        compiler_params=_params("arbitrary", "arbitrary"),
        name="prep_lat" if grid_shift else "prep_ctx",
    )(p, p, p, *prm)


def _wkv_chunks(dirs, emit):
    c = WKV_CHUNK
    n = HEAD_DIM
    gw = WKV_GROUP * n
    n_groups = RWKV_WIDTH // gw
    ii = lax.broadcasted_iota(jnp.int32, (c, gw), 0)
    assert c == n and n & (n - 1) == 0
    jj = lax.broadcasted_iota(jnp.int32, (c, gw), 1) & (c - 1)
    eye = (ii == jj).astype(F32)
    eye_n = (lax.broadcasted_iota(jnp.int32, (n, n), 0)
             == lax.broadcasted_iota(jnp.int32, (n, n), 1)).astype(F32)
    head_shift = n.bit_length() - 1
    same_head = (lax.broadcasted_iota(jnp.int32, (gw, gw), 0) >> head_shift
                 == lax.broadcasted_iota(jnp.int32, (gw, gw), 1) >> head_shift)
    nt = ((1,), (1,))
    tn = ((0,), (0,))
    sl = [slice(h * n, (h + 1) * n) for h in range(RWKV_HEADS)]

    def bd(x):
        x = x.astype(BF16)
        return jnp.where(same_head, jnp.concatenate([x] * WKV_GROUP, axis=0), jnp.zeros((), BF16))

    prob = []
    scales = []
    for d, (r, v, a, k, bb, lw, z) in enumerate(dirs):
        incl = (jj <= ii) if d == 0 else (jj >= ii)
        strict = (jj < ii) if d == 0 else (jj > ii)
        lc = _mm_sel_x(incl[:, :c].astype(BF16), lw)
        lx = lc - lw
        ltot = lc[c - 1:c] if d == 0 else lc[0:1]
        inv = jnp.exp(-lc)
        tail = jnp.exp(ltot - lc)
        etot = jnp.exp(ltot)
        at = (a * jnp.exp(lx)).astype(BF16)
        rt = (r * jnp.exp(lc)).astype(BF16)
        kt = (k * inv).astype(BF16)
        bt = (bb * inv).astype(BF16)
        kh = (k * tail).astype(BF16)
        bh = (bb * tail).astype(BF16)
        vb = v.astype(BF16)
        zb = z.astype(BF16)
        scales.append(jnp.concatenate(
            [jnp.broadcast_to(jnp.sum(eye_n * etot[:, s], axis=1, keepdims=True), (n, n))
             for s in sl], axis=1))
        for gi in range(n_groups):
            gs = slice(gi * gw, (gi + 1) * gw)
            prob.append(dict(incl=incl, strict=strict, at=at[:, gs], rt=rt[:, gs], kt=kt[:, gs],
                             bt=bt[:, gs], kh=kh[:, gs], bh=bh[:, gs], v=vb[:, gs], z=zb[:, gs]))

    for p in prob:
        lhs = jnp.concatenate([p["at"], p["rt"]], axis=0) if emit else p["at"]
        p["lhs_z"] = lhs
        p["gk"] = _mm(lhs, bd(p["kt"]), nt)
        p["gb"] = _mm(lhs, bd(p["bt"]), nt)
    for p in prob:
        a_ab = jnp.where(p["strict"], p["gb"][:c], 0.0)
        p["tinv"] = eye + a_ab
        p["x"] = _mm(a_ab, bd(a_ab))
    for level in range(1, 6):
        for p in prob:
            if level < 5:
                both = _mm(jnp.concatenate([p["x"], p["tinv"]], axis=0), bd(p["x"]))
                p["x"] = both[:c]
                p["tinv"] = p["tinv"] + both[c:]
            else:
                p["tinv"] = p["tinv"] + _mm(p["tinv"], bd(p["x"]))
    for p in prob:
        lhs_v = jnp.where(p["strict"], p["gk"][:c], 0.0)
        if emit:
            lhs_v = jnp.concatenate([lhs_v, jnp.where(p["incl"], p["gk"][c:], 0.0)], axis=0)
        p["zv"] = _mm(p["lhs_z"], bd(p["z"])) + _mm(lhs_v, bd(p["v"]))
    for p in prob:
        p["u"] = _mm(p["tinv"], bd(p["zv"][:c])).astype(BF16)
    for p in prob:
        if emit:
            p["y"] = p["zv"][c:] + _mm(jnp.where(p["incl"], p["gb"][c:], 0.0), bd(p["u"]))
        p["z_new"] = [_mm(jnp.concatenate([p["kh"][:, s], p["bh"][:, s]], axis=0),
                          jnp.concatenate([p["v"][:, s], p["u"][:, s]], axis=0), tn)
                      for s in sl[:WKV_GROUP]]

    out = []
    for d, dd in enumerate(dirs):
        mine = prob[d * n_groups:(d + 1) * n_groups]
        z_cat = jnp.concatenate([m for p in mine for m in p["z_new"]], axis=1) + scales[d] * dd[6]
        y = jnp.concatenate([p["y"] for p in mine], axis=1) if emit else None
        out.append((y, z_cat))
    return out


def _wkv_kernel(*refs, emit, has_init):
    refs = list(refs)
    ins = [[refs.pop(0) for _ in range(6)] for _ in range(2)]
    z0_ref = refs.pop(0) if has_init else None
    outs = [refs.pop(0) for _ in range(2 if emit else 1)]
    z_scr = refs.pop(0)

    @pl.when(pl.program_id(1) == 0)
    def _():
        if has_init:
            z_scr[...] = z0_ref[:, 0]
        else:
            z_scr[...] = jnp.zeros_like(z_scr)

    dirs = []
    for d in range(2):
        r_ref, v_ref, a_ref, k_ref, b_ref, lw_ref = ins[d]
        dirs.append((r_ref[0], v_ref[0], a_ref[0], k_ref[0, 0], b_ref[0, 0], lw_ref[0, 0],
                     z_scr[d]))
    for d, (y, z_new) in enumerate(_wkv_chunks(dirs, emit)):
        z_scr[d] = z_new
        if emit:
            outs[d][0] = y
        else:
            outs[0][d, 0] = z_new


def _stage_wkv(r, v, a, k, bb, lw, z0=None, emit=True):
    b, t, w = r.shape
    c = WKV_CHUNK
    nch = t // c
    n = HEAD_DIM
    pos = (lambda s: s, lambda s: nch - 1 - s)
    in_specs, args = [], []
    for d in range(2):
        tok = pl.BlockSpec((1, c, w), lambda i, s, d=d: (i, pos[d](s), 0))
        tok2 = pl.BlockSpec((1, 1, c, w), lambda i, s, d=d: (d, i, pos[d](s), 0))
        in_specs += [tok, tok, tok, tok2, tok2, tok2]
        args += [r, v, a, k, bb, lw]
    zspec = pl.BlockSpec((2, 1, n, w), lambda i, s: (0, i, 0, 0))
    if z0 is not None:
        in_specs.append(zspec)
        args.append(z0)
    if emit:
        out_specs = [pl.BlockSpec((1, c, w), lambda i, s, d=d: (i, pos[d](s), 0)) for d in range(2)]
        out_shape = [jax.ShapeDtypeStruct((b, t, w), F32)] * 2
    else:
        out_specs = [zspec]
        out_shape = [jax.ShapeDtypeStruct((2, b, n, w), F32)]
    return pl.pallas_call(
        functools.partial(_wkv_kernel, emit=emit, has_init=z0 is not None),
        grid=(b, nch),
        in_specs=in_specs,
        out_specs=out_specs,
        out_shape=out_shape,
        scratch_shapes=[pltpu.VMEM((2, n, w), F32)],
        compiler_params=_params("arbitrary", "arbitrary"),
        name="wkv_lat" if emit else "wkv_ctx",
    )(*args)


def _merge_kernel(x_ref, yf_ref, yb_ref, g_ref, bon_ref, sh_ref, sc_ref, ga_ref, n1_ref, w2_ref,
                  gnw_ref, gnb_ref, hs_ref, wor_ref, lnw_ref, lnb_ref, wsp_ref, bsp_ref,
                  wos_ref, wo_ref, o_ref, *, tm):
    x = x_ref[0]
    h = _norm_mod(x, n1_ref[...], sh_ref[0], sc_ref[0])
    p2 = _mm(h, w2_ref[...])

    ps = p2[:, :2 * SGU_WIDTH]
    ge = 0.5 * ps * (1.0 + lax.erf(ps * (1.0 / math.sqrt(2.0))))
    u = ge[:, :SGU_WIDTH]
    z = ge[:, SGU_WIDTH:]
    mu = jnp.mean(z, axis=-1, keepdims=True)
    zc = z - mu
    var = jnp.mean(zc * zc, axis=-1, keepdims=True)
    z = zc * lax.rsqrt(var + LN_EPS) * lnw_ref[...] + lnb_ref[...]
    gw = SGU_WIDTH // SGU_GROUPS
    rows = []
    for c in range(tm // SGU_CHUNK):
        zc = z[c * SGU_CHUNK:(c + 1) * SGU_CHUNK]
        cols = [_mm(wsp_ref[gi], zc[:, gi * gw:(gi + 1) * gw]) for gi in range(SGU_GROUPS)]
        rows.append(jnp.concatenate(cols, axis=1) + bsp_ref[...])
    s = jnp.concatenate(rows, axis=0)
    y_b = _mm(u * s, wos_ref[...])

    y = yf_ref[0] + yb_ref[0]
    hs = hs_ref[...]
    ym = _mm(y, hs) * (1.0 / HEAD_DIM)
    yc = y - ym
    yv = _mm(yc * yc, hs) * (1.0 / HEAD_DIM)
    yn = yc * lax.rsqrt(yv + GN_EPS) * gnw_ref[...] + gnb_ref[...]
    y_a = _mm((yn + bon_ref[0]) * g_ref[0], wor_ref[...])

    gates = jax.nn.sigmoid(p2[:, 2 * SGU_WIDTH:])
    mix = gates[:, :D_MODEL] * y_a + gates[:, D_MODEL:] * y_b
    o_ref[0] = x + ga_ref[0] * _mm(mix, wo_ref[...])


def _stage_merge(x, yf, yb, g, bonus, sh, sc, ga, consts, tm):
    b, t, d = x.shape
    w = RWKV_WIDTH
    mod = pl.BlockSpec((1, 1, d), lambda i, j: (i, 0, 0))
    tok = pl.BlockSpec((1, tm, w), lambda i, j: (i, j, 0))
    return pl.pallas_call(
        functools.partial(_merge_kernel, tm=tm),
        grid=(b, t // tm),
        in_specs=[pl.BlockSpec((1, tm, d), lambda i, j: (i, j, 0)), tok, tok, tok, tok,
                  mod, mod, mod] + [_full(a.shape) for a in consts],
        out_specs=pl.BlockSpec((1, tm, d), lambda i, j: (i, j, 0)),
        out_shape=jax.ShapeDtypeStruct((b, t, d), F32),
        compiler_params=_params("arbitrary", "arbitrary"),
        name="merge",
    )(x, yf, yb, g, bonus, sh, sc, ga, *consts)


def _to_token_tiles(ref, val, rows):
    for cc in range(val.shape[1] // 128):
        ref[pl.ds(cc, rows, stride=8), :] = val[:, cc * 128:(cc + 1) * 128]


def _from_token_tiles(ref, rows):
    return jnp.concatenate([ref[pl.ds(cc, rows, stride=8), :] for cc in range(8)], axis=1)


def _router_kernel(x_ref, sh_ref, sc_ref, n2_ref, rwt_ref, rb_ref,
                   h_ref, xt_ref, e_ref, w_ref, rank_ref, cnt_ref, carry, *, tm):
    @pl.when(pl.program_id(0) == 0)
    def _():
        carry[...] = jnp.zeros_like(carry)

    x = x_ref[...]
    h = _norm_mod(x, n2_ref[...], sh_ref[0], sc_ref[0])
    _to_token_tiles(h_ref, h, tm)
    _to_token_tiles(xt_ref, x, tm)
    logits = _mm(rwt_ref[...], h, dims=((1,), (1,)), exact=True) + rb_ref[...]
    eio = lax.broadcasted_iota(jnp.int32, (N_EXPERTS, tm), 0)
    vals, sels = [], []
    for _ in range(TOP_K):
        m = jnp.max(logits, axis=0, keepdims=True)
        idx = jnp.min(jnp.where(logits == m, eio, N_EXPERTS), axis=0, keepdims=True)
        sel = eio == idx
        logits = jnp.where(sel, -jnp.inf, logits)
        vals.append(m)
        sels.append(sel)
        e_ref[len(vals) - 1:len(vals), :] = idx
    ex = [jnp.exp(vk - vals[0]) for vk in vals]
    tot = ex[0] + ex[1] + ex[2] + ex[3]
    for kk in range(TOP_K):
        w_ref[kk:kk + 1, :] = ex[kk] / tot
    cnt = (sels[0] | sels[1] | sels[2] | sels[3]).astype(F32)
    ti = lax.broadcasted_iota(jnp.int32, (tm, tm), 0)
    tj = lax.broadcasted_iota(jnp.int32, (tm, tm), 1)
    before = _mm(cnt, (ti < tj).astype(F32))
    base = carry[...] + before
    for kk in range(TOP_K):
        rank_ref[kk:kk + 1, :] = jnp.sum(jnp.where(sels[kk], base, 0.0), axis=0,
                                         keepdims=True).astype(jnp.int32)
    new = carry[...] + jnp.sum(cnt, axis=1, keepdims=True)
    carry[...] = new
    cnt_ref[...] = jnp.broadcast_to(new, cnt_ref.shape).astype(jnp.int32)


def _stage_router(x2, sh, sc, n2g, rwt, rb, tokens_per_batch, tm):
    n, d = x2.shape
    per = tokens_per_batch // tm
    mod = pl.BlockSpec((1, 1, d), lambda i: (i // per, 0, 0))
    lane = pl.BlockSpec((TOP_K, tm), lambda i: (0, i))
    assert d == 8 * 128, "token-tile layout stores one (8, 128) tile per token"
    tiles = pl.BlockSpec((tm * 8, 128), lambda i: (i, 0))
    return pl.pallas_call(
        functools.partial(_router_kernel, tm=tm),
        grid=(n // tm,),
        in_specs=[pl.BlockSpec((tm, d), lambda i: (i, 0)), mod, mod,
                  _full(n2g.shape), _full(rwt.shape), _full(rb.shape)],
        out_specs=[tiles, tiles, lane, lane, lane, _full((N_EXPERTS, 128))],
        out_shape=[jax.ShapeDtypeStruct((n * 8, 128), F32),
                   jax.ShapeDtypeStruct((n * 8, 128), F32),
                   jax.ShapeDtypeStruct((TOP_K, n), jnp.int32),
                   jax.ShapeDtypeStruct((TOP_K, n), F32),
                   jax.ShapeDtypeStruct((TOP_K, n), jnp.int32),
                   jax.ShapeDtypeStruct((N_EXPERTS, 128), jnp.int32)],
        scratch_shapes=[pltpu.VMEM((N_EXPERTS, 1), F32)],
        compiler_params=_params("arbitrary"),
        name="router",
    )(x2, sh, sc, n2g, rwt, rb)


def _expert_kernel(be_ref, nu_ref, x_ref, wg_ref, bg_ref, wu_ref, bu_ref, wd_ref, bd_ref,
                   o_ref, wg_s, wu_s, wd_s):
    i = pl.program_id(0)
    prev = be_ref[jnp.maximum(i - 1, 0)]

    @pl.when((i == 0) | (be_ref[i] != prev))
    def _():
        wg_s[...] = wg_ref[0].astype(BF16)
        wu_s[...] = wu_ref[0].astype(BF16)
        wd_s[...] = wd_ref[0].astype(BF16)

    @pl.when(i < nu_ref[0])
    def _():
        x = _from_token_tiles(x_ref, EXPERT_ROWS)
        gate = _mm(x, wg_s[...]) + bg_ref[0]
        up = _mm(x, wu_s[...]) + bu_ref[0]
        gate = jnp.minimum(gate, SWIGLU_LIMIT)
        up = jnp.clip(up, -SWIGLU_LIMIT, SWIGLU_LIMIT)
        act = gate * jax.nn.sigmoid(SWIGLU_ALPHA * gate) * (up + 1.0)
        _to_token_tiles(o_ref, _mm(act, wd_s[...]) + bd_ref[0], EXPERT_ROWS)

    @pl.when(i >= nu_ref[0])
    def _():
        o_ref[...] = jnp.zeros_like(o_ref)


def _stage_experts(block_e, n_used, xs, wg, bg, wu, bu, wd, bd):
    n_rows = xs.shape[0] // 8
    d, f = wg.shape[1:]
    bm = EXPERT_ROWS
    wspec = lambda a, b_: pl.BlockSpec((1, a, b_), lambda i, be, nu: (be[i], 0, 0))
    return pl.pallas_call(
        _expert_kernel,
        grid_spec=pltpu.PrefetchScalarGridSpec(
            num_scalar_prefetch=2,
            grid=(n_rows // bm,),
            in_specs=[pl.BlockSpec((bm * 8, 128),
                                   lambda i, be, nu: (jnp.clip(nu[0] - 1, 0, i), 0)),
                      wspec(d, f), wspec(1, f), wspec(d, f), wspec(1, f),
                      wspec(f, d), wspec(1, d)],
            out_specs=pl.BlockSpec((bm * 8, 128), lambda i, be, nu: (i, 0)),
            scratch_shapes=[pltpu.VMEM((d, f), BF16), pltpu.VMEM((d, f), BF16),
                            pltpu.VMEM((f, d), BF16)]),
        out_shape=jax.ShapeDtypeStruct((n_rows * 8, 128), F32),
        compiler_params=_params("arbitrary"),
        name="experts",
    )(block_e, n_used, xs, wg, bg, wu, bu, wd, bd)


def _tile_copy(src_ref, src_row, dst_ref, dst_row, sem):
    src = src_ref.at[pl.ds(pl.multiple_of(src_row * 8, 8), 8)]
    dst = dst_ref.at[pl.ds(pl.multiple_of(dst_row * 8, 8), 8)]
    return pltpu.make_async_copy(src, dst, sem)


def _dispatch_kernel(dest_ref, zstart_ref, zon_ref, h_ref, xs_ref, zeros, sem, zsem, *, tm, n_tok):
    i = pl.program_id(0)
    bm8 = EXPERT_ROWS * 8

    def zero_fill(e):
        dst = xs_ref.at[pl.ds(pl.multiple_of(zstart_ref[e] * 8, 8), bm8)]
        return pltpu.make_async_copy(zeros, dst, zsem)

    @pl.when(i == 0)
    def _():
        zeros[...] = jnp.zeros_like(zeros)
        for e in range(zon_ref.shape[0]):
            @pl.when(zon_ref[e] > 0)
            def _():
                zero_fill(e).start()
        for e in range(zon_ref.shape[0]):
            @pl.when(zon_ref[e] > 0)
            def _():
                zero_fill(e).wait()

    def body(t, carry):
        for kk in range(TOP_K):
            _tile_copy(h_ref, t, xs_ref, dest_ref[kk * n_tok + i * tm + t], sem).start()
        return carry

    lax.fori_loop(0, tm, body, 0)
    for _ in range(TOP_K):
        pltpu.make_async_copy(h_ref, xs_ref.at[pl.ds(0, tm * 8)], sem).wait()


def _stage_dispatch(dest_flat, zstart, zon, h2t, n_rows, tm):
    n = h2t.shape[0] // 8
    return pl.pallas_call(
        functools.partial(_dispatch_kernel, tm=tm, n_tok=n),
        grid_spec=pltpu.PrefetchScalarGridSpec(
            num_scalar_prefetch=3,
            grid=(n // tm,),
            in_specs=[pl.BlockSpec((tm * 8, 128), lambda i, *_: (i, 0))],
            out_specs=pl.BlockSpec(memory_space=pl.ANY),
            scratch_shapes=[pltpu.VMEM((EXPERT_ROWS * 8, 128), F32),
                            pltpu.SemaphoreType.DMA(()), pltpu.SemaphoreType.DMA(())]),
        out_shape=jax.ShapeDtypeStruct((n_rows * 8, 128), F32),
        compiler_params=_params("arbitrary"),
        name="dispatch",
    )(dest_flat, zstart, zon, h2t)


def _combine_kernel(dest_ref, w_ref, xt_ref, ga_ref, g_ref, yb_ref, o_ref, buf, res, sem,
                    *, tm, n_tok):
    i = pl.program_id(0)
    ga = ga_ref[0]

    def issue(tile, slot, t):
        for kk in range(TOP_K):
            _tile_copy(yb_ref, dest_ref[kk * n_tok + tile * tm + t], buf.at[slot, kk], t,
                       sem.at[slot]).start()

    def combine(slot, t):
        rows = pl.ds(pl.multiple_of(t * 8, 8), 8)
        acc = buf[slot, 0, rows, :] * w_ref[i * tm + t]
        for kk in range(1, TOP_K):
            acc = acc + buf[slot, kk, rows, :] * w_ref[kk * n_tok + i * tm + t]
        res[rows, :] = xt_ref[rows, :] + ga * acc

    def loop(body):
        lax.fori_loop(0, tm, lambda t, carry: (body(t), carry)[1], 0, unroll=4)

    @pl.when(i == 0)
    def _():
        loop(lambda t: issue(0, 0, t))

    for slot in range(2):
        @pl.when(i % 2 == slot)
        def _():
            for kk in range(TOP_K):
                pltpu.make_async_copy(buf.at[slot, kk], buf.at[slot, kk], sem.at[slot]).wait()

            @pl.when(i + 1 < pl.num_programs(0))
            def _():
                loop(lambda t: (issue(i + 1, 1 - slot, t), combine(slot, t)))

            @pl.when(i + 1 >= pl.num_programs(0))
            def _():
                loop(lambda t: combine(slot, t))

    x = _from_token_tiles(res, tm)
    o_ref[...] = x * lax.rsqrt(jnp.mean(x * x, axis=-1, keepdims=True) + RMS_EPS) * g_ref[...]


def _stage_combine(dest_flat, w_flat, x1t, ga_t, g_t, yb, tokens_per_batch, tm):
    n = x1t.shape[0] // 8
    d = 8 * 128
    per = tokens_per_batch // tm
    return pl.pallas_call(
        functools.partial(_combine_kernel, tm=tm, n_tok=n),
        grid_spec=pltpu.PrefetchScalarGridSpec(
            num_scalar_prefetch=2,
            grid=(n // tm,),
            in_specs=[pl.BlockSpec((tm * 8, 128), lambda i, *_: (i, 0)),
                      pl.BlockSpec((1, 8, 128), lambda i, *_: (i // per, 0, 0)),
                      pl.BlockSpec((1, d), lambda i, *_: (0, 0)),
                      pl.BlockSpec(memory_space=pl.ANY)],
            out_specs=pl.BlockSpec((tm, d), lambda i, *_: (i, 0)),
            scratch_shapes=[pltpu.VMEM((2, TOP_K, tm * 8, 128), F32),
                            pltpu.VMEM((tm * 8, 128), F32),
                            pltpu.SemaphoreType.DMA((2,))]),
        out_shape=jax.ShapeDtypeStruct((n, d), F32),
        compiler_params=_params("arbitrary"),
        name="combine",
    )(dest_flat, w_flat, x1t, ga_t, g_t, yb)


def _rwkv_branch(p_lat, p_ctx, prm):
    r, v, a, _, _, k, bb, lw = _stage_prep(p_ctx, prm, tm=p_ctx.shape[1], grid_shift=False)
    (z_ctx,) = _stage_wkv(r, v, a, k, bb, lw, emit=False)
    r, v, a, g, bonus, k, bb, lw = _stage_prep(p_lat, prm, tm=256, grid_shift=True)
    y_f, y_b = _stage_wkv(r, v, a, k, bb, lw, z0=z_ctx, emit=True)
    return y_f, y_b, g, bonus


def _route(top_e, rank, counts, n_tok):
    bm = EXPERT_ROWS
    n_rows = n_tok * TOP_K + N_EXPERTS * bm
    padded = (counts + bm - 1) // bm * bm
    pad_end = jnp.cumsum(padded)
    pad_start = pad_end - padded
    experts = jnp.arange(N_EXPERTS, dtype=jnp.int32)
    start_of = jnp.sum(jnp.where(top_e[..., None] == experts, pad_start, 0), axis=-1)
    dest = (start_of + rank).astype(jnp.int32).reshape(-1)
    block_start = jnp.arange(n_rows // bm, dtype=jnp.int32) * bm
    block_e = jnp.minimum(jnp.sum(pad_end[None, :] <= block_start[:, None], axis=1),
                          N_EXPERTS - 1).astype(jnp.int32)
    n_used = (pad_end[-1] // bm).astype(jnp.int32).reshape(1)
    tail = n_used[0] + experts
    zstart = jnp.concatenate([jnp.maximum(pad_end - bm, 0),
                              jnp.minimum(tail, n_rows // bm - 1) * bm]).astype(jnp.int32)
    zon = jnp.concatenate([padded > 0, tail < n_rows // bm]).astype(jnp.int32)
    return dest, block_e, n_used, zstart, zon, n_rows


def kernel(x, c, ctx, c_ctx, w_ada, b_ada, norm1_g, w_in, shift_mu, decay_w0, decay_lora_b,
           iclr_a0, iclr_lora_b, gate_lora_b, k_k, k_a, r_k, gn_w, gn_b, w_out_rwkv,
           sgu_ln_w, sgu_ln_b, sgu_w_spatial, sgu_b_spatial, w_out_sgu, w_o, norm2_g,
           router_w, router_b, exp_w_gate, exp_b_gate, exp_w_up, exp_b_up, exp_w_down,
           exp_b_down, final_norm_g):
    assert w_ada.shape[0] == 1, "single-layer problem"
    b, t, d = x.shape
    n_tok = b * t
    w = RWKV_WIDTH
    row = lambda a: a.reshape(1, -1)

    cs = jnp.zeros((8, d), F32).at[:b].set(c).at[b].set(c_ctx)
    mod = _stage_mods(cs, w_ada[0], row(b_ada[0]))
    sh1, sc1, ga1, sh2, sc2, ga2 = [m[:b, None, :] for m in jnp.split(mod, 6, axis=-1)]
    csh1, csc1 = [jnp.broadcast_to(m[b][None, None, :], (b, 1, d))
                  for m in jnp.split(mod, 6, axis=-1)[:2]]

    w_in_bf = w_in[0].astype(BF16)
    n1 = row(norm1_g[0])
    p_lat = _stage_inproj(x, sh1, sc1, n1, w_in_bf[:, :RWKV_COLS], tm=512)
    p_ctx = _stage_inproj(ctx, csh1, csc1, n1, w_in_bf[:, :RWKV_COLS], tm=ctx.shape[1])

    head_id = jnp.arange(w, dtype=jnp.int32) // HEAD_DIM
    headsum = (head_id[:, None] == head_id[None, :]).astype(BF16)
    prm = [row(shift_mu[0]), row(k_k[0]), row(k_a[0]), row(r_k[0]), decay_w0[0],
           decay_lora_b[0], iclr_a0[0], iclr_lora_b[0], gate_lora_b[0], headsum]
    y_f, y_b, g, bonus = _rwkv_branch(p_lat, p_ctx, prm)

    bsp = jnp.repeat(sgu_b_spatial[0].T, SGU_WIDTH // SGU_GROUPS, axis=1)
    consts = [n1, w_in_bf[:, RWKV_COLS:], row(gn_w[0]), row(gn_b[0]), headsum,
              w_out_rwkv[0].astype(BF16), row(sgu_ln_w[0]), row(sgu_ln_b[0]),
              sgu_w_spatial[0].astype(BF16), bsp, w_out_sgu[0].astype(BF16),
              w_o[0].astype(BF16)]
    x1 = _stage_merge(x, y_f, y_b, g, bonus, sh1, sc1, ga1, consts, tm=256)

    x1f = x1.reshape(n_tok, d)
    h2t, x1t, top_e, top_w, rank, counts = _stage_router(
        x1f, sh2, sc2, row(norm2_g[0]), router_w[0].T, router_b[0].reshape(-1, 1), t, tm=512)
    dest, block_e, n_used, zstart, zon, n_rows = _route(top_e, rank, counts[:, 0], n_tok)
    xs = _stage_dispatch(dest, zstart, zon, h2t, n_rows, tm=512)
    e3 = lambda a: a.reshape(N_EXPERTS, 1, -1)
    yb = _stage_experts(block_e, n_used, xs, exp_w_gate[0], e3(exp_b_gate[0]), exp_w_up[0],
                        e3(exp_b_up[0]), exp_w_down[0], e3(exp_b_down[0]))
    out = _stage_combine(dest, top_w.reshape(-1), x1t, ga2.reshape(b, 8, 128),
                         row(final_norm_g), yb, t, tm=256)
    return out.reshape(b, t, d)
```

```python
import functools
import math

import jax
import jax.numpy as jnp
from jax import lax
from jax.experimental import pallas as pl
from jax.experimental.pallas import tpu as pltpu

F32 = jnp.float32
BF16 = jnp.bfloat16
HIGHEST = lax.Precision.HIGHEST

D_MODEL = 1024
GRID_W = 64
RWKV_HEADS = 8
HEAD_DIM = 64
RWKV_WIDTH = RWKV_HEADS * HEAD_DIM
DECAY_LORA = 64
ICLR_LORA = 64
GATE_LORA = 128
RWKV_COLS = 3 * RWKV_WIDTH + 2 * DECAY_LORA + 2 * ICLR_LORA + GATE_LORA
SGU_WIDTH = 512
SGU_GROUPS = 8
SGU_CHUNK = 128
N_EXPERTS = 32
TOP_K = 4
SWIGLU_LIMIT = 7.0
SWIGLU_ALPHA = 1.702
RMS_EPS = 1e-6
LN_EPS = 1e-5
GN_EPS = 64e-5

WKV_CHUNK = 64
WKV_GROUP = 4
WKV_CHUNKS_PER_STEP = 4
EXPERT_ROWS = 256
VMEM_LIMIT = 48 * 1024 * 1024


def _params(*sem):
    return pltpu.CompilerParams(dimension_semantics=sem, vmem_limit_bytes=VMEM_LIMIT)


def _mm(a, b, dims=((1,), (0,)), exact=False):
    dn = (dims, ((), ()))
    if exact:
        return lax.dot_general(a, b, dn, precision=HIGHEST, preferred_element_type=F32)
    return lax.dot_general(a.astype(BF16), b.astype(BF16), dn, preferred_element_type=F32)


def _split3(x):
    hi = x.astype(BF16)
    r1 = x - hi.astype(F32)
    mid = r1.astype(BF16)
    lo = (r1 - mid.astype(F32)).astype(BF16)
    return hi, mid, lo


def _mm_sel_x(sel, x):
    return sum(_mm(sel, p) for p in _split3(x))


def _full(shape):
    n = len(shape)
    return pl.BlockSpec(shape, lambda *_: (0,) * n)


def _norm_mod(x, g, shift, scale):
    y = x * lax.rsqrt(jnp.mean(x * x, axis=-1, keepdims=True) + RMS_EPS) * g
    return y * (1.0 + scale) + shift


def _mods_kernel(c_ref, w_ref, b_ref, o_ref):
    c = c_ref[...]
    s = c * jax.nn.sigmoid(c)
    o_ref[...] = _mm(s, w_ref[...], exact=True) + b_ref[...]


def _stage_mods(cs, w_ada, b_ada):
    rows, d = cs.shape
    n = w_ada.shape[1]
    tn = 1536
    return pl.pallas_call(
        _mods_kernel,
        grid=(n // tn,),
        in_specs=[_full((rows, d)),
                  pl.BlockSpec((d, tn), lambda j: (0, j)),
                  pl.BlockSpec((1, tn), lambda j: (0, j))],
        out_specs=pl.BlockSpec((rows, tn), lambda j: (0, j)),
        out_shape=jax.ShapeDtypeStruct((rows, n), F32),
        compiler_params=_params("arbitrary"),
        name="mods",
    )(cs, w_ada, b_ada)


def _inproj_kernel(x_ref, sh_ref, sc_ref, g_ref, w_ref, o_ref):
    h = _norm_mod(x_ref[0], g_ref[...], sh_ref[0], sc_ref[0])
    o_ref[0] = _mm(h, w_ref[...])


def _stage_inproj(x, shift, scale, g, w_bf16, tm):
    b, t, d = x.shape
    n = w_bf16.shape[1]
    return pl.pallas_call(
        _inproj_kernel,
        grid=(b, t // tm),
        in_specs=[pl.BlockSpec((1, tm, d), lambda i, j: (i, j, 0)),
                  pl.BlockSpec((1, 1, d), lambda i, j: (i, 0, 0)),
                  pl.BlockSpec((1, 1, d), lambda i, j: (i, 0, 0)),
                  _full((1, d)),
                  _full((d, n))],
        out_specs=pl.BlockSpec((1, tm, n), lambda i, j: (i, j, 0)),
        out_shape=jax.ShapeDtypeStruct((b, t, n), F32),
        compiler_params=_params("arbitrary", "arbitrary"),
        name="inproj",
    )(x, shift, scale, g, w_bf16)


def _rwkv_feats(p, mu_kk, mu_ka, r_k, w0, dlb, a0, ilb, glb, headsum):
    w = RWKV_WIDTH
    r = p[:, 0:w]
    k = p[:, w:2 * w]
    v = p[:, 2 * w:3 * w]
    o = 3 * w
    wd = (p[:, o:o + DECAY_LORA], p[:, o + DECAY_LORA:o + 2 * DECAY_LORA])
    o += 2 * DECAY_LORA
    ad = (p[:, o:o + ICLR_LORA], p[:, o + ICLR_LORA:o + 2 * ICLR_LORA])
    o += 2 * ICLR_LORA
    gd = p[:, o:o + GATE_LORA]

    kk = k * mu_kk
    kk = kk * lax.rsqrt(_mm(kk * kk, headsum) + 1e-12)
    g = _mm(jax.nn.sigmoid(gd), glb)
    ks, bs, lws = [], [], []
    ksum = None
    for d in range(2):
        z = w0[d:d + 1] + _mm(jnp.tanh(wd[d]), dlb[d])
        lws.append(-math.exp(-0.5) * jax.nn.sigmoid(z))
        ic = jax.nn.sigmoid(a0[d:d + 1] + _mm(ad[d], ilb[d]))
        kd = k * (1.0 + (ic - 1.0) * mu_ka)
        ks.append(kd)
        bs.append(kk * ic)
        ksum = kd if ksum is None else ksum + kd
    bonus = _mm(r * ksum * r_k, headsum) * v
    return r, v, -kk, g, bonus, ks, bs, lws


def _prep_kernel(pm_ref, pp_ref, pn_ref, mu_ref, kk_ref, ka_ref, rk_ref, w0_ref, dlb_ref,
                 a0_ref, ilb_ref, glb_ref, hs_ref,
                 r_ref, v_ref, a_ref, g_ref, bon_ref, k_ref, b_ref, lw_ref, *, tm, grid_shift):
    main = pm_ref[0]
    lane = lax.broadcasted_iota(jnp.int32, (1, RWKV_COLS), 1)
    if grid_shift:
        ext = jnp.concatenate([pp_ref[0], main, pn_ref[0]], axis=0)
        t = pl.program_id(1) * tm + lax.broadcasted_iota(jnp.int32, (tm, 1), 0)
        col = t & (GRID_W - 1)
        row = t >> (GRID_W.bit_length() - 1)
        n_rows = pl.num_programs(1) * tm // GRID_W
        left = jnp.where(col > 0, ext[GRID_W - 1:GRID_W - 1 + tm], 0.0)
        right = jnp.where(col < GRID_W - 1, ext[GRID_W + 1:GRID_W + 1 + tm], 0.0)
        up = jnp.where(row > 0, ext[0:tm], 0.0)
        down = jnp.where(row < n_rows - 1, ext[2 * GRID_W:2 * GRID_W + tm], 0.0)
        cm = lane & 3
        shifted = jnp.where(cm == 0, left, jnp.where(cm == 1, right, jnp.where(cm == 2, up, down)))
    else:
        zero = jnp.zeros((1, RWKV_COLS), F32)
        prev = jnp.concatenate([zero, main[:tm - 1]], axis=0)
        nxt = jnp.concatenate([main[1:], zero], axis=0)
        shifted = jnp.where((lane & 1) == 0, prev, nxt)
    p = main + mu_ref[...] * (shifted - main)
    r, v, a, g, bonus, ks, bs, lws = _rwkv_feats(
        p, kk_ref[...], ka_ref[...], rk_ref[...], w0_ref[...], dlb_ref, a0_ref[...], ilb_ref,
        glb_ref[...], hs_ref[...])
    r_ref[0] = r
    v_ref[0] = v
    a_ref[0] = a
    g_ref[0] = g
    bon_ref[0] = bonus
    for d in range(2):
        k_ref[d, 0] = ks[d]
        b_ref[d, 0] = bs[d]
        lw_ref[d, 0] = lws[d]


def _stage_prep(p, prm, tm, grid_shift):
    b, t, c = p.shape
    w = RWKV_WIDTH
    hb = GRID_W if grid_shift else 8
    per = tm // hb
    last = t // hb - 1
    tok = pl.BlockSpec((1, tm, w), lambda i, j: (i, j, 0))
    tok2 = pl.BlockSpec((2, 1, tm, w), lambda i, j: (0, i, j, 0))
    in_specs = [
        pl.BlockSpec((1, tm, c), lambda i, j: (i, j, 0)),
        pl.BlockSpec((1, hb, c), lambda i, j: (i, jnp.maximum(j * per - 1, 0), 0)),
        pl.BlockSpec((1, hb, c), lambda i, j: (i, jnp.minimum((j + 1) * per, last), 0)),
    ] + [_full(a.shape) for a in prm]
    return pl.pallas_call(
        functools.partial(_prep_kernel, tm=tm, grid_shift=grid_shift),
        grid=(b, t // tm),
        in_specs=in_specs,
        out_specs=[tok] * 5 + [tok2] * 3,
        out_shape=[jax.ShapeDtypeStruct((b, t, w), F32)] * 5
        + [jax.ShapeDtypeStruct((2, b, t, w), F32)] * 3,
        compiler_params=_params("arbitrary", "arbitrary"),
        name="prep_lat" if grid_shift else "prep_ctx",
    )(p, p, p, *prm)


def _wkv_chunks(dirs, emit):
    c = WKV_CHUNK
    n = HEAD_DIM
    gw = WKV_GROUP * n
    n_groups = RWKV_WIDTH // gw
    ii = lax.broadcasted_iota(jnp.int32, (c, gw), 0)
    assert c == n and n & (n - 1) == 0
    jj = lax.broadcasted_iota(jnp.int32, (c, gw), 1) & (c - 1)
    eye = (ii == jj).astype(F32)
    eye_n = (lax.broadcasted_iota(jnp.int32, (n, n), 0)
             == lax.broadcasted_iota(jnp.int32, (n, n), 1)).astype(F32)
    head_shift = n.bit_length() - 1
    same_head = (lax.broadcasted_iota(jnp.int32, (gw, gw), 0) >> head_shift
                 == lax.broadcasted_iota(jnp.int32, (gw, gw), 1) >> head_shift)
    nt = ((1,), (1,))
    tn = ((0,), (0,))
    sl = [slice(h * n, (h + 1) * n) for h in range(RWKV_HEADS)]

    def bd(x):
        x = x.astype(BF16)
        return jnp.where(same_head, jnp.concatenate([x] * WKV_GROUP, axis=0), jnp.zeros((), BF16))

    prob = []
    scales = []
    for d, (r, v, a, k, bb, lw, z) in enumerate(dirs):
        incl = (jj <= ii) if d == 0 else (jj >= ii)
        strict = (jj < ii) if d == 0 else (jj > ii)
        lc = _mm_sel_x(incl[:, :c].astype(BF16), lw)
        lx = lc - lw
        ltot = lc[c - 1:c] if d == 0 else lc[0:1]
        inv = jnp.exp(-lc)
        tail = jnp.exp(ltot - lc)
        etot = jnp.exp(ltot)
        at = (a * jnp.exp(lx)).astype(BF16)
        rt = (r * jnp.exp(lc)).astype(BF16)
        kt = (k * inv).astype(BF16)
        bt = (bb * inv).astype(BF16)
        kh = (k * tail).astype(BF16)
        bh = (bb * tail).astype(BF16)
        vb = v.astype(BF16)
        zb = z.astype(BF16)
        scales.append(jnp.concatenate(
            [jnp.broadcast_to(jnp.sum(eye_n * etot[:, s], axis=1, keepdims=True), (n, n))
             for s in sl], axis=1))
        for gi in range(n_groups):
            gs = slice(gi * gw, (gi + 1) * gw)
            prob.append(dict(incl=incl, strict=strict, at=at[:, gs], rt=rt[:, gs], kt=kt[:, gs],
                             bt=bt[:, gs], kh=kh[:, gs], bh=bh[:, gs], v=vb[:, gs], z=zb[:, gs]))

    for p in prob:
        lhs = jnp.concatenate([p["at"], p["rt"]], axis=0) if emit else p["at"]
        p["lhs_z"] = lhs
        p["gk"] = _mm(lhs, bd(p["kt"]), nt)
        p["gb"] = _mm(lhs, bd(p["bt"]), nt)
    for p in prob:
        a_ab = jnp.where(p["strict"], p["gb"][:c], 0.0)
        p["tinv"] = eye + a_ab
        p["x"] = _mm(a_ab, bd(a_ab))
    for level in range(1, 6):
        for p in prob:
            if level < 5:
                both = _mm(jnp.concatenate([p["x"], p["tinv"]], axis=0), bd(p["x"]))
                p["x"] = both[:c]
                p["tinv"] = p["tinv"] + both[c:]
            else:
                p["tinv"] = p["tinv"] + _mm(p["tinv"], bd(p["x"]))
    for p in prob:
        lhs_v = jnp.where(p["strict"], p["gk"][:c], 0.0)
        if emit:
            lhs_v = jnp.concatenate([lhs_v, jnp.where(p["incl"], p["gk"][c:], 0.0)], axis=0)
        p["zv"] = _mm(p["lhs_z"], bd(p["z"])) + _mm(lhs_v, bd(p["v"]))
    for p in prob:
        p["u"] = _mm(p["tinv"], bd(p["zv"][:c])).astype(BF16)
    for p in prob:
        if emit:
            p["y"] = p["zv"][c:] + _mm(jnp.where(p["incl"], p["gb"][c:], 0.0), bd(p["u"]))
        p["z_new"] = [_mm(jnp.concatenate([p["kh"][:, s], p["bh"][:, s]], axis=0),
                          jnp.concatenate([p["v"][:, s], p["u"][:, s]], axis=0), tn)
                      for s in sl[:WKV_GROUP]]

    out = []
    for d, dd in enumerate(dirs):
        mine = prob[d * n_groups:(d + 1) * n_groups]
        z_cat = jnp.concatenate([m for p in mine for m in p["z_new"]], axis=1) + scales[d] * dd[6]
        y = jnp.concatenate([p["y"] for p in mine], axis=1) if emit else None
        out.append((y, z_cat))
    return out


def _wkv_kernel(*refs, emit, has_init):
    refs = list(refs)
    ins = [[refs.pop(0) for _ in range(6)] for _ in range(2)]
    z0_ref = refs.pop(0) if has_init else None
    outs = [refs.pop(0) for _ in range(2 if emit else 1)]
    z_scr = refs.pop(0)

    @pl.when(pl.program_id(1) == 0)
    def _():
        if has_init:
            z_scr[...] = z0_ref[:, 0]
        else:
            z_scr[...] = jnp.zeros_like(z_scr)

    c = WKV_CHUNK
    per = ins[0][0].shape[1] // c

    def step(j, carry):
        dirs, rows = [], []
        for d in range(2):
            sub = j if d == 0 else per - 1 - j
            rows.append(pl.ds(pl.multiple_of(sub * c, c), c))
            r_ref, v_ref, a_ref, k_ref, b_ref, lw_ref = ins[d]
            dirs.append((r_ref[0, rows[d]], v_ref[0, rows[d]], a_ref[0, rows[d]],
                         k_ref[0, 0, rows[d]], b_ref[0, 0, rows[d]], lw_ref[0, 0, rows[d]],
                         z_scr[d]))
        for d, (y, z_new) in enumerate(_wkv_chunks(dirs, emit)):
            z_scr[d] = z_new
            if emit:
                outs[d][0, rows[d]] = y
        return carry

    lax.fori_loop(0, per, step, 0)
    if not emit:
        outs[0][:, 0] = z_scr[...]


def _stage_wkv(r, v, a, k, bb, lw, z0=None, emit=True):
    b, t, w = r.shape
    c = WKV_CHUNK * min(WKV_CHUNKS_PER_STEP, t // WKV_CHUNK)
    nch = t // c
    n = HEAD_DIM
    pos = (lambda s: s, lambda s: nch - 1 - s)
    in_specs, args = [], []
    for d in range(2):
        tok = pl.BlockSpec((1, c, w), lambda i, s, d=d: (i, pos[d](s), 0))
        tok2 = pl.BlockSpec((1, 1, c, w), lambda i, s, d=d: (d, i, pos[d](s), 0))
        in_specs += [tok, tok, tok, tok2, tok2, tok2]
        args += [r, v, a, k, bb, lw]
    zspec = pl.BlockSpec((2, 1, n, w), lambda i, s: (0, i, 0, 0))
    if z0 is not None:
        in_specs.append(zspec)
        args.append(z0)
    if emit:
        out_specs = [pl.BlockSpec((1, c, w), lambda i, s, d=d: (i, pos[d](s), 0)) for d in range(2)]
        out_shape = [jax.ShapeDtypeStruct((b, t, w), F32)] * 2
    else:
        out_specs = [zspec]
        out_shape = [jax.ShapeDtypeStruct((2, b, n, w), F32)]
    return pl.pallas_call(
        functools.partial(_wkv_kernel, emit=emit, has_init=z0 is not None),
        grid=(b, nch),
        in_specs=in_specs,
        out_specs=out_specs,
        out_shape=out_shape,
        scratch_shapes=[pltpu.VMEM((2, n, w), F32)],
        compiler_params=_params("arbitrary", "arbitrary"),
        name="wkv_lat" if emit else "wkv_ctx",
    )(*args)


def _merge_kernel(x_ref, yf_ref, yb_ref, g_ref, bon_ref, sh_ref, sc_ref, ga_ref, n1_ref, w2_ref,
                  gnw_ref, gnb_ref, hs_ref, wor_ref, lnw_ref, lnb_ref, wsp_ref, bsp_ref,
                  wos_ref, wo_ref, o_ref, *, tm):
    x = x_ref[0]
    h = _norm_mod(x, n1_ref[...], sh_ref[0], sc_ref[0])
    p2 = _mm(h, w2_ref[...])

    ps = p2[:, :2 * SGU_WIDTH]
    ge = 0.5 * ps * (1.0 + lax.erf(ps * (1.0 / math.sqrt(2.0))))
    u = ge[:, :SGU_WIDTH]
    z = ge[:, SGU_WIDTH:]
    mu = jnp.mean(z, axis=-1, keepdims=True)
    zc = z - mu
    var = jnp.mean(zc * zc, axis=-1, keepdims=True)
    z = zc * lax.rsqrt(var + LN_EPS) * lnw_ref[...] + lnb_ref[...]
    gw = SGU_WIDTH // SGU_GROUPS
    rows = []
    for c in range(tm // SGU_CHUNK):
        zc = z[c * SGU_CHUNK:(c + 1) * SGU_CHUNK]
        cols = [_mm(wsp_ref[gi], zc[:, gi * gw:(gi + 1) * gw]) for gi in range(SGU_GROUPS)]
        rows.append(jnp.concatenate(cols, axis=1) + bsp_ref[...])
    s = jnp.concatenate(rows, axis=0)
    y_b = _mm(u * s, wos_ref[...])

    y = yf_ref[0] + yb_ref[0]
    hs = hs_ref[...]
    ym = _mm(y, hs) * (1.0 / HEAD_DIM)
    yc = y - ym
    yv = _mm(yc * yc, hs) * (1.0 / HEAD_DIM)
    yn = yc * lax.rsqrt(yv + GN_EPS) * gnw_ref[...] + gnb_ref[...]
    y_a = _mm((yn + bon_ref[0]) * g_ref[0], wor_ref[...])

    gates = jax.nn.sigmoid(p2[:, 2 * SGU_WIDTH:])
    mix = gates[:, :D_MODEL] * y_a + gates[:, D_MODEL:] * y_b
    o_ref[0] = x + ga_ref[0] * _mm(mix, wo_ref[...])


def _stage_merge(x, yf, yb, g, bonus, sh, sc, ga, consts, tm):
    b, t, d = x.shape
    w = RWKV_WIDTH
    mod = pl.BlockSpec((1, 1, d), lambda i, j: (i, 0, 0))
    tok = pl.BlockSpec((1, tm, w), lambda i, j: (i, j, 0))
    return pl.pallas_call(
        functools.partial(_merge_kernel, tm=tm),
        grid=(b, t // tm),
        in_specs=[pl.BlockSpec((1, tm, d), lambda i, j: (i, j, 0)), tok, tok, tok, tok,
                  mod, mod, mod] + [_full(a.shape) for a in consts],
        out_specs=pl.BlockSpec((1, tm, d), lambda i, j: (i, j, 0)),
        out_shape=jax.ShapeDtypeStruct((b, t, d), F32),
        compiler_params=_params("arbitrary", "arbitrary"),
        name="merge",
    )(x, yf, yb, g, bonus, sh, sc, ga, *consts)


def _to_token_tiles(ref, val, rows):
    for cc in range(val.shape[1] // 128):
        ref[pl.ds(cc, rows, stride=8), :] = val[:, cc * 128:(cc + 1) * 128]


def _from_token_tiles(ref, rows):
    return jnp.concatenate([ref[pl.ds(cc, rows, stride=8), :] for cc in range(8)], axis=1)


def _router_kernel(x_ref, sh_ref, sc_ref, n2_ref, rwt_ref, rb_ref,
                   h_ref, xt_ref, e_ref, w_ref, rank_ref, cnt_ref, carry, *, tm):
    @pl.when(pl.program_id(0) == 0)
    def _():
        carry[...] = jnp.zeros_like(carry)

    x = x_ref[...]
    h = _norm_mod(x, n2_ref[...], sh_ref[0], sc_ref[0])
    _to_token_tiles(h_ref, h, tm)
    _to_token_tiles(xt_ref, x, tm)
    logits = _mm(rwt_ref[...], h, dims=((1,), (1,)), exact=True) + rb_ref[...]
    eio = lax.broadcasted_iota(jnp.int32, (N_EXPERTS, tm), 0)
    vals, sels = [], []
    for _ in range(TOP_K):
        m = jnp.max(logits, axis=0, keepdims=True)
        idx = jnp.min(jnp.where(logits == m, eio, N_EXPERTS), axis=0, keepdims=True)
        sel = eio == idx
        logits = jnp.where(sel, -jnp.inf, logits)
        vals.append(m)
        sels.append(sel)
        e_ref[len(vals) - 1:len(vals), :] = idx
    ex = [jnp.exp(vk - vals[0]) for vk in vals]
    tot = ex[0] + ex[1] + ex[2] + ex[3]
    for kk in range(TOP_K):
        w_ref[kk:kk + 1, :] = ex[kk] / tot
    cnt = (sels[0] | sels[1] | sels[2] | sels[3]).astype(F32)
    ti = lax.broadcasted_iota(jnp.int32, (tm, tm), 0)
    tj = lax.broadcasted_iota(jnp.int32, (tm, tm), 1)
    before = _mm(cnt, (ti < tj).astype(F32))
    base = carry[...] + before
    for kk in range(TOP_K):
        rank_ref[kk:kk + 1, :] = jnp.sum(jnp.where(sels[kk], base, 0.0), axis=0,
                                         keepdims=True).astype(jnp.int32)
    new = carry[...] + jnp.sum(cnt, axis=1, keepdims=True)
    carry[...] = new
    cnt_ref[...] = jnp.broadcast_to(new, cnt_ref.shape).astype(jnp.int32)


def _stage_router(x2, sh, sc, n2g, rwt, rb, tokens_per_batch, tm):
    n, d = x2.shape
    per = tokens_per_batch // tm
    mod = pl.BlockSpec((1, 1, d), lambda i: (i // per, 0, 0))
    lane = pl.BlockSpec((TOP_K, tm), lambda i: (0, i))
    assert d == 8 * 128, "token-tile layout stores one (8, 128) tile per token"
    tiles = pl.BlockSpec((tm * 8, 128), lambda i: (i, 0))
    return pl.pallas_call(
        functools.partial(_router_kernel, tm=tm),
        grid=(n // tm,),
        in_specs=[pl.BlockSpec((tm, d), lambda i: (i, 0)), mod, mod,
                  _full(n2g.shape), _full(rwt.shape), _full(rb.shape)],
        out_specs=[tiles, tiles, lane, lane, lane, _full((N_EXPERTS, 128))],
        out_shape=[jax.ShapeDtypeStruct((n * 8, 128), F32),
                   jax.ShapeDtypeStruct((n * 8, 128), F32),
                   jax.ShapeDtypeStruct((TOP_K, n), jnp.int32),
                   jax.ShapeDtypeStruct((TOP_K, n), F32),
                   jax.ShapeDtypeStruct((TOP_K, n), jnp.int32),
                   jax.ShapeDtypeStruct((N_EXPERTS, 128), jnp.int32)],
        scratch_shapes=[pltpu.VMEM((N_EXPERTS, 1), F32)],
        compiler_params=_params("arbitrary"),
        name="router",
    )(x2, sh, sc, n2g, rwt, rb)


def _expert_kernel(be_ref, nu_ref, x_ref, wg_ref, bg_ref, wu_ref, bu_ref, wd_ref, bd_ref,
                   o_ref, wg_s, wu_s, wd_s):
    i = pl.program_id(0)
    prev = be_ref[jnp.maximum(i - 1, 0)]

    @pl.when((i == 0) | (be_ref[i] != prev))
    def _():
        wg_s[...] = wg_ref[0].astype(BF16)
        wu_s[...] = wu_ref[0].astype(BF16)
        wd_s[...] = wd_ref[0].astype(BF16)

    @pl.when(i < nu_ref[0])
    def _():
        x = _from_token_tiles(x_ref, EXPERT_ROWS)
        gate = _mm(x, wg_s[...]) + bg_ref[0]
        up = _mm(x, wu_s[...]) + bu_ref[0]
        gate = jnp.minimum(gate, SWIGLU_LIMIT)
        up = jnp.clip(up, -SWIGLU_LIMIT, SWIGLU_LIMIT)
        act = gate * jax.nn.sigmoid(SWIGLU_ALPHA * gate) * (up + 1.0)
        _to_token_tiles(o_ref, _mm(act, wd_s[...]) + bd_ref[0], EXPERT_ROWS)

    @pl.when(i >= nu_ref[0])
    def _():
        o_ref[...] = jnp.zeros_like(o_ref)


def _stage_experts(block_e, n_used, xs, wg, bg, wu, bu, wd, bd):
    n_rows = xs.shape[0] // 8
    d, f = wg.shape[1:]
    bm = EXPERT_ROWS
    wspec = lambda a, b_: pl.BlockSpec((1, a, b_), lambda i, be, nu: (be[i], 0, 0))
    return pl.pallas_call(
        _expert_kernel,
        grid_spec=pltpu.PrefetchScalarGridSpec(
            num_scalar_prefetch=2,
            grid=(n_rows // bm,),
            in_specs=[pl.BlockSpec((bm * 8, 128),
                                   lambda i, be, nu: (jnp.clip(nu[0] - 1, 0, i), 0)),
                      wspec(d, f), wspec(1, f), wspec(d, f), wspec(1, f),
                      wspec(f, d), wspec(1, d)],
            out_specs=pl.BlockSpec((bm * 8, 128), lambda i, be, nu: (i, 0)),
            scratch_shapes=[pltpu.VMEM((d, f), BF16), pltpu.VMEM((d, f), BF16),
                            pltpu.VMEM((f, d), BF16)]),
        out_shape=jax.ShapeDtypeStruct((n_rows * 8, 128), F32),
        compiler_params=_params("arbitrary"),
        name="experts",
    )(block_e, n_used, xs, wg, bg, wu, bu, wd, bd)


def _tile_copy(src_ref, src_row, dst_ref, dst_row, sem):
    src = src_ref.at[pl.ds(pl.multiple_of(src_row * 8, 8), 8)]
    dst = dst_ref.at[pl.ds(pl.multiple_of(dst_row * 8, 8), 8)]
    return pltpu.make_async_copy(src, dst, sem)


def _dispatch_kernel(dest_ref, zstart_ref, zon_ref, h_ref, xs_ref, zeros, sem, zsem, *, tm, n_tok):
    i = pl.program_id(0)
    bm8 = EXPERT_ROWS * 8

    def zero_fill(e):
        dst = xs_ref.at[pl.ds(pl.multiple_of(zstart_ref[e] * 8, 8), bm8)]
        return pltpu.make_async_copy(zeros, dst, zsem)

    @pl.when(i == 0)
    def _():
        zeros[...] = jnp.zeros_like(zeros)
        for e in range(zon_ref.shape[0]):
            @pl.when(zon_ref[e] > 0)
            def _():
                zero_fill(e).start()
        for e in range(zon_ref.shape[0]):
            @pl.when(zon_ref[e] > 0)
            def _():
                zero_fill(e).wait()

    def body(t, carry):
        for kk in range(TOP_K):
            _tile_copy(h_ref, t, xs_ref, dest_ref[kk * n_tok + i * tm + t], sem).start(
                priority=kk % 2)
        return carry

    lax.fori_loop(0, tm, body, 0)
    for _ in range(TOP_K):
        pltpu.make_async_copy(h_ref, xs_ref.at[pl.ds(0, tm * 8)], sem).wait()


def _stage_dispatch(dest_flat, zstart, zon, h2t, n_rows, tm):
    n = h2t.shape[0] // 8
    return pl.pallas_call(
        functools.partial(_dispatch_kernel, tm=tm, n_tok=n),
        grid_spec=pltpu.PrefetchScalarGridSpec(
            num_scalar_prefetch=3,
            grid=(n // tm,),
            in_specs=[pl.BlockSpec((tm * 8, 128), lambda i, *_: (i, 0))],
            out_specs=pl.BlockSpec(memory_space=pl.ANY),
            scratch_shapes=[pltpu.VMEM((EXPERT_ROWS * 8, 128), F32),
                            pltpu.SemaphoreType.DMA(()), pltpu.SemaphoreType.DMA(())]),
        out_shape=jax.ShapeDtypeStruct((n_rows * 8, 128), F32),
        compiler_params=_params("arbitrary"),
        name="dispatch",
    )(dest_flat, zstart, zon, h2t)


def _combine_kernel(dest_ref, w_ref, xt_ref, ga_ref, g_ref, yb_ref, o_ref, buf, res, sem,
                    *, tm, n_tok):
    i = pl.program_id(0)
    ga = ga_ref[0]

    def issue(tile, slot, t):
        for kk in range(TOP_K):
            _tile_copy(yb_ref, dest_ref[kk * n_tok + tile * tm + t], buf.at[slot, kk], t,
                       sem.at[slot]).start(priority=kk % 2)

    def combine(slot, t):
        rows = pl.ds(pl.multiple_of(t * 8, 8), 8)
        acc = buf[slot, 0, rows, :] * w_ref[i * tm + t]
        for kk in range(1, TOP_K):
            acc = acc + buf[slot, kk, rows, :] * w_ref[kk * n_tok + i * tm + t]
        res[rows, :] = xt_ref[rows, :] + ga * acc

    def loop(body):
        lax.fori_loop(0, tm, lambda t, carry: (body(t), carry)[1], 0, unroll=4)

    @pl.when(i == 0)
    def _():
        loop(lambda t: issue(0, 0, t))

    for slot in range(2):
        @pl.when(i % 2 == slot)
        def _():
            for kk in range(TOP_K):
                pltpu.make_async_copy(buf.at[slot, kk], buf.at[slot, kk], sem.at[slot]).wait()

            @pl.when(i + 1 < pl.num_programs(0))
            def _():
                loop(lambda t: (issue(i + 1, 1 - slot, t), combine(slot, t)))

            @pl.when(i + 1 >= pl.num_programs(0))
            def _():
                loop(lambda t: combine(slot, t))

    x = _from_token_tiles(res, tm)
    o_ref[...] = x * lax.rsqrt(jnp.mean(x * x, axis=-1, keepdims=True) + RMS_EPS) * g_ref[...]


def _stage_combine(dest_flat, w_flat, x1t, ga_t, g_t, yb, tokens_per_batch, tm):
    n = x1t.shape[0] // 8
    d = 8 * 128
    per = tokens_per_batch // tm
    return pl.pallas_call(
        functools.partial(_combine_kernel, tm=tm, n_tok=n),
        grid_spec=pltpu.PrefetchScalarGridSpec(
            num_scalar_prefetch=2,
            grid=(n // tm,),
            in_specs=[pl.BlockSpec((tm * 8, 128), lambda i, *_: (i, 0)),
                      pl.BlockSpec((1, 8, 128), lambda i, *_: (i // per, 0, 0)),
                      pl.BlockSpec((1, d), lambda i, *_: (0, 0)),
                      pl.BlockSpec(memory_space=pl.ANY)],
            out_specs=pl.BlockSpec((tm, d), lambda i, *_: (i, 0)),
            scratch_shapes=[pltpu.VMEM((2, TOP_K, tm * 8, 128), F32),
                            pltpu.VMEM((tm * 8, 128), F32),
                            pltpu.SemaphoreType.DMA((2,))]),
        out_shape=jax.ShapeDtypeStruct((n, d), F32),
        compiler_params=_params("arbitrary"),
        name="combine",
    )(dest_flat, w_flat, x1t, ga_t, g_t, yb)


def _rwkv_branch(p_lat, p_ctx, prm):
    r, v, a, _, _, k, bb, lw = _stage_prep(p_ctx, prm, tm=p_ctx.shape[1], grid_shift=False)
    (z_ctx,) = _stage_wkv(r, v, a, k, bb, lw, emit=False)
    r, v, a, g, bonus, k, bb, lw = _stage_prep(p_lat, prm, tm=256, grid_shift=True)
    y_f, y_b = _stage_wkv(r, v, a, k, bb, lw, z0=z_ctx, emit=True)
    return y_f, y_b, g, bonus


def _route(top_e, rank, counts, n_tok):
    bm = EXPERT_ROWS
    n_rows = n_tok * TOP_K + N_EXPERTS * bm
    padded = (counts + bm - 1) // bm * bm
    pad_end = jnp.cumsum(padded)
    pad_start = pad_end - padded
    experts = jnp.arange(N_EXPERTS, dtype=jnp.int32)
    start_of = jnp.sum(jnp.where(top_e[..., None] == experts, pad_start, 0), axis=-1)
    dest = (start_of + rank).astype(jnp.int32).reshape(-1)
    block_start = jnp.arange(n_rows // bm, dtype=jnp.int32) * bm
    block_e = jnp.minimum(jnp.sum(pad_end[None, :] <= block_start[:, None], axis=1),
                          N_EXPERTS - 1).astype(jnp.int32)
    n_used = (pad_end[-1] // bm).astype(jnp.int32).reshape(1)
    tail = n_used[0] + experts
    zstart = jnp.concatenate([jnp.maximum(pad_end - bm, 0),
                              jnp.minimum(tail, n_rows // bm - 1) * bm]).astype(jnp.int32)
    zon = jnp.concatenate([padded > 0, tail < n_rows // bm]).astype(jnp.int32)
    return dest, block_e, n_used, zstart, zon, n_rows


def kernel(x, c, ctx, c_ctx, w_ada, b_ada, norm1_g, w_in, shift_mu, decay_w0, decay_lora_b,
           iclr_a0, iclr_lora_b, gate_lora_b, k_k, k_a, r_k, gn_w, gn_b, w_out_rwkv,
           sgu_ln_w, sgu_ln_b, sgu_w_spatial, sgu_b_spatial, w_out_sgu, w_o, norm2_g,
           router_w, router_b, exp_w_gate, exp_b_gate, exp_w_up, exp_b_up, exp_w_down,
           exp_b_down, final_norm_g):
    assert w_ada.shape[0] == 1, "single-layer problem"
    b, t, d = x.shape
    n_tok = b * t
    w = RWKV_WIDTH
    row = lambda a: a.reshape(1, -1)

    cs = jnp.zeros((8, d), F32).at[:b].set(c).at[b].set(c_ctx)
    mod = _stage_mods(cs, w_ada[0], row(b_ada[0]))
    sh1, sc1, ga1, sh2, sc2, ga2 = [m[:b, None, :] for m in jnp.split(mod, 6, axis=-1)]
    csh1, csc1 = [jnp.broadcast_to(m[b][None, None, :], (b, 1, d))
                  for m in jnp.split(mod, 6, axis=-1)[:2]]

    w_in_bf = w_in[0].astype(BF16)
    n1 = row(norm1_g[0])
    p_lat = _stage_inproj(x, sh1, sc1, n1, w_in_bf[:, :RWKV_COLS], tm=512)
    p_ctx = _stage_inproj(ctx, csh1, csc1, n1, w_in_bf[:, :RWKV_COLS], tm=ctx.shape[1])

    head_id = jnp.arange(w, dtype=jnp.int32) // HEAD_DIM
    headsum = (head_id[:, None] == head_id[None, :]).astype(BF16)
    prm = [row(shift_mu[0]), row(k_k[0]), row(k_a[0]), row(r_k[0]), decay_w0[0],
           decay_lora_b[0], iclr_a0[0], iclr_lora_b[0], gate_lora_b[0], headsum]
    y_f, y_b, g, bonus = _rwkv_branch(p_lat, p_ctx, prm)

    bsp = jnp.repeat(sgu_b_spatial[0].T, SGU_WIDTH // SGU_GROUPS, axis=1)
    consts = [n1, w_in_bf[:, RWKV_COLS:], row(gn_w[0]), row(gn_b[0]), headsum,
              w_out_rwkv[0].astype(BF16), row(sgu_ln_w[0]), row(sgu_ln_b[0]),
              sgu_w_spatial[0].astype(BF16), bsp, w_out_sgu[0].astype(BF16),
              w_o[0].astype(BF16)]
    x1 = _stage_merge(x, y_f, y_b, g, bonus, sh1, sc1, ga1, consts, tm=256)

    x1f = x1.reshape(n_tok, d)
    h2t, x1t, top_e, top_w, rank, counts = _stage_router(
        x1f, sh2, sc2, row(norm2_g[0]), router_w[0].T, router_b[0].reshape(-1, 1), t, tm=512)
    dest, block_e, n_used, zstart, zon, n_rows = _route(top_e, rank, counts[:, 0], n_tok)
    xs = _stage_dispatch(dest, zstart, zon, h2t, n_rows, tm=512)
    e3 = lambda a: a.reshape(N_EXPERTS, 1, -1)
    yb = _stage_experts(block_e, n_used, xs, exp_w_gate[0], e3(exp_b_gate[0]), exp_w_up[0],
                        e3(exp_b_up[0]), exp_w_down[0], e3(exp_b_down[0]))
    out = _stage_combine(dest, top_w.reshape(-1), x1t, ga2.reshape(b, 8, 128),
                         row(final_norm_g), yb, t, tm=256)
    return out.reshape(b, t, d)
```

```python
import functools
import math

import jax
import jax.numpy as jnp
from jax import lax
from jax.experimental import pallas as pl
from jax.experimental.pallas import tpu as pltpu

F32 = jnp.float32
BF16 = jnp.bfloat16
HIGHEST = lax.Precision.HIGHEST

D_MODEL = 1024
GRID_W = 64
RWKV_HEADS = 8
HEAD_DIM = 64
RWKV_WIDTH = RWKV_HEADS * HEAD_DIM
DECAY_LORA = 64
ICLR_LORA = 64
GATE_LORA = 128
RWKV_COLS = 3 * RWKV_WIDTH + 2 * DECAY_LORA + 2 * ICLR_LORA + GATE_LORA
SGU_WIDTH = 512
SGU_GROUPS = 8
SGU_CHUNK = 128
N_EXPERTS = 32
TOP_K = 4
SWIGLU_LIMIT = 7.0
SWIGLU_ALPHA = 1.702
RMS_EPS = 1e-6
LN_EPS = 1e-5
GN_EPS = 64e-5

WKV_CHUNK = 64
WKV_GROUP = 4
WKV_CHUNKS_PER_STEP = 4
EXPERT_ROWS = 512
VMEM_LIMIT = 48 * 1024 * 1024


def _params(*sem):
    return pltpu.CompilerParams(dimension_semantics=sem, vmem_limit_bytes=VMEM_LIMIT)


def _mm(a, b, dims=((1,), (0,)), exact=False):
    dn = (dims, ((), ()))
    if exact:
        return lax.dot_general(a, b, dn, precision=HIGHEST, preferred_element_type=F32)
    return lax.dot_general(a.astype(BF16), b.astype(BF16), dn, preferred_element_type=F32)


def _split3(x):
    hi = x.astype(BF16)
    r1 = x - hi.astype(F32)
    mid = r1.astype(BF16)
    lo = (r1 - mid.astype(F32)).astype(BF16)
    return hi, mid, lo


def _mm_sel_x(sel, x):
    return sum(_mm(sel, p) for p in _split3(x))


def _full(shape):
    n = len(shape)
    return pl.BlockSpec(shape, lambda *_: (0,) * n)


def _norm_mod(x, g, shift, scale):
    y = x * lax.rsqrt(jnp.mean(x * x, axis=-1, keepdims=True) + RMS_EPS) * g
    return y * (1.0 + scale) + shift


def _mods_kernel(c_ref, w_ref, b_ref, o_ref):
    c = c_ref[...]
    s = c * jax.nn.sigmoid(c)
    o_ref[...] = _mm(s, w_ref[...], exact=True) + b_ref[...]


def _stage_mods(cs, w_ada, b_ada):
    rows, d = cs.shape
    n = w_ada.shape[1]
    tn = 1536
    return pl.pallas_call(
        _mods_kernel,
        grid=(n // tn,),
        in_specs=[_full((rows, d)),
                  pl.BlockSpec((d, tn), lambda j: (0, j)),
                  pl.BlockSpec((1, tn), lambda j: (0, j))],
        out_specs=pl.BlockSpec((rows, tn), lambda j: (0, j)),
        out_shape=jax.ShapeDtypeStruct((rows, n), F32),
        compiler_params=_params("arbitrary"),
        name="mods",
    )(cs, w_ada, b_ada)


def _inproj_kernel(x_ref, sh_ref, sc_ref, g_ref, w_ref, o_ref):
    h = _norm_mod(x_ref[0], g_ref[...], sh_ref[0], sc_ref[0])
    o_ref[0] = _mm(h, w_ref[...])


def _stage_inproj(x, shift, scale, g, w_bf16, tm):
    b, t, d = x.shape
    n = w_bf16.shape[1]
    return pl.pallas_call(
        _inproj_kernel,
        grid=(b, t // tm),
        in_specs=[pl.BlockSpec((1, tm, d), lambda i, j: (i, j, 0)),
                  pl.BlockSpec((1, 1, d), lambda i, j: (i, 0, 0)),
                  pl.BlockSpec((1, 1, d), lambda i, j: (i, 0, 0)),
                  _full((1, d)),
                  _full((d, n))],
        out_specs=pl.BlockSpec((1, tm, n), lambda i, j: (i, j, 0)),
        out_shape=jax.ShapeDtypeStruct((b, t, n), F32),
        compiler_params=_params("arbitrary", "arbitrary"),
        name="inproj",
    )(x, shift, scale, g, w_bf16)


def _rwkv_feats(p, mu_kk, mu_ka, r_k, w0, dlb, a0, ilb, glb, headsum):
    w = RWKV_WIDTH
    r = p[:, 0:w]
    k = p[:, w:2 * w]
    v = p[:, 2 * w:3 * w]
    o = 3 * w
    wd = (p[:, o:o + DECAY_LORA], p[:, o + DECAY_LORA:o + 2 * DECAY_LORA])
    o += 2 * DECAY_LORA
    ad = (p[:, o:o + ICLR_LORA], p[:, o + ICLR_LORA:o + 2 * ICLR_LORA])
    o += 2 * ICLR_LORA
    gd = p[:, o:o + GATE_LORA]

    kk = k * mu_kk
    kk = kk * lax.rsqrt(_mm(kk * kk, headsum) + 1e-12)
    g = _mm(jax.nn.sigmoid(gd), glb)
    ks, bs, lws = [], [], []
    ksum = None
    for d in range(2):
        z = w0[d:d + 1] + _mm(jnp.tanh(wd[d]), dlb[d])
        lws.append(-math.exp(-0.5) * jax.nn.sigmoid(z))
        ic = jax.nn.sigmoid(a0[d:d + 1] + _mm(ad[d], ilb[d]))
        kd = k * (1.0 + (ic - 1.0) * mu_ka)
        ks.append(kd)
        bs.append(kk * ic)
        ksum = kd if ksum is None else ksum + kd
    bonus = _mm(r * ksum * r_k, headsum) * v
    return r, v, -kk, g, bonus, ks, bs, lws


def _prep_kernel(pm_ref, pp_ref, pn_ref, mu_ref, kk_ref, ka_ref, rk_ref, w0_ref, dlb_ref,
                 a0_ref, ilb_ref, glb_ref, hs_ref,
                 r_ref, v_ref, a_ref, g_ref, bon_ref, k_ref, b_ref, lw_ref, *, tm, grid_shift):
    main = pm_ref[0]
    lane = lax.broadcasted_iota(jnp.int32, (1, RWKV_COLS), 1)
    if grid_shift:
        ext = jnp.concatenate([pp_ref[0], main, pn_ref[0]], axis=0)
        t = pl.program_id(1) * tm + lax.broadcasted_iota(jnp.int32, (tm, 1), 0)
        col = t & (GRID_W - 1)
        row = t >> (GRID_W.bit_length() - 1)
        n_rows = pl.num_programs(1) * tm // GRID_W
        left = jnp.where(col > 0, ext[GRID_W - 1:GRID_W - 1 + tm], 0.0)
        right = jnp.where(col < GRID_W - 1, ext[GRID_W + 1:GRID_W + 1 + tm], 0.0)
        up = jnp.where(row > 0, ext[0:tm], 0.0)
        down = jnp.where(row < n_rows - 1, ext[2 * GRID_W:2 * GRID_W + tm], 0.0)
        cm = lane & 3
        shifted = jnp.where(cm == 0, left, jnp.where(cm == 1, right, jnp.where(cm == 2, up, down)))
    else:
        zero = jnp.zeros((1, RWKV_COLS), F32)
        prev = jnp.concatenate([zero, main[:tm - 1]], axis=0)
        nxt = jnp.concatenate([main[1:], zero], axis=0)
        shifted = jnp.where((lane & 1) == 0, prev, nxt)
    p = main + mu_ref[...] * (shifted - main)
    r, v, a, g, bonus, ks, bs, lws = _rwkv_feats(
        p, kk_ref[...], ka_ref[...], rk_ref[...], w0_ref[...], dlb_ref, a0_ref[...], ilb_ref,
        glb_ref[...], hs_ref[...])
    r_ref[0] = r
    v_ref[0] = v
    a_ref[0] = a
    g_ref[0] = g
    bon_ref[0] = bonus
    for d in range(2):
        k_ref[d, 0] = ks[d]
        b_ref[d, 0] = bs[d]
        lw_ref[d, 0] = lws[d]


def _stage_prep(p, prm, tm, grid_shift):
    b, t, c = p.shape
    w = RWKV_WIDTH
    hb = GRID_W if grid_shift else 8
    per = tm // hb
    last = t // hb - 1
    tok = pl.BlockSpec((1, tm, w), lambda i, j: (i, j, 0))
    tok2 = pl.BlockSpec((2, 1, tm, w), lambda i, j: (0, i, j, 0))
    in_specs = [
        pl.BlockSpec((1, tm, c), lambda i, j: (i, j, 0)),
        pl.BlockSpec((1, hb, c), lambda i, j: (i, jnp.maximum(j * per - 1, 0), 0)),
        pl.BlockSpec((1, hb, c), lambda i, j: (i, jnp.minimum((j + 1) * per, last), 0)),
    ] + [_full(a.shape) for a in prm]
    return pl.pallas_call(
        functools.partial(_prep_kernel, tm=tm, grid_shift=grid_shift),
        grid=(b, t // tm),
        in_specs=in_specs,
        out_specs=[tok] * 5 + [tok2] * 3,
        out_shape=[jax.ShapeDtypeStruct((b, t, w), F32)] * 5
        + [jax.ShapeDtypeStruct((2, b, t, w), F32)] * 3,
        compiler_params=_params("arbitrary", "arbitrary"),
        name="prep_lat" if grid_shift else "prep_ctx",
    )(p, p, p, *prm)


def _wkv_chunks(dirs, emit):
    c = WKV_CHUNK
    n = HEAD_DIM
    gw = WKV_GROUP * n
    n_groups = RWKV_WIDTH // gw
    ii = lax.broadcasted_iota(jnp.int32, (c, gw), 0)
    assert c == n and n & (n - 1) == 0
    jj = lax.broadcasted_iota(jnp.int32, (c, gw), 1) & (c - 1)
    eye = (ii == jj).astype(F32)
    eye_n = (lax.broadcasted_iota(jnp.int32, (n, n), 0)
             == lax.broadcasted_iota(jnp.int32, (n, n), 1)).astype(F32)
    head_shift = n.bit_length() - 1
    same_head = (lax.broadcasted_iota(jnp.int32, (gw, gw), 0) >> head_shift
                 == lax.broadcasted_iota(jnp.int32, (gw, gw), 1) >> head_shift)
    nt = ((1,), (1,))
    tn = ((0,), (0,))
    sl = [slice(h * n, (h + 1) * n) for h in range(RWKV_HEADS)]

    def bd(x):
        x = x.astype(BF16)
        return jnp.where(same_head, jnp.concatenate([x] * WKV_GROUP, axis=0), jnp.zeros((), BF16))

    prob = []
    scales = []
    for d, (r, v, a, k, bb, lw, z) in enumerate(dirs):
        incl = (jj <= ii) if d == 0 else (jj >= ii)
        strict = (jj < ii) if d == 0 else (jj > ii)
        lc = _mm_sel_x(incl[:, :c].astype(BF16), lw)
        lx = lc - lw
        ltot = lc[c - 1:c] if d == 0 else lc[0:1]
        inv = jnp.exp(-lc)
        tail = jnp.exp(ltot - lc)
        etot = jnp.exp(ltot)
        at = (a * jnp.exp(lx)).astype(BF16)
        rt = (r * jnp.exp(lc)).astype(BF16)
        kt = (k * inv).astype(BF16)
        bt = (bb * inv).astype(BF16)
        kh = (k * tail).astype(BF16)
        bh = (bb * tail).astype(BF16)
        vb = v.astype(BF16)
        zb = z.astype(BF16)
        scales.append(jnp.concatenate(
            [jnp.broadcast_to(jnp.sum(eye_n * etot[:, s], axis=1, keepdims=True), (n, n))
             for s in sl], axis=1))
        for gi in range(n_groups):
            gs = slice(gi * gw, (gi + 1) * gw)
            prob.append(dict(incl=incl, strict=strict, at=at[:, gs], rt=rt[:, gs], kt=kt[:, gs],
                             bt=bt[:, gs], kh=kh[:, gs], bh=bh[:, gs], v=vb[:, gs], z=zb[:, gs]))

    for p in prob:
        lhs = jnp.concatenate([p["at"], p["rt"]], axis=0) if emit else p["at"]
        p["lhs_z"] = lhs
        p["gk"] = _mm(lhs, bd(p["kt"]), nt)
        p["gb"] = _mm(lhs, bd(p["bt"]), nt)
    for p in prob:
        a_ab = jnp.where(p["strict"], p["gb"][:c], 0.0)
        p["tinv"] = eye + a_ab
        p["x"] = _mm(a_ab, bd(a_ab))
    for level in range(1, 6):
        for p in prob:
            if level < 5:
                both = _mm(jnp.concatenate([p["x"], p["tinv"]], axis=0), bd(p["x"]))
                p["x"] = both[:c]
                p["tinv"] = p["tinv"] + both[c:]
            else:
                p["tinv"] = p["tinv"] + _mm(p["tinv"], bd(p["x"]))
    for p in prob:
        lhs_v = jnp.where(p["strict"], p["gk"][:c], 0.0)
        if emit:
            lhs_v = jnp.concatenate([lhs_v, jnp.where(p["incl"], p["gk"][c:], 0.0)], axis=0)
        p["zv"] = _mm(p["lhs_z"], bd(p["z"])) + _mm(lhs_v, bd(p["v"]))
    for p in prob:
        p["u"] = _mm(p["tinv"], bd(p["zv"][:c])).astype(BF16)
    for p in prob:
        if emit:
            p["y"] = p["zv"][c:] + _mm(jnp.where(p["incl"], p["gb"][c:], 0.0), bd(p["u"]))
        p["z_new"] = [_mm(jnp.concatenate([p["kh"][:, s], p["bh"][:, s]], axis=0),
                          jnp.concatenate([p["v"][:, s], p["u"][:, s]], axis=0), tn)
                      for s in sl[:WKV_GROUP]]

    out = []
    for d, dd in enumerate(dirs):
        mine = prob[d * n_groups:(d + 1) * n_groups]
        z_cat = jnp.concatenate([m for p in mine for m in p["z_new"]], axis=1) + scales[d] * dd[6]
        y = jnp.concatenate([p["y"] for p in mine], axis=1) if emit else None
        out.append((y, z_cat))
    return out


def _wkv_kernel(*refs, emit, has_init):
    refs = list(refs)
    ins = [[refs.pop(0) for _ in range(6)] for _ in range(2)]
    z0_ref = refs.pop(0) if has_init else None
    outs = [refs.pop(0) for _ in range(2 if emit else 1)]
    z_scr = refs.pop(0)

    @pl.when(pl.program_id(1) == 0)
    def _():
        if has_init:
            z_scr[...] = z0_ref[:, 0]
        else:
            z_scr[...] = jnp.zeros_like(z_scr)

    c = WKV_CHUNK
    per = ins[0][0].shape[1] // c

    def step(j, carry):
        dirs, rows = [], []
        for d in range(2):
            sub = j if d == 0 else per - 1 - j
            rows.append(pl.ds(pl.multiple_of(sub * c, c), c))
            r_ref, v_ref, a_ref, k_ref, b_ref, lw_ref = ins[d]
            dirs.append((r_ref[0, rows[d]], v_ref[0, rows[d]], a_ref[0, rows[d]],
                         k_ref[0, 0, rows[d]], b_ref[0, 0, rows[d]], lw_ref[0, 0, rows[d]],
                         z_scr[d]))
        for d, (y, z_new) in enumerate(_wkv_chunks(dirs, emit)):
            z_scr[d] = z_new
            if emit:
                outs[d][0, rows[d]] = y
        return carry

    lax.fori_loop(0, per, step, 0)
    if not emit:
        outs[0][:, 0] = z_scr[...]


def _stage_wkv(r, v, a, k, bb, lw, z0=None, emit=True):
    b, t, w = r.shape
    c = WKV_CHUNK * min(WKV_CHUNKS_PER_STEP, t // WKV_CHUNK)
    nch = t // c
    n = HEAD_DIM
    pos = (lambda s: s, lambda s: nch - 1 - s)
    in_specs, args = [], []
    for d in range(2):
        tok = pl.BlockSpec((1, c, w), lambda i, s, d=d: (i, pos[d](s), 0))
        tok2 = pl.BlockSpec((1, 1, c, w), lambda i, s, d=d: (d, i, pos[d](s), 0))
        in_specs += [tok, tok, tok, tok2, tok2, tok2]
        args += [r, v, a, k, bb, lw]
    zspec = pl.BlockSpec((2, 1, n, w), lambda i, s: (0, i, 0, 0))
    if z0 is not None:
        in_specs.append(zspec)
        args.append(z0)
    if emit:
        out_specs = [pl.BlockSpec((1, c, w), lambda i, s, d=d: (i, pos[d](s), 0)) for d in range(2)]
        out_shape = [jax.ShapeDtypeStruct((b, t, w), F32)] * 2
    else:
        out_specs = [zspec]
        out_shape = [jax.ShapeDtypeStruct((2, b, n, w), F32)]
    return pl.pallas_call(
        functools.partial(_wkv_kernel, emit=emit, has_init=z0 is not None),
        grid=(b, nch),
        in_specs=in_specs,
        out_specs=out_specs,
        out_shape=out_shape,
        scratch_shapes=[pltpu.VMEM((2, n, w), F32)],
        compiler_params=_params("arbitrary", "arbitrary"),
        name="wkv_lat" if emit else "wkv_ctx",
    )(*args)


def _merge_kernel(x_ref, yf_ref, yb_ref, g_ref, bon_ref, sh_ref, sc_ref, ga_ref, n1_ref, w2_ref,
                  gnw_ref, gnb_ref, hs_ref, wor_ref, lnw_ref, lnb_ref, wsp_ref, bsp_ref,
                  wos_ref, wo_ref, o_ref, *, tm):
    x = x_ref[0]
    h = _norm_mod(x, n1_ref[...], sh_ref[0], sc_ref[0])
    p2 = _mm(h, w2_ref[...])

    ps = p2[:, :2 * SGU_WIDTH]
    ge = 0.5 * ps * (1.0 + lax.erf(ps * (1.0 / math.sqrt(2.0))))
    u = ge[:, :SGU_WIDTH]
    z = ge[:, SGU_WIDTH:]
    mu = jnp.mean(z, axis=-1, keepdims=True)
    zc = z - mu
    var = jnp.mean(zc * zc, axis=-1, keepdims=True)
    z = zc * lax.rsqrt(var + LN_EPS) * lnw_ref[...] + lnb_ref[...]
    gw = SGU_WIDTH // SGU_GROUPS
    rows = []
    for c in range(tm // SGU_CHUNK):
        zc = z[c * SGU_CHUNK:(c + 1) * SGU_CHUNK]
        cols = [_mm(wsp_ref[gi], zc[:, gi * gw:(gi + 1) * gw]) for gi in range(SGU_GROUPS)]
        rows.append(jnp.concatenate(cols, axis=1) + bsp_ref[...])
    s = jnp.concatenate(rows, axis=0)
    y_b = _mm(u * s, wos_ref[...])

    y = yf_ref[0] + yb_ref[0]
    hs = hs_ref[...]
    ym = _mm(y, hs) * (1.0 / HEAD_DIM)
    yc = y - ym
    yv = _mm(yc * yc, hs) * (1.0 / HEAD_DIM)
    yn = yc * lax.rsqrt(yv + GN_EPS) * gnw_ref[...] + gnb_ref[...]
    y_a = _mm((yn + bon_ref[0]) * g_ref[0], wor_ref[...])

    gates = jax.nn.sigmoid(p2[:, 2 * SGU_WIDTH:])
    mix = gates[:, :D_MODEL] * y_a + gates[:, D_MODEL:] * y_b
    o_ref[0] = x + ga_ref[0] * _mm(mix, wo_ref[...])


def _stage_merge(x, yf, yb, g, bonus, sh, sc, ga, consts, tm):
    b, t, d = x.shape
    w = RWKV_WIDTH
    mod = pl.BlockSpec((1, 1, d), lambda i, j: (i, 0, 0))
    tok = pl.BlockSpec((1, tm, w), lambda i, j: (i, j, 0))
    return pl.pallas_call(
        functools.partial(_merge_kernel, tm=tm),
        grid=(b, t // tm),
        in_specs=[pl.BlockSpec((1, tm, d), lambda i, j: (i, j, 0)), tok, tok, tok, tok,
                  mod, mod, mod] + [_full(a.shape) for a in consts],
        out_specs=pl.BlockSpec((1, tm, d), lambda i, j: (i, j, 0)),
        out_shape=jax.ShapeDtypeStruct((b, t, d), F32),
        compiler_params=_params("arbitrary", "arbitrary"),
        name="merge",
    )(x, yf, yb, g, bonus, sh, sc, ga, *consts)


def _to_token_tiles(ref, val, rows):
    for cc in range(val.shape[1] // 128):
        ref[pl.ds(cc, rows, stride=8), :] = val[:, cc * 128:(cc + 1) * 128]


def _from_token_tiles(ref, rows):
    return jnp.concatenate([ref[pl.ds(cc, rows, stride=8), :] for cc in range(8)], axis=1)


def _router_kernel(x_ref, sh_ref, sc_ref, n2_ref, rwt_ref, rb_ref,
                   h_ref, xt_ref, e_ref, w_ref, rank_ref, cnt_ref, carry, *, tm):
    @pl.when(pl.program_id(0) == 0)
    def _():
        carry[...] = jnp.zeros_like(carry)

    x = x_ref[...]
    h = _norm_mod(x, n2_ref[...], sh_ref[0], sc_ref[0])
    _to_token_tiles(h_ref, h, tm)
    _to_token_tiles(xt_ref, x, tm)
    logits = _mm(rwt_ref[...], h, dims=((1,), (1,)), exact=True) + rb_ref[...]
    eio = lax.broadcasted_iota(jnp.int32, (N_EXPERTS, tm), 0)
    vals, sels = [], []
    for _ in range(TOP_K):
        m = jnp.max(logits, axis=0, keepdims=True)
        idx = jnp.min(jnp.where(logits == m, eio, N_EXPERTS), axis=0, keepdims=True)
        sel = eio == idx
        logits = jnp.where(sel, -jnp.inf, logits)
        vals.append(m)
        sels.append(sel)
        e_ref[len(vals) - 1:len(vals), :] = idx
    ex = [jnp.exp(vk - vals[0]) for vk in vals]
    tot = ex[0] + ex[1] + ex[2] + ex[3]
    for kk in range(TOP_K):
        w_ref[kk:kk + 1, :] = ex[kk] / tot
    cnt = (sels[0] | sels[1] | sels[2] | sels[3]).astype(F32)
    ti = lax.broadcasted_iota(jnp.int32, (tm, tm), 0)
    tj = lax.broadcasted_iota(jnp.int32, (tm, tm), 1)
    before = _mm(cnt, (ti < tj).astype(F32))
    base = carry[...] + before
    for kk in range(TOP_K):
        rank_ref[kk:kk + 1, :] = jnp.sum(jnp.where(sels[kk], base, 0.0), axis=0,
                                         keepdims=True).astype(jnp.int32)
    new = carry[...] + jnp.sum(cnt, axis=1, keepdims=True)
    carry[...] = new
    cnt_ref[...] = jnp.broadcast_to(new, cnt_ref.shape).astype(jnp.int32)


def _stage_router(x2, sh, sc, n2g, rwt, rb, tokens_per_batch, tm):
    n, d = x2.shape
    per = tokens_per_batch // tm
    mod = pl.BlockSpec((1, 1, d), lambda i: (i // per, 0, 0))
    lane = pl.BlockSpec((TOP_K, tm), lambda i: (0, i))
    assert d == 8 * 128, "token-tile layout stores one (8, 128) tile per token"
    tiles = pl.BlockSpec((tm * 8, 128), lambda i: (i, 0))
    return pl.pallas_call(
        functools.partial(_router_kernel, tm=tm),
        grid=(n // tm,),
        in_specs=[pl.BlockSpec((tm, d), lambda i: (i, 0)), mod, mod,
                  _full(n2g.shape), _full(rwt.shape), _full(rb.shape)],
        out_specs=[tiles, tiles, lane, lane, lane, _full((N_EXPERTS, 128))],
        out_shape=[jax.ShapeDtypeStruct((n * 8, 128), F32),
                   jax.ShapeDtypeStruct((n * 8, 128), F32),
                   jax.ShapeDtypeStruct((TOP_K, n), jnp.int32),
                   jax.ShapeDtypeStruct((TOP_K, n), F32),
                   jax.ShapeDtypeStruct((TOP_K, n), jnp.int32),
                   jax.ShapeDtypeStruct((N_EXPERTS, 128), jnp.int32)],
        scratch_shapes=[pltpu.VMEM((N_EXPERTS, 1), F32)],
        compiler_params=_params("arbitrary"),
        name="router",
    )(x2, sh, sc, n2g, rwt, rb)


def _expert_kernel(be_ref, nu_ref, nx_ref, x_ref, bg_ref, bu_ref, bd_ref, wg_hbm, wu_hbm, wd_hbm,
                   o_ref, stage, wg_s, wu_s, wd_s, sem):
    i = pl.program_id(0)
    e = be_ref[i]
    used = i < nu_ref[0]
    prev = be_ref[jnp.maximum(i - 1, 0)]

    def fetch(expert):
        return [pltpu.make_async_copy(w.at[expert], stage.at[j], sem.at[j])
                for j, w in enumerate((wg_hbm, wu_hbm, wd_hbm))]

    @pl.when(i == 0)
    def _():
        for cp in fetch(e):
            cp.start()

    @pl.when(used & ((i == 0) | (e != prev)))
    def _():
        for cp, w_s, j in zip(fetch(e), (wg_s, wu_s, wd_s), range(3)):
            cp.wait()
            w_s[...] = stage[j].astype(BF16)
        nxt = nx_ref[e]

        @pl.when(nxt >= 0)
        def _():
            for cp in fetch(nxt):
                cp.start()

    @pl.when(used)
    def _():
        x = _from_token_tiles(x_ref, EXPERT_ROWS)
        gate = _mm(x, wg_s[...]) + bg_ref[0]
        up = _mm(x, wu_s[...]) + bu_ref[0]
        gate = jnp.minimum(gate, SWIGLU_LIMIT)
        up = jnp.clip(up, -SWIGLU_LIMIT, SWIGLU_LIMIT)
        act = gate * jax.nn.sigmoid(SWIGLU_ALPHA * gate) * (up + 1.0)
        _to_token_tiles(o_ref, _mm(act, wd_s[...]) + bd_ref[0], EXPERT_ROWS)

    @pl.when(jnp.logical_not(used))
    def _():
        o_ref[...] = jnp.zeros_like(o_ref)


def _stage_experts(block_e, n_used, next_e, xs, wg, bg, wu, bu, wd, bd):
    n_rows = xs.shape[0] // 8
    d, f = wg.shape[1:]
    assert d == f, "one staging buffer shape serves all three weight matrices"
    bm = EXPERT_ROWS
    bspec = lambda n_: pl.BlockSpec((1, 1, n_), lambda i, be, nu, nx: (be[i], 0, 0))
    hbm = pl.BlockSpec(memory_space=pl.ANY)
    return pl.pallas_call(
        _expert_kernel,
        grid_spec=pltpu.PrefetchScalarGridSpec(
            num_scalar_prefetch=3,
            grid=(n_rows // bm,),
            in_specs=[pl.BlockSpec((bm * 8, 128),
                                   lambda i, be, nu, nx: (jnp.clip(nu[0] - 1, 0, i), 0)),
                      bspec(f), bspec(f), bspec(d), hbm, hbm, hbm],
            out_specs=pl.BlockSpec((bm * 8, 128), lambda i, be, nu, nx: (i, 0)),
            scratch_shapes=[pltpu.VMEM((3, d, f), F32), pltpu.VMEM((d, f), BF16),
                            pltpu.VMEM((d, f), BF16), pltpu.VMEM((f, d), BF16),
                            pltpu.SemaphoreType.DMA((3,))]),
        out_shape=jax.ShapeDtypeStruct((n_rows * 8, 128), F32),
        compiler_params=_params("arbitrary"),
        name="experts",
    )(block_e, n_used, next_e, xs, bg, bu, bd, wg, wu, wd)


def _tile_copy(src_ref, src_row, dst_ref, dst_row, sem):
    src = src_ref.at[pl.ds(pl.multiple_of(src_row * 8, 8), 8)]
    dst = dst_ref.at[pl.ds(pl.multiple_of(dst_row * 8, 8), 8)]
    return pltpu.make_async_copy(src, dst, sem)


def _dispatch_kernel(dest_ref, zstart_ref, zon_ref, h_ref, xs_ref, zeros, sem, zsem, *, tm, n_tok):
    i = pl.program_id(0)
    bm8 = EXPERT_ROWS * 8

    def zero_fill(e):
        dst = xs_ref.at[pl.ds(pl.multiple_of(zstart_ref[e] * 8, 8), bm8)]
        return pltpu.make_async_copy(zeros, dst, zsem)

    @pl.when(i == 0)
    def _():
        zeros[...] = jnp.zeros_like(zeros)
        for e in range(zon_ref.shape[0]):
            @pl.when(zon_ref[e] > 0)
            def _():
                zero_fill(e).start()
        for e in range(zon_ref.shape[0]):
            @pl.when(zon_ref[e] > 0)
            def _():
                zero_fill(e).wait()

    def body(t, carry):
        for kk in range(TOP_K):
            _tile_copy(h_ref, t, xs_ref, dest_ref[kk * n_tok + i * tm + t], sem).start(
                priority=kk % 2)
        return carry

    lax.fori_loop(0, tm, body, 0)
    for _ in range(TOP_K):
        pltpu.make_async_copy(h_ref, xs_ref.at[pl.ds(0, tm * 8)], sem).wait()


def _stage_dispatch(dest_flat, zstart, zon, h2t, n_rows, tm):
    n = h2t.shape[0] // 8
    return pl.pallas_call(
        functools.partial(_dispatch_kernel, tm=tm, n_tok=n),
        grid_spec=pltpu.PrefetchScalarGridSpec(
            num_scalar_prefetch=3,
            grid=(n // tm,),
            in_specs=[pl.BlockSpec((tm * 8, 128), lambda i, *_: (i, 0))],
            out_specs=pl.BlockSpec(memory_space=pl.ANY),
            scratch_shapes=[pltpu.VMEM((EXPERT_ROWS * 8, 128), F32),
                            pltpu.SemaphoreType.DMA(()), pltpu.SemaphoreType.DMA(())]),
        out_shape=jax.ShapeDtypeStruct((n_rows * 8, 128), F32),
        compiler_params=_params("arbitrary"),
        name="dispatch",
    )(dest_flat, zstart, zon, h2t)


def _combine_kernel(dest_ref, w_ref, xt_ref, ga_ref, g_ref, yb_ref, o_ref, buf, res, sem,
                    *, tm, n_tok):
    i = pl.program_id(0)
    ga = ga_ref[0]

    def issue(tile, slot, t):
        for kk in range(TOP_K):
            _tile_copy(yb_ref, dest_ref[kk * n_tok + tile * tm + t], buf.at[slot, kk], t,
                       sem.at[slot]).start(priority=kk % 2)

    def combine(slot, t):
        rows = pl.ds(pl.multiple_of(t * 8, 8), 8)
        acc = buf[slot, 0, rows, :] * w_ref[i * tm + t]
        for kk in range(1, TOP_K):
            acc = acc + buf[slot, kk, rows, :] * w_ref[kk * n_tok + i * tm + t]
        res[rows, :] = xt_ref[rows, :] + ga * acc

    def loop(body):
        lax.fori_loop(0, tm, lambda t, carry: (body(t), carry)[1], 0, unroll=4)

    @pl.when(i == 0)
    def _():
        loop(lambda t: issue(0, 0, t))

    for slot in range(2):
        @pl.when(i % 2 == slot)
        def _():
            for kk in range(TOP_K):
                pltpu.make_async_copy(buf.at[slot, kk], buf.at[slot, kk], sem.at[slot]).wait()

            @pl.when(i + 1 < pl.num_programs(0))
            def _():
                loop(lambda t: (issue(i + 1, 1 - slot, t), combine(slot, t)))

            @pl.when(i + 1 >= pl.num_programs(0))
            def _():
                loop(lambda t: combine(slot, t))

    x = _from_token_tiles(res, tm)
    o_ref[...] = x * lax.rsqrt(jnp.mean(x * x, axis=-1, keepdims=True) + RMS_EPS) * g_ref[...]


def _stage_combine(dest_flat, w_flat, x1t, ga_t, g_t, yb, tokens_per_batch, tm):
    n = x1t.shape[0] // 8
    d = 8 * 128
    per = tokens_per_batch // tm
    return pl.pallas_call(
        functools.partial(_combine_kernel, tm=tm, n_tok=n),
        grid_spec=pltpu.PrefetchScalarGridSpec(
            num_scalar_prefetch=2,
            grid=(n // tm,),
            in_specs=[pl.BlockSpec((tm * 8, 128), lambda i, *_: (i, 0)),
                      pl.BlockSpec((1, 8, 128), lambda i, *_: (i // per, 0, 0)),
                      pl.BlockSpec((1, d), lambda i, *_: (0, 0)),
                      pl.BlockSpec(memory_space=pl.ANY)],
            out_specs=pl.BlockSpec((tm, d), lambda i, *_: (i, 0)),
            scratch_shapes=[pltpu.VMEM((2, TOP_K, tm * 8, 128), F32),
                            pltpu.VMEM((tm * 8, 128), F32),
                            pltpu.SemaphoreType.DMA((2,))]),
        out_shape=jax.ShapeDtypeStruct((n, d), F32),
        compiler_params=_params("arbitrary"),
        name="combine",
    )(dest_flat, w_flat, x1t, ga_t, g_t, yb)


def _rwkv_branch(p_lat, p_ctx, prm):
    r, v, a, _, _, k, bb, lw = _stage_prep(p_ctx, prm, tm=p_ctx.shape[1], grid_shift=False)
    (z_ctx,) = _stage_wkv(r, v, a, k, bb, lw, emit=False)
    r, v, a, g, bonus, k, bb, lw = _stage_prep(p_lat, prm, tm=256, grid_shift=True)
    y_f, y_b = _stage_wkv(r, v, a, k, bb, lw, z0=z_ctx, emit=True)
    return y_f, y_b, g, bonus


def _route(top_e, rank, counts, n_tok):
    bm = EXPERT_ROWS
    n_rows = n_tok * TOP_K + N_EXPERTS * bm
    padded = (counts + bm - 1) // bm * bm
    pad_end = jnp.cumsum(padded)
    pad_start = pad_end - padded
    experts = jnp.arange(N_EXPERTS, dtype=jnp.int32)
    start_of = jnp.sum(jnp.where(top_e[..., None] == experts, pad_start, 0), axis=-1)
    dest = (start_of + rank).astype(jnp.int32).reshape(-1)
    block_start = jnp.arange(n_rows // bm, dtype=jnp.int32) * bm
    block_e = jnp.minimum(jnp.sum(pad_end[None, :] <= block_start[:, None], axis=1),
                          N_EXPERTS - 1).astype(jnp.int32)
    n_used = (pad_end[-1] // bm).astype(jnp.int32).reshape(1)
    tail = n_used[0] + experts
    zstart = jnp.concatenate([jnp.maximum(pad_end - bm, 0),
                              jnp.minimum(tail, n_rows // bm - 1) * bm]).astype(jnp.int32)
    zon = jnp.concatenate([padded > 0, tail < n_rows // bm]).astype(jnp.int32)
    first_at = lax.cummin(jnp.where(padded > 0, experts, N_EXPERTS), axis=0, reverse=True)
    next_e = jnp.concatenate([first_at[1:], jnp.full((1,), N_EXPERTS, jnp.int32)])
    next_e = jnp.where(next_e < N_EXPERTS, next_e, -1).astype(jnp.int32)
    return dest, block_e, n_used, next_e, zstart, zon, n_rows


def kernel(x, c, ctx, c_ctx, w_ada, b_ada, norm1_g, w_in, shift_mu, decay_w0, decay_lora_b,
           iclr_a0, iclr_lora_b, gate_lora_b, k_k, k_a, r_k, gn_w, gn_b, w_out_rwkv,
           sgu_ln_w, sgu_ln_b, sgu_w_spatial, sgu_b_spatial, w_out_sgu, w_o, norm2_g,
           router_w, router_b, exp_w_gate, exp_b_gate, exp_w_up, exp_b_up, exp_w_down,
           exp_b_down, final_norm_g):
    assert w_ada.shape[0] == 1, "single-layer problem"
    b, t, d = x.shape
    n_tok = b * t
    w = RWKV_WIDTH
    row = lambda a: a.reshape(1, -1)

    cs = jnp.zeros((8, d), F32).at[:b].set(c).at[b].set(c_ctx)
    mod = _stage_mods(cs, w_ada[0], row(b_ada[0]))
    sh1, sc1, ga1, sh2, sc2, ga2 = [m[:b, None, :] for m in jnp.split(mod, 6, axis=-1)]
    csh1, csc1 = [jnp.broadcast_to(m[b][None, None, :], (b, 1, d))
                  for m in jnp.split(mod, 6, axis=-1)[:2]]

    w_in_bf = w_in[0].astype(BF16)
    n1 = row(norm1_g[0])
    p_lat = _stage_inproj(x, sh1, sc1, n1, w_in_bf[:, :RWKV_COLS], tm=512)
    p_ctx = _stage_inproj(ctx, csh1, csc1, n1, w_in_bf[:, :RWKV_COLS], tm=ctx.shape[1])

    head_id = jnp.arange(w, dtype=jnp.int32) // HEAD_DIM
    headsum = (head_id[:, None] == head_id[None, :]).astype(BF16)
    prm = [row(shift_mu[0]), row(k_k[0]), row(k_a[0]), row(r_k[0]), decay_w0[0],
           decay_lora_b[0], iclr_a0[0], iclr_lora_b[0], gate_lora_b[0], headsum]
    y_f, y_b, g, bonus = _rwkv_branch(p_lat, p_ctx, prm)

    bsp = jnp.repeat(sgu_b_spatial[0].T, SGU_WIDTH // SGU_GROUPS, axis=1)
    consts = [n1, w_in_bf[:, RWKV_COLS:], row(gn_w[0]), row(gn_b[0]), headsum,
              w_out_rwkv[0].astype(BF16), row(sgu_ln_w[0]), row(sgu_ln_b[0]),
              sgu_w_spatial[0].astype(BF16), bsp, w_out_sgu[0].astype(BF16),
              w_o[0].astype(BF16)]
    x1 = _stage_merge(x, y_f, y_b, g, bonus, sh1, sc1, ga1, consts, tm=256)

    x1f = x1.reshape(n_tok, d)
    h2t, x1t, top_e, top_w, rank, counts = _stage_router(
        x1f, sh2, sc2, row(norm2_g[0]), router_w[0].T, router_b[0].reshape(-1, 1), t, tm=512)
    dest, block_e, n_used, next_e, zstart, zon, n_rows = _route(top_e, rank, counts[:, 0], n_tok)
    xs = _stage_dispatch(dest, zstart, zon, h2t, n_rows, tm=512)
    e3 = lambda a: a.reshape(N_EXPERTS, 1, -1)
    yb = _stage_experts(block_e, n_used, next_e, xs, exp_w_gate[0], e3(exp_b_gate[0]), exp_w_up[0],
                        e3(exp_b_up[0]), exp_w_down[0], e3(exp_b_down[0]))
    out = _stage_combine(dest, top_w.reshape(-1), x1t, ga2.reshape(b, 8, 128),
                         row(final_norm_g), yb, t, tm=256)
    return out.reshape(b, t, d)
```

```python
import functools
import math

import jax
import jax.numpy as jnp
from jax import lax
from jax.experimental import pallas as pl
from jax.experimental.pallas import tpu as pltpu

F32 = jnp.float32
BF16 = jnp.bfloat16
HIGHEST = lax.Precision.HIGHEST

D_MODEL = 1024
GRID_W = 64
RWKV_HEADS = 8
HEAD_DIM = 64
RWKV_WIDTH = RWKV_HEADS * HEAD_DIM
DECAY_LORA = 64
ICLR_LORA = 64
GATE_LORA = 128
RWKV_COLS = 3 * RWKV_WIDTH + 2 * DECAY_LORA + 2 * ICLR_LORA + GATE_LORA
SGU_WIDTH = 512
SGU_GROUPS = 8
SGU_CHUNK = 128
N_EXPERTS = 32
TOP_K = 4
SWIGLU_LIMIT = 7.0
SWIGLU_ALPHA = 1.702
RMS_EPS = 1e-6
LN_EPS = 1e-5
GN_EPS = 64e-5

WKV_CHUNK = 64
WKV_GROUP = 4
EXPERT_ROWS = 512
VMEM_LIMIT = 48 * 1024 * 1024


def _params(*sem):
    return pltpu.CompilerParams(dimension_semantics=sem, vmem_limit_bytes=VMEM_LIMIT)


def _mm(a, b, dims=((1,), (0,)), exact=False):
    dn = (dims, ((), ()))
    if exact:
        return lax.dot_general(a, b, dn, precision=HIGHEST, preferred_element_type=F32)
    return lax.dot_general(a.astype(BF16), b.astype(BF16), dn, preferred_element_type=F32)


def _split3(x):
    hi = x.astype(BF16)
    r1 = x - hi.astype(F32)
    mid = r1.astype(BF16)
    lo = (r1 - mid.astype(F32)).astype(BF16)
    return hi, mid, lo


def _mm_sel_x(sel, x):
    return sum(_mm(sel, p) for p in _split3(x))


def _full(shape):
    n = len(shape)
    return pl.BlockSpec(shape, lambda *_: (0,) * n)


def _norm_mod(x, g, shift, scale):
    y = x * lax.rsqrt(jnp.mean(x * x, axis=-1, keepdims=True) + RMS_EPS) * g
    return y * (1.0 + scale) + shift


def _mods_kernel(c_ref, w_ref, b_ref, o_ref):
    c = c_ref[...]
    s = c * jax.nn.sigmoid(c)
    o_ref[...] = _mm(s, w_ref[...], exact=True) + b_ref[...]


def _stage_mods(cs, w_ada, b_ada):
    rows, d = cs.shape
    n = w_ada.shape[1]
    tn = 1536
    return pl.pallas_call(
        _mods_kernel,
        grid=(n // tn,),
        in_specs=[_full((rows, d)),
                  pl.BlockSpec((d, tn), lambda j: (0, j)),
                  pl.BlockSpec((1, tn), lambda j: (0, j))],
        out_specs=pl.BlockSpec((rows, tn), lambda j: (0, j)),
        out_shape=jax.ShapeDtypeStruct((rows, n), F32),
        compiler_params=_params("arbitrary"),
        name="mods",
    )(cs, w_ada, b_ada)


def _inproj_kernel(x_ref, sh_ref, sc_ref, g_ref, w_ref, o_ref):
    h = _norm_mod(x_ref[0], g_ref[...], sh_ref[0], sc_ref[0])
    o_ref[0] = _mm(h, w_ref[...])


def _stage_inproj(x, shift, scale, g, w_bf16, tm):
    b, t, d = x.shape
    n = w_bf16.shape[1]
    return pl.pallas_call(
        _inproj_kernel,
        grid=(b, t // tm),
        in_specs=[pl.BlockSpec((1, tm, d), lambda i, j: (i, j, 0)),
                  pl.BlockSpec((1, 1, d), lambda i, j: (i, 0, 0)),
                  pl.BlockSpec((1, 1, d), lambda i, j: (i, 0, 0)),
                  _full((1, d)),
                  _full((d, n))],
        out_specs=pl.BlockSpec((1, tm, n), lambda i, j: (i, j, 0)),
        out_shape=jax.ShapeDtypeStruct((b, t, n), F32),
        compiler_params=_params("arbitrary", "arbitrary"),
        name="inproj",
    )(x, shift, scale, g, w_bf16)


def _rwkv_feats(p, mu_kk, mu_ka, r_k, w0, dlb, a0, ilb, glb, headsum):
    w = RWKV_WIDTH
    r = p[:, 0:w]
    k = p[:, w:2 * w]
    v = p[:, 2 * w:3 * w]
    o = 3 * w
    wd = (p[:, o:o + DECAY_LORA], p[:, o + DECAY_LORA:o + 2 * DECAY_LORA])
    o += 2 * DECAY_LORA
    ad = (p[:, o:o + ICLR_LORA], p[:, o + ICLR_LORA:o + 2 * ICLR_LORA])
    o += 2 * ICLR_LORA
    gd = p[:, o:o + GATE_LORA]

    kk = k * mu_kk
    kk = kk * lax.rsqrt(_mm(kk * kk, headsum) + 1e-12)
    g = _mm(jax.nn.sigmoid(gd), glb)
    ks, bs, lws = [], [], []
    ksum = None
    for d in range(2):
        z = w0[d:d + 1] + _mm(jnp.tanh(wd[d]), dlb[d])
        lws.append(-math.exp(-0.5) * jax.nn.sigmoid(z))
        ic = jax.nn.sigmoid(a0[d:d + 1] + _mm(ad[d], ilb[d]))
        kd = k * (1.0 + (ic - 1.0) * mu_ka)
        ks.append(kd)
        bs.append(kk * ic)
        ksum = kd if ksum is None else ksum + kd
    bonus = _mm(r * ksum * r_k, headsum) * v
    return r, v, -kk, g, bonus, ks, bs, lws


def _prep_kernel(pm_ref, pp_ref, pn_ref, mu_ref, kk_ref, ka_ref, rk_ref, w0_ref, dlb_ref,
                 a0_ref, ilb_ref, glb_ref, hs_ref,
                 r_ref, v_ref, a_ref, g_ref, bon_ref, k_ref, b_ref, lw_ref, *, tm, grid_shift):
    main = pm_ref[0]
    lane = lax.broadcasted_iota(jnp.int32, (1, RWKV_COLS), 1)
    if grid_shift:
        ext = jnp.concatenate([pp_ref[0], main, pn_ref[0]], axis=0)
        t = pl.program_id(1) * tm + lax.broadcasted_iota(jnp.int32, (tm, 1), 0)
        col = t & (GRID_W - 1)
        row = t >> (GRID_W.bit_length() - 1)
        n_rows = pl.num_programs(1) * tm // GRID_W
        left = jnp.where(col > 0, ext[GRID_W - 1:GRID_W - 1 + tm], 0.0)
        right = jnp.where(col < GRID_W - 1, ext[GRID_W + 1:GRID_W + 1 + tm], 0.0)
        up = jnp.where(row > 0, ext[0:tm], 0.0)
        down = jnp.where(row < n_rows - 1, ext[2 * GRID_W:2 * GRID_W + tm], 0.0)
        cm = lane & 3
        shifted = jnp.where(cm == 0, left, jnp.where(cm == 1, right, jnp.where(cm == 2, up, down)))
    else:
        zero = jnp.zeros((1, RWKV_COLS), F32)
        prev = jnp.concatenate([zero, main[:tm - 1]], axis=0)
        nxt = jnp.concatenate([main[1:], zero], axis=0)
        shifted = jnp.where((lane & 1) == 0, prev, nxt)
    p = main + mu_ref[...] * (shifted - main)
    r, v, a, g, bonus, ks, bs, lws = _rwkv_feats(
        p, kk_ref[...], ka_ref[...], rk_ref[...], w0_ref[...], dlb_ref, a0_ref[...], ilb_ref,
        glb_ref[...], hs_ref[...])
    r_ref[0] = r
    v_ref[0] = v
    a_ref[0] = a
    g_ref[0] = g
    bon_ref[0] = bonus
    for d in range(2):
        k_ref[d, 0] = ks[d]
        b_ref[d, 0] = bs[d]
        lw_ref[d, 0] = lws[d]


def _stage_prep(p, prm, tm, grid_shift):
    b, t, c = p.shape
    w = RWKV_WIDTH
    hb = GRID_W if grid_shift else 8
    per = tm // hb
    last = t // hb - 1
    tok = pl.BlockSpec((1, tm, w), lambda i, j: (i, j, 0))
    tok2 = pl.BlockSpec((2, 1, tm, w), lambda i, j: (0, i, j, 0))
    in_specs = [
        pl.BlockSpec((1, tm, c), lambda i, j: (i, j, 0)),
        pl.BlockSpec((1, hb, c), lambda i, j: (i, jnp.maximum(j * per - 1, 0), 0)),
        pl.BlockSpec((1, hb, c), lambda i, j: (i, jnp.minimum((j + 1) * per, last), 0)),
    ] + [_full(a.shape) for a in prm]
    return pl.pallas_call(
        functools.partial(_prep_kernel, tm=tm, grid_shift=grid_shift),
        grid=(b, t // tm),
        in_specs=in_specs,
        out_specs=[tok] * 5 + [tok2] * 3,
        out_shape=[jax.ShapeDtypeStruct((b, t, w), F32)] * 5
        + [jax.ShapeDtypeStruct((2, b, t, w), F32)] * 3,
        compiler_params=_params("arbitrary", "arbitrary"),
        name="prep_lat" if grid_shift else "prep_ctx",
    )(p, p, p, *prm)


def _wkv_prepare(d, r, v, a, k, bb, lw):
    c = WKV_CHUNK
    n = HEAD_DIM
    ii = lax.broadcasted_iota(jnp.int32, (c, c), 0)
    jj = lax.broadcasted_iota(jnp.int32, (c, c), 1)
    incl = (jj <= ii) if d == 0 else (jj >= ii)
    lc = _mm_sel_x(incl.astype(BF16), lw)
    lx = lc - lw
    ltot = lc[c - 1:c] if d == 0 else lc[0:1]
    inv = jnp.exp(-lc)
    tail = jnp.exp(ltot - lc)
    etot = jnp.exp(ltot)
    eye_n = (lax.broadcasted_iota(jnp.int32, (n, n), 0)
             == lax.broadcasted_iota(jnp.int32, (n, n), 1)).astype(F32)
    scale = jnp.concatenate(
        [jnp.broadcast_to(jnp.sum(eye_n * etot[:, h * n:(h + 1) * n], axis=1, keepdims=True), (n, n))
         for h in range(RWKV_HEADS)], axis=1)
    bf = lambda x: x.astype(BF16)
    return (bf(a * jnp.exp(lx)), bf(r * jnp.exp(lc)), bf(k * inv), bf(bb * inv), bf(k * tail),
            bf(bb * tail), bf(v), scale)


def _wkv_consts():
    c = WKV_CHUNK
    n = HEAD_DIM
    gw = WKV_GROUP * n
    assert c == n and n & (n - 1) == 0
    ii = lax.broadcasted_iota(jnp.int32, (c, gw), 0)
    jj = lax.broadcasted_iota(jnp.int32, (c, gw), 1) & (c - 1)
    head_shift = n.bit_length() - 1
    same_head = (lax.broadcasted_iota(jnp.int32, (gw, gw), 0) >> head_shift
                 == lax.broadcasted_iota(jnp.int32, (gw, gw), 1) >> head_shift)

    def bd(x):
        x = x.astype(BF16)
        return jnp.where(same_head, jnp.concatenate([x] * WKV_GROUP, axis=0), jnp.zeros((), BF16))

    return ii, jj, bd


def _wkv_factor(chunks, emit):
    c = WKV_CHUNK
    gw = WKV_GROUP * HEAD_DIM
    n_groups = RWKV_WIDTH // gw
    ii, jj, bd = _wkv_consts()
    eye = (ii == jj).astype(F32)
    nt = ((1,), (1,))
    tn = ((0,), (0,))
    sl = [slice(h * HEAD_DIM, (h + 1) * HEAD_DIM) for h in range(WKV_GROUP)]

    prob = []
    for preps in chunks:
        for d, (at, rt, kt, bt, kh, bh, vb, _) in enumerate(preps):
            incl = (jj <= ii) if d == 0 else (jj >= ii)
            strict = (jj < ii) if d == 0 else (jj > ii)
            for gi in range(n_groups):
                gs = slice(gi * gw, (gi + 1) * gw)
                prob.append(dict(incl=incl, strict=strict, at=at[:, gs], rt=rt[:, gs],
                                 kt=kt[:, gs], bt=bt[:, gs], kh=kh[:, gs], bh=bh[:, gs],
                                 v=vb[:, gs]))
    for p in prob:
        lhs = jnp.concatenate([p["at"], p["rt"]], axis=0) if emit else p["at"]
        p["lhs_z"] = lhs
        gk = _mm(lhs, bd(p["kt"]), nt)
        gb = _mm(lhs, bd(p["bt"]), nt)
        p["a_ab"] = jnp.where(p["strict"], gb[:c], 0.0)
        lhs_v = jnp.where(p["strict"], gk[:c], 0.0)
        if emit:
            lhs_v = jnp.concatenate([lhs_v, jnp.where(p["incl"], gk[c:], 0.0)], axis=0)
            p["a_rb"] = jnp.where(p["incl"], gb[c:], 0.0).astype(BF16)
        p["lhs_v"] = lhs_v
    for p in prob:
        p["tinv"] = eye + p["a_ab"]
        p["x"] = _mm(p["a_ab"], bd(p["a_ab"]))
        p["av"] = _mm(p["lhs_v"], bd(p["v"]))
    for level in range(1, 6):
        for p in prob:
            if level < 5:
                both = _mm(jnp.concatenate([p["x"], p["tinv"]], axis=0), bd(p["x"]))
                p["x"] = both[:c]
                p["tinv"] = p["tinv"] + both[c:]
            else:
                p["tinv"] = (p["tinv"] + _mm(p["tinv"], bd(p["x"]))).astype(BF16)
    per_chunk = 2 * n_groups
    return [prob[j * per_chunk:(j + 1) * per_chunk] for j in range(len(chunks))]


def _wkv_apply(probs, chunks, states, emit):
    c = WKV_CHUNK
    n = HEAD_DIM
    gw = WKV_GROUP * n
    n_groups = RWKV_WIDTH // gw
    _, _, bd = _wkv_consts()
    tn = ((0,), (0,))
    sl = [slice(h * n, (h + 1) * n) for h in range(WKV_GROUP)]
    flat = []
    for prob, zs in zip(probs, states):
        zb = [z.astype(BF16) for z in zs]
        for i, p in enumerate(prob):
            d, gi = divmod(i, n_groups)
            p["z"] = zb[d][:, gi * gw:(gi + 1) * gw]
            flat.append(p)
    for p in flat:
        p["zv"] = _mm(p["lhs_z"], bd(p["z"])) + p["av"]
    for p in flat:
        p["u"] = _mm(p["tinv"], bd(p["zv"][:c])).astype(BF16)
    for p in flat:
        if emit:
            p["y"] = p["zv"][c:] + _mm(p["a_rb"], bd(p["u"]))
        p["z_new"] = [_mm(jnp.concatenate([p["kh"][:, s], p["bh"][:, s]], axis=0),
                          jnp.concatenate([p["v"][:, s], p["u"][:, s]], axis=0), tn) for s in sl]
    out = []
    for prob, preps, zs in zip(probs, chunks, states):
        res = []
        for d, (prep, z) in enumerate(zip(preps, zs)):
            mine = prob[d * n_groups:(d + 1) * n_groups]
            z_cat = jnp.concatenate([m for p in mine for m in p["z_new"]], axis=1) + prep[7] * z
            y = jnp.concatenate([p["y"] for p in mine], axis=1) if emit else None
            res.append((y, z_cat))
        out.append(res)
    return out


def _wkv_kernel(*refs, emit, has_init):
    refs = list(refs)
    ins = [[refs.pop(0) for _ in range(6)] for _ in range(2)]
    z0_ref = refs.pop(0) if has_init else None
    outs = [refs.pop(0) for _ in range(2 if emit else 1)]
    z_scr = refs.pop(0)
    n_seq = z_scr.shape[1]

    @pl.when(pl.program_id(0) == 0)
    def _():
        if has_init:
            z_scr[...] = z0_ref[...]
        else:
            z_scr[...] = jnp.zeros_like(z_scr)

    chunks = []
    for i in range(n_seq):
        preps = []
        for d in range(2):
            r_ref, v_ref, a_ref, k_ref, b_ref, lw_ref = ins[d]
            preps.append(_wkv_prepare(d, r_ref[i], v_ref[i], a_ref[i], k_ref[0, i], b_ref[0, i],
                                      lw_ref[0, i]))
        chunks.append(preps)
    probs = _wkv_factor(chunks, emit)
    states = [(z_scr[0, i], z_scr[1, i]) for i in range(n_seq)]
    for i, res in enumerate(_wkv_apply(probs, chunks, states, emit)):
        for d, (y, z_new) in enumerate(res):
            z_scr[d, i] = z_new
            if emit:
                outs[d][i] = y
    if not emit:
        outs[0][...] = z_scr[...]


def _stage_wkv(r, v, a, k, bb, lw, z0=None, emit=True):
    b, t, w = r.shape
    c = WKV_CHUNK
    nch = t // c
    n = HEAD_DIM
    pos = (lambda s: s, lambda s: nch - 1 - s)
    in_specs, args = [], []
    for d in range(2):
        tok = pl.BlockSpec((b, c, w), lambda s, d=d: (0, pos[d](s), 0))
        tok2 = pl.BlockSpec((1, b, c, w), lambda s, d=d: (d, 0, pos[d](s), 0))
        in_specs += [tok, tok, tok, tok2, tok2, tok2]
        args += [r, v, a, k, bb, lw]
    zspec = pl.BlockSpec((2, b, n, w), lambda s: (0, 0, 0, 0))
    if z0 is not None:
        in_specs.append(zspec)
        args.append(z0)
    if emit:
        out_specs = [pl.BlockSpec((b, c, w), lambda s, d=d: (0, pos[d](s), 0)) for d in range(2)]
        out_shape = [jax.ShapeDtypeStruct((b, t, w), F32)] * 2
    else:
        out_specs = [zspec]
        out_shape = [jax.ShapeDtypeStruct((2, b, n, w), F32)]
    return pl.pallas_call(
        functools.partial(_wkv_kernel, emit=emit, has_init=z0 is not None),
        grid=(nch,),
        in_specs=in_specs,
        out_specs=out_specs,
        out_shape=out_shape,
        scratch_shapes=[pltpu.VMEM((2, b, n, w), F32)],
        compiler_params=_params("arbitrary"),
        name="wkv_lat" if emit else "wkv_ctx",
    )(*args)


def _merge_kernel(x_ref, yf_ref, yb_ref, g_ref, bon_ref, sh_ref, sc_ref, ga_ref, n1_ref, w2_ref,
                  gnw_ref, gnb_ref, hs_ref, wor_ref, lnw_ref, lnb_ref, wsp_ref, bsp_ref,
                  wos_ref, wo_ref, o_ref, *, tm):
    x = x_ref[0]
    h = _norm_mod(x, n1_ref[...], sh_ref[0], sc_ref[0])
    p2 = _mm(h, w2_ref[...])

    ps = p2[:, :2 * SGU_WIDTH]
    ge = 0.5 * ps * (1.0 + lax.erf(ps * (1.0 / math.sqrt(2.0))))
    u = ge[:, :SGU_WIDTH]
    z = ge[:, SGU_WIDTH:]
    mu = jnp.mean(z, axis=-1, keepdims=True)
    zc = z - mu
    var = jnp.mean(zc * zc, axis=-1, keepdims=True)
    z = zc * lax.rsqrt(var + LN_EPS) * lnw_ref[...] + lnb_ref[...]
    gw = SGU_WIDTH // SGU_GROUPS
    rows = []
    for c in range(tm // SGU_CHUNK):
        zc = z[c * SGU_CHUNK:(c + 1) * SGU_CHUNK]
        cols = [_mm(wsp_ref[gi], zc[:, gi * gw:(gi + 1) * gw]) for gi in range(SGU_GROUPS)]
        rows.append(jnp.concatenate(cols, axis=1) + bsp_ref[...])
    s = jnp.concatenate(rows, axis=0)
    y_b = _mm(u * s, wos_ref[...])

    y = yf_ref[0] + yb_ref[0]
    hs = hs_ref[...]
    ym = _mm(y, hs) * (1.0 / HEAD_DIM)
    yc = y - ym
    yv = _mm(yc * yc, hs) * (1.0 / HEAD_DIM)
    yn = yc * lax.rsqrt(yv + GN_EPS) * gnw_ref[...] + gnb_ref[...]
    y_a = _mm((yn + bon_ref[0]) * g_ref[0], wor_ref[...])

    gates = jax.nn.sigmoid(p2[:, 2 * SGU_WIDTH:])
    mix = gates[:, :D_MODEL] * y_a + gates[:, D_MODEL:] * y_b
    o_ref[0] = x + ga_ref[0] * _mm(mix, wo_ref[...])


def _stage_merge(x, yf, yb, g, bonus, sh, sc, ga, consts, tm):
    b, t, d = x.shape
    w = RWKV_WIDTH
    mod = pl.BlockSpec((1, 1, d), lambda i, j: (i, 0, 0))
    tok = pl.BlockSpec((1, tm, w), lambda i, j: (i, j, 0))
    return pl.pallas_call(
        functools.partial(_merge_kernel, tm=tm),
        grid=(b, t // tm),
        in_specs=[pl.BlockSpec((1, tm, d), lambda i, j: (i, j, 0)), tok, tok, tok, tok,
                  mod, mod, mod] + [_full(a.shape) for a in consts],
        out_specs=pl.BlockSpec((1, tm, d), lambda i, j: (i, j, 0)),
        out_shape=jax.ShapeDtypeStruct((b, t, d), F32),
        compiler_params=_params("arbitrary", "arbitrary"),
        name="merge",
    )(x, yf, yb, g, bonus, sh, sc, ga, *consts)


def _to_token_tiles(ref, val, rows):
    for cc in range(val.shape[1] // 128):
        ref[pl.ds(cc, rows, stride=8), :] = val[:, cc * 128:(cc + 1) * 128]


def _from_token_tiles(ref, rows):
    return jnp.concatenate([ref[pl.ds(cc, rows, stride=8), :] for cc in range(8)], axis=1)


def _router_kernel(x_ref, sh_ref, sc_ref, n2_ref, rwt_ref, rb_ref,
                   h_ref, xt_ref, e_ref, w_ref, rank_ref, cnt_ref, carry, *, tm):
    @pl.when(pl.program_id(0) == 0)
    def _():
        carry[...] = jnp.zeros_like(carry)

    x = x_ref[...]
    h = _norm_mod(x, n2_ref[...], sh_ref[0], sc_ref[0])
    _to_token_tiles(h_ref, h, tm)
    _to_token_tiles(xt_ref, x, tm)
    logits = _mm(rwt_ref[...], h, dims=((1,), (1,)), exact=True) + rb_ref[...]
    eio = lax.broadcasted_iota(jnp.int32, (N_EXPERTS, tm), 0)
    vals, sels = [], []
    for _ in range(TOP_K):
        m = jnp.max(logits, axis=0, keepdims=True)
        idx = jnp.min(jnp.where(logits == m, eio, N_EXPERTS), axis=0, keepdims=True)
        sel = eio == idx
        logits = jnp.where(sel, -jnp.inf, logits)
        vals.append(m)
        sels.append(sel)
        e_ref[len(vals) - 1:len(vals), :] = idx
    ex = [jnp.exp(vk - vals[0]) for vk in vals]
    tot = ex[0] + ex[1] + ex[2] + ex[3]
    for kk in range(TOP_K):
        w_ref[kk:kk + 1, :] = ex[kk] / tot
    cnt = (sels[0] | sels[1] | sels[2] | sels[3]).astype(F32)
    ti = lax.broadcasted_iota(jnp.int32, (tm, tm), 0)
    tj = lax.broadcasted_iota(jnp.int32, (tm, tm), 1)
    before = _mm(cnt, (ti < tj).astype(F32))
    base = carry[...] + before
    for kk in range(TOP_K):
        rank_ref[kk:kk + 1, :] = jnp.sum(jnp.where(sels[kk], base, 0.0), axis=0,
                                         keepdims=True).astype(jnp.int32)
    new = carry[...] + jnp.sum(cnt, axis=1, keepdims=True)
    carry[...] = new
    cnt_ref[...] = jnp.broadcast_to(new, cnt_ref.shape).astype(jnp.int32)


def _stage_router(x2, sh, sc, n2g, rwt, rb, tokens_per_batch, tm):
    n, d = x2.shape
    per = tokens_per_batch // tm
    mod = pl.BlockSpec((1, 1, d), lambda i: (i // per, 0, 0))
    lane = pl.BlockSpec((TOP_K, tm), lambda i: (0, i))
    assert d == 8 * 128, "token-tile layout stores one (8, 128) tile per token"
    tiles = pl.BlockSpec((tm * 8, 128), lambda i: (i, 0))
    return pl.pallas_call(
        functools.partial(_router_kernel, tm=tm),
        grid=(n // tm,),
        in_specs=[pl.BlockSpec((tm, d), lambda i: (i, 0)), mod, mod,
                  _full(n2g.shape), _full(rwt.shape), _full(rb.shape)],
        out_specs=[tiles, tiles, lane, lane, lane, _full((N_EXPERTS, 128))],
        out_shape=[jax.ShapeDtypeStruct((n * 8, 128), F32),
                   jax.ShapeDtypeStruct((n * 8, 128), F32),
                   jax.ShapeDtypeStruct((TOP_K, n), jnp.int32),
                   jax.ShapeDtypeStruct((TOP_K, n), F32),
                   jax.ShapeDtypeStruct((TOP_K, n), jnp.int32),
                   jax.ShapeDtypeStruct((N_EXPERTS, 128), jnp.int32)],
        scratch_shapes=[pltpu.VMEM((N_EXPERTS, 1), F32)],
        compiler_params=_params("arbitrary"),
        name="router",
    )(x2, sh, sc, n2g, rwt, rb)


def _expert_kernel(be_ref, nu_ref, nx_ref, x_ref, bg_ref, bu_ref, bd_ref, wg_hbm, wu_hbm, wd_hbm,
                   o_ref, stage, wg_s, wu_s, wd_s, sem):
    i = pl.program_id(0)
    e = be_ref[i]
    used = i < nu_ref[0]
    prev = be_ref[jnp.maximum(i - 1, 0)]

    def fetch(expert):
        return [pltpu.make_async_copy(w.at[expert], stage.at[j], sem.at[j])
                for j, w in enumerate((wg_hbm, wu_hbm, wd_hbm))]

    @pl.when(i == 0)
    def _():
        for cp in fetch(e):
            cp.start()

    @pl.when(used & ((i == 0) | (e != prev)))
    def _():
        for cp, w_s, j in zip(fetch(e), (wg_s, wu_s, wd_s), range(3)):
            cp.wait()
            w_s[...] = stage[j].astype(BF16)
        nxt = nx_ref[e]

        @pl.when(nxt >= 0)
        def _():
            for cp in fetch(nxt):
                cp.start()

    @pl.when(used)
    def _():
        x = _from_token_tiles(x_ref, EXPERT_ROWS)
        gate = _mm(x, wg_s[...]) + bg_ref[0]
        up = _mm(x, wu_s[...]) + bu_ref[0]
        gate = jnp.minimum(gate, SWIGLU_LIMIT)
        up = jnp.clip(up, -SWIGLU_LIMIT, SWIGLU_LIMIT)
        act = gate * jax.nn.sigmoid(SWIGLU_ALPHA * gate) * (up + 1.0)
        _to_token_tiles(o_ref, _mm(act, wd_s[...]) + bd_ref[0], EXPERT_ROWS)

    @pl.when(jnp.logical_not(used))
    def _():
        o_ref[...] = jnp.zeros_like(o_ref)


def _stage_experts(block_e, n_used, next_e, xs, wg, bg, wu, bu, wd, bd):
    n_rows = xs.shape[0] // 8
    d, f = wg.shape[1:]
    assert d == f, "one staging buffer shape serves all three weight matrices"
    bm = EXPERT_ROWS
    bspec = lambda n_: pl.BlockSpec((1, 1, n_), lambda i, be, nu, nx: (be[i], 0, 0))
    hbm = pl.BlockSpec(memory_space=pl.ANY)
    return pl.pallas_call(
        _expert_kernel,
        grid_spec=pltpu.PrefetchScalarGridSpec(
            num_scalar_prefetch=3,
            grid=(n_rows // bm,),
            in_specs=[pl.BlockSpec((bm * 8, 128),
                                   lambda i, be, nu, nx: (jnp.clip(nu[0] - 1, 0, i), 0)),
                      bspec(f), bspec(f), bspec(d), hbm, hbm, hbm],
            out_specs=pl.BlockSpec((bm * 8, 128), lambda i, be, nu, nx: (i, 0)),
            scratch_shapes=[pltpu.VMEM((3, d, f), F32), pltpu.VMEM((d, f), BF16),
                            pltpu.VMEM((d, f), BF16), pltpu.VMEM((f, d), BF16),
                            pltpu.SemaphoreType.DMA((3,))]),
        out_shape=jax.ShapeDtypeStruct((n_rows * 8, 128), F32),
        compiler_params=_params("arbitrary"),
        name="experts",
    )(block_e, n_used, next_e, xs, bg, bu, bd, wg, wu, wd)


def _tile_copy(src_ref, src_row, dst_ref, dst_row, sem):
    src = src_ref.at[pl.ds(pl.multiple_of(src_row * 8, 8), 8)]
    dst = dst_ref.at[pl.ds(pl.multiple_of(dst_row * 8, 8), 8)]
    return pltpu.make_async_copy(src, dst, sem)


def _dispatch_kernel(dest_ref, zstart_ref, zon_ref, h_ref, xs_ref, zeros, sem, zsem, *, tm, n_tok):
    i = pl.program_id(0)
    bm8 = EXPERT_ROWS * 8

    def zero_fill(e):
        dst = xs_ref.at[pl.ds(pl.multiple_of(zstart_ref[e] * 8, 8), bm8)]
        return pltpu.make_async_copy(zeros, dst, zsem)

    @pl.when(i == 0)
    def _():
        zeros[...] = jnp.zeros_like(zeros)
        for e in range(zon_ref.shape[0]):
            @pl.when(zon_ref[e] > 0)
            def _():
                zero_fill(e).start()
        for e in range(zon_ref.shape[0]):
            @pl.when(zon_ref[e] > 0)
            def _():
                zero_fill(e).wait()

    def body(t, carry):
        for kk in range(TOP_K):
            _tile_copy(h_ref, t, xs_ref, dest_ref[kk * n_tok + i * tm + t], sem).start(
                priority=kk % 2)
        return carry

    lax.fori_loop(0, tm, body, 0)
    for _ in range(TOP_K):
        pltpu.make_async_copy(h_ref, xs_ref.at[pl.ds(0, tm * 8)], sem).wait()


def _stage_dispatch(dest_flat, zstart, zon, h2t, n_rows, tm):
    n = h2t.shape[0] // 8
    return pl.pallas_call(
        functools.partial(_dispatch_kernel, tm=tm, n_tok=n),
        grid_spec=pltpu.PrefetchScalarGridSpec(
            num_scalar_prefetch=3,
            grid=(n // tm,),
            in_specs=[pl.BlockSpec((tm * 8, 128), lambda i, *_: (i, 0))],
            out_specs=pl.BlockSpec(memory_space=pl.ANY),
            scratch_shapes=[pltpu.VMEM((EXPERT_ROWS * 8, 128), F32),
                            pltpu.SemaphoreType.DMA(()), pltpu.SemaphoreType.DMA(())]),
        out_shape=jax.ShapeDtypeStruct((n_rows * 8, 128), F32),
        compiler_params=_params("arbitrary"),
        name="dispatch",
    )(dest_flat, zstart, zon, h2t)


def _combine_kernel(dest_ref, w_ref, xt_ref, ga_ref, g_ref, yb_ref, o_ref, buf, res, sem,
                    *, tm, n_tok):
    i = pl.program_id(0)
    ga = ga_ref[0]

    def issue(tile, slot, t):
        for kk in range(TOP_K):
            _tile_copy(yb_ref, dest_ref[kk * n_tok + tile * tm + t], buf.at[slot, kk], t,
                       sem.at[slot]).start(priority=kk % 2)

    def combine(slot, t):
        rows = pl.ds(pl.multiple_of(t * 8, 8), 8)
        acc = buf[slot, 0, rows, :] * w_ref[i * tm + t]
        for kk in range(1, TOP_K):
            acc = acc + buf[slot, kk, rows, :] * w_ref[kk * n_tok + i * tm + t]
        res[rows, :] = xt_ref[rows, :] + ga * acc

    def loop(body):
        lax.fori_loop(0, tm, lambda t, carry: (body(t), carry)[1], 0, unroll=4)

    @pl.when(i == 0)
    def _():
        loop(lambda t: issue(0, 0, t))

    for slot in range(2):
        @pl.when(i % 2 == slot)
        def _():
            for kk in range(TOP_K):
                pltpu.make_async_copy(buf.at[slot, kk], buf.at[slot, kk], sem.at[slot]).wait()

            @pl.when(i + 1 < pl.num_programs(0))
            def _():
                loop(lambda t: (issue(i + 1, 1 - slot, t), combine(slot, t)))

            @pl.when(i + 1 >= pl.num_programs(0))
            def _():
                loop(lambda t: combine(slot, t))

    x = _from_token_tiles(res, tm)
    o_ref[...] = x * lax.rsqrt(jnp.mean(x * x, axis=-1, keepdims=True) + RMS_EPS) * g_ref[...]


def _stage_combine(dest_flat, w_flat, x1t, ga_t, g_t, yb, tokens_per_batch, tm):
    n = x1t.shape[0] // 8
    d = 8 * 128
    per = tokens_per_batch // tm
    return pl.pallas_call(
        functools.partial(_combine_kernel, tm=tm, n_tok=n),
        grid_spec=pltpu.PrefetchScalarGridSpec(
            num_scalar_prefetch=2,
            grid=(n // tm,),
            in_specs=[pl.BlockSpec((tm * 8, 128), lambda i, *_: (i, 0)),
                      pl.BlockSpec((1, 8, 128), lambda i, *_: (i // per, 0, 0)),
                      pl.BlockSpec((1, d), lambda i, *_: (0, 0)),
                      pl.BlockSpec(memory_space=pl.ANY)],
            out_specs=pl.BlockSpec((tm, d), lambda i, *_: (i, 0)),
            scratch_shapes=[pltpu.VMEM((2, TOP_K, tm * 8, 128), F32),
                            pltpu.VMEM((tm * 8, 128), F32),
                            pltpu.SemaphoreType.DMA((2,))]),
        out_shape=jax.ShapeDtypeStruct((n, d), F32),
        compiler_params=_params("arbitrary"),
        name="combine",
    )(dest_flat, w_flat, x1t, ga_t, g_t, yb)


def _rwkv_branch(p_lat, p_ctx, prm):
    r, v, a, _, _, k, bb, lw = _stage_prep(p_ctx, prm, tm=p_ctx.shape[1], grid_shift=False)
    (z_ctx,) = _stage_wkv(r, v, a, k, bb, lw, emit=False)
    r, v, a, g, bonus, k, bb, lw = _stage_prep(p_lat, prm, tm=256, grid_shift=True)
    y_f, y_b = _stage_wkv(r, v, a, k, bb, lw, z0=z_ctx, emit=True)
    return y_f, y_b, g, bonus


def _route(top_e, rank, counts, n_tok):
    bm = EXPERT_ROWS
    n_rows = n_tok * TOP_K + N_EXPERTS * bm
    padded = (counts + bm - 1) // bm * bm
    pad_end = jnp.cumsum(padded)
    pad_start = pad_end - padded
    experts = jnp.arange(N_EXPERTS, dtype=jnp.int32)
    start_of = jnp.sum(jnp.where(top_e[..., None] == experts, pad_start, 0), axis=-1)
    dest = (start_of + rank).astype(jnp.int32).reshape(-1)
    block_start = jnp.arange(n_rows // bm, dtype=jnp.int32) * bm
    block_e = jnp.minimum(jnp.sum(pad_end[None, :] <= block_start[:, None], axis=1),
                          N_EXPERTS - 1).astype(jnp.int32)
    n_used = (pad_end[-1] // bm).astype(jnp.int32).reshape(1)
    tail = n_used[0] + experts
    zstart = jnp.concatenate([jnp.maximum(pad_end - bm, 0),
                              jnp.minimum(tail, n_rows // bm - 1) * bm]).astype(jnp.int32)
    zon = jnp.concatenate([padded > 0, tail < n_rows // bm]).astype(jnp.int32)
    first_at = lax.cummin(jnp.where(padded > 0, experts, N_EXPERTS), axis=0, reverse=True)
    next_e = jnp.concatenate([first_at[1:], jnp.full((1,), N_EXPERTS, jnp.int32)])
    next_e = jnp.where(next_e < N_EXPERTS, next_e, -1).astype(jnp.int32)
    return dest, block_e, n_used, next_e, zstart, zon, n_rows


def kernel(x, c, ctx, c_ctx, w_ada, b_ada, norm1_g, w_in, shift_mu, decay_w0, decay_lora_b,
           iclr_a0, iclr_lora_b, gate_lora_b, k_k, k_a, r_k, gn_w, gn_b, w_out_rwkv,
           sgu_ln_w, sgu_ln_b, sgu_w_spatial, sgu_b_spatial, w_out_sgu, w_o, norm2_g,
           router_w, router_b, exp_w_gate, exp_b_gate, exp_w_up, exp_b_up, exp_w_down,
           exp_b_down, final_norm_g):
    assert w_ada.shape[0] == 1, "single-layer problem"
    b, t, d = x.shape
    n_tok = b * t
    w = RWKV_WIDTH
    row = lambda a: a.reshape(1, -1)

    cs = jnp.zeros((8, d), F32).at[:b].set(c).at[b].set(c_ctx)
    mod = _stage_mods(cs, w_ada[0], row(b_ada[0]))
    sh1, sc1, ga1, sh2, sc2, ga2 = [m[:b, None, :] for m in jnp.split(mod, 6, axis=-1)]
    csh1, csc1 = [jnp.broadcast_to(m[b][None, None, :], (b, 1, d))
                  for m in jnp.split(mod, 6, axis=-1)[:2]]

    w_in_bf = w_in[0].astype(BF16)
    n1 = row(norm1_g[0])
    p_lat = _stage_inproj(x, sh1, sc1, n1, w_in_bf[:, :RWKV_COLS], tm=512)
    p_ctx = _stage_inproj(ctx, csh1, csc1, n1, w_in_bf[:, :RWKV_COLS], tm=ctx.shape[1])

    head_id = jnp.arange(w, dtype=jnp.int32) // HEAD_DIM
    headsum = (head_id[:, None] == head_id[None, :]).astype(BF16)
    prm = [row(shift_mu[0]), row(k_k[0]), row(k_a[0]), row(r_k[0]), decay_w0[0],
           decay_lora_b[0], iclr_a0[0], iclr_lora_b[0], gate_lora_b[0], headsum]
    y_f, y_b, g, bonus = _rwkv_branch(p_lat, p_ctx, prm)

    bsp = jnp.repeat(sgu_b_spatial[0].T, SGU_WIDTH // SGU_GROUPS, axis=1)
    consts = [n1, w_in_bf[:, RWKV_COLS:], row(gn_w[0]), row(gn_b[0]), headsum,
              w_out_rwkv[0].astype(BF16), row(sgu_ln_w[0]), row(sgu_ln_b[0]),
              sgu_w_spatial[0].astype(BF16), bsp, w_out_sgu[0].astype(BF16),
              w_o[0].astype(BF16)]
    x1 = _stage_merge(x, y_f, y_b, g, bonus, sh1, sc1, ga1, consts, tm=256)

    x1f = x1.reshape(n_tok, d)
    h2t, x1t, top_e, top_w, rank, counts = _stage_router(
        x1f, sh2, sc2, row(norm2_g[0]), router_w[0].T, router_b[0].reshape(-1, 1), t, tm=512)
    dest, block_e, n_used, next_e, zstart, zon, n_rows = _route(top_e, rank, counts[:, 0], n_tok)
    xs = _stage_dispatch(dest, zstart, zon, h2t, n_rows, tm=512)
    e3 = lambda a: a.reshape(N_EXPERTS, 1, -1)
    yb = _stage_experts(block_e, n_used, next_e, xs, exp_w_gate[0], e3(exp_b_gate[0]), exp_w_up[0],
                        e3(exp_b_up[0]), exp_w_down[0], e3(exp_b_down[0]))
    out = _stage_combine(dest, top_w.reshape(-1), x1t, ga2.reshape(b, 8, 128),
                         row(final_norm_g), yb, t, tm=256)
    return out.reshape(b, t, d)
```

```python
import functools
import math

import jax
import jax.numpy as jnp
from jax import lax
from jax.experimental import pallas as pl
from jax.experimental.pallas import tpu as pltpu

F32 = jnp.float32
BF16 = jnp.bfloat16
HIGHEST = lax.Precision.HIGHEST

D_MODEL = 1024
GRID_W = 64
RWKV_HEADS = 8
HEAD_DIM = 64
RWKV_WIDTH = RWKV_HEADS * HEAD_DIM
DECAY_LORA = 64
ICLR_LORA = 64
GATE_LORA = 128
RWKV_COLS = 3 * RWKV_WIDTH + 2 * DECAY_LORA + 2 * ICLR_LORA + GATE_LORA
SGU_WIDTH = 512
SGU_GROUPS = 8
SGU_CHUNK = 128
N_EXPERTS = 32
TOP_K = 4
SWIGLU_LIMIT = 7.0
SWIGLU_ALPHA = 1.702
RMS_EPS = 1e-6
LN_EPS = 1e-5
GN_EPS = 64e-5

WKV_CHUNK = 64
WKV_GROUP = 4
EXPERT_ROWS = 512
VMEM_LIMIT = 48 * 1024 * 1024


def _params(*sem):
    return pltpu.CompilerParams(dimension_semantics=sem, vmem_limit_bytes=VMEM_LIMIT)


def _mm(a, b, dims=((1,), (0,)), exact=False):
    dn = (dims, ((), ()))
    if exact:
        return lax.dot_general(a, b, dn, precision=HIGHEST, preferred_element_type=F32)
    return lax.dot_general(a.astype(BF16), b.astype(BF16), dn, preferred_element_type=F32)


def _split3(x):
    hi = x.astype(BF16)
    r1 = x - hi.astype(F32)
    mid = r1.astype(BF16)
    lo = (r1 - mid.astype(F32)).astype(BF16)
    return hi, mid, lo


def _mm_hi_lo(a, b, dims):
    a_hi = a.astype(BF16)
    b_hi = b.astype(BF16)
    a_lo = a - a_hi.astype(F32)
    b_lo = b - b_hi.astype(F32)
    return _mm(a_hi, b_hi, dims) + _mm(a_hi, b_lo, dims) + _mm(a_lo, b_hi, dims)


def _mm_sel_x(sel, x):
    return sum(_mm(sel, p) for p in _split3(x))


def _full(shape):
    n = len(shape)
    return pl.BlockSpec(shape, lambda *_: (0,) * n)


def _norm_mod(x, g, shift, scale):
    y = x * lax.rsqrt(jnp.mean(x * x, axis=-1, keepdims=True) + RMS_EPS) * g
    return y * (1.0 + scale) + shift


def _mods_kernel(c_ref, w_ref, b_ref, o_ref):
    c = c_ref[...]
    s = c * jax.nn.sigmoid(c)
    o_ref[...] = _mm(s, w_ref[...], exact=True) + b_ref[...]


def _stage_mods(cs, w_ada, b_ada):
    rows, d = cs.shape
    n = w_ada.shape[1]
    tn = 1536
    return pl.pallas_call(
        _mods_kernel,
        grid=(n // tn,),
        in_specs=[_full((rows, d)),
                  pl.BlockSpec((d, tn), lambda j: (0, j)),
                  pl.BlockSpec((1, tn), lambda j: (0, j))],
        out_specs=pl.BlockSpec((rows, tn), lambda j: (0, j)),
        out_shape=jax.ShapeDtypeStruct((rows, n), F32),
        compiler_params=_params("arbitrary"),
        name="mods",
    )(cs, w_ada, b_ada)


def _inproj_kernel(x_ref, sh_ref, sc_ref, g_ref, w_ref, o_ref):
    h = _norm_mod(x_ref[0], g_ref[...], sh_ref[0], sc_ref[0])
    o_ref[0] = _mm(h, w_ref[...])


def _stage_inproj(x, shift, scale, g, w_bf16, tm):
    b, t, d = x.shape
    n = w_bf16.shape[1]
    return pl.pallas_call(
        _inproj_kernel,
        grid=(b, t // tm),
        in_specs=[pl.BlockSpec((1, tm, d), lambda i, j: (i, j, 0)),
                  pl.BlockSpec((1, 1, d), lambda i, j: (i, 0, 0)),
                  pl.BlockSpec((1, 1, d), lambda i, j: (i, 0, 0)),
                  _full((1, d)),
                  _full((d, n))],
        out_specs=pl.BlockSpec((1, tm, n), lambda i, j: (i, j, 0)),
        out_shape=jax.ShapeDtypeStruct((b, t, n), F32),
        compiler_params=_params("arbitrary", "arbitrary"),
        name="inproj",
    )(x, shift, scale, g, w_bf16)


def _rwkv_feats(p, mu_kk, mu_ka, r_k, w0, dlb, a0, ilb, glb, headsum):
    w = RWKV_WIDTH
    r = p[:, 0:w]
    k = p[:, w:2 * w]
    v = p[:, 2 * w:3 * w]
    o = 3 * w
    wd = (p[:, o:o + DECAY_LORA], p[:, o + DECAY_LORA:o + 2 * DECAY_LORA])
    o += 2 * DECAY_LORA
    ad = (p[:, o:o + ICLR_LORA], p[:, o + ICLR_LORA:o + 2 * ICLR_LORA])
    o += 2 * ICLR_LORA
    gd = p[:, o:o + GATE_LORA]

    kk = k * mu_kk
    kk = kk * lax.rsqrt(_mm(kk * kk, headsum) + 1e-12)
    g = _mm(jax.nn.sigmoid(gd), glb)
    ks, bs, lws = [], [], []
    ksum = None
    for d in range(2):
        z = w0[d:d + 1] + _mm(jnp.tanh(wd[d]), dlb[d])
        lws.append(-math.exp(-0.5) * jax.nn.sigmoid(z))
        ic = jax.nn.sigmoid(a0[d:d + 1] + _mm(ad[d], ilb[d]))
        kd = k * (1.0 + (ic - 1.0) * mu_ka)
        ks.append(kd)
        bs.append(kk * ic)
        ksum = kd if ksum is None else ksum + kd
    bonus = _mm(r * ksum * r_k, headsum) * v
    return r, v, -kk, g, bonus, ks, bs, lws


def _prep_kernel(pm_ref, pp_ref, pn_ref, mu_ref, kk_ref, ka_ref, rk_ref, w0_ref, dlb_ref,
                 a0_ref, ilb_ref, glb_ref, hs_ref,
                 r_ref, v_ref, a_ref, g_ref, bon_ref, k_ref, b_ref, lw_ref, *, tm, grid_shift):
    main = pm_ref[0]
    lane = lax.broadcasted_iota(jnp.int32, (1, RWKV_COLS), 1)
    if grid_shift:
        ext = jnp.concatenate([pp_ref[0], main, pn_ref[0]], axis=0)
        t = pl.program_id(1) * tm + lax.broadcasted_iota(jnp.int32, (tm, 1), 0)
        col = t & (GRID_W - 1)
        row = t >> (GRID_W.bit_length() - 1)
        n_rows = pl.num_programs(1) * tm // GRID_W
        left = jnp.where(col > 0, ext[GRID_W - 1:GRID_W - 1 + tm], 0.0)
        right = jnp.where(col < GRID_W - 1, ext[GRID_W + 1:GRID_W + 1 + tm], 0.0)
        up = jnp.where(row > 0, ext[0:tm], 0.0)
        down = jnp.where(row < n_rows - 1, ext[2 * GRID_W:2 * GRID_W + tm], 0.0)
        cm = lane & 3
        shifted = jnp.where(cm == 0, left, jnp.where(cm == 1, right, jnp.where(cm == 2, up, down)))
    else:
        zero = jnp.zeros((1, RWKV_COLS), F32)
        prev = jnp.concatenate([zero, main[:tm - 1]], axis=0)
        nxt = jnp.concatenate([main[1:], zero], axis=0)
        shifted = jnp.where((lane & 1) == 0, prev, nxt)
    p = main + mu_ref[...] * (shifted - main)
    r, v, a, g, bonus, ks, bs, lws = _rwkv_feats(
        p, kk_ref[...], ka_ref[...], rk_ref[...], w0_ref[...], dlb_ref, a0_ref[...], ilb_ref,
        glb_ref[...], hs_ref[...])
    r_ref[0] = r.astype(BF16)
    v_ref[0] = v.astype(BF16)
    a_ref[0] = a.astype(BF16)
    g_ref[0] = g.astype(BF16)
    bon_ref[0] = bonus.astype(BF16)
    for d in range(2):
        k_ref[d, 0] = ks[d].astype(BF16)
        b_ref[d, 0] = bs[d].astype(BF16)
        lw_ref[d, 0] = lws[d]


def _stage_prep(p, prm, tm, grid_shift):
    b, t, c = p.shape
    w = RWKV_WIDTH
    hb = GRID_W if grid_shift else 8
    per = tm // hb
    last = t // hb - 1
    tok = pl.BlockSpec((1, tm, w), lambda i, j: (i, j, 0))
    tok2 = pl.BlockSpec((2, 1, tm, w), lambda i, j: (0, i, j, 0))
    in_specs = [
        pl.BlockSpec((1, tm, c), lambda i, j: (i, j, 0)),
        pl.BlockSpec((1, hb, c), lambda i, j: (i, jnp.maximum(j * per - 1, 0), 0)),
        pl.BlockSpec((1, hb, c), lambda i, j: (i, jnp.minimum((j + 1) * per, last), 0)),
    ] + [_full(a.shape) for a in prm]
    return pl.pallas_call(
        functools.partial(_prep_kernel, tm=tm, grid_shift=grid_shift),
        grid=(b, t // tm),
        in_specs=in_specs,
        out_specs=[tok] * 5 + [tok2] * 3,
        out_shape=[jax.ShapeDtypeStruct((b, t, w), BF16)] * 5
        + [jax.ShapeDtypeStruct((2, b, t, w), BF16)] * 2
        + [jax.ShapeDtypeStruct((2, b, t, w), F32)],
        compiler_params=_params("arbitrary", "arbitrary"),
        name="prep_lat" if grid_shift else "prep_ctx",
    )(p, p, p, *prm)


def _wkv_prepare(d, r, v, a, k, bb, lw):
    c = WKV_CHUNK
    n = HEAD_DIM
    ii = lax.broadcasted_iota(jnp.int32, (c, c), 0)
    jj = lax.broadcasted_iota(jnp.int32, (c, c), 1)
    incl = (jj <= ii) if d == 0 else (jj >= ii)
    lc = _mm_sel_x(incl.astype(BF16), lw)
    lx = lc - lw
    ltot = lc[c - 1:c] if d == 0 else lc[0:1]
    inv = jnp.exp(-lc)
    tail = jnp.exp(ltot - lc)
    etot = jnp.exp(ltot)
    eye_n = (lax.broadcasted_iota(jnp.int32, (n, n), 0)
             == lax.broadcasted_iota(jnp.int32, (n, n), 1)).astype(F32)
    scale = jnp.concatenate(
        [jnp.broadcast_to(jnp.sum(eye_n * etot[:, h * n:(h + 1) * n], axis=1, keepdims=True), (n, n))
         for h in range(RWKV_HEADS)], axis=1)
    bf = lambda x: x.astype(BF16)
    return (bf(a * jnp.exp(lx)), bf(r * jnp.exp(lc)), bf(k * inv), bf(bb * inv), bf(k * tail),
            bf(bb * tail), bf(v), scale)


def _wkv_consts():
    c = WKV_CHUNK
    n = HEAD_DIM
    gw = WKV_GROUP * n
    assert c == n and n & (n - 1) == 0
    ii = lax.broadcasted_iota(jnp.int32, (c, gw), 0)
    jj = lax.broadcasted_iota(jnp.int32, (c, gw), 1) & (c - 1)
    head_shift = n.bit_length() - 1
    same_head = (lax.broadcasted_iota(jnp.int32, (gw, gw), 0) >> head_shift
                 == lax.broadcasted_iota(jnp.int32, (gw, gw), 1) >> head_shift)

    def bd(x):
        x = x.astype(BF16)
        return jnp.where(same_head, jnp.concatenate([x] * WKV_GROUP, axis=0), jnp.zeros((), BF16))

    return ii, jj, bd


def _wkv_factor(chunks, emit):
    c = WKV_CHUNK
    gw = WKV_GROUP * HEAD_DIM
    n_groups = RWKV_WIDTH // gw
    ii, jj, bd = _wkv_consts()
    eye = (ii == jj).astype(F32)
    nt = ((1,), (1,))
    tn = ((0,), (0,))
    sl = [slice(h * HEAD_DIM, (h + 1) * HEAD_DIM) for h in range(WKV_GROUP)]

    prob = []
    for preps in chunks:
        for d, (at, rt, kt, bt, kh, bh, vb, _) in enumerate(preps):
            incl = (jj <= ii) if d == 0 else (jj >= ii)
            strict = (jj < ii) if d == 0 else (jj > ii)
            for gi in range(n_groups):
                gs = slice(gi * gw, (gi + 1) * gw)
                prob.append(dict(incl=incl, strict=strict, at=at[:, gs], rt=rt[:, gs],
                                 kt=kt[:, gs], bt=bt[:, gs], kh=kh[:, gs], bh=bh[:, gs],
                                 v=vb[:, gs]))
    for p in prob:
        lhs = jnp.concatenate([p["at"], p["rt"]], axis=0) if emit else p["at"]
        p["lhs_z"] = lhs
        gk = _mm(lhs, bd(p["kt"]), nt)
        gb = _mm(lhs, bd(p["bt"]), nt)
        p["a_ab"] = jnp.where(p["strict"], gb[:c], 0.0)
        lhs_v = jnp.where(p["strict"], gk[:c], 0.0)
        if emit:
            lhs_v = jnp.concatenate([lhs_v, jnp.where(p["incl"], gk[c:], 0.0)], axis=0)
            p["a_rb"] = jnp.where(p["incl"], gb[c:], 0.0).astype(BF16)
        p["lhs_v"] = lhs_v
    for p in prob:
        p["tinv"] = eye + p["a_ab"]
        p["x"] = _mm(p["a_ab"], bd(p["a_ab"]))
        p["av"] = _mm(p["lhs_v"], bd(p["v"]))
    for level in range(1, 6):
        for p in prob:
            if level < 5:
                both = _mm(jnp.concatenate([p["x"], p["tinv"]], axis=0), bd(p["x"]))
                p["x"] = both[:c]
                p["tinv"] = p["tinv"] + both[c:]
            else:
                p["tinv"] = (p["tinv"] + _mm(p["tinv"], bd(p["x"]))).astype(BF16)
    per_chunk = 2 * n_groups
    return [prob[j * per_chunk:(j + 1) * per_chunk] for j in range(len(chunks))]


def _wkv_apply(probs, chunks, states, emit):
    c = WKV_CHUNK
    n = HEAD_DIM
    gw = WKV_GROUP * n
    n_groups = RWKV_WIDTH // gw
    _, _, bd = _wkv_consts()
    tn = ((0,), (0,))
    sl = [slice(h * n, (h + 1) * n) for h in range(WKV_GROUP)]
    flat = []
    for prob, zs in zip(probs, states):
        zb = [z.astype(BF16) for z in zs]
        for i, p in enumerate(prob):
            d, gi = divmod(i, n_groups)
            p["z"] = zb[d][:, gi * gw:(gi + 1) * gw]
            flat.append(p)
    for p in flat:
        p["zv"] = _mm(p["lhs_z"], bd(p["z"])) + p["av"]
    for p in flat:
        p["u"] = _mm(p["tinv"], bd(p["zv"][:c])).astype(BF16)
    for p in flat:
        if emit:
            p["y"] = p["zv"][c:] + _mm(p["a_rb"], bd(p["u"]))
        p["z_new"] = [_mm(jnp.concatenate([p["kh"][:, s], p["bh"][:, s]], axis=0),
                          jnp.concatenate([p["v"][:, s], p["u"][:, s]], axis=0), tn) for s in sl]
    out = []
    for prob, preps, zs in zip(probs, chunks, states):
        res = []
        for d, (prep, z) in enumerate(zip(preps, zs)):
            mine = prob[d * n_groups:(d + 1) * n_groups]
            z_cat = jnp.concatenate([m for p in mine for m in p["z_new"]], axis=1) + prep[7] * z
            y = jnp.concatenate([p["y"] for p in mine], axis=1) if emit else None
            res.append((y, z_cat))
        out.append(res)
    return out


def _wkv_kernel(*refs, emit, has_init):
    refs = list(refs)
    ins = [[refs.pop(0) for _ in range(6)] for _ in range(2)]
    z0_ref = refs.pop(0) if has_init else None
    outs = [refs.pop(0) for _ in range(2 if emit else 1)]
    z_scr = refs.pop(0)
    n_seq = z_scr.shape[1]

    @pl.when(pl.program_id(0) == 0)
    def _():
        if has_init:
            z_scr[...] = z0_ref[...]
        else:
            z_scr[...] = jnp.zeros_like(z_scr)

    chunks = []
    for i in range(n_seq):
        preps = []
        for d in range(2):
            r_ref, v_ref, a_ref, k_ref, b_ref, lw_ref = ins[d]
            preps.append(_wkv_prepare(d, r_ref[i], v_ref[i], a_ref[i], k_ref[0, i], b_ref[0, i],
                                      lw_ref[0, i]))
        chunks.append(preps)
    probs = _wkv_factor(chunks, emit)
    states = [(z_scr[0, i], z_scr[1, i]) for i in range(n_seq)]
    for i, res in enumerate(_wkv_apply(probs, chunks, states, emit)):
        for d, (y, z_new) in enumerate(res):
            z_scr[d, i] = z_new
            if emit:
                outs[d][i] = y
    if not emit:
        outs[0][...] = z_scr[...]


def _stage_wkv(r, v, a, k, bb, lw, z0=None, emit=True):
    b, t, w = r.shape
    c = WKV_CHUNK
    nch = t // c
    n = HEAD_DIM
    pos = (lambda s: s, lambda s: nch - 1 - s)
    in_specs, args = [], []
    for d in range(2):
        tok = pl.BlockSpec((b, c, w), lambda s, d=d: (0, pos[d](s), 0))
        tok2 = pl.BlockSpec((1, b, c, w), lambda s, d=d: (d, 0, pos[d](s), 0))
        in_specs += [tok, tok, tok, tok2, tok2, tok2]
        args += [r, v, a, k, bb, lw]
    zspec = pl.BlockSpec((2, b, n, w), lambda s: (0, 0, 0, 0))
    if z0 is not None:
        in_specs.append(zspec)
        args.append(z0)
    if emit:
        out_specs = [pl.BlockSpec((b, c, w), lambda s, d=d: (0, pos[d](s), 0)) for d in range(2)]
        out_shape = [jax.ShapeDtypeStruct((b, t, w), F32)] * 2
    else:
        out_specs = [zspec]
        out_shape = [jax.ShapeDtypeStruct((2, b, n, w), F32)]
    return pl.pallas_call(
        functools.partial(_wkv_kernel, emit=emit, has_init=z0 is not None),
        grid=(nch,),
        in_specs=in_specs,
        out_specs=out_specs,
        out_shape=out_shape,
        scratch_shapes=[pltpu.VMEM((2, b, n, w), F32)],
        compiler_params=_params("arbitrary"),
        name="wkv_lat" if emit else "wkv_ctx",
    )(*args)


def _merge_kernel(x_ref, yf_ref, yb_ref, g_ref, bon_ref, sh_ref, sc_ref, ga_ref, n1_ref, w2_ref,
                  gnw_ref, gnb_ref, hs_ref, wor_ref, lnw_ref, lnb_ref, wsp_ref, bsp_ref,
                  wos_ref, wo_ref, o_ref, *, tm):
    x = x_ref[0]
    h = _norm_mod(x, n1_ref[...], sh_ref[0], sc_ref[0])
    p2 = _mm(h, w2_ref[...])

    ps = p2[:, :2 * SGU_WIDTH]
    ge = 0.5 * ps * (1.0 + lax.erf(ps * (1.0 / math.sqrt(2.0))))
    u = ge[:, :SGU_WIDTH]
    z = ge[:, SGU_WIDTH:]
    mu = jnp.mean(z, axis=-1, keepdims=True)
    zc = z - mu
    var = jnp.mean(zc * zc, axis=-1, keepdims=True)
    z = zc * lax.rsqrt(var + LN_EPS) * lnw_ref[...] + lnb_ref[...]
    gw = SGU_WIDTH // SGU_GROUPS
    rows = []
    for c in range(tm // SGU_CHUNK):
        zc = z[c * SGU_CHUNK:(c + 1) * SGU_CHUNK]
        cols = [_mm(wsp_ref[gi], zc[:, gi * gw:(gi + 1) * gw]) for gi in range(SGU_GROUPS)]
        rows.append(jnp.concatenate(cols, axis=1) + bsp_ref[...])
    s = jnp.concatenate(rows, axis=0)
    y_b = _mm(u * s, wos_ref[...])

    y = yf_ref[0] + yb_ref[0]
    hs = hs_ref[...]
    ym = _mm(y, hs) * (1.0 / HEAD_DIM)
    yc = y - ym
    yv = _mm(yc * yc, hs) * (1.0 / HEAD_DIM)
    yn = yc * lax.rsqrt(yv + GN_EPS) * gnw_ref[...] + gnb_ref[...]
    y_a = _mm((yn + bon_ref[0]) * g_ref[0], wor_ref[...])

    gates = jax.nn.sigmoid(p2[:, 2 * SGU_WIDTH:])
    mix = gates[:, :D_MODEL] * y_a + gates[:, D_MODEL:] * y_b
    o_ref[0] = x + ga_ref[0] * _mm(mix, wo_ref[...])


def _stage_merge(x, yf, yb, g, bonus, sh, sc, ga, consts, tm):
    b, t, d = x.shape
    w = RWKV_WIDTH
    mod = pl.BlockSpec((1, 1, d), lambda i, j: (i, 0, 0))
    tok = pl.BlockSpec((1, tm, w), lambda i, j: (i, j, 0))
    return pl.pallas_call(
        functools.partial(_merge_kernel, tm=tm),
        grid=(b, t // tm),
        in_specs=[pl.BlockSpec((1, tm, d), lambda i, j: (i, j, 0)), tok, tok, tok, tok,
                  mod, mod, mod] + [_full(a.shape) for a in consts],
        out_specs=pl.BlockSpec((1, tm, d), lambda i, j: (i, j, 0)),
        out_shape=jax.ShapeDtypeStruct((b, t, d), F32),
        compiler_params=_params("arbitrary", "arbitrary"),
        name="merge",
    )(x, yf, yb, g, bonus, sh, sc, ga, *consts)


def _to_token_tiles(ref, val, rows, first=0):
    for cc in range(val.shape[1] // 128):
        ref[pl.ds(first * 8 + cc, rows, stride=8), :] = val[:, cc * 128:(cc + 1) * 128]


def _from_token_tiles(ref, rows, first=0):
    return jnp.concatenate([ref[pl.ds(first * 8 + cc, rows, stride=8), :] for cc in range(8)],
                           axis=1)


def _router_kernel(x_ref, sh_ref, sc_ref, n2_ref, rwt_ref, rb_ref,
                   h_ref, xt_ref, e_ref, w_ref, rank_ref, cnt_ref, carry, *, tm):
    @pl.when(pl.program_id(0) == 0)
    def _():
        carry[...] = jnp.zeros_like(carry)

    x = x_ref[...]
    h = _norm_mod(x, n2_ref[...], sh_ref[0], sc_ref[0])
    _to_token_tiles(h_ref, h, tm)
    _to_token_tiles(xt_ref, x, tm)
    logits = _mm_hi_lo(rwt_ref[...], h, dims=((1,), (1,))) + rb_ref[...]
    eio = lax.broadcasted_iota(jnp.int32, (N_EXPERTS, tm), 0)
    vals, sels = [], []
    for _ in range(TOP_K):
        m = jnp.max(logits, axis=0, keepdims=True)
        idx = jnp.min(jnp.where(logits == m, eio, N_EXPERTS), axis=0, keepdims=True)
        sel = eio == idx
        logits = jnp.where(sel, -jnp.inf, logits)
        vals.append(m)
        sels.append(sel)
        e_ref[len(vals) - 1:len(vals), :] = idx
    ex = [jnp.exp(vk - vals[0]) for vk in vals]
    tot = ex[0] + ex[1] + ex[2] + ex[3]
    for kk in range(TOP_K):
        w_ref[kk:kk + 1, :] = ex[kk] / tot
    cnt = (sels[0] | sels[1] | sels[2] | sels[3]).astype(F32)
    ti = lax.broadcasted_iota(jnp.int32, (tm, tm), 0)
    tj = lax.broadcasted_iota(jnp.int32, (tm, tm), 1)
    before = _mm(cnt, (ti < tj).astype(F32))
    base = carry[...] + before
    for kk in range(TOP_K):
        rank_ref[kk:kk + 1, :] = jnp.sum(jnp.where(sels[kk], base, 0.0), axis=0,
                                         keepdims=True).astype(jnp.int32)
    new = carry[...] + jnp.sum(cnt, axis=1, keepdims=True)
    carry[...] = new
    cnt_ref[...] = jnp.broadcast_to(new, cnt_ref.shape).astype(jnp.int32)


def _stage_router(x2, sh, sc, n2g, rwt, rb, tokens_per_batch, tm):
    n, d = x2.shape
    per = tokens_per_batch // tm
    mod = pl.BlockSpec((1, 1, d), lambda i: (i // per, 0, 0))
    lane = pl.BlockSpec((TOP_K, tm), lambda i: (0, i))
    assert d == 8 * 128, "token-tile layout stores one (8, 128) tile per token"
    tiles = pl.BlockSpec((tm * 8, 128), lambda i: (i, 0))
    return pl.pallas_call(
        functools.partial(_router_kernel, tm=tm),
        grid=(n // tm,),
        in_specs=[pl.BlockSpec((tm, d), lambda i: (i, 0)), mod, mod,
                  _full(n2g.shape), _full(rwt.shape), _full(rb.shape)],
        out_specs=[tiles, tiles, lane, lane, lane, _full((N_EXPERTS, 128))],
        out_shape=[jax.ShapeDtypeStruct((n * 8, 128), F32),
                   jax.ShapeDtypeStruct((n * 8, 128), F32),
                   jax.ShapeDtypeStruct((TOP_K, n), jnp.int32),
                   jax.ShapeDtypeStruct((TOP_K, n), F32),
                   jax.ShapeDtypeStruct((TOP_K, n), jnp.int32),
                   jax.ShapeDtypeStruct((N_EXPERTS, 128), jnp.int32)],
        scratch_shapes=[pltpu.VMEM((N_EXPERTS, 1), F32)],
        compiler_params=_params("arbitrary"),
        name="router",
    )(x2, sh, sc, n2g, rwt, rb)


def _expert_kernel(be_ref, nu_ref, nx_ref, nrow_ref, x_ref, bg_ref, bu_ref, bd_ref, wg_hbm, wu_hbm, wd_hbm,
                   o_ref, stage, wg_s, wu_s, wd_s, sem):
    i = pl.program_id(0)
    e = be_ref[i]
    used = i < nu_ref[0]
    prev = be_ref[jnp.maximum(i - 1, 0)]

    def fetch(expert):
        return [pltpu.make_async_copy(w.at[expert], stage.at[j], sem.at[j])
                for j, w in enumerate((wg_hbm, wu_hbm, wd_hbm))]

    @pl.when(i == 0)
    def _():
        for cp in fetch(e):
            cp.start()

    @pl.when(used & ((i == 0) | (e != prev)))
    def _():
        for cp, w_s, j in zip(fetch(e), (wg_s, wu_s, wd_s), range(3)):
            cp.wait()
            w_s[...] = stage[j].astype(BF16)
        nxt = nx_ref[e]

        @pl.when(nxt >= 0)
        def _():
            for cp in fetch(nxt):
                cp.start()

    half = EXPERT_ROWS // 2
    for hf in range(2):
        live = used & (nrow_ref[i] > hf * half)

        @pl.when(live)
        def _():
            x = _from_token_tiles(x_ref, half, hf * half)
            gate = _mm(x, wg_s[...]) + bg_ref[0]
            up = _mm(x, wu_s[...]) + bu_ref[0]
            gate = jnp.minimum(gate, SWIGLU_LIMIT)
            up = jnp.clip(up, -SWIGLU_LIMIT, SWIGLU_LIMIT)
            act = gate * jax.nn.sigmoid(SWIGLU_ALPHA * gate) * (up + 1.0)
            _to_token_tiles(o_ref, _mm(act, wd_s[...]) + bd_ref[0], half, hf * half)

        @pl.when(jnp.logical_not(live))
        def _():
            o_ref[hf * half * 8:(hf + 1) * half * 8, :] = jnp.zeros((half * 8, 128), F32)


def _stage_experts(block_e, n_used, next_e, block_rows, xs, wg, bg, wu, bu, wd, bd):
    n_rows = xs.shape[0] // 8
    d, f = wg.shape[1:]
    assert d == f, "one staging buffer shape serves all three weight matrices"
    bm = EXPERT_ROWS
    bspec = lambda n_: pl.BlockSpec((1, 1, n_), lambda i, be, *_: (be[i], 0, 0))
    hbm = pl.BlockSpec(memory_space=pl.ANY)
    return pl.pallas_call(
        _expert_kernel,
        grid_spec=pltpu.PrefetchScalarGridSpec(
            num_scalar_prefetch=4,
            grid=(n_rows // bm,),
            in_specs=[pl.BlockSpec((bm * 8, 128),
                                   lambda i, be, nu, *_: (jnp.clip(nu[0] - 1, 0, i), 0)),
                      bspec(f), bspec(f), bspec(d), hbm, hbm, hbm],
            out_specs=pl.BlockSpec((bm * 8, 128), lambda i, *_: (i, 0)),
            scratch_shapes=[pltpu.VMEM((3, d, f), F32), pltpu.VMEM((d, f), BF16),
                            pltpu.VMEM((d, f), BF16), pltpu.VMEM((f, d), BF16),
                            pltpu.SemaphoreType.DMA((3,))]),
        out_shape=jax.ShapeDtypeStruct((n_rows * 8, 128), F32),
        compiler_params=_params("arbitrary"),
        name="experts",
    )(block_e, n_used, next_e, block_rows, xs, bg, bu, bd, wg, wu, wd)


def _tile_copy(src_ref, src_row, dst_ref, dst_row, sem):
    src = src_ref.at[pl.ds(pl.multiple_of(src_row * 8, 8), 8)]
    dst = dst_ref.at[pl.ds(pl.multiple_of(dst_row * 8, 8), 8)]
    return pltpu.make_async_copy(src, dst, sem)


def _dispatch_kernel(dest_ref, zstart_ref, zon_ref, h_ref, xs_ref, zeros, sem, zsem, *, tm, n_tok):
    i = pl.program_id(0)
    bm8 = EXPERT_ROWS * 8

    def zero_fill(e):
        dst = xs_ref.at[pl.ds(pl.multiple_of(zstart_ref[e] * 8, 8), bm8)]
        return pltpu.make_async_copy(zeros, dst, zsem)

    @pl.when(i == 0)
    def _():
        zeros[...] = jnp.zeros_like(zeros)
        for e in range(zon_ref.shape[0]):
            @pl.when(zon_ref[e] > 0)
            def _():
                zero_fill(e).start()
        for e in range(zon_ref.shape[0]):
            @pl.when(zon_ref[e] > 0)
            def _():
                zero_fill(e).wait()

    def body(t, carry):
        for kk in range(TOP_K):
            _tile_copy(h_ref, t, xs_ref, dest_ref[kk * n_tok + i * tm + t], sem).start(
                priority=kk % 2)
        return carry

    lax.fori_loop(0, tm, body, 0)
    for _ in range(TOP_K):
        pltpu.make_async_copy(h_ref, xs_ref.at[pl.ds(0, tm * 8)], sem).wait()


def _stage_dispatch(dest_flat, zstart, zon, h2t, n_rows, tm):
    n = h2t.shape[0] // 8
    return pl.pallas_call(
        functools.partial(_dispatch_kernel, tm=tm, n_tok=n),
        grid_spec=pltpu.PrefetchScalarGridSpec(
            num_scalar_prefetch=3,
            grid=(n // tm,),
            in_specs=[pl.BlockSpec((tm * 8, 128), lambda i, *_: (i, 0))],
            out_specs=pl.BlockSpec(memory_space=pl.ANY),
            scratch_shapes=[pltpu.VMEM((EXPERT_ROWS * 8, 128), F32),
                            pltpu.SemaphoreType.DMA(()), pltpu.SemaphoreType.DMA(())]),
        out_shape=jax.ShapeDtypeStruct((n_rows * 8, 128), F32),
        compiler_params=_params("arbitrary"),
        name="dispatch",
    )(dest_flat, zstart, zon, h2t)


def _combine_kernel(dest_ref, w_ref, xt_ref, ga_ref, g_ref, yb_ref, o_ref, buf, res, sem,
                    *, tm, n_tok):
    i = pl.program_id(0)
    ga = ga_ref[0]

    def issue(tile, slot, t):
        for kk in range(TOP_K):
            _tile_copy(yb_ref, dest_ref[kk * n_tok + tile * tm + t], buf.at[slot, kk], t,
                       sem.at[slot]).start(priority=kk % 2)

    def combine(slot, t):
        rows = pl.ds(pl.multiple_of(t * 8, 8), 8)
        acc = buf[slot, 0, rows, :] * w_ref[i * tm + t]
        for kk in range(1, TOP_K):
            acc = acc + buf[slot, kk, rows, :] * w_ref[kk * n_tok + i * tm + t]
        res[rows, :] = xt_ref[rows, :] + ga * acc

    def loop(body):
        lax.fori_loop(0, tm, lambda t, carry: (body(t), carry)[1], 0, unroll=4)

    @pl.when(i == 0)
    def _():
        loop(lambda t: issue(0, 0, t))

    for slot in range(2):
        @pl.when(i % 2 == slot)
        def _():
            for kk in range(TOP_K):
                pltpu.make_async_copy(buf.at[slot, kk], buf.at[slot, kk], sem.at[slot]).wait()

            @pl.when(i + 1 < pl.num_programs(0))
            def _():
                loop(lambda t: (issue(i + 1, 1 - slot, t), combine(slot, t)))

            @pl.when(i + 1 >= pl.num_programs(0))
            def _():
                loop(lambda t: combine(slot, t))

    x = _from_token_tiles(res, tm)
    o_ref[...] = x * lax.rsqrt(jnp.mean(x * x, axis=-1, keepdims=True) + RMS_EPS) * g_ref[...]


def _stage_combine(dest_flat, w_flat, x1t, ga_t, g_t, yb, tokens_per_batch, tm):
    n = x1t.shape[0] // 8
    d = 8 * 128
    per = tokens_per_batch // tm
    return pl.pallas_call(
        functools.partial(_combine_kernel, tm=tm, n_tok=n),
        grid_spec=pltpu.PrefetchScalarGridSpec(
            num_scalar_prefetch=2,
            grid=(n // tm,),
            in_specs=[pl.BlockSpec((tm * 8, 128), lambda i, *_: (i, 0)),
                      pl.BlockSpec((1, 8, 128), lambda i, *_: (i // per, 0, 0)),
                      pl.BlockSpec((1, d), lambda i, *_: (0, 0)),
                      pl.BlockSpec(memory_space=pl.ANY)],
            out_specs=pl.BlockSpec((tm, d), lambda i, *_: (i, 0)),
            scratch_shapes=[pltpu.VMEM((2, TOP_K, tm * 8, 128), F32),
                            pltpu.VMEM((tm * 8, 128), F32),
                            pltpu.SemaphoreType.DMA((2,))]),
        out_shape=jax.ShapeDtypeStruct((n, d), F32),
        compiler_params=_params("arbitrary"),
        name="combine",
    )(dest_flat, w_flat, x1t, ga_t, g_t, yb)


def _rwkv_branch(p_lat, p_ctx, prm):
    r, v, a, _, _, k, bb, lw = _stage_prep(p_ctx, prm, tm=p_ctx.shape[1], grid_shift=False)
    (z_ctx,) = _stage_wkv(r, v, a, k, bb, lw, emit=False)
    r, v, a, g, bonus, k, bb, lw = _stage_prep(p_lat, prm, tm=256, grid_shift=True)
    y_f, y_b = _stage_wkv(r, v, a, k, bb, lw, z0=z_ctx, emit=True)
    return y_f, y_b, g, bonus


def _route(top_e, rank, counts, n_tok):
    bm = EXPERT_ROWS
    n_rows = n_tok * TOP_K + N_EXPERTS * bm
    padded = (counts + bm - 1) // bm * bm
    pad_end = jnp.cumsum(padded)
    pad_start = pad_end - padded
    experts = jnp.arange(N_EXPERTS, dtype=jnp.int32)
    start_of = jnp.sum(jnp.where(top_e[..., None] == experts, pad_start, 0), axis=-1)
    dest = (start_of + rank).astype(jnp.int32).reshape(-1)
    block_start = jnp.arange(n_rows // bm, dtype=jnp.int32) * bm
    block_e = jnp.minimum(jnp.sum(pad_end[None, :] <= block_start[:, None], axis=1),
                          N_EXPERTS - 1).astype(jnp.int32)
    n_used = (pad_end[-1] // bm).astype(jnp.int32).reshape(1)
    tail = n_used[0] + experts
    zstart = jnp.concatenate([jnp.maximum(pad_end - bm, 0),
                              jnp.minimum(tail, n_rows // bm - 1) * bm]).astype(jnp.int32)
    zon = jnp.concatenate([padded > 0, tail < n_rows // bm]).astype(jnp.int32)
    first_at = lax.cummin(jnp.where(padded > 0, experts, N_EXPERTS), axis=0, reverse=True)
    next_e = jnp.concatenate([first_at[1:], jnp.full((1,), N_EXPERTS, jnp.int32)])
    next_e = jnp.where(next_e < N_EXPERTS, next_e, -1).astype(jnp.int32)
    row_end = jnp.sum(jnp.where(block_e[:, None] == experts[None, :], pad_start + counts, 0), axis=1)
    block_rows = jnp.clip(row_end - block_start, 0, bm).astype(jnp.int32)
    return dest, block_e, n_used, next_e, block_rows, zstart, zon, n_rows


def kernel(x, c, ctx, c_ctx, w_ada, b_ada, norm1_g, w_in, shift_mu, decay_w0, decay_lora_b,
           iclr_a0, iclr_lora_b, gate_lora_b, k_k, k_a, r_k, gn_w, gn_b, w_out_rwkv,
           sgu_ln_w, sgu_ln_b, sgu_w_spatial, sgu_b_spatial, w_out_sgu, w_o, norm2_g,
           router_w, router_b, exp_w_gate, exp_b_gate, exp_w_up, exp_b_up, exp_w_down,
           exp_b_down, final_norm_g):
    assert w_ada.shape[0] == 1, "single-layer problem"
    b, t, d = x.shape
    n_tok = b * t
    w = RWKV_WIDTH
    row = lambda a: a.reshape(1, -1)

    cs = jnp.zeros((8, d), F32).at[:b].set(c).at[b].set(c_ctx)
    mod = _stage_mods(cs, w_ada[0], row(b_ada[0]))
    sh1, sc1, ga1, sh2, sc2, ga2 = [m[:b, None, :] for m in jnp.split(mod, 6, axis=-1)]
    csh1, csc1 = [jnp.broadcast_to(m[b][None, None, :], (b, 1, d))
                  for m in jnp.split(mod, 6, axis=-1)[:2]]

    w_in_bf = w_in[0].astype(BF16)
    n1 = row(norm1_g[0])
    p_lat = _stage_inproj(x, sh1, sc1, n1, w_in_bf[:, :RWKV_COLS], tm=512)
    p_ctx = _stage_inproj(ctx, csh1, csc1, n1, w_in_bf[:, :RWKV_COLS], tm=ctx.shape[1])

    head_id = jnp.arange(w, dtype=jnp.int32) // HEAD_DIM
    headsum = (head_id[:, None] == head_id[None, :]).astype(BF16)
    prm = [row(shift_mu[0]), row(k_k[0]), row(k_a[0]), row(r_k[0]), decay_w0[0],
           decay_lora_b[0], iclr_a0[0], iclr_lora_b[0], gate_lora_b[0], headsum]
    y_f, y_b, g, bonus = _rwkv_branch(p_lat, p_ctx, prm)

    bsp = jnp.repeat(sgu_b_spatial[0].T, SGU_WIDTH // SGU_GROUPS, axis=1)
    consts = [n1, w_in_bf[:, RWKV_COLS:], row(gn_w[0]), row(gn_b[0]), headsum,
              w_out_rwkv[0].astype(BF16), row(sgu_ln_w[0]), row(sgu_ln_b[0]),
              sgu_w_spatial[0].astype(BF16), bsp, w_out_sgu[0].astype(BF16),
              w_o[0].astype(BF16)]
    x1 = _stage_merge(x, y_f, y_b, g, bonus, sh1, sc1, ga1, consts, tm=256)

    x1f = x1.reshape(n_tok, d)
    h2t, x1t, top_e, top_w, rank, counts = _stage_router(
        x1f, sh2, sc2, row(norm2_g[0]), router_w[0].T, router_b[0].reshape(-1, 1), t, tm=512)
    dest, block_e, n_used, next_e, block_rows, zstart, zon, n_rows = _route(top_e, rank, counts[:, 0], n_tok)
    xs = _stage_dispatch(dest, zstart, zon, h2t, n_rows, tm=512)
    e3 = lambda a: a.reshape(N_EXPERTS, 1, -1)
    yb = _stage_experts(block_e, n_used, next_e, block_rows, xs, exp_w_gate[0], e3(exp_b_gate[0]), exp_w_up[0],
                        e3(exp_b_up[0]), exp_w_down[0], e3(exp_b_down[0]))
    out = _stage_combine(dest, top_w.reshape(-1), x1t, ga2.reshape(b, 8, 128),
                         row(final_norm_g), yb, t, tm=256)
    return out.reshape(b, t, d)
```

```python
import functools
import math

import jax
import jax.numpy as jnp
from jax import lax
from jax.experimental import pallas as pl
from jax.experimental.pallas import tpu as pltpu

F32 = jnp.float32
BF16 = jnp.bfloat16
HIGHEST = lax.Precision.HIGHEST

D_MODEL = 1024
GRID_W = 64
RWKV_HEADS = 8
HEAD_DIM = 64
RWKV_WIDTH = RWKV_HEADS * HEAD_DIM
DECAY_LORA = 64
ICLR_LORA = 64
GATE_LORA = 128
RWKV_COLS = 3 * RWKV_WIDTH + 2 * DECAY_LORA + 2 * ICLR_LORA + GATE_LORA
SGU_WIDTH = 512
SGU_GROUPS = 8
SGU_CHUNK = 128
N_EXPERTS = 32
TOP_K = 4
SWIGLU_LIMIT = 7.0
SWIGLU_ALPHA = 1.702
RMS_EPS = 1e-6
LN_EPS = 1e-5
GN_EPS = 64e-5

WKV_CHUNK = 64
WKV_GROUP = 4
EXPERT_ROWS = 512
VMEM_LIMIT = 48 * 1024 * 1024


def _params(*sem):
    return pltpu.CompilerParams(dimension_semantics=sem, vmem_limit_bytes=VMEM_LIMIT)


def _mm(a, b, dims=((1,), (0,)), exact=False):
    dn = (dims, ((), ()))
    if exact:
        return lax.dot_general(a, b, dn, precision=HIGHEST, preferred_element_type=F32)
    return lax.dot_general(a.astype(BF16), b.astype(BF16), dn, preferred_element_type=F32)


def _split3(x):
    hi = x.astype(BF16)
    r1 = x - hi.astype(F32)
    mid = r1.astype(BF16)
    lo = (r1 - mid.astype(F32)).astype(BF16)
    return hi, mid, lo


def _mm_hi_lo(a, b, dims):
    a_hi = a.astype(BF16)
    b_hi = b.astype(BF16)
    a_lo = a - a_hi.astype(F32)
    b_lo = b - b_hi.astype(F32)
    return _mm(a_hi, b_hi, dims) + _mm(a_hi, b_lo, dims) + _mm(a_lo, b_hi, dims)


def _mm_sel_x(sel, x):
    return sum(_mm(sel, p) for p in _split3(x))


def _full(shape):
    n = len(shape)
    return pl.BlockSpec(shape, lambda *_: (0,) * n)


def _norm_mod(x, g, shift, scale):
    y = x * lax.rsqrt(jnp.mean(x * x, axis=-1, keepdims=True) + RMS_EPS) * g
    return y * (1.0 + scale) + shift


def _mods_kernel(c_ref, w_ref, b_ref, o_ref):
    c = c_ref[...]
    s = c * jax.nn.sigmoid(c)
    o_ref[...] = _mm(s, w_ref[...], exact=True) + b_ref[...]


def _stage_mods(cs, w_ada, b_ada):
    rows, d = cs.shape
    n = w_ada.shape[1]
    tn = 1536
    return pl.pallas_call(
        _mods_kernel,
        grid=(n // tn,),
        in_specs=[_full((rows, d)),
                  pl.BlockSpec((d, tn), lambda j: (0, j)),
                  pl.BlockSpec((1, tn), lambda j: (0, j))],
        out_specs=pl.BlockSpec((rows, tn), lambda j: (0, j)),
        out_shape=jax.ShapeDtypeStruct((rows, n), F32),
        compiler_params=_params("arbitrary"),
        name="mods",
    )(cs, w_ada, b_ada)


def _rwkv_feats(p, mu_kk, mu_ka, r_k, w0, dlb, a0, ilb, glb, headsum):
    w = RWKV_WIDTH
    r = p[:, 0:w]
    k = p[:, w:2 * w]
    v = p[:, 2 * w:3 * w]
    o = 3 * w
    wd = (p[:, o:o + DECAY_LORA], p[:, o + DECAY_LORA:o + 2 * DECAY_LORA])
    o += 2 * DECAY_LORA
    ad = (p[:, o:o + ICLR_LORA], p[:, o + ICLR_LORA:o + 2 * ICLR_LORA])
    o += 2 * ICLR_LORA
    gd = p[:, o:o + GATE_LORA]

    kk = k * mu_kk
    kk = kk * lax.rsqrt(_mm(kk * kk, headsum) + 1e-12)
    g = _mm(jax.nn.sigmoid(gd), glb)
    ks, bs, lws = [], [], []
    ksum = None
    for d in range(2):
        z = w0[d:d + 1] + _mm(jnp.tanh(wd[d]), dlb[d])
        lws.append(-math.exp(-0.5) * jax.nn.sigmoid(z))
        ic = jax.nn.sigmoid(a0[d:d + 1] + _mm(ad[d], ilb[d]))
        kd = k * (1.0 + (ic - 1.0) * mu_ka)
        ks.append(kd)
        bs.append(kk * ic)
        ksum = kd if ksum is None else ksum + kd
    bonus = _mm(r * ksum * r_k, headsum) * v
    return r, v, -kk, g, bonus, ks, bs, lws


def _prep_kernel(xm_ref, xp_ref, xn_ref, sh_ref, sc_ref, n1_ref, w_ref,
                 mu_ref, kk_ref, ka_ref, rk_ref, w0_ref, dlb_ref, a0_ref, ilb_ref, glb_ref, hs_ref,
                 r_ref, v_ref, a_ref, g_ref, bon_ref, k_ref, b_ref, lw_ref, *, tm, grid_shift):
    project = lambda x: _mm(_norm_mod(x, n1_ref[...], sh_ref[0], sc_ref[0]), w_ref[...])
    lane = lax.broadcasted_iota(jnp.int32, (1, RWKV_COLS), 1)
    if grid_shift:
        ext = project(jnp.concatenate([xp_ref[0], xm_ref[0], xn_ref[0]], axis=0))
        main = ext[GRID_W:GRID_W + tm]
        t = pl.program_id(1) * tm + lax.broadcasted_iota(jnp.int32, (tm, 1), 0)
        col = t & (GRID_W - 1)
        row = t >> (GRID_W.bit_length() - 1)
        n_rows = pl.num_programs(1) * tm // GRID_W
        left = jnp.where(col > 0, ext[GRID_W - 1:GRID_W - 1 + tm], 0.0)
        right = jnp.where(col < GRID_W - 1, ext[GRID_W + 1:GRID_W + 1 + tm], 0.0)
        up = jnp.where(row > 0, ext[0:tm], 0.0)
        down = jnp.where(row < n_rows - 1, ext[2 * GRID_W:2 * GRID_W + tm], 0.0)
        cm = lane & 3
        shifted = jnp.where(cm == 0, left, jnp.where(cm == 1, right, jnp.where(cm == 2, up, down)))
    else:
        main = project(xm_ref[0])
        zero = jnp.zeros((1, RWKV_COLS), F32)
        prev = jnp.concatenate([zero, main[:tm - 1]], axis=0)
        nxt = jnp.concatenate([main[1:], zero], axis=0)
        shifted = jnp.where((lane & 1) == 0, prev, nxt)
    p = main + mu_ref[...] * (shifted - main)
    r, v, a, g, bonus, ks, bs, lws = _rwkv_feats(
        p, kk_ref[...], ka_ref[...], rk_ref[...], w0_ref[...], dlb_ref, a0_ref[...], ilb_ref,
        glb_ref[...], hs_ref[...])
    r_ref[0] = r.astype(BF16)
    v_ref[0] = v.astype(BF16)
    a_ref[0] = a.astype(BF16)
    g_ref[0] = g.astype(BF16)
    bon_ref[0] = bonus.astype(BF16)
    for d in range(2):
        k_ref[d, 0] = ks[d].astype(BF16)
        b_ref[d, 0] = bs[d].astype(BF16)
        lw_ref[d, 0] = lws[d]


def _stage_prep(x, shift, scale, g, w_bf16, prm, tm, grid_shift):
    b, t, c = x.shape
    w = RWKV_WIDTH
    mod = pl.BlockSpec((1, 1, c), lambda i, j: (i, 0, 0))
    hb = GRID_W if grid_shift else 8
    per = tm // hb
    last = t // hb - 1
    tok = pl.BlockSpec((1, tm, w), lambda i, j: (i, j, 0))
    tok2 = pl.BlockSpec((2, 1, tm, w), lambda i, j: (0, i, j, 0))
    in_specs = [
        pl.BlockSpec((1, tm, c), lambda i, j: (i, j, 0)),
        pl.BlockSpec((1, hb, c), lambda i, j: (i, jnp.maximum(j * per - 1, 0), 0)),
        pl.BlockSpec((1, hb, c), lambda i, j: (i, jnp.minimum((j + 1) * per, last), 0)),
        mod, mod, _full(g.shape), _full(w_bf16.shape),
    ] + [_full(a.shape) for a in prm]
    return pl.pallas_call(
        functools.partial(_prep_kernel, tm=tm, grid_shift=grid_shift),
        grid=(b, t // tm),
        in_specs=in_specs,
        out_specs=[tok] * 5 + [tok2] * 3,
        out_shape=[jax.ShapeDtypeStruct((b, t, w), BF16)] * 5
        + [jax.ShapeDtypeStruct((2, b, t, w), BF16)] * 2
        + [jax.ShapeDtypeStruct((2, b, t, w), F32)],
        compiler_params=_params("arbitrary", "arbitrary"),
        name="prep_lat" if grid_shift else "prep_ctx",
    )(x, x, x, shift, scale, g, w_bf16, *prm)


def _wkv_prepare(d, r, v, a, k, bb, lw):
    c = WKV_CHUNK
    n = HEAD_DIM
    ii = lax.broadcasted_iota(jnp.int32, (c, c), 0)
    jj = lax.broadcasted_iota(jnp.int32, (c, c), 1)
    incl = (jj <= ii) if d == 0 else (jj >= ii)
    lc = _mm_sel_x(incl.astype(BF16), lw)
    lx = lc - lw
    ltot = lc[c - 1:c] if d == 0 else lc[0:1]
    inv = jnp.exp(-lc)
    tail = jnp.exp(ltot - lc)
    etot = jnp.exp(ltot)
    eye_n = (lax.broadcasted_iota(jnp.int32, (n, n), 0)
             == lax.broadcasted_iota(jnp.int32, (n, n), 1)).astype(F32)
    scale = jnp.concatenate(
        [jnp.broadcast_to(jnp.sum(eye_n * etot[:, h * n:(h + 1) * n], axis=1, keepdims=True), (n, n))
         for h in range(RWKV_HEADS)], axis=1)
    bf = lambda x: x.astype(BF16)
    return (bf(a * jnp.exp(lx)), bf(r * jnp.exp(lc)), bf(k * inv), bf(bb * inv), bf(k * tail),
            bf(bb * tail), bf(v), scale)


def _wkv_consts():
    c = WKV_CHUNK
    n = HEAD_DIM
    gw = WKV_GROUP * n
    assert c == n and n & (n - 1) == 0
    ii = lax.broadcasted_iota(jnp.int32, (c, gw), 0)
    jj = lax.broadcasted_iota(jnp.int32, (c, gw), 1) & (c - 1)
    head_shift = n.bit_length() - 1
    same_head = (lax.broadcasted_iota(jnp.int32, (gw, gw), 0) >> head_shift
                 == lax.broadcasted_iota(jnp.int32, (gw, gw), 1) >> head_shift)

    def bd(x):
        x = x.astype(BF16)
        return jnp.where(same_head, jnp.concatenate([x] * WKV_GROUP, axis=0), jnp.zeros((), BF16))

    return ii, jj, bd


def _wkv_factor(chunks, emit):
    c = WKV_CHUNK
    gw = WKV_GROUP * HEAD_DIM
    n_groups = RWKV_WIDTH // gw
    ii, jj, bd = _wkv_consts()
    eye = (ii == jj).astype(F32)
    nt = ((1,), (1,))
    tn = ((0,), (0,))
    sl = [slice(h * HEAD_DIM, (h + 1) * HEAD_DIM) for h in range(WKV_GROUP)]

    prob = []
    for preps in chunks:
        for d, (at, rt, kt, bt, kh, bh, vb, _) in enumerate(preps):
            incl = (jj <= ii) if d == 0 else (jj >= ii)
            strict = (jj < ii) if d == 0 else (jj > ii)
            for gi in range(n_groups):
                gs = slice(gi * gw, (gi + 1) * gw)
                prob.append(dict(incl=incl, strict=strict, at=at[:, gs], rt=rt[:, gs],
                                 kt=kt[:, gs], bt=bt[:, gs], kh=kh[:, gs], bh=bh[:, gs],
                                 v=vb[:, gs]))
    for p in prob:
        lhs = jnp.concatenate([p["at"], p["rt"]], axis=0) if emit else p["at"]
        p["lhs_z"] = lhs
        gk = _mm(lhs, bd(p["kt"]), nt)
        gb = _mm(lhs, bd(p["bt"]), nt)
        p["a_ab"] = jnp.where(p["strict"], gb[:c], 0.0)
        lhs_v = jnp.where(p["strict"], gk[:c], 0.0)
        if emit:
            lhs_v = jnp.concatenate([lhs_v, jnp.where(p["incl"], gk[c:], 0.0)], axis=0)
            p["a_rb"] = jnp.where(p["incl"], gb[c:], 0.0).astype(BF16)
        p["lhs_v"] = lhs_v
    for p in prob:
        p["tinv"] = eye + p["a_ab"]
        p["x"] = _mm(p["a_ab"], bd(p["a_ab"]))
        p["av"] = _mm(p["lhs_v"], bd(p["v"]))
    for level in range(1, 6):
        for p in prob:
            if level < 5:
                both = _mm(jnp.concatenate([p["x"], p["tinv"]], axis=0), bd(p["x"]))
                p["x"] = both[:c]
                p["tinv"] = p["tinv"] + both[c:]
            else:
                p["tinv"] = (p["tinv"] + _mm(p["tinv"], bd(p["x"]))).astype(BF16)
    per_chunk = 2 * n_groups
    return [prob[j * per_chunk:(j + 1) * per_chunk] for j in range(len(chunks))]


def _wkv_apply(probs, chunks, states, emit):
    c = WKV_CHUNK
    n = HEAD_DIM
    gw = WKV_GROUP * n
    n_groups = RWKV_WIDTH // gw
    _, _, bd = _wkv_consts()
    tn = ((0,), (0,))
    sl = [slice(h * n, (h + 1) * n) for h in range(WKV_GROUP)]
    flat = []
    for prob, zs in zip(probs, states):
        zb = [z.astype(BF16) for z in zs]
        for i, p in enumerate(prob):
            d, gi = divmod(i, n_groups)
            p["z"] = zb[d][:, gi * gw:(gi + 1) * gw]
            flat.append(p)
    for p in flat:
        p["zv"] = _mm(p["lhs_z"], bd(p["z"])) + p["av"]
    for p in flat:
        p["u"] = _mm(p["tinv"], bd(p["zv"][:c])).astype(BF16)
    for p in flat:
        if emit:
            p["y"] = p["zv"][c:] + _mm(p["a_rb"], bd(p["u"]))
        p["z_new"] = [_mm(jnp.concatenate([p["kh"][:, s], p["bh"][:, s]], axis=0),
                          jnp.concatenate([p["v"][:, s], p["u"][:, s]], axis=0), tn) for s in sl]
    out = []
    for prob, preps, zs in zip(probs, chunks, states):
        res = []
        for d, (prep, z) in enumerate(zip(preps, zs)):
            mine = prob[d * n_groups:(d + 1) * n_groups]
            z_cat = jnp.concatenate([m for p in mine for m in p["z_new"]], axis=1) + prep[7] * z
            y = jnp.concatenate([p["y"] for p in mine], axis=1) if emit else None
            res.append((y, z_cat))
        out.append(res)
    return out


def _wkv_kernel(*refs, emit, has_init):
    refs = list(refs)
    ins = [[refs.pop(0) for _ in range(6)] for _ in range(2)]
    z0_ref = refs.pop(0) if has_init else None
    outs = [refs.pop(0) for _ in range(2 if emit else 1)]
    z_scr = refs.pop(0)
    n_seq = z_scr.shape[1]

    @pl.when(pl.program_id(0) == 0)
    def _():
        if has_init:
            z_scr[...] = z0_ref[...]
        else:
            z_scr[...] = jnp.zeros_like(z_scr)

    chunks = []
    for i in range(n_seq):
        preps = []
        for d in range(2):
            r_ref, v_ref, a_ref, k_ref, b_ref, lw_ref = ins[d]
            preps.append(_wkv_prepare(d, r_ref[i], v_ref[i], a_ref[i], k_ref[0, i], b_ref[0, i],
                                      lw_ref[0, i]))
        chunks.append(preps)
    probs = _wkv_factor(chunks, emit)
    states = [(z_scr[0, i], z_scr[1, i]) for i in range(n_seq)]
    for i, res in enumerate(_wkv_apply(probs, chunks, states, emit)):
        for d, (y, z_new) in enumerate(res):
            z_scr[d, i] = z_new
            if emit:
                outs[d][i] = y
    if not emit:
        outs[0][...] = z_scr[...]


def _stage_wkv(r, v, a, k, bb, lw, z0=None, emit=True):
    b, t, w = r.shape
    c = WKV_CHUNK
    nch = t // c
    n = HEAD_DIM
    pos = (lambda s: s, lambda s: nch - 1 - s)
    in_specs, args = [], []
    for d in range(2):
        tok = pl.BlockSpec((b, c, w), lambda s, d=d: (0, pos[d](s), 0))
        tok2 = pl.BlockSpec((1, b, c, w), lambda s, d=d: (d, 0, pos[d](s), 0))
        in_specs += [tok, tok, tok, tok2, tok2, tok2]
        args += [r, v, a, k, bb, lw]
    zspec = pl.BlockSpec((2, b, n, w), lambda s: (0, 0, 0, 0))
    if z0 is not None:
        in_specs.append(zspec)
        args.append(z0)
    if emit:
        out_specs = [pl.BlockSpec((b, c, w), lambda s, d=d: (0, pos[d](s), 0)) for d in range(2)]
        out_shape = [jax.ShapeDtypeStruct((b, t, w), F32)] * 2
    else:
        out_specs = [zspec]
        out_shape = [jax.ShapeDtypeStruct((2, b, n, w), F32)]
    return pl.pallas_call(
        functools.partial(_wkv_kernel, emit=emit, has_init=z0 is not None),
        grid=(nch,),
        in_specs=in_specs,
        out_specs=out_specs,
        out_shape=out_shape,
        scratch_shapes=[pltpu.VMEM((2, b, n, w), F32)],
        compiler_params=_params("arbitrary"),
        name="wkv_lat" if emit else "wkv_ctx",
    )(*args)


def _to_token_tiles(ref, val, rows):
    for cc in range(val.shape[1] // 128):
        ref[pl.ds(cc, rows, stride=8), :] = val[:, cc * 128:(cc + 1) * 128]


def _from_token_tiles(ref, rows):
    return jnp.concatenate([ref[pl.ds(cc, rows, stride=8), :] for cc in range(8)], axis=1)


def _route_tokens(h, rwt_ref, rb_ref, e_ref, w_ref, rank_ref, cnt_ref, carry, tm):
    logits = _mm_hi_lo(rwt_ref[...], h, dims=((1,), (1,))) + rb_ref[...]
    eio = lax.broadcasted_iota(jnp.int32, (N_EXPERTS, tm), 0)
    vals, sels = [], []
    for _ in range(TOP_K):
        m = jnp.max(logits, axis=0, keepdims=True)
        idx = jnp.min(jnp.where(logits == m, eio, N_EXPERTS), axis=0, keepdims=True)
        sel = eio == idx
        logits = jnp.where(sel, -jnp.inf, logits)
        vals.append(m)
        sels.append(sel)
        e_ref[len(vals) - 1:len(vals), :] = idx
    ex = [jnp.exp(vk - vals[0]) for vk in vals]
    tot = ex[0] + ex[1] + ex[2] + ex[3]
    for kk in range(TOP_K):
        w_ref[kk:kk + 1, :] = ex[kk] / tot
    cnt = (sels[0] | sels[1] | sels[2] | sels[3]).astype(F32)
    ti = lax.broadcasted_iota(jnp.int32, (tm, tm), 0)
    tj = lax.broadcasted_iota(jnp.int32, (tm, tm), 1)
    before = _mm(cnt, (ti < tj).astype(F32))
    base = carry[...] + before
    for kk in range(TOP_K):
        rank_ref[kk:kk + 1, :] = jnp.sum(jnp.where(sels[kk], base, 0.0), axis=0,
                                         keepdims=True).astype(jnp.int32)
    new = carry[...] + jnp.sum(cnt, axis=1, keepdims=True)
    carry[...] = new
    cnt_ref[...] = jnp.broadcast_to(new, cnt_ref.shape).astype(jnp.int32)


def _merge_kernel(x_ref, yf_ref, yb_ref, g_ref, bon_ref, sh_ref, sc_ref, ga_ref, sh2_ref, sc2_ref,
                  n1_ref, w2_ref, gnw_ref, gnb_ref, hs_ref, wor_ref, lnw_ref, lnb_ref, wsp_ref,
                  bsp_ref, wos_ref, wo_ref, n2_ref, rwt_ref, rb_ref,
                  h_ref, xt_ref, e_ref, w_ref, rank_ref, cnt_ref, carry, *, tm):
    @pl.when((pl.program_id(0) == 0) & (pl.program_id(1) == 0))
    def _():
        carry[...] = jnp.zeros_like(carry)

    x = x_ref[0]
    h = _norm_mod(x, n1_ref[...], sh_ref[0], sc_ref[0])
    p2 = _mm(h, w2_ref[...])

    ps = p2[:, :2 * SGU_WIDTH]
    ge = 0.5 * ps * (1.0 + lax.erf(ps * (1.0 / math.sqrt(2.0))))
    u = ge[:, :SGU_WIDTH]
    z = ge[:, SGU_WIDTH:]
    mu = jnp.mean(z, axis=-1, keepdims=True)
    zc = z - mu
    var = jnp.mean(zc * zc, axis=-1, keepdims=True)
    z = zc * lax.rsqrt(var + LN_EPS) * lnw_ref[...] + lnb_ref[...]
    gw = SGU_WIDTH // SGU_GROUPS
    rows = []
    for c in range(tm // SGU_CHUNK):
        zc = z[c * SGU_CHUNK:(c + 1) * SGU_CHUNK]
        cols = [_mm(wsp_ref[gi], zc[:, gi * gw:(gi + 1) * gw]) for gi in range(SGU_GROUPS)]
        rows.append(jnp.concatenate(cols, axis=1) + bsp_ref[...])
    s = jnp.concatenate(rows, axis=0)
    y_b = _mm(u * s, wos_ref[...])

    y = yf_ref[0] + yb_ref[0]
    hs = hs_ref[...]
    ym = _mm(y, hs) * (1.0 / HEAD_DIM)
    yc = y - ym
    yv = _mm(yc * yc, hs) * (1.0 / HEAD_DIM)
    yn = yc * lax.rsqrt(yv + GN_EPS) * gnw_ref[...] + gnb_ref[...]
    y_a = _mm((yn + bon_ref[0]) * g_ref[0], wor_ref[...])

    gates = jax.nn.sigmoid(p2[:, 2 * SGU_WIDTH:])
    mix = gates[:, :D_MODEL] * y_a + gates[:, D_MODEL:] * y_b
    x1 = x + ga_ref[0] * _mm(mix, wo_ref[...])

    h2 = _norm_mod(x1, n2_ref[...], sh2_ref[0], sc2_ref[0])
    _to_token_tiles(h_ref, h2, tm)
    _to_token_tiles(xt_ref, x1, tm)
    _route_tokens(h2, rwt_ref, rb_ref, e_ref, w_ref, rank_ref, cnt_ref, carry, tm)


def _stage_merge(x, yf, yb, g, bonus, mods, consts, tm):
    b, t, d = x.shape
    n = b * t
    w = RWKV_WIDTH
    per = t // tm
    assert d == 8 * 128, "token-tile layout stores one (8, 128) tile per token"
    mod = pl.BlockSpec((1, 1, d), lambda i, j: (i, 0, 0))
    tok = pl.BlockSpec((1, tm, w), lambda i, j: (i, j, 0))
    tiles = pl.BlockSpec((tm * 8, 128), lambda i, j: (i * per + j, 0))
    lane = pl.BlockSpec((TOP_K, tm), lambda i, j: (0, i * per + j))
    return pl.pallas_call(
        functools.partial(_merge_kernel, tm=tm),
        grid=(b, per),
        in_specs=[pl.BlockSpec((1, tm, d), lambda i, j: (i, j, 0)), tok, tok, tok, tok]
        + [mod] * len(mods) + [_full(a.shape) for a in consts],
        out_specs=[tiles, tiles, lane, lane, lane, _full((N_EXPERTS, 128))],
        out_shape=[jax.ShapeDtypeStruct((n * 8, 128), F32),
                   jax.ShapeDtypeStruct((n * 8, 128), F32),
                   jax.ShapeDtypeStruct((TOP_K, n), jnp.int32),
                   jax.ShapeDtypeStruct((TOP_K, n), F32),
                   jax.ShapeDtypeStruct((TOP_K, n), jnp.int32),
                   jax.ShapeDtypeStruct((N_EXPERTS, 128), jnp.int32)],
        scratch_shapes=[pltpu.VMEM((N_EXPERTS, 1), F32)],
        compiler_params=_params("arbitrary", "arbitrary"),
        name="merge",
    )(x, yf, yb, g, bonus, *mods, *consts)


def _expert_kernel(be_ref, nu_ref, nx_ref, x_ref, bg_ref, bu_ref, bd_ref, wg_hbm, wu_hbm, wd_hbm,
                   o_ref, stage, wg_s, wu_s, wd_s, sem):
    i = pl.program_id(0)
    e = be_ref[i]
    used = i < nu_ref[0]
    prev = be_ref[jnp.maximum(i - 1, 0)]

    def fetch(expert):
        return [pltpu.make_async_copy(w.at[expert], stage.at[j], sem.at[j])
                for j, w in enumerate((wg_hbm, wu_hbm, wd_hbm))]

    @pl.when(i == 0)
    def _():
        for cp in fetch(e):
            cp.start()

    @pl.when(used & ((i == 0) | (e != prev)))
    def _():
        for cp, w_s, j in zip(fetch(e), (wg_s, wu_s, wd_s), range(3)):
            cp.wait()
            w_s[...] = stage[j].astype(BF16)
        nxt = nx_ref[e]

        @pl.when(nxt >= 0)
        def _():
            for cp in fetch(nxt):
                cp.start()

    @pl.when(used)
    def _():
        x = _from_token_tiles(x_ref, EXPERT_ROWS)
        gate = _mm(x, wg_s[...]) + bg_ref[0]
        up = _mm(x, wu_s[...]) + bu_ref[0]
        gate = jnp.minimum(gate, SWIGLU_LIMIT)
        up = jnp.clip(up, -SWIGLU_LIMIT, SWIGLU_LIMIT)
        act = gate * jax.nn.sigmoid(SWIGLU_ALPHA * gate) * (up + 1.0)
        _to_token_tiles(o_ref, _mm(act, wd_s[...]) + bd_ref[0], EXPERT_ROWS)

    @pl.when(jnp.logical_not(used))
    def _():
        o_ref[...] = jnp.zeros_like(o_ref)


def _stage_experts(block_e, n_used, next_e, xs, wg, bg, wu, bu, wd, bd):
    n_rows = xs.shape[0] // 8
    d, f = wg.shape[1:]
    assert d == f, "one staging buffer shape serves all three weight matrices"
    bm = EXPERT_ROWS
    bspec = lambda n_: pl.BlockSpec((1, 1, n_), lambda i, be, *_: (be[i], 0, 0))
    hbm = pl.BlockSpec(memory_space=pl.ANY)
    return pl.pallas_call(
        _expert_kernel,
        grid_spec=pltpu.PrefetchScalarGridSpec(
            num_scalar_prefetch=3,
            grid=(n_rows // bm,),
            in_specs=[pl.BlockSpec((bm * 8, 128),
                                   lambda i, be, nu, *_: (jnp.clip(nu[0] - 1, 0, i), 0)),
                      bspec(f), bspec(f), bspec(d), hbm, hbm, hbm],
            out_specs=pl.BlockSpec((bm * 8, 128), lambda i, *_: (i, 0)),
            scratch_shapes=[pltpu.VMEM((3, d, f), F32), pltpu.VMEM((d, f), BF16),
                            pltpu.VMEM((d, f), BF16), pltpu.VMEM((f, d), BF16),
                            pltpu.SemaphoreType.DMA((3,))]),
        out_shape=jax.ShapeDtypeStruct((n_rows * 8, 128), F32),
        compiler_params=_params("arbitrary"),
        name="experts",
    )(block_e, n_used, next_e, xs, bg, bu, bd, wg, wu, wd)


def _tile_copy(src_ref, src_row, dst_ref, dst_row, sem):
    src = src_ref.at[pl.ds(pl.multiple_of(src_row * 8, 8), 8)]
    dst = dst_ref.at[pl.ds(pl.multiple_of(dst_row * 8, 8), 8)]
    return pltpu.make_async_copy(src, dst, sem)


def _dispatch_kernel(dest_ref, zstart_ref, zon_ref, h_ref, xs_ref, zeros, sem, zsem, *, tm, n_tok):
    i = pl.program_id(0)
    bm8 = EXPERT_ROWS * 8

    def zero_fill(e):
        dst = xs_ref.at[pl.ds(pl.multiple_of(zstart_ref[e] * 8, 8), bm8)]
        return pltpu.make_async_copy(zeros, dst, zsem)

    @pl.when(i == 0)
    def _():
        zeros[...] = jnp.zeros_like(zeros)
        for e in range(zon_ref.shape[0]):
            @pl.when(zon_ref[e] > 0)
            def _():
                zero_fill(e).start()
        for e in range(zon_ref.shape[0]):
            @pl.when(zon_ref[e] > 0)
            def _():
                zero_fill(e).wait()

    def body(t, carry):
        for kk in range(TOP_K):
            _tile_copy(h_ref, t, xs_ref, dest_ref[kk * n_tok + i * tm + t], sem).start(
                priority=kk % 2)
        return carry

    lax.fori_loop(0, tm, body, 0)
    for _ in range(TOP_K):
        pltpu.make_async_copy(h_ref, xs_ref.at[pl.ds(0, tm * 8)], sem).wait()


def _stage_dispatch(dest_flat, zstart, zon, h2t, n_rows, tm):
    n = h2t.shape[0] // 8
    return pl.pallas_call(
        functools.partial(_dispatch_kernel, tm=tm, n_tok=n),
        grid_spec=pltpu.PrefetchScalarGridSpec(
            num_scalar_prefetch=3,
            grid=(n // tm,),
            in_specs=[pl.BlockSpec((tm * 8, 128), lambda i, *_: (i, 0))],
            out_specs=pl.BlockSpec(memory_space=pl.ANY),
            scratch_shapes=[pltpu.VMEM((EXPERT_ROWS * 8, 128), F32),
                            pltpu.SemaphoreType.DMA(()), pltpu.SemaphoreType.DMA(())]),
        out_shape=jax.ShapeDtypeStruct((n_rows * 8, 128), F32),
        compiler_params=_params("arbitrary"),
        name="dispatch",
    )(dest_flat, zstart, zon, h2t)


def _combine_kernel(dest_ref, w_ref, xt_ref, ga_ref, g_ref, yb_ref, o_ref, buf, res, sem,
                    *, tm, n_tok):
    i = pl.program_id(0)
    ga = ga_ref[0]

    def issue(tile, slot, t):
        for kk in range(TOP_K):
            _tile_copy(yb_ref, dest_ref[kk * n_tok + tile * tm + t], buf.at[slot, kk], t,
                       sem.at[slot]).start(priority=kk % 2)

    def combine(slot, t):
        rows = pl.ds(pl.multiple_of(t * 8, 8), 8)
        acc = buf[slot, 0, rows, :] * w_ref[i * tm + t]
        for kk in range(1, TOP_K):
            acc = acc + buf[slot, kk, rows, :] * w_ref[kk * n_tok + i * tm + t]
        res[rows, :] = xt_ref[rows, :] + ga * acc

    def loop(body):
        lax.fori_loop(0, tm, lambda t, carry: (body(t), carry)[1], 0, unroll=4)

    @pl.when(i == 0)
    def _():
        loop(lambda t: issue(0, 0, t))

    for slot in range(2):
        @pl.when(i % 2 == slot)
        def _():
            for kk in range(TOP_K):
                pltpu.make_async_copy(buf.at[slot, kk], buf.at[slot, kk], sem.at[slot]).wait()

            @pl.when(i + 1 < pl.num_programs(0))
            def _():
                loop(lambda t: (issue(i + 1, 1 - slot, t), combine(slot, t)))

            @pl.when(i + 1 >= pl.num_programs(0))
            def _():
                loop(lambda t: combine(slot, t))

    x = _from_token_tiles(res, tm)
    o_ref[...] = x * lax.rsqrt(jnp.mean(x * x, axis=-1, keepdims=True) + RMS_EPS) * g_ref[...]


def _stage_combine(dest_flat, w_flat, x1t, ga_t, g_t, yb, tokens_per_batch, tm):
    n = x1t.shape[0] // 8
    d = 8 * 128
    per = tokens_per_batch // tm
    return pl.pallas_call(
        functools.partial(_combine_kernel, tm=tm, n_tok=n),
        grid_spec=pltpu.PrefetchScalarGridSpec(
            num_scalar_prefetch=2,
            grid=(n // tm,),
            in_specs=[pl.BlockSpec((tm * 8, 128), lambda i, *_: (i, 0)),
                      pl.BlockSpec((1, 8, 128), lambda i, *_: (i // per, 0, 0)),
                      pl.BlockSpec((1, d), lambda i, *_: (0, 0)),
                      pl.BlockSpec(memory_space=pl.ANY)],
            out_specs=pl.BlockSpec((tm, d), lambda i, *_: (i, 0)),
            scratch_shapes=[pltpu.VMEM((2, TOP_K, tm * 8, 128), F32),
                            pltpu.VMEM((tm * 8, 128), F32),
                            pltpu.SemaphoreType.DMA((2,))]),
        out_shape=jax.ShapeDtypeStruct((n, d), F32),
        compiler_params=_params("arbitrary"),
        name="combine",
    )(dest_flat, w_flat, x1t, ga_t, g_t, yb)


def _rwkv_branch(x, ctx, mods_lat, mods_ctx, n1, w_rwkv, prm):
    r, v, a, _, _, k, bb, lw = _stage_prep(ctx, *mods_ctx, n1, w_rwkv, prm, tm=ctx.shape[1],
                                           grid_shift=False)
    (z_ctx,) = _stage_wkv(r, v, a, k, bb, lw, emit=False)
    r, v, a, g, bonus, k, bb, lw = _stage_prep(x, *mods_lat, n1, w_rwkv, prm, tm=256,
                                               grid_shift=True)
    y_f, y_b = _stage_wkv(r, v, a, k, bb, lw, z0=z_ctx, emit=True)
    return y_f, y_b, g, bonus


def _route(top_e, rank, counts, n_tok):
    bm = EXPERT_ROWS
    n_rows = n_tok * TOP_K + N_EXPERTS * bm
    padded = (counts + bm - 1) // bm * bm
    pad_end = jnp.cumsum(padded)
    pad_start = pad_end - padded
    experts = jnp.arange(N_EXPERTS, dtype=jnp.int32)
    start_of = jnp.sum(jnp.where(top_e[..., None] == experts, pad_start, 0), axis=-1)
    dest = (start_of + rank).astype(jnp.int32).reshape(-1)
    block_start = jnp.arange(n_rows // bm, dtype=jnp.int32) * bm
    block_e = jnp.minimum(jnp.sum(pad_end[None, :] <= block_start[:, None], axis=1),
                          N_EXPERTS - 1).astype(jnp.int32)
    n_used = (pad_end[-1] // bm).astype(jnp.int32).reshape(1)
    tail = n_used[0] + experts
    zstart = jnp.concatenate([jnp.maximum(pad_end - bm, 0),
                              jnp.minimum(tail, n_rows // bm - 1) * bm]).astype(jnp.int32)
    zon = jnp.concatenate([padded > 0, tail < n_rows // bm]).astype(jnp.int32)
    first_at = lax.cummin(jnp.where(padded > 0, experts, N_EXPERTS), axis=0, reverse=True)
    next_e = jnp.concatenate([first_at[1:], jnp.full((1,), N_EXPERTS, jnp.int32)])
    next_e = jnp.where(next_e < N_EXPERTS, next_e, -1).astype(jnp.int32)
    return dest, block_e, n_used, next_e, zstart, zon, n_rows


def kernel(x, c, ctx, c_ctx, w_ada, b_ada, norm1_g, w_in, shift_mu, decay_w0, decay_lora_b,
           iclr_a0, iclr_lora_b, gate_lora_b, k_k, k_a, r_k, gn_w, gn_b, w_out_rwkv,
           sgu_ln_w, sgu_ln_b, sgu_w_spatial, sgu_b_spatial, w_out_sgu, w_o, norm2_g,
           router_w, router_b, exp_w_gate, exp_b_gate, exp_w_up, exp_b_up, exp_w_down,
           exp_b_down, final_norm_g):
    assert w_ada.shape[0] == 1, "single-layer problem"
    b, t, d = x.shape
    n_tok = b * t
    w = RWKV_WIDTH
    row = lambda a: a.reshape(1, -1)

    cs = jnp.zeros((8, d), F32).at[:b].set(c).at[b].set(c_ctx)
    mod = _stage_mods(cs, w_ada[0], row(b_ada[0]))
    sh1, sc1, ga1, sh2, sc2, ga2 = [m[:b, None, :] for m in jnp.split(mod, 6, axis=-1)]
    csh1, csc1 = [jnp.broadcast_to(m[b][None, None, :], (b, 1, d))
                  for m in jnp.split(mod, 6, axis=-1)[:2]]

    w_in_bf = w_in[0].astype(BF16)
    n1 = row(norm1_g[0])
    head_id = jnp.arange(w, dtype=jnp.int32) // HEAD_DIM
    headsum = (head_id[:, None] == head_id[None, :]).astype(BF16)
    prm = [row(shift_mu[0]), row(k_k[0]), row(k_a[0]), row(r_k[0]), decay_w0[0],
           decay_lora_b[0], iclr_a0[0], iclr_lora_b[0], gate_lora_b[0], headsum]
    y_f, y_b, g, bonus = _rwkv_branch(x, ctx, (sh1, sc1), (csh1, csc1), n1,
                                      w_in_bf[:, :RWKV_COLS], prm)

    bsp = jnp.repeat(sgu_b_spatial[0].T, SGU_WIDTH // SGU_GROUPS, axis=1)
    consts = [n1, w_in_bf[:, RWKV_COLS:], row(gn_w[0]), row(gn_b[0]), headsum,
              w_out_rwkv[0].astype(BF16), row(sgu_ln_w[0]), row(sgu_ln_b[0]),
              sgu_w_spatial[0].astype(BF16), bsp, w_out_sgu[0].astype(BF16),
              w_o[0].astype(BF16), row(norm2_g[0]), router_w[0].T, router_b[0].reshape(-1, 1)]
    h2t, x1t, top_e, top_w, rank, counts = _stage_merge(
        x, y_f, y_b, g, bonus, (sh1, sc1, ga1, sh2, sc2), consts, tm=256)
    dest, block_e, n_used, next_e, zstart, zon, n_rows = _route(top_e, rank, counts[:, 0], n_tok)
    xs = _stage_dispatch(dest, zstart, zon, h2t, n_rows, tm=512)
    e3 = lambda a: a.reshape(N_EXPERTS, 1, -1)
    yb = _stage_experts(block_e, n_used, next_e, xs, exp_w_gate[0], e3(exp_b_gate[0]), exp_w_up[0],
                        e3(exp_b_up[0]), exp_w_down[0], e3(exp_b_down[0]))
    out = _stage_combine(dest, top_w.reshape(-1), x1t, ga2.reshape(b, 8, 128),
                         row(final_norm_g), yb, t, tm=256)
    return out.reshape(b, t, d)
```

```python
import functools
import math

import jax
import jax.numpy as jnp
from jax import lax
from jax.experimental import pallas as pl
from jax.experimental.pallas import tpu as pltpu

F32 = jnp.float32
BF16 = jnp.bfloat16
HIGHEST = lax.Precision.HIGHEST

D_MODEL = 1024
GRID_W = 64
RWKV_HEADS = 8
HEAD_DIM = 64
RWKV_WIDTH = RWKV_HEADS * HEAD_DIM
DECAY_LORA = 64
ICLR_LORA = 64
GATE_LORA = 128
RWKV_COLS = 3 * RWKV_WIDTH + 2 * DECAY_LORA + 2 * ICLR_LORA + GATE_LORA
SGU_WIDTH = 512
SGU_GROUPS = 8
SGU_CHUNK = 128
N_EXPERTS = 32
TOP_K = 4
SWIGLU_LIMIT = 7.0
SWIGLU_ALPHA = 1.702
RMS_EPS = 1e-6
LN_EPS = 1e-5
GN_EPS = 64e-5

WKV_CHUNK = 64
WKV_GROUP = 4
WKV_CHUNKS_PER_STEP = 1
EXPERT_ROWS = 512
VMEM_LIMIT = 48 * 1024 * 1024


def _params(*sem):
    return pltpu.CompilerParams(dimension_semantics=sem, vmem_limit_bytes=VMEM_LIMIT)


def _mm(a, b, dims=((1,), (0,)), exact=False):
    dn = (dims, ((), ()))
    if exact:
        return lax.dot_general(a, b, dn, precision=HIGHEST, preferred_element_type=F32)
    return lax.dot_general(a.astype(BF16), b.astype(BF16), dn, preferred_element_type=F32)


def _split3(x):
    hi = x.astype(BF16)
    r1 = x - hi.astype(F32)
    mid = r1.astype(BF16)
    lo = (r1 - mid.astype(F32)).astype(BF16)
    return hi, mid, lo


def _mm_hi_lo(a, b, dims):
    a_hi = a.astype(BF16)
    b_hi = b.astype(BF16)
    a_lo = a - a_hi.astype(F32)
    b_lo = b - b_hi.astype(F32)
    return _mm(a_hi, b_hi, dims) + _mm(a_hi, b_lo, dims) + _mm(a_lo, b_hi, dims)


def _mm_sel_x(sel, x):
    return sum(_mm(sel, p) for p in _split3(x))


def _full(shape):
    n = len(shape)
    return pl.BlockSpec(shape, lambda *_: (0,) * n)


def _norm_mod(x, g, shift, scale):
    y = x * lax.rsqrt(jnp.mean(x * x, axis=-1, keepdims=True) + RMS_EPS) * g
    return y * (1.0 + scale) + shift


def _mods_kernel(c_ref, w_ref, b_ref, o_ref):
    c = c_ref[...]
    s = c * jax.nn.sigmoid(c)
    o_ref[...] = _mm(s, w_ref[...], exact=True) + b_ref[...]


def _stage_mods(cs, w_ada, b_ada):
    rows, d = cs.shape
    n = w_ada.shape[1]
    tn = 1536
    return pl.pallas_call(
        _mods_kernel,
        grid=(n // tn,),
        in_specs=[_full((rows, d)),
                  pl.BlockSpec((d, tn), lambda j: (0, j)),
                  pl.BlockSpec((1, tn), lambda j: (0, j))],
        out_specs=pl.BlockSpec((rows, tn), lambda j: (0, j)),
        out_shape=jax.ShapeDtypeStruct((rows, n), F32),
        compiler_params=_params("arbitrary"),
        name="mods",
    )(cs, w_ada, b_ada)


def _rwkv_feats(p, mu_kk, mu_ka, r_k, w0, dlb, a0, ilb, glb, headsum):
    w = RWKV_WIDTH
    r = p[:, 0:w]
    k = p[:, w:2 * w]
    v = p[:, 2 * w:3 * w]
    o = 3 * w
    wd = (p[:, o:o + DECAY_LORA], p[:, o + DECAY_LORA:o + 2 * DECAY_LORA])
    o += 2 * DECAY_LORA
    ad = (p[:, o:o + ICLR_LORA], p[:, o + ICLR_LORA:o + 2 * ICLR_LORA])
    o += 2 * ICLR_LORA
    gd = p[:, o:o + GATE_LORA]

    kk = k * mu_kk
    kk = kk * lax.rsqrt(_mm(kk * kk, headsum) + 1e-12)
    g = _mm(jax.nn.sigmoid(gd), glb)
    ks, bs, lws = [], [], []
    ksum = None
    for d in range(2):
        z = w0[d:d + 1] + _mm(jnp.tanh(wd[d]), dlb[d])
        lws.append(-math.exp(-0.5) * jax.nn.sigmoid(z))
        ic = jax.nn.sigmoid(a0[d:d + 1] + _mm(ad[d], ilb[d]))
        kd = k * (1.0 + (ic - 1.0) * mu_ka)
        ks.append(kd)
        bs.append(kk * ic)
        ksum = kd if ksum is None else ksum + kd
    bonus = _mm(r * ksum * r_k, headsum) * v
    return r, v, -kk, g, bonus, ks, bs, lws


def _prep_kernel(xm_ref, xp_ref, xn_ref, sh_ref, sc_ref, n1_ref, w_ref,
                 mu_ref, kk_ref, ka_ref, rk_ref, w0_ref, dlb_ref, a0_ref, ilb_ref, glb_ref, hs_ref,
                 r_ref, v_ref, a_ref, g_ref, bon_ref, k_ref, b_ref, lw_ref, *, tm, grid_shift):
    project = lambda x: _mm(_norm_mod(x, n1_ref[...], sh_ref[0], sc_ref[0]), w_ref[...])
    lane = lax.broadcasted_iota(jnp.int32, (1, RWKV_COLS), 1)
    if grid_shift:
        ext = project(jnp.concatenate([xp_ref[0], xm_ref[0], xn_ref[0]], axis=0))
        main = ext[GRID_W:GRID_W + tm]
        t = pl.program_id(1) * tm + lax.broadcasted_iota(jnp.int32, (tm, 1), 0)
        col = t & (GRID_W - 1)
        row = t >> (GRID_W.bit_length() - 1)
        n_rows = pl.num_programs(1) * tm // GRID_W
        left = jnp.where(col > 0, ext[GRID_W - 1:GRID_W - 1 + tm], 0.0)
        right = jnp.where(col < GRID_W - 1, ext[GRID_W + 1:GRID_W + 1 + tm], 0.0)
        up = jnp.where(row > 0, ext[0:tm], 0.0)
        down = jnp.where(row < n_rows - 1, ext[2 * GRID_W:2 * GRID_W + tm], 0.0)
        cm = lane & 3
        shifted = jnp.where(cm == 0, left, jnp.where(cm == 1, right, jnp.where(cm == 2, up, down)))
    else:
        main = project(xm_ref[0])
        zero = jnp.zeros((1, RWKV_COLS), F32)
        prev = jnp.concatenate([zero, main[:tm - 1]], axis=0)
        nxt = jnp.concatenate([main[1:], zero], axis=0)
        shifted = jnp.where((lane & 1) == 0, prev, nxt)
    p = main + mu_ref[...] * (shifted - main)
    r, v, a, g, bonus, ks, bs, lws = _rwkv_feats(
        p, kk_ref[...], ka_ref[...], rk_ref[...], w0_ref[...], dlb_ref, a0_ref[...], ilb_ref,
        glb_ref[...], hs_ref[...])
    r_ref[0] = r.astype(BF16)
    v_ref[0] = v.astype(BF16)
    a_ref[0] = a.astype(BF16)
    g_ref[0] = g.astype(BF16)
    bon_ref[0] = bonus.astype(BF16)
    for d in range(2):
        k_ref[d, 0] = ks[d].astype(BF16)
        b_ref[d, 0] = bs[d].astype(BF16)
        lw_ref[d, 0] = lws[d]


def _stage_prep(x, shift, scale, g, w_bf16, prm, tm, grid_shift):
    b, t, c = x.shape
    w = RWKV_WIDTH
    mod = pl.BlockSpec((1, 1, c), lambda i, j: (i, 0, 0))
    hb = GRID_W if grid_shift else 8
    per = tm // hb
    last = t // hb - 1
    tok = pl.BlockSpec((1, tm, w), lambda i, j: (i, j, 0))
    tok2 = pl.BlockSpec((2, 1, tm, w), lambda i, j: (0, i, j, 0))
    in_specs = [
        pl.BlockSpec((1, tm, c), lambda i, j: (i, j, 0)),
        pl.BlockSpec((1, hb, c), lambda i, j: (i, jnp.maximum(j * per - 1, 0), 0)),
        pl.BlockSpec((1, hb, c), lambda i, j: (i, jnp.minimum((j + 1) * per, last), 0)),
        mod, mod, _full(g.shape), _full(w_bf16.shape),
    ] + [_full(a.shape) for a in prm]
    return pl.pallas_call(
        functools.partial(_prep_kernel, tm=tm, grid_shift=grid_shift),
        grid=(b, t // tm),
        in_specs=in_specs,
        out_specs=[tok] * 5 + [tok2] * 3,
        out_shape=[jax.ShapeDtypeStruct((b, t, w), BF16)] * 5
        + [jax.ShapeDtypeStruct((2, b, t, w), BF16)] * 2
        + [jax.ShapeDtypeStruct((2, b, t, w), F32)],
        compiler_params=_params("arbitrary", "arbitrary"),
        name="prep_lat" if grid_shift else "prep_ctx",
    )(x, x, x, shift, scale, g, w_bf16, *prm)


def _wkv_prepare(d, r, v, a, k, bb, lw):
    c = WKV_CHUNK
    n = HEAD_DIM
    ii = lax.broadcasted_iota(jnp.int32, (c, c), 0)
    jj = lax.broadcasted_iota(jnp.int32, (c, c), 1)
    incl = (jj <= ii) if d == 0 else (jj >= ii)
    lc = _mm_sel_x(incl.astype(BF16), lw)
    lx = lc - lw
    ltot = lc[c - 1:c] if d == 0 else lc[0:1]
    inv = jnp.exp(-lc)
    tail = jnp.exp(ltot - lc)
    etot = jnp.exp(ltot)
    eye_n = (lax.broadcasted_iota(jnp.int32, (n, n), 0)
             == lax.broadcasted_iota(jnp.int32, (n, n), 1)).astype(F32)
    scale = jnp.concatenate(
        [jnp.broadcast_to(jnp.sum(eye_n * etot[:, h * n:(h + 1) * n], axis=1, keepdims=True), (n, n))
         for h in range(RWKV_HEADS)], axis=1)
    bf = lambda x: x.astype(BF16)
    return (bf(a * jnp.exp(lx)), bf(r * jnp.exp(lc)), bf(k * inv), bf(bb * inv), bf(k * tail),
            bf(bb * tail), bf(v), scale)


def _wkv_consts():
    c = WKV_CHUNK
    n = HEAD_DIM
    gw = WKV_GROUP * n
    assert c == n and n & (n - 1) == 0
    ii = lax.broadcasted_iota(jnp.int32, (c, gw), 0)
    jj = lax.broadcasted_iota(jnp.int32, (c, gw), 1) & (c - 1)
    head_shift = n.bit_length() - 1
    same_head = (lax.broadcasted_iota(jnp.int32, (gw, gw), 0) >> head_shift
                 == lax.broadcasted_iota(jnp.int32, (gw, gw), 1) >> head_shift)

    def bd(x):
        x = x.astype(BF16)
        return jnp.where(same_head, jnp.concatenate([x] * WKV_GROUP, axis=0), jnp.zeros((), BF16))

    return ii, jj, bd


def _wkv_factor(chunks, emit):
    c = WKV_CHUNK
    gw = WKV_GROUP * HEAD_DIM
    n_groups = RWKV_WIDTH // gw
    ii, jj, bd = _wkv_consts()
    eye = (ii == jj).astype(F32)
    nt = ((1,), (1,))
    tn = ((0,), (0,))
    sl = [slice(h * HEAD_DIM, (h + 1) * HEAD_DIM) for h in range(WKV_GROUP)]

    prob = []
    for preps in chunks:
        for d, (at, rt, kt, bt, kh, bh, vb, _) in enumerate(preps):
            incl = (jj <= ii) if d == 0 else (jj >= ii)
            strict = (jj < ii) if d == 0 else (jj > ii)
            for gi in range(n_groups):
                gs = slice(gi * gw, (gi + 1) * gw)
                prob.append(dict(incl=incl, strict=strict, at=at[:, gs], rt=rt[:, gs],
                                 kt=kt[:, gs], bt=bt[:, gs], kh=kh[:, gs], bh=bh[:, gs],
                                 v=vb[:, gs]))
    for p in prob:
        lhs = jnp.concatenate([p["at"], p["rt"]], axis=0) if emit else p["at"]
        p["lhs_z"] = lhs
        gk = _mm(lhs, bd(p["kt"]), nt)
        gb = _mm(lhs, bd(p["bt"]), nt)
        p["a_ab"] = jnp.where(p["strict"], gb[:c], 0.0)
        lhs_v = jnp.where(p["strict"], gk[:c], 0.0)
        if emit:
            lhs_v = jnp.concatenate([lhs_v, jnp.where(p["incl"], gk[c:], 0.0)], axis=0)
            p["a_rb"] = jnp.where(p["incl"], gb[c:], 0.0).astype(BF16)
        p["lhs_v"] = lhs_v
    for p in prob:
        p["tinv"] = eye + p["a_ab"]
        p["x"] = _mm(p["a_ab"], bd(p["a_ab"]))
        p["av"] = _mm(p["lhs_v"], bd(p["v"]))
    for level in range(1, 6):
        for p in prob:
            if level < 5:
                both = _mm(jnp.concatenate([p["x"], p["tinv"]], axis=0), bd(p["x"]))
                p["x"] = both[:c]
                p["tinv"] = p["tinv"] + both[c:]
            else:
                p["tinv"] = (p["tinv"] + _mm(p["tinv"], bd(p["x"]))).astype(BF16)
    per_chunk = 2 * n_groups
    return [prob[j * per_chunk:(j + 1) * per_chunk] for j in range(len(chunks))]


def _wkv_apply(probs, chunks, states, emit):
    c = WKV_CHUNK
    n = HEAD_DIM
    gw = WKV_GROUP * n
    n_groups = RWKV_WIDTH // gw
    _, _, bd = _wkv_consts()
    tn = ((0,), (0,))
    sl = [slice(h * n, (h + 1) * n) for h in range(WKV_GROUP)]
    flat = []
    for prob, zs in zip(probs, states):
        zb = [z.astype(BF16) for z in zs]
        for i, p in enumerate(prob):
            d, gi = divmod(i, n_groups)
            p["z"] = zb[d][:, gi * gw:(gi + 1) * gw]
            flat.append(p)
    for p in flat:
        p["zv"] = _mm(p["lhs_z"], bd(p["z"])) + p["av"]
    for p in flat:
        p["u"] = _mm(p["tinv"], bd(p["zv"][:c])).astype(BF16)
    for p in flat:
        if emit:
            p["y"] = p["zv"][c:] + _mm(p["a_rb"], bd(p["u"]))
        p["z_new"] = [_mm(jnp.concatenate([p["kh"][:, s], p["bh"][:, s]], axis=0),
                          jnp.concatenate([p["v"][:, s], p["u"][:, s]], axis=0), tn) for s in sl]
    out = []
    for prob, preps, zs in zip(probs, chunks, states):
        res = []
        for d, (prep, z) in enumerate(zip(preps, zs)):
            mine = prob[d * n_groups:(d + 1) * n_groups]
            z_cat = jnp.concatenate([m for p in mine for m in p["z_new"]], axis=1) + prep[7] * z
            y = jnp.concatenate([p["y"] for p in mine], axis=1) if emit else None
            res.append((y, z_cat))
        out.append(res)
    return out


def _wkv_kernel(*refs, emit, has_init):
    refs = list(refs)
    ins = [[refs.pop(0) for _ in range(6)] for _ in range(2)]
    z0_ref = refs.pop(0) if has_init else None
    outs = [refs.pop(0) for _ in range(2 if emit else 1)]
    z_scr = refs.pop(0)
    n_seq = z_scr.shape[1]

    @pl.when(pl.program_id(0) == 0)
    def _():
        if has_init:
            z_scr[...] = z0_ref[...]
        else:
            z_scr[...] = jnp.zeros_like(z_scr)

    c = WKV_CHUNK
    per = ins[0][0].shape[1] // c

    def rows_of(d, j):
        sub = j if d == 0 else per - 1 - j
        return slice(sub * c, (sub + 1) * c)

    chunks = []
    for j in range(per):
        for i in range(n_seq):
            preps = []
            for d in range(2):
                rows = rows_of(d, j)
                r_ref, v_ref, a_ref, k_ref, b_ref, lw_ref = ins[d]
                preps.append(_wkv_prepare(d, r_ref[i, rows], v_ref[i, rows], a_ref[i, rows],
                                          k_ref[0, i, rows], b_ref[0, i, rows], lw_ref[0, i, rows]))
            chunks.append(preps)
    probs = _wkv_factor(chunks, emit)
    states = [(z_scr[0, i], z_scr[1, i]) for i in range(n_seq)]
    for j in range(per):
        now = slice(j * n_seq, (j + 1) * n_seq)
        res = _wkv_apply(probs[now], chunks[now], states, emit)
        states = [tuple(z_new for _, z_new in seq) for seq in res]
        if emit:
            for i, seq in enumerate(res):
                for d, (y, _) in enumerate(seq):
                    outs[d][i, rows_of(d, j)] = y
    for i in range(n_seq):
        for d in range(2):
            z_scr[d, i] = states[i][d]
    if not emit:
        outs[0][...] = z_scr[...]


def _stage_wkv(r, v, a, k, bb, lw, z0=None, emit=True):
    b, t, w = r.shape
    c = WKV_CHUNK * WKV_CHUNKS_PER_STEP
    nch = t // c
    n = HEAD_DIM
    pos = (lambda s: s, lambda s: nch - 1 - s)
    in_specs, args = [], []
    for d in range(2):
        tok = pl.BlockSpec((b, c, w), lambda s, d=d: (0, pos[d](s), 0))
        tok2 = pl.BlockSpec((1, b, c, w), lambda s, d=d: (d, 0, pos[d](s), 0))
        in_specs += [tok, tok, tok, tok2, tok2, tok2]
        args += [r, v, a, k, bb, lw]
    zspec = pl.BlockSpec((2, b, n, w), lambda s: (0, 0, 0, 0))
    if z0 is not None:
        in_specs.append(zspec)
        args.append(z0)
    if emit:
        out_specs = [pl.BlockSpec((b, c, w), lambda s, d=d: (0, pos[d](s), 0)) for d in range(2)]
        out_shape = [jax.ShapeDtypeStruct((b, t, w), F32)] * 2
    else:
        out_specs = [zspec]
        out_shape = [jax.ShapeDtypeStruct((2, b, n, w), F32)]
    return pl.pallas_call(
        functools.partial(_wkv_kernel, emit=emit, has_init=z0 is not None),
        grid=(nch,),
        in_specs=in_specs,
        out_specs=out_specs,
        out_shape=out_shape,
        scratch_shapes=[pltpu.VMEM((2, b, n, w), F32)],
        compiler_params=_params("arbitrary"),
        name="wkv_lat" if emit else "wkv_ctx",
    )(*args)


def _to_token_tiles(ref, val, rows):
    for cc in range(val.shape[1] // 128):
        ref[pl.ds(cc, rows, stride=8), :] = val[:, cc * 128:(cc + 1) * 128]


def _from_token_tiles(ref, rows):
    return jnp.concatenate([ref[pl.ds(cc, rows, stride=8), :] for cc in range(8)], axis=1)


def _route_tokens(h, rwt_ref, rb_ref, e_ref, w_ref, rank_ref, cnt_ref, carry, tm):
    logits = _mm_hi_lo(rwt_ref[...], h, dims=((1,), (1,))) + rb_ref[...]
    eio = lax.broadcasted_iota(jnp.int32, (N_EXPERTS, tm), 0)
    vals, sels, idxs = [], [], []
    for _ in range(TOP_K):
        m = jnp.max(logits, axis=0, keepdims=True)
        idx = jnp.min(jnp.where(logits == m, eio, N_EXPERTS), axis=0, keepdims=True)
        sel = eio == idx
        logits = jnp.where(sel, -jnp.inf, logits)
        vals.append(m)
        sels.append(sel)
        idxs.append(idx)
        e_ref[len(vals) - 1:len(vals), :] = idx
    ex = [jnp.exp(vk - vals[0]) for vk in vals]
    tot = ex[0] + ex[1] + ex[2] + ex[3]
    for kk in range(TOP_K):
        w_ref[kk:kk + 1, :] = ex[kk] / tot
    cnt = (sels[0] | sels[1] | sels[2] | sels[3]).astype(F32)
    ti = lax.broadcasted_iota(jnp.int32, (tm, tm), 0)
    tj = lax.broadcasted_iota(jnp.int32, (tm, tm), 1)
    before = _mm(cnt, (ti < tj).astype(F32))
    base = carry[...] + before
    ranks = []
    for kk in range(TOP_K):
        ranks.append(jnp.sum(jnp.where(sels[kk], base, 0.0), axis=0,
                             keepdims=True).astype(jnp.int32))
        rank_ref[kk:kk + 1, :] = ranks[kk]
    new = carry[...] + jnp.sum(cnt, axis=1, keepdims=True)
    carry[...] = new
    cnt_ref[...] = jnp.broadcast_to(new, cnt_ref.shape).astype(jnp.int32)
    return idxs, ranks


def _merge_kernel(x_ref, yf_ref, yb_ref, g_ref, bon_ref, sh_ref, sc_ref, ga_ref, sh2_ref, sc2_ref,
                  n1_ref, w2_ref, gnw_ref, gnb_ref, hs_ref, wor_ref, lnw_ref, lnb_ref, wsp_ref,
                  bsp_ref, wos_ref, wo_ref, n2_ref, rwt_ref, rb_ref,
                  xt_ref, e_ref, w_ref, rank_ref, cnt_ref, xs_ref,
                  carry, h2buf, zeros, dest_v, dest_s, cnt_s, ssem, csem, zsem, *, tm, cap):
    step = pl.program_id(0) * pl.num_programs(1) + pl.program_id(1)
    n_steps = pl.num_programs(0) * pl.num_programs(1)
    slot = step % 2
    dump = N_EXPERTS * cap

    def dest_copy():
        return pltpu.make_async_copy(dest_v, dest_s, csem)

    def drain(which):
        for _ in range(TOP_K):
            pltpu.make_async_copy(h2buf.at[which], h2buf.at[which], ssem.at[which]).wait()

    @pl.when(step == 0)
    def _():
        carry[...] = jnp.zeros_like(carry)
        zeros[...] = jnp.zeros_like(zeros)
        h2buf[1] = jnp.zeros(h2buf.shape[1:], F32)
        dest_v[...] = dump + (lax.broadcasted_iota(jnp.int32, (TOP_K, tm), 0) * tm
                              + lax.broadcasted_iota(jnp.int32, (TOP_K, tm), 1))
        dest_copy().start()

    dest_copy().wait()

    @pl.when(step > 0)
    def _():
        drain(slot)

    for t in range(tm):
        for kk in range(TOP_K):
            _tile_copy(h2buf.at[1 - slot], t, xs_ref, dest_s[kk, t], ssem.at[1 - slot]).start(
                priority=kk % 2)

    x = x_ref[0]
    h = _norm_mod(x, n1_ref[...], sh_ref[0], sc_ref[0])
    p2 = _mm(h, w2_ref[...])

    ps = p2[:, :2 * SGU_WIDTH]
    ge = 0.5 * ps * (1.0 + lax.erf(ps * (1.0 / math.sqrt(2.0))))
    u = ge[:, :SGU_WIDTH]
    z = ge[:, SGU_WIDTH:]
    mu = jnp.mean(z, axis=-1, keepdims=True)
    zc = z - mu
    var = jnp.mean(zc * zc, axis=-1, keepdims=True)
    z = zc * lax.rsqrt(var + LN_EPS) * lnw_ref[...] + lnb_ref[...]
    gw = SGU_WIDTH // SGU_GROUPS
    rows = []
    for c in range(tm // SGU_CHUNK):
        zc = z[c * SGU_CHUNK:(c + 1) * SGU_CHUNK]
        cols = [_mm(wsp_ref[gi], zc[:, gi * gw:(gi + 1) * gw]) for gi in range(SGU_GROUPS)]
        rows.append(jnp.concatenate(cols, axis=1) + bsp_ref[...])
    s = jnp.concatenate(rows, axis=0)
    y_b = _mm(u * s, wos_ref[...])

    y = yf_ref[0] + yb_ref[0]
    hs = hs_ref[...]
    ym = _mm(y, hs) * (1.0 / HEAD_DIM)
    yc = y - ym
    yv = _mm(yc * yc, hs) * (1.0 / HEAD_DIM)
    yn = yc * lax.rsqrt(yv + GN_EPS) * gnw_ref[...] + gnb_ref[...]
    y_a = _mm((yn + bon_ref[0]) * g_ref[0], wor_ref[...])

    gates = jax.nn.sigmoid(p2[:, 2 * SGU_WIDTH:])
    mix = gates[:, :D_MODEL] * y_a + gates[:, D_MODEL:] * y_b
    x1 = x + ga_ref[0] * _mm(mix, wo_ref[...])

    h2 = _norm_mod(x1, n2_ref[...], sh2_ref[0], sc2_ref[0])
    _to_token_tiles(h2buf.at[slot], h2, tm)
    _to_token_tiles(xt_ref, x1, tm)
    idxs, ranks = _route_tokens(h2, rwt_ref, rb_ref, e_ref, w_ref, rank_ref, cnt_ref, carry, tm)
    for kk in range(TOP_K):
        dest_v[kk:kk + 1, :] = idxs[kk] * cap + ranks[kk]
    dest_copy().start()

    @pl.when(step == n_steps - 1)
    def _():
        dest_copy().wait()

        def last(t, c):
            for kk in range(TOP_K):
                _tile_copy(h2buf.at[slot], t, xs_ref, dest_s[kk, t], ssem.at[slot]).start(
                    priority=kk % 2)
            return c

        lax.fori_loop(0, tm, last, 0)
        counts = pltpu.make_async_copy(cnt_ref, cnt_s, csem)
        counts.start()
        counts.wait()
        for e in range(N_EXPERTS):
            first = pl.multiple_of((e * cap + cnt_s[e, 0]) * 8, 8)
            pltpu.make_async_copy(zeros, xs_ref.at[pl.ds(first, zeros.shape[0])], zsem).start()
        for e in range(N_EXPERTS):
            pltpu.make_async_copy(zeros, xs_ref.at[pl.ds(0, zeros.shape[0])], zsem).wait()
        drain(1 - slot)
        drain(slot)


def _stage_merge(x, yf, yb, g, bonus, mods, consts, tm):
    b, t, d = x.shape
    n = b * t
    w = RWKV_WIDTH
    per = t // tm
    cap = n + EXPERT_ROWS
    assert d == 8 * 128, "token-tile layout stores one (8, 128) tile per token"
    mod = pl.BlockSpec((1, 1, d), lambda i, j: (i, 0, 0))
    tok = pl.BlockSpec((1, tm, w), lambda i, j: (i, j, 0))
    tiles = pl.BlockSpec((tm * 8, 128), lambda i, j: (i * per + j, 0))
    lane = pl.BlockSpec((TOP_K, tm), lambda i, j: (0, i * per + j))
    return pl.pallas_call(
        functools.partial(_merge_kernel, tm=tm, cap=cap),
        grid=(b, per),
        in_specs=[pl.BlockSpec((1, tm, d), lambda i, j: (i, j, 0)), tok, tok, tok, tok]
        + [mod] * len(mods) + [_full(a.shape) for a in consts],
        out_specs=[tiles, lane, lane, lane, _full((N_EXPERTS, 128)),
                   pl.BlockSpec(memory_space=pl.ANY)],
        out_shape=[jax.ShapeDtypeStruct((n * 8, 128), F32),
                   jax.ShapeDtypeStruct((TOP_K, n), jnp.int32),
                   jax.ShapeDtypeStruct((TOP_K, n), F32),
                   jax.ShapeDtypeStruct((TOP_K, n), jnp.int32),
                   jax.ShapeDtypeStruct((N_EXPERTS, 128), jnp.int32),
                   jax.ShapeDtypeStruct(((N_EXPERTS * cap + TOP_K * tm) * 8, 128), F32)],
        scratch_shapes=[pltpu.VMEM((N_EXPERTS, 1), F32),
                        pltpu.VMEM((2, tm * 8, 128), F32),
                        pltpu.VMEM((EXPERT_ROWS * 8, 128), F32),
                        pltpu.VMEM((TOP_K, tm), jnp.int32),
                        pltpu.SMEM((TOP_K, tm), jnp.int32),
                        pltpu.SMEM((N_EXPERTS, 128), jnp.int32),
                        pltpu.SemaphoreType.DMA((2,)), pltpu.SemaphoreType.DMA(()),
                        pltpu.SemaphoreType.DMA(())],
        compiler_params=_params("arbitrary", "arbitrary"),
        name="merge",
    )(x, yf, yb, g, bonus, *mods, *consts), cap


def _expert_kernel(be_ref, nu_ref, nx_ref, xb_ref, x_ref, bg_ref, bu_ref, bd_ref, wg_hbm, wu_hbm, wd_hbm,
                   o_ref, stage, wg_s, wu_s, wd_s, sem):
    i = pl.program_id(0)
    e = be_ref[i]
    used = i < nu_ref[0]
    prev = be_ref[jnp.maximum(i - 1, 0)]

    def fetch(expert):
        return [pltpu.make_async_copy(w.at[expert], stage.at[j], sem.at[j])
                for j, w in enumerate((wg_hbm, wu_hbm, wd_hbm))]

    @pl.when(i == 0)
    def _():
        for cp in fetch(e):
            cp.start()

    @pl.when(used & ((i == 0) | (e != prev)))
    def _():
        for cp, w_s, j in zip(fetch(e), (wg_s, wu_s, wd_s), range(3)):
            cp.wait()
            w_s[...] = stage[j].astype(BF16)
        nxt = nx_ref[e]

        @pl.when(nxt >= 0)
        def _():
            for cp in fetch(nxt):
                cp.start()

    @pl.when(used)
    def _():
        x = _from_token_tiles(x_ref, EXPERT_ROWS)
        gate = _mm(x, wg_s[...]) + bg_ref[0]
        up = _mm(x, wu_s[...]) + bu_ref[0]
        gate = jnp.minimum(gate, SWIGLU_LIMIT)
        up = jnp.clip(up, -SWIGLU_LIMIT, SWIGLU_LIMIT)
        act = gate * jax.nn.sigmoid(SWIGLU_ALPHA * gate) * (up + 1.0)
        _to_token_tiles(o_ref, _mm(act, wd_s[...]) + bd_ref[0], EXPERT_ROWS)

    @pl.when(jnp.logical_not(used))
    def _():
        o_ref[...] = jnp.zeros_like(o_ref)


def _stage_experts(block_e, n_used, next_e, x_block, n_rows, xs, wg, bg, wu, bu, wd, bd):
    d, f = wg.shape[1:]
    assert d == f, "one staging buffer shape serves all three weight matrices"
    bm = EXPERT_ROWS
    bspec = lambda n_: pl.BlockSpec((1, 1, n_), lambda i, be, *_: (be[i], 0, 0))
    hbm = pl.BlockSpec(memory_space=pl.ANY)
    return pl.pallas_call(
        _expert_kernel,
        grid_spec=pltpu.PrefetchScalarGridSpec(
            num_scalar_prefetch=4,
            grid=(n_rows // bm,),
            in_specs=[pl.BlockSpec((bm * 8, 128), lambda i, be, nu, nx, xb: (xb[i], 0)),
                      bspec(f), bspec(f), bspec(d), hbm, hbm, hbm],
            out_specs=pl.BlockSpec((bm * 8, 128), lambda i, *_: (i, 0)),
            scratch_shapes=[pltpu.VMEM((3, d, f), F32), pltpu.VMEM((d, f), BF16),
                            pltpu.VMEM((d, f), BF16), pltpu.VMEM((f, d), BF16),
                            pltpu.SemaphoreType.DMA((3,))]),
        out_shape=jax.ShapeDtypeStruct((n_rows * 8, 128), F32),
        compiler_params=_params("arbitrary"),
        name="experts",
    )(block_e, n_used, next_e, x_block, xs, bg, bu, bd, wg, wu, wd)


def _tile_copy(src_ref, src_row, dst_ref, dst_row, sem):
    src = src_ref.at[pl.ds(pl.multiple_of(src_row * 8, 8), 8)]
    dst = dst_ref.at[pl.ds(pl.multiple_of(dst_row * 8, 8), 8)]
    return pltpu.make_async_copy(src, dst, sem)


def _combine_kernel(dest_ref, w_ref, xt_ref, ga_ref, g_ref, yb_ref, o_ref, buf0, buf1, res, sem,
                    *, tm, n_tok):
    i = pl.program_id(0)
    ga = ga_ref[0]
    bufs = (buf0, buf1)

    def issue(tile, slot, t):
        for kk in range(TOP_K):
            _tile_copy(yb_ref, dest_ref[kk * n_tok + tile * tm + t], bufs[slot].at[kk], t,
                       sem.at[slot]).start(priority=kk % 2)

    def combine(slot, t):
        rows = pl.ds(pl.multiple_of(t * 8, 8), 8)
        acc = bufs[slot][0, rows, :] * w_ref[i * tm + t]
        for kk in range(1, TOP_K):
            acc = acc + bufs[slot][kk, rows, :] * w_ref[kk * n_tok + i * tm + t]
        res[rows, :] = xt_ref[rows, :] + ga * acc

    def loop(body):
        lax.fori_loop(0, tm, lambda t, carry: (body(t), carry)[1], 0, unroll=16)

    @pl.when(i == 0)
    def _():
        loop(lambda t: issue(0, 0, t))

    for slot in range(2):
        @pl.when(i % 2 == slot)
        def _():
            for kk in range(TOP_K):
                pltpu.make_async_copy(bufs[slot].at[kk], bufs[slot].at[kk], sem.at[slot]).wait()

            @pl.when(i + 1 < pl.num_programs(0))
            def _():
                loop(lambda t: (issue(i + 1, 1 - slot, t), combine(slot, t)))

            @pl.when(i + 1 >= pl.num_programs(0))
            def _():
                loop(lambda t: combine(slot, t))

    x = _from_token_tiles(res, tm)
    o_ref[...] = x * lax.rsqrt(jnp.mean(x * x, axis=-1, keepdims=True) + RMS_EPS) * g_ref[...]


def _stage_combine(dest_flat, w_flat, x1t, ga_t, g_t, yb, tokens_per_batch, tm):
    n = x1t.shape[0] // 8
    d = 8 * 128
    per = tokens_per_batch // tm
    return pl.pallas_call(
        functools.partial(_combine_kernel, tm=tm, n_tok=n),
        grid_spec=pltpu.PrefetchScalarGridSpec(
            num_scalar_prefetch=2,
            grid=(n // tm,),
            in_specs=[pl.BlockSpec((tm * 8, 128), lambda i, *_: (i, 0)),
                      pl.BlockSpec((1, 8, 128), lambda i, *_: (i // per, 0, 0)),
                      pl.BlockSpec((1, d), lambda i, *_: (0, 0)),
                      pl.BlockSpec(memory_space=pl.ANY)],
            out_specs=pl.BlockSpec((tm, d), lambda i, *_: (i, 0)),
            scratch_shapes=[pltpu.VMEM((TOP_K, tm * 8, 128), F32),
                            pltpu.VMEM((TOP_K, tm * 8, 128), F32),
                            pltpu.VMEM((tm * 8, 128), F32),
                            pltpu.SemaphoreType.DMA((2,))]),
        out_shape=jax.ShapeDtypeStruct((n, d), F32),
        compiler_params=_params("arbitrary"),
        name="combine",
    )(dest_flat, w_flat, x1t, ga_t, g_t, yb)


def _rwkv_branch(x, ctx, mods_lat, mods_ctx, n1, w_rwkv, prm):
    r, v, a, _, _, k, bb, lw = _stage_prep(ctx, *mods_ctx, n1, w_rwkv, prm, tm=ctx.shape[1],
                                           grid_shift=False)
    (z_ctx,) = _stage_wkv(r, v, a, k, bb, lw, emit=False)
    r, v, a, g, bonus, k, bb, lw = _stage_prep(x, *mods_lat, n1, w_rwkv, prm, tm=512,
                                               grid_shift=True)
    y_f, y_b = _stage_wkv(r, v, a, k, bb, lw, z0=z_ctx, emit=True)
    return y_f, y_b, g, bonus


def _route(top_e, rank, counts, n_tok, cap):
    bm = EXPERT_ROWS
    n_rows = n_tok * TOP_K + N_EXPERTS * bm
    n_blocks = n_rows // bm
    padded = (counts + bm - 1) // bm * bm
    pad_end = jnp.cumsum(padded)
    pad_start = pad_end - padded
    experts = jnp.arange(N_EXPERTS, dtype=jnp.int32)
    start_of = jnp.sum(jnp.where(top_e[..., None] == experts, pad_start, 0), axis=-1)
    dest = (start_of + rank).astype(jnp.int32).reshape(-1)
    block_start = jnp.arange(n_blocks, dtype=jnp.int32) * bm
    n_used = (pad_end[-1] // bm).astype(jnp.int32).reshape(1)
    block_e = jnp.minimum(jnp.sum(pad_end[None, :] <= block_start[:, None], axis=1),
                          N_EXPERTS - 1).astype(jnp.int32)
    is_e = block_e[:, None] == experts[None, :]
    within = block_start - jnp.sum(jnp.where(is_e, pad_start, 0), axis=1)
    x_block = (block_e * (cap // bm) + within // bm).astype(jnp.int32)
    last_used = jnp.sum(jnp.where(jnp.arange(n_blocks) == n_used[0] - 1, x_block, 0))
    x_block = jnp.where(jnp.arange(n_blocks) < n_used[0], x_block, last_used).astype(jnp.int32)
    first_at = lax.cummin(jnp.where(padded > 0, experts, N_EXPERTS), axis=0, reverse=True)
    next_e = jnp.concatenate([first_at[1:], jnp.full((1,), N_EXPERTS, jnp.int32)])
    next_e = jnp.where(next_e < N_EXPERTS, next_e, -1).astype(jnp.int32)
    return dest, block_e, n_used, next_e, x_block, n_rows


def kernel(x, c, ctx, c_ctx, w_ada, b_ada, norm1_g, w_in, shift_mu, decay_w0, decay_lora_b,
           iclr_a0, iclr_lora_b, gate_lora_b, k_k, k_a, r_k, gn_w, gn_b, w_out_rwkv,
           sgu_ln_w, sgu_ln_b, sgu_w_spatial, sgu_b_spatial, w_out_sgu, w_o, norm2_g,
           router_w, router_b, exp_w_gate, exp_b_gate, exp_w_up, exp_b_up, exp_w_down,
           exp_b_down, final_norm_g):
    assert w_ada.shape[0] == 1, "single-layer problem"
    b, t, d = x.shape
    n_tok = b * t
    w = RWKV_WIDTH
    row = lambda a: a.reshape(1, -1)

    cs = jnp.zeros((8, d), F32).at[:b].set(c).at[b].set(c_ctx)
    mod = _stage_mods(cs, w_ada[0], row(b_ada[0]))
    sh1, sc1, ga1, sh2, sc2, ga2 = [m[:b, None, :] for m in jnp.split(mod, 6, axis=-1)]
    csh1, csc1 = [jnp.broadcast_to(m[b][None, None, :], (b, 1, d))
                  for m in jnp.split(mod, 6, axis=-1)[:2]]

    w_in_bf = w_in[0].astype(BF16)
    n1 = row(norm1_g[0])
    head_id = jnp.arange(w, dtype=jnp.int32) // HEAD_DIM
    headsum = (head_id[:, None] == head_id[None, :]).astype(BF16)
    prm = [row(shift_mu[0]), row(k_k[0]), row(k_a[0]), row(r_k[0]), decay_w0[0],
           decay_lora_b[0], iclr_a0[0], iclr_lora_b[0], gate_lora_b[0], headsum]
    y_f, y_b, g, bonus = _rwkv_branch(x, ctx, (sh1, sc1), (csh1, csc1), n1,
                                      w_in_bf[:, :RWKV_COLS], prm)

    bsp = jnp.repeat(sgu_b_spatial[0].T, SGU_WIDTH // SGU_GROUPS, axis=1)
    consts = [n1, w_in_bf[:, RWKV_COLS:], row(gn_w[0]), row(gn_b[0]), headsum,
              w_out_rwkv[0].astype(BF16), row(sgu_ln_w[0]), row(sgu_ln_b[0]),
              sgu_w_spatial[0].astype(BF16), bsp, w_out_sgu[0].astype(BF16),
              w_o[0].astype(BF16), row(norm2_g[0]), router_w[0].T, router_b[0].reshape(-1, 1)]
    (x1t, top_e, top_w, rank, counts, xs), cap = _stage_merge(
        x, y_f, y_b, g, bonus, (sh1, sc1, ga1, sh2, sc2), consts, tm=256)
    dest, block_e, n_used, next_e, x_block, n_rows = _route(top_e, rank, counts[:, 0], n_tok, cap)
    e3 = lambda a: a.reshape(N_EXPERTS, 1, -1)
    yb = _stage_experts(block_e, n_used, next_e, x_block, n_rows, xs, exp_w_gate[0],
                        e3(exp_b_gate[0]), exp_w_up[0], e3(exp_b_up[0]), exp_w_down[0],
                        e3(exp_b_down[0]))
    out = _stage_combine(dest, top_w.reshape(-1), x1t, ga2.reshape(b, 8, 128),
                         row(final_norm_g), yb, t, tm=256)
    return out.reshape(b, t, d)
```

```python
import functools
import math

import jax
import jax.numpy as jnp
from jax import lax
from jax.experimental import pallas as pl
from jax.experimental.pallas import tpu as pltpu

F32 = jnp.float32
BF16 = jnp.bfloat16
HIGHEST = lax.Precision.HIGHEST

D_MODEL = 1024
GRID_W = 64
RWKV_HEADS = 8
HEAD_DIM = 64
RWKV_WIDTH = RWKV_HEADS * HEAD_DIM
DECAY_LORA = 64
ICLR_LORA = 64
GATE_LORA = 128
RWKV_COLS = 3 * RWKV_WIDTH + 2 * DECAY_LORA + 2 * ICLR_LORA + GATE_LORA
SGU_WIDTH = 512
SGU_GROUPS = 8
SGU_CHUNK = 128
N_EXPERTS = 32
TOP_K = 4
SWIGLU_LIMIT = 7.0
SWIGLU_ALPHA = 1.702
RMS_EPS = 1e-6
LN_EPS = 1e-5
GN_EPS = 64e-5

WKV_CHUNK = 64
WKV_GROUP = 4
WKV_CHUNKS_PER_STEP = 1
EXPERT_ROWS = 512
VMEM_LIMIT = 48 * 1024 * 1024


def _params(*sem):
    return pltpu.CompilerParams(dimension_semantics=sem, vmem_limit_bytes=VMEM_LIMIT)


def _mm(a, b, dims=((1,), (0,)), exact=False):
    dn = (dims, ((), ()))
    if exact:
        return lax.dot_general(a, b, dn, precision=HIGHEST, preferred_element_type=F32)
    return lax.dot_general(a.astype(BF16), b.astype(BF16), dn, preferred_element_type=F32)


def _split3(x):
    hi = x.astype(BF16)
    r1 = x - hi.astype(F32)
    mid = r1.astype(BF16)
    lo = (r1 - mid.astype(F32)).astype(BF16)
    return hi, mid, lo


def _mm_hi_lo(a, b, dims):
    a_hi = a.astype(BF16)
    b_hi = b.astype(BF16)
    a_lo = a - a_hi.astype(F32)
    b_lo = b - b_hi.astype(F32)
    return _mm(a_hi, b_hi, dims) + _mm(a_hi, b_lo, dims) + _mm(a_lo, b_hi, dims)


def _mm_sel_x(sel, x):
    return sum(_mm(sel, p) for p in _split3(x))


def _full(shape):
    n = len(shape)
    return pl.BlockSpec(shape, lambda *_: (0,) * n)


def _norm_mod(x, g, shift, scale):
    y = x * lax.rsqrt(jnp.mean(x * x, axis=-1, keepdims=True) + RMS_EPS) * g
    return y * (1.0 + scale) + shift


def _mods_kernel(c_ref, w_ref, b_ref, o_ref):
    c = c_ref[...]
    s = c * jax.nn.sigmoid(c)
    o_ref[...] = _mm(s, w_ref[...], exact=True) + b_ref[...]


def _stage_mods(cs, w_ada, b_ada):
    rows, d = cs.shape
    n = w_ada.shape[1]
    tn = 1536
    return pl.pallas_call(
        _mods_kernel,
        grid=(n // tn,),
        in_specs=[_full((rows, d)),
                  pl.BlockSpec((d, tn), lambda j: (0, j)),
                  pl.BlockSpec((1, tn), lambda j: (0, j))],
        out_specs=pl.BlockSpec((rows, tn), lambda j: (0, j)),
        out_shape=jax.ShapeDtypeStruct((rows, n), F32),
        compiler_params=_params("arbitrary"),
        name="mods",
    )(cs, w_ada, b_ada)


def _rwkv_feats(p, mu_kk, mu_ka, r_k, w0, dlb, a0, ilb, glb, headsum):
    w = RWKV_WIDTH
    r = p[:, 0:w]
    k = p[:, w:2 * w]
    v = p[:, 2 * w:3 * w]
    o = 3 * w
    wd = (p[:, o:o + DECAY_LORA], p[:, o + DECAY_LORA:o + 2 * DECAY_LORA])
    o += 2 * DECAY_LORA
    ad = (p[:, o:o + ICLR_LORA], p[:, o + ICLR_LORA:o + 2 * ICLR_LORA])
    o += 2 * ICLR_LORA
    gd = p[:, o:o + GATE_LORA]

    kk = k * mu_kk
    kk = kk * lax.rsqrt(_mm(kk * kk, headsum) + 1e-12)
    g = _mm(jax.nn.sigmoid(gd), glb)
    ks, bs, lws = [], [], []
    ksum = None
    for d in range(2):
        z = w0[d:d + 1] + _mm(jnp.tanh(wd[d]), dlb[d])
        lws.append(-math.exp(-0.5) * jax.nn.sigmoid(z))
        ic = jax.nn.sigmoid(a0[d:d + 1] + _mm(ad[d], ilb[d]))
        kd = k * (1.0 + (ic - 1.0) * mu_ka)
        ks.append(kd)
        bs.append(kk * ic)
        ksum = kd if ksum is None else ksum + kd
    bonus = _mm(r * ksum * r_k, headsum) * v
    return r, v, -kk, g, bonus, ks, bs, lws


def _prep_kernel(xm_ref, xp_ref, xn_ref, sh_ref, sc_ref, n1_ref, w_ref,
                 mu_ref, kk_ref, ka_ref, rk_ref, w0_ref, dlb_ref, a0_ref, ilb_ref, glb_ref, hs_ref,
                 r_ref, v_ref, a_ref, g_ref, bon_ref, k_ref, b_ref, lw_ref, *, tm, grid_shift):
    project = lambda x: _mm(_norm_mod(x, n1_ref[...], sh_ref[0], sc_ref[0]), w_ref[...])
    lane = lax.broadcasted_iota(jnp.int32, (1, RWKV_COLS), 1)
    if grid_shift:
        ext = project(jnp.concatenate([xp_ref[0], xm_ref[0], xn_ref[0]], axis=0))
        main = ext[GRID_W:GRID_W + tm]
        t = pl.program_id(1) * tm + lax.broadcasted_iota(jnp.int32, (tm, 1), 0)
        col = t & (GRID_W - 1)
        row = t >> (GRID_W.bit_length() - 1)
        n_rows = pl.num_programs(1) * tm // GRID_W
        left = jnp.where(col > 0, ext[GRID_W - 1:GRID_W - 1 + tm], 0.0)
        right = jnp.where(col < GRID_W - 1, ext[GRID_W + 1:GRID_W + 1 + tm], 0.0)
        up = jnp.where(row > 0, ext[0:tm], 0.0)
        down = jnp.where(row < n_rows - 1, ext[2 * GRID_W:2 * GRID_W + tm], 0.0)
        cm = lane & 3
        shifted = jnp.where(cm == 0, left, jnp.where(cm == 1, right, jnp.where(cm == 2, up, down)))
    else:
        main = project(xm_ref[0])
        zero = jnp.zeros((1, RWKV_COLS), F32)
        prev = jnp.concatenate([zero, main[:tm - 1]], axis=0)
        nxt = jnp.concatenate([main[1:], zero], axis=0)
        shifted = jnp.where((lane & 1) == 0, prev, nxt)
    p = main + mu_ref[...] * (shifted - main)
    r, v, a, g, bonus, ks, bs, lws = _rwkv_feats(
        p, kk_ref[...], ka_ref[...], rk_ref[...], w0_ref[...], dlb_ref, a0_ref[...], ilb_ref,
        glb_ref[...], hs_ref[...])
    r_ref[0] = r.astype(BF16)
    v_ref[0] = v.astype(BF16)
    a_ref[0] = a.astype(BF16)
    g_ref[0] = g.astype(BF16)
    bon_ref[0] = bonus.astype(BF16)
    for d in range(2):
        k_ref[d, 0] = ks[d].astype(BF16)
        b_ref[d, 0] = bs[d].astype(BF16)
        lw_ref[d, 0] = lws[d]


def _stage_prep(x, shift, scale, g, w_bf16, prm, tm, grid_shift):
    b, t, c = x.shape
    w = RWKV_WIDTH
    mod = pl.BlockSpec((1, 1, c), lambda i, j: (i, 0, 0))
    hb = GRID_W if grid_shift else 8
    per = tm // hb
    last = t // hb - 1
    tok = pl.BlockSpec((1, tm, w), lambda i, j: (i, j, 0))
    tok2 = pl.BlockSpec((2, 1, tm, w), lambda i, j: (0, i, j, 0))
    in_specs = [
        pl.BlockSpec((1, tm, c), lambda i, j: (i, j, 0)),
        pl.BlockSpec((1, hb, c), lambda i, j: (i, jnp.maximum(j * per - 1, 0), 0)),
        pl.BlockSpec((1, hb, c), lambda i, j: (i, jnp.minimum((j + 1) * per, last), 0)),
        mod, mod, _full(g.shape), _full(w_bf16.shape),
    ] + [_full(a.shape) for a in prm]
    return pl.pallas_call(
        functools.partial(_prep_kernel, tm=tm, grid_shift=grid_shift),
        grid=(b, t // tm),
        in_specs=in_specs,
        out_specs=[tok] * 5 + [tok2] * 3,
        out_shape=[jax.ShapeDtypeStruct((b, t, w), BF16)] * 5
        + [jax.ShapeDtypeStruct((2, b, t, w), BF16)] * 2
        + [jax.ShapeDtypeStruct((2, b, t, w), F32)],
        compiler_params=_params("arbitrary", "arbitrary"),
        name="prep_lat" if grid_shift else "prep_ctx",
    )(x, x, x, shift, scale, g, w_bf16, *prm)


def _wkv_prepare(d, r, v, a, k, bb, lw):
    c = WKV_CHUNK
    n = HEAD_DIM
    ii = lax.broadcasted_iota(jnp.int32, (c, c), 0)
    jj = lax.broadcasted_iota(jnp.int32, (c, c), 1)
    incl = (jj <= ii) if d == 0 else (jj >= ii)
    lc = _mm_sel_x(incl.astype(BF16), lw)
    lx = lc - lw
    ltot = lc[c - 1:c] if d == 0 else lc[0:1]
    inv = jnp.exp(-lc)
    tail = jnp.exp(ltot - lc)
    etot = jnp.exp(ltot)
    eye_n = (lax.broadcasted_iota(jnp.int32, (n, n), 0)
             == lax.broadcasted_iota(jnp.int32, (n, n), 1)).astype(F32)
    scale = jnp.concatenate(
        [jnp.broadcast_to(jnp.sum(eye_n * etot[:, h * n:(h + 1) * n], axis=1, keepdims=True), (n, n))
         for h in range(RWKV_HEADS)], axis=1)
    bf = lambda x: x.astype(BF16)
    return (bf(a * jnp.exp(lx)), bf(r * jnp.exp(lc)), bf(k * inv), bf(bb * inv), bf(k * tail),
            bf(bb * tail), bf(v), scale)


def _wkv_consts():
    c = WKV_CHUNK
    n = HEAD_DIM
    gw = WKV_GROUP * n
    assert c == n and n & (n - 1) == 0
    ii = lax.broadcasted_iota(jnp.int32, (c, gw), 0)
    jj = lax.broadcasted_iota(jnp.int32, (c, gw), 1) & (c - 1)
    head_shift = n.bit_length() - 1
    same_head = (lax.broadcasted_iota(jnp.int32, (gw, gw), 0) >> head_shift
                 == lax.broadcasted_iota(jnp.int32, (gw, gw), 1) >> head_shift)

    def bd(x):
        x = x.astype(BF16)
        return jnp.where(same_head, jnp.concatenate([x] * WKV_GROUP, axis=0), jnp.zeros((), BF16))

    return ii, jj, bd


def _wkv_factor(chunks, emit):
    c = WKV_CHUNK
    gw = WKV_GROUP * HEAD_DIM
    n_groups = RWKV_WIDTH // gw
    ii, jj, bd = _wkv_consts()
    eye = (ii == jj).astype(F32)
    nt = ((1,), (1,))
    tn = ((0,), (0,))
    sl = [slice(h * HEAD_DIM, (h + 1) * HEAD_DIM) for h in range(WKV_GROUP)]

    prob = []
    for preps in chunks:
        for d, (at, rt, kt, bt, kh, bh, vb, _) in enumerate(preps):
            incl = (jj <= ii) if d == 0 else (jj >= ii)
            strict = (jj < ii) if d == 0 else (jj > ii)
            for gi in range(n_groups):
                gs = slice(gi * gw, (gi + 1) * gw)
                prob.append(dict(incl=incl, strict=strict, at=at[:, gs], rt=rt[:, gs],
                                 kt=kt[:, gs], bt=bt[:, gs], kh=kh[:, gs], bh=bh[:, gs],
                                 v=vb[:, gs]))
    for p in prob:
        lhs = jnp.concatenate([p["at"], p["rt"]], axis=0) if emit else p["at"]
        p["lhs_z"] = lhs
        gk = _mm(lhs, bd(p["kt"]), nt)
        gb = _mm(lhs, bd(p["bt"]), nt)
        p["a_ab"] = jnp.where(p["strict"], gb[:c], 0.0)
        lhs_v = jnp.where(p["strict"], gk[:c], 0.0)
        if emit:
            lhs_v = jnp.concatenate([lhs_v, jnp.where(p["incl"], gk[c:], 0.0)], axis=0)
            p["a_rb"] = jnp.where(p["incl"], gb[c:], 0.0).astype(BF16)
        p["lhs_v"] = lhs_v
    for p in prob:
        p["tinv"] = eye + p["a_ab"]
        p["x"] = _mm(p["a_ab"], bd(p["a_ab"]))
        p["av"] = _mm(p["lhs_v"], bd(p["v"]))
    for level in range(1, 6):
        for p in prob:
            if level < 5:
                both = _mm(jnp.concatenate([p["x"], p["tinv"]], axis=0), bd(p["x"]))
                p["x"] = both[:c]
                p["tinv"] = p["tinv"] + both[c:]
            else:
                p["tinv"] = (p["tinv"] + _mm(p["tinv"], bd(p["x"]))).astype(BF16)
    per_chunk = 2 * n_groups
    return [prob[j * per_chunk:(j + 1) * per_chunk] for j in range(len(chunks))]


def _wkv_apply(probs, chunks, states, emit):
    c = WKV_CHUNK
    n = HEAD_DIM
    gw = WKV_GROUP * n
    n_groups = RWKV_WIDTH // gw
    _, _, bd = _wkv_consts()
    tn = ((0,), (0,))
    sl = [slice(h * n, (h + 1) * n) for h in range(WKV_GROUP)]
    flat = []
    for prob, zs in zip(probs, states):
        zb = [z.astype(BF16) for z in zs]
        for i, p in enumerate(prob):
            d, gi = divmod(i, n_groups)
            p["z"] = zb[d][:, gi * gw:(gi + 1) * gw]
            flat.append(p)
    for p in flat:
        p["zv"] = _mm(p["lhs_z"], bd(p["z"])) + p["av"]
    for p in flat:
        p["u"] = _mm(p["tinv"], bd(p["zv"][:c])).astype(BF16)
    for p in flat:
        if emit:
            p["y"] = p["zv"][c:] + _mm(p["a_rb"], bd(p["u"]))
        p["z_new"] = [_mm(jnp.concatenate([p["kh"][:, s], p["bh"][:, s]], axis=0),
                          jnp.concatenate([p["v"][:, s], p["u"][:, s]], axis=0), tn) for s in sl]
    out = []
    for prob, preps, zs in zip(probs, chunks, states):
        res = []
        for d, (prep, z) in enumerate(zip(preps, zs)):
            mine = prob[d * n_groups:(d + 1) * n_groups]
            z_cat = jnp.concatenate([m for p in mine for m in p["z_new"]], axis=1) + prep[7] * z
            y = jnp.concatenate([p["y"] for p in mine], axis=1) if emit else None
            res.append((y, z_cat))
        out.append(res)
    return out


def _wkv_kernel(*refs, emit, has_init):
    refs = list(refs)
    ins = [[refs.pop(0) for _ in range(6)] for _ in range(2)]
    z0_ref = refs.pop(0) if has_init else None
    outs = [refs.pop(0) for _ in range(2 if emit else 1)]
    z_scr = refs.pop(0)
    n_seq = z_scr.shape[1]

    @pl.when(pl.program_id(0) == 0)
    def _():
        if has_init:
            z_scr[...] = z0_ref[...]
        else:
            z_scr[...] = jnp.zeros_like(z_scr)

    c = WKV_CHUNK
    per = ins[0][0].shape[1] // c

    def rows_of(d, j):
        sub = j if d == 0 else per - 1 - j
        return slice(sub * c, (sub + 1) * c)

    chunks = []
    for j in range(per):
        for i in range(n_seq):
            preps = []
            for d in range(2):
                rows = rows_of(d, j)
                r_ref, v_ref, a_ref, k_ref, b_ref, lw_ref = ins[d]
                preps.append(_wkv_prepare(d, r_ref[i, rows], v_ref[i, rows], a_ref[i, rows],
                                          k_ref[0, i, rows], b_ref[0, i, rows], lw_ref[0, i, rows]))
            chunks.append(preps)
    probs = _wkv_factor(chunks, emit)
    states = [(z_scr[0, i], z_scr[1, i]) for i in range(n_seq)]
    for j in range(per):
        now = slice(j * n_seq, (j + 1) * n_seq)
        res = _wkv_apply(probs[now], chunks[now], states, emit)
        states = [tuple(z_new for _, z_new in seq) for seq in res]
        if emit:
            for i, seq in enumerate(res):
                for d, (y, _) in enumerate(seq):
                    outs[d][i, rows_of(d, j)] = y
    for i in range(n_seq):
        for d in range(2):
            z_scr[d, i] = states[i][d]
    if not emit:
        outs[0][...] = z_scr[...]


def _stage_wkv(r, v, a, k, bb, lw, z0=None, emit=True):
    b, t, w = r.shape
    c = WKV_CHUNK * WKV_CHUNKS_PER_STEP
    nch = t // c
    n = HEAD_DIM
    pos = (lambda s: s, lambda s: nch - 1 - s)
    in_specs, args = [], []
    for d in range(2):
        tok = pl.BlockSpec((b, c, w), lambda s, d=d: (0, pos[d](s), 0))
        tok2 = pl.BlockSpec((1, b, c, w), lambda s, d=d: (d, 0, pos[d](s), 0))
        in_specs += [tok, tok, tok, tok2, tok2, tok2]
        args += [r, v, a, k, bb, lw]
    zspec = pl.BlockSpec((2, b, n, w), lambda s: (0, 0, 0, 0))
    if z0 is not None:
        in_specs.append(zspec)
        args.append(z0)
    if emit:
        out_specs = [pl.BlockSpec((b, c, w), lambda s, d=d: (0, pos[d](s), 0)) for d in range(2)]
        out_shape = [jax.ShapeDtypeStruct((b, t, w), F32)] * 2
    else:
        out_specs = [zspec]
        out_shape = [jax.ShapeDtypeStruct((2, b, n, w), F32)]
    return pl.pallas_call(
        functools.partial(_wkv_kernel, emit=emit, has_init=z0 is not None),
        grid=(nch,),
        in_specs=in_specs,
        out_specs=out_specs,
        out_shape=out_shape,
        scratch_shapes=[pltpu.VMEM((2, b, n, w), F32)],
        compiler_params=_params("arbitrary"),
        name="wkv_lat" if emit else "wkv_ctx",
    )(*args)


def _to_token_tiles(ref, val, rows):
    for cc in range(val.shape[1] // 128):
        ref[pl.ds(cc, rows, stride=8), :] = val[:, cc * 128:(cc + 1) * 128]


def _from_token_tiles(ref, rows):
    return jnp.concatenate([ref[pl.ds(cc, rows, stride=8), :] for cc in range(8)], axis=1)


def _route_tokens(h, rwt_ref, rb_ref, e_ref, w_ref, rank_ref, cnt_ref, carry, tm):
    logits = _mm_hi_lo(rwt_ref[...], h, dims=((1,), (1,))) + rb_ref[...]
    eio = lax.broadcasted_iota(jnp.int32, (N_EXPERTS, tm), 0)
    vals, sels, idxs = [], [], []
    for _ in range(TOP_K):
        m = jnp.max(logits, axis=0, keepdims=True)
        idx = jnp.min(jnp.where(logits == m, eio, N_EXPERTS), axis=0, keepdims=True)
        sel = eio == idx
        logits = jnp.where(sel, -jnp.inf, logits)
        vals.append(m)
        sels.append(sel)
        idxs.append(idx)
        e_ref[len(vals) - 1:len(vals), :] = idx
    ex = [jnp.exp(vk - vals[0]) for vk in vals]
    tot = ex[0] + ex[1] + ex[2] + ex[3]
    for kk in range(TOP_K):
        w_ref[kk:kk + 1, :] = ex[kk] / tot
    cnt = (sels[0] | sels[1] | sels[2] | sels[3]).astype(F32)
    ti = lax.broadcasted_iota(jnp.int32, (tm, tm), 0)
    tj = lax.broadcasted_iota(jnp.int32, (tm, tm), 1)
    before = _mm(cnt, (ti < tj).astype(F32))
    base = carry[...] + before
    ranks = []
    for kk in range(TOP_K):
        ranks.append(jnp.sum(jnp.where(sels[kk], base, 0.0), axis=0,
                             keepdims=True).astype(jnp.int32))
        rank_ref[kk:kk + 1, :] = ranks[kk]
    new = carry[...] + jnp.sum(cnt, axis=1, keepdims=True)
    carry[...] = new
    cnt_ref[...] = jnp.broadcast_to(new, cnt_ref.shape).astype(jnp.int32)
    return idxs, ranks


def _merge_kernel(x_ref, yf_ref, yb_ref, g_ref, bon_ref, sh_ref, sc_ref, ga_ref, sh2_ref, sc2_ref,
                  n1_ref, w2_ref, gnw_ref, gnb_ref, hs_ref, wor_ref, lnw_ref, lnb_ref, wsp_ref,
                  bsp_ref, wos_ref, wo_ref, n2_ref, rwt_ref, rb_ref,
                  xt_ref, e_ref, w_ref, rank_ref, cnt_ref, xs_ref,
                  carry, h2buf, zeros, dest_v, dest_s, cnt_s, ssem, csem, zsem, *, tm, cap):
    step = pl.program_id(0) * pl.num_programs(1) + pl.program_id(1)
    n_steps = pl.num_programs(0) * pl.num_programs(1)
    dump = N_EXPERTS * cap

    def dest_copy(which):
        return pltpu.make_async_copy(dest_v, dest_s.at[which], csem)

    def drain(which):
        for _ in range(TOP_K):
            pltpu.make_async_copy(h2buf.at[which], h2buf.at[which], ssem.at[which]).wait()

    @pl.when(step == 0)
    def _():
        carry[...] = jnp.zeros_like(carry)
        zeros[...] = jnp.zeros_like(zeros)
        h2buf[1] = jnp.zeros(h2buf.shape[1:], F32)
        h2buf[2] = jnp.zeros(h2buf.shape[1:], F32)
        dest_v[...] = dump + (lax.broadcasted_iota(jnp.int32, (TOP_K, tm), 0) * tm
                              + lax.broadcasted_iota(jnp.int32, (TOP_K, tm), 1))
        for which in range(2):
            dest_copy(which).start()
            dest_copy(which).wait()

    @pl.when(step > 0)
    def _():
        drain(step % 3)

    src = (step + 1) % 3
    for t in range(tm):
        for kk in range(TOP_K):
            _tile_copy(h2buf.at[src], t, xs_ref, dest_s[step % 2, kk, t], ssem.at[src]).start(
                priority=kk % 2)

    x = x_ref[0]
    h = _norm_mod(x, n1_ref[...], sh_ref[0], sc_ref[0])
    p2 = _mm(h, w2_ref[...])

    ps = p2[:, :2 * SGU_WIDTH]
    ge = 0.5 * ps * (1.0 + lax.erf(ps * (1.0 / math.sqrt(2.0))))
    u = ge[:, :SGU_WIDTH]
    z = ge[:, SGU_WIDTH:]
    mu = jnp.mean(z, axis=-1, keepdims=True)
    zc = z - mu
    var = jnp.mean(zc * zc, axis=-1, keepdims=True)
    z = zc * lax.rsqrt(var + LN_EPS) * lnw_ref[...] + lnb_ref[...]
    gw = SGU_WIDTH // SGU_GROUPS
    rows = []
    for c in range(tm // SGU_CHUNK):
        zc = z[c * SGU_CHUNK:(c + 1) * SGU_CHUNK]
        cols = [_mm(wsp_ref[gi], zc[:, gi * gw:(gi + 1) * gw]) for gi in range(SGU_GROUPS)]
        rows.append(jnp.concatenate(cols, axis=1) + bsp_ref[...])
    s = jnp.concatenate(rows, axis=0)
    y_b = _mm(u * s, wos_ref[...])

    y = yf_ref[0] + yb_ref[0]
    hs = hs_ref[...]
    ym = _mm(y, hs) * (1.0 / HEAD_DIM)
    yc = y - ym
    yv = _mm(yc * yc, hs) * (1.0 / HEAD_DIM)
    yn = yc * lax.rsqrt(yv + GN_EPS) * gnw_ref[...] + gnb_ref[...]
    y_a = _mm((yn + bon_ref[0]) * g_ref[0], wor_ref[...])

    gates = jax.nn.sigmoid(p2[:, 2 * SGU_WIDTH:])
    mix = gates[:, :D_MODEL] * y_a + gates[:, D_MODEL:] * y_b
    x1 = x + ga_ref[0] * _mm(mix, wo_ref[...])

    h2 = _norm_mod(x1, n2_ref[...], sh2_ref[0], sc2_ref[0])
    _to_token_tiles(h2buf.at[step % 3], h2, tm)
    _to_token_tiles(xt_ref, x1, tm)
    idxs, ranks = _route_tokens(h2, rwt_ref, rb_ref, e_ref, w_ref, rank_ref, cnt_ref, carry, tm)

    @pl.when(step > 0)
    def _():
        dest_copy((step - 1) % 2).wait()

    for kk in range(TOP_K):
        dest_v[kk:kk + 1, :] = idxs[kk] * cap + ranks[kk]
    dest_copy(step % 2).start()

    @pl.when(step == n_steps - 1)
    def _():
        dest_copy(step % 2).wait()
        for back in (1, 0):

            def flush(t, c, back=back):
                for kk in range(TOP_K):
                    _tile_copy(h2buf.at[(step - back) % 3], t, xs_ref,
                               dest_s[(step - back) % 2, kk, t],
                               ssem.at[(step - back) % 3]).start(priority=kk % 2)
                return c

            lax.fori_loop(0, tm, flush, 0)
        counts = pltpu.make_async_copy(cnt_ref, cnt_s, csem)
        counts.start()
        counts.wait()
        for e in range(N_EXPERTS):
            first = pl.multiple_of((e * cap + cnt_s[e, 0]) * 8, 8)
            pltpu.make_async_copy(zeros, xs_ref.at[pl.ds(first, zeros.shape[0])], zsem).start()
        for e in range(N_EXPERTS):
            pltpu.make_async_copy(zeros, xs_ref.at[pl.ds(0, zeros.shape[0])], zsem).wait()
        for which in range(3):
            drain(which)


def _stage_merge(x, yf, yb, g, bonus, mods, consts, tm):
    b, t, d = x.shape
    n = b * t
    w = RWKV_WIDTH
    per = t // tm
    cap = n + EXPERT_ROWS
    assert d == 8 * 128, "token-tile layout stores one (8, 128) tile per token"
    mod = pl.BlockSpec((1, 1, d), lambda i, j: (i, 0, 0))
    tok = pl.BlockSpec((1, tm, w), lambda i, j: (i, j, 0))
    tiles = pl.BlockSpec((tm * 8, 128), lambda i, j: (i * per + j, 0))
    lane = pl.BlockSpec((TOP_K, tm), lambda i, j: (0, i * per + j))
    return pl.pallas_call(
        functools.partial(_merge_kernel, tm=tm, cap=cap),
        grid=(b, per),
        in_specs=[pl.BlockSpec((1, tm, d), lambda i, j: (i, j, 0)), tok, tok, tok, tok]
        + [mod] * len(mods) + [_full(a.shape) for a in consts],
        out_specs=[tiles, lane, lane, lane, _full((N_EXPERTS, 128)),
                   pl.BlockSpec(memory_space=pl.ANY)],
        out_shape=[jax.ShapeDtypeStruct((n * 8, 128), F32),
                   jax.ShapeDtypeStruct((TOP_K, n), jnp.int32),
                   jax.ShapeDtypeStruct((TOP_K, n), F32),
                   jax.ShapeDtypeStruct((TOP_K, n), jnp.int32),
                   jax.ShapeDtypeStruct((N_EXPERTS, 128), jnp.int32),
                   jax.ShapeDtypeStruct(((N_EXPERTS * cap + TOP_K * tm) * 8, 128), F32)],
        scratch_shapes=[pltpu.VMEM((N_EXPERTS, 1), F32),
                        pltpu.VMEM((3, tm * 8, 128), F32),
                        pltpu.VMEM((EXPERT_ROWS * 8, 128), F32),
                        pltpu.VMEM((TOP_K, tm), jnp.int32),
                        pltpu.SMEM((2, TOP_K, tm), jnp.int32),
                        pltpu.SMEM((N_EXPERTS, 128), jnp.int32),
                        pltpu.SemaphoreType.DMA((3,)), pltpu.SemaphoreType.DMA(()),
                        pltpu.SemaphoreType.DMA(())],
        compiler_params=_params("arbitrary", "arbitrary"),
        name="merge",
    )(x, yf, yb, g, bonus, *mods, *consts), cap


def _expert_kernel(be_ref, nu_ref, nx_ref, xb_ref, x_ref, bg_ref, bu_ref, bd_ref, wg_hbm, wu_hbm, wd_hbm,
                   o_ref, stage, wg_s, wu_s, wd_s, sem):
    i = pl.program_id(0)
    e = be_ref[i]
    used = i < nu_ref[0]
    prev = be_ref[jnp.maximum(i - 1, 0)]

    def fetch(expert):
        return [pltpu.make_async_copy(w.at[expert], stage.at[j], sem.at[j])
                for j, w in enumerate((wg_hbm, wu_hbm, wd_hbm))]

    @pl.when(i == 0)
    def _():
        for cp in fetch(e):
            cp.start()

    @pl.when(used & ((i == 0) | (e != prev)))
    def _():
        for cp, w_s, j in zip(fetch(e), (wg_s, wu_s, wd_s), range(3)):
            cp.wait()
            w_s[...] = stage[j].astype(BF16)
        nxt = nx_ref[e]

        @pl.when(nxt >= 0)
        def _():
            for cp in fetch(nxt):
                cp.start()

    @pl.when(used)
    def _():
        x = _from_token_tiles(x_ref, EXPERT_ROWS)
        gate = _mm(x, wg_s[...]) + bg_ref[0]
        up = _mm(x, wu_s[...]) + bu_ref[0]
        gate = jnp.minimum(gate, SWIGLU_LIMIT)
        up = jnp.clip(up, -SWIGLU_LIMIT, SWIGLU_LIMIT)
        act = gate * jax.nn.sigmoid(SWIGLU_ALPHA * gate) * (up + 1.0)
        _to_token_tiles(o_ref, _mm(act, wd_s[...]) + bd_ref[0], EXPERT_ROWS)

    @pl.when(jnp.logical_not(used))
    def _():
        o_ref[...] = jnp.zeros_like(o_ref)


def _stage_experts(block_e, n_used, next_e, x_block, n_rows, xs, wg, bg, wu, bu, wd, bd):
    d, f = wg.shape[1:]
    assert d == f, "one staging buffer shape serves all three weight matrices"
    bm = EXPERT_ROWS
    bspec = lambda n_: pl.BlockSpec((1, 1, n_), lambda i, be, *_: (be[i], 0, 0))
    hbm = pl.BlockSpec(memory_space=pl.ANY)
    return pl.pallas_call(
        _expert_kernel,
        grid_spec=pltpu.PrefetchScalarGridSpec(
            num_scalar_prefetch=4,
            grid=(n_rows // bm,),
            in_specs=[pl.BlockSpec((bm * 8, 128), lambda i, be, nu, nx, xb: (xb[i], 0)),
                      bspec(f), bspec(f), bspec(d), hbm, hbm, hbm],
            out_specs=pl.BlockSpec((bm * 8, 128), lambda i, *_: (i, 0)),
            scratch_shapes=[pltpu.VMEM((3, d, f), F32), pltpu.VMEM((d, f), BF16),
                            pltpu.VMEM((d, f), BF16), pltpu.VMEM((f, d), BF16),
                            pltpu.SemaphoreType.DMA((3,))]),
        out_shape=jax.ShapeDtypeStruct((n_rows * 8, 128), F32),
        compiler_params=_params("arbitrary"),
        name="experts",
    )(block_e, n_used, next_e, x_block, xs, bg, bu, bd, wg, wu, wd)


def _tile_copy(src_ref, src_row, dst_ref, dst_row, sem):
    src = src_ref.at[pl.ds(pl.multiple_of(src_row * 8, 8), 8)]
    dst = dst_ref.at[pl.ds(pl.multiple_of(dst_row * 8, 8), 8)]
    return pltpu.make_async_copy(src, dst, sem)


def _combine_kernel(dest_ref, w_ref, xt_ref, ga_ref, g_ref, yb_ref, o_ref, buf0, buf1, res, sem,
                    *, tm, n_tok):
    i = pl.program_id(0)
    ga = ga_ref[0]
    bufs = (buf0, buf1)

    def issue(tile, slot, t):
        for kk in range(TOP_K):
            _tile_copy(yb_ref, dest_ref[kk * n_tok + tile * tm + t], bufs[slot].at[kk], t,
                       sem.at[slot]).start(priority=kk % 2)

    def combine(slot, t):
        rows = pl.ds(pl.multiple_of(t * 8, 8), 8)
        acc = bufs[slot][0, rows, :] * w_ref[i * tm + t]
        for kk in range(1, TOP_K):
            acc = acc + bufs[slot][kk, rows, :] * w_ref[kk * n_tok + i * tm + t]
        res[rows, :] = xt_ref[rows, :] + ga * acc

    def loop(body):
        lax.fori_loop(0, tm, lambda t, carry: (body(t), carry)[1], 0, unroll=16)

    @pl.when(i == 0)
    def _():
        loop(lambda t: issue(0, 0, t))

    for slot in range(2):
        @pl.when(i % 2 == slot)
        def _():
            for kk in range(TOP_K):
                pltpu.make_async_copy(bufs[slot].at[kk], bufs[slot].at[kk], sem.at[slot]).wait()

            @pl.when(i + 1 < pl.num_programs(0))
            def _():
                loop(lambda t: (issue(i + 1, 1 - slot, t), combine(slot, t)))

            @pl.when(i + 1 >= pl.num_programs(0))
            def _():
                loop(lambda t: combine(slot, t))

    x = _from_token_tiles(res, tm)
    o_ref[...] = x * lax.rsqrt(jnp.mean(x * x, axis=-1, keepdims=True) + RMS_EPS) * g_ref[...]


def _stage_combine(dest_flat, w_flat, x1t, ga_t, g_t, yb, tokens_per_batch, tm):
    n = x1t.shape[0] // 8
    d = 8 * 128
    per = tokens_per_batch // tm
    return pl.pallas_call(
        functools.partial(_combine_kernel, tm=tm, n_tok=n),
        grid_spec=pltpu.PrefetchScalarGridSpec(
            num_scalar_prefetch=2,
            grid=(n // tm,),
            in_specs=[pl.BlockSpec((tm * 8, 128), lambda i, *_: (i, 0)),
                      pl.BlockSpec((1, 8, 128), lambda i, *_: (i // per, 0, 0)),
                      pl.BlockSpec((1, d), lambda i, *_: (0, 0)),
                      pl.BlockSpec(memory_space=pl.ANY)],
            out_specs=pl.BlockSpec((tm, d), lambda i, *_: (i, 0)),
            scratch_shapes=[pltpu.VMEM((TOP_K, tm * 8, 128), F32),
                            pltpu.VMEM((TOP_K, tm * 8, 128), F32),
                            pltpu.VMEM((tm * 8, 128), F32),
                            pltpu.SemaphoreType.DMA((2,))]),
        out_shape=jax.ShapeDtypeStruct((n, d), F32),
        compiler_params=_params("arbitrary"),
        name="combine",
    )(dest_flat, w_flat, x1t, ga_t, g_t, yb)


def _rwkv_branch(x, ctx, mods_lat, mods_ctx, n1, w_rwkv, prm):
    r, v, a, _, _, k, bb, lw = _stage_prep(ctx, *mods_ctx, n1, w_rwkv, prm, tm=ctx.shape[1],
                                           grid_shift=False)
    (z_ctx,) = _stage_wkv(r, v, a, k, bb, lw, emit=False)
    r, v, a, g, bonus, k, bb, lw = _stage_prep(x, *mods_lat, n1, w_rwkv, prm, tm=512,
                                               grid_shift=True)
    y_f, y_b = _stage_wkv(r, v, a, k, bb, lw, z0=z_ctx, emit=True)
    return y_f, y_b, g, bonus


def _route(top_e, rank, counts, n_tok, cap):
    bm = EXPERT_ROWS
    n_rows = n_tok * TOP_K + N_EXPERTS * bm
    n_blocks = n_rows // bm
    padded = (counts + bm - 1) // bm * bm
    pad_end = jnp.cumsum(padded)
    pad_start = pad_end - padded
    experts = jnp.arange(N_EXPERTS, dtype=jnp.int32)
    start_of = jnp.sum(jnp.where(top_e[..., None] == experts, pad_start, 0), axis=-1)
    dest = (start_of + rank).astype(jnp.int32).reshape(-1)
    block_start = jnp.arange(n_blocks, dtype=jnp.int32) * bm
    n_used = (pad_end[-1] // bm).astype(jnp.int32).reshape(1)
    block_e = jnp.minimum(jnp.sum(pad_end[None, :] <= block_start[:, None], axis=1),
                          N_EXPERTS - 1).astype(jnp.int32)
    is_e = block_e[:, None] == experts[None, :]
    within = block_start - jnp.sum(jnp.where(is_e, pad_start, 0), axis=1)
    x_block = (block_e * (cap // bm) + within // bm).astype(jnp.int32)
    last_used = jnp.sum(jnp.where(jnp.arange(n_blocks) == n_used[0] - 1, x_block, 0))
    x_block = jnp.where(jnp.arange(n_blocks) < n_used[0], x_block, last_used).astype(jnp.int32)
    first_at = lax.cummin(jnp.where(padded > 0, experts, N_EXPERTS), axis=0, reverse=True)
    next_e = jnp.concatenate([first_at[1:], jnp.full((1,), N_EXPERTS, jnp.int32)])
    next_e = jnp.where(next_e < N_EXPERTS, next_e, -1).astype(jnp.int32)
    return dest, block_e, n_used, next_e, x_block, n_rows


def kernel(x, c, ctx, c_ctx, w_ada, b_ada, norm1_g, w_in, shift_mu, decay_w0, decay_lora_b,
           iclr_a0, iclr_lora_b, gate_lora_b, k_k, k_a, r_k, gn_w, gn_b, w_out_rwkv,
           sgu_ln_w, sgu_ln_b, sgu_w_spatial, sgu_b_spatial, w_out_sgu, w_o, norm2_g,
           router_w, router_b, exp_w_gate, exp_b_gate, exp_w_up, exp_b_up, exp_w_down,
           exp_b_down, final_norm_g):
    assert w_ada.shape[0] == 1, "single-layer problem"
    b, t, d = x.shape
    n_tok = b * t
    w = RWKV_WIDTH
    row = lambda a: a.reshape(1, -1)

    cs = jnp.zeros((8, d), F32).at[:b].set(c).at[b].set(c_ctx)
    mod = _stage_mods(cs, w_ada[0], row(b_ada[0]))
    sh1, sc1, ga1, sh2, sc2, ga2 = [m[:b, None, :] for m in jnp.split(mod, 6, axis=-1)]
    csh1, csc1 = [jnp.broadcast_to(m[b][None, None, :], (b, 1, d))
                  for m in jnp.split(mod, 6, axis=-1)[:2]]

    w_in_bf = w_in[0].astype(BF16)
    n1 = row(norm1_g[0])
    head_id = jnp.arange(w, dtype=jnp.int32) // HEAD_DIM
    headsum = (head_id[:, None] == head_id[None, :]).astype(BF16)
    prm = [row(shift_mu[0]), row(k_k[0]), row(k_a[0]), row(r_k[0]), decay_w0[0],
           decay_lora_b[0], iclr_a0[0], iclr_lora_b[0], gate_lora_b[0], headsum]
    y_f, y_b, g, bonus = _rwkv_branch(x, ctx, (sh1, sc1), (csh1, csc1), n1,
                                      w_in_bf[:, :RWKV_COLS], prm)

    bsp = jnp.repeat(sgu_b_spatial[0].T, SGU_WIDTH // SGU_GROUPS, axis=1)
    consts = [n1, w_in_bf[:, RWKV_COLS:], row(gn_w[0]), row(gn_b[0]), headsum,
              w_out_rwkv[0].astype(BF16), row(sgu_ln_w[0]), row(sgu_ln_b[0]),
              sgu_w_spatial[0].astype(BF16), bsp, w_out_sgu[0].astype(BF16),
              w_o[0].astype(BF16), row(norm2_g[0]), router_w[0].T, router_b[0].reshape(-1, 1)]
    (x1t, top_e, top_w, rank, counts, xs), cap = _stage_merge(
        x, y_f, y_b, g, bonus, (sh1, sc1, ga1, sh2, sc2), consts, tm=256)
    dest, block_e, n_used, next_e, x_block, n_rows = _route(top_e, rank, counts[:, 0], n_tok, cap)
    e3 = lambda a: a.reshape(N_EXPERTS, 1, -1)
    yb = _stage_experts(block_e, n_used, next_e, x_block, n_rows, xs, exp_w_gate[0],
                        e3(exp_b_gate[0]), exp_w_up[0], e3(exp_b_up[0]), exp_w_down[0],
                        e3(exp_b_down[0]))
    out = _stage_combine(dest, top_w.reshape(-1), x1t, ga2.reshape(b, 8, 128),
                         row(final_norm_g), yb, t, tm=256)
    return out.reshape(b, t, d)
```

```python
import functools
import math

import jax
import jax.numpy as jnp
from jax import lax
from jax.experimental import pallas as pl
from jax.experimental.pallas import tpu as pltpu

F32 = jnp.float32
BF16 = jnp.bfloat16
HIGHEST = lax.Precision.HIGHEST

D_MODEL = 1024
GRID_W = 64
RWKV_HEADS = 8
HEAD_DIM = 64
RWKV_WIDTH = RWKV_HEADS * HEAD_DIM
DECAY_LORA = 64
ICLR_LORA = 64
GATE_LORA = 128
RWKV_COLS = 3 * RWKV_WIDTH + 2 * DECAY_LORA + 2 * ICLR_LORA + GATE_LORA
SGU_WIDTH = 512
SGU_GROUPS = 8
SGU_CHUNK = 128
N_EXPERTS = 32
TOP_K = 4
SWIGLU_LIMIT = 7.0
SWIGLU_ALPHA = 1.702
RMS_EPS = 1e-6
LN_EPS = 1e-5
GN_EPS = 64e-5

WKV_CHUNK = 64
WKV_GROUP = 4
WKV_CHUNKS_PER_STEP = 1
EXPERT_ROWS = 512
VMEM_LIMIT = 48 * 1024 * 1024


def _params(*sem):
    return pltpu.CompilerParams(dimension_semantics=sem, vmem_limit_bytes=VMEM_LIMIT)


def _mm(a, b, dims=((1,), (0,)), exact=False):
    dn = (dims, ((), ()))
    if exact:
        return lax.dot_general(a, b, dn, precision=HIGHEST, preferred_element_type=F32)
    return lax.dot_general(a.astype(BF16), b.astype(BF16), dn, preferred_element_type=F32)


def _split3(x):
    hi = x.astype(BF16)
    r1 = x - hi.astype(F32)
    mid = r1.astype(BF16)
    lo = (r1 - mid.astype(F32)).astype(BF16)
    return hi, mid, lo


def _mm_hi_lo(a, b, dims):
    a_hi = a.astype(BF16)
    b_hi = b.astype(BF16)
    a_lo = a - a_hi.astype(F32)
    b_lo = b - b_hi.astype(F32)
    return _mm(a_hi, b_hi, dims) + _mm(a_hi, b_lo, dims) + _mm(a_lo, b_hi, dims)


def _mm_sel_x(sel, x):
    return sum(_mm(sel, p) for p in _split3(x))


def _full(shape):
    n = len(shape)
    return pl.BlockSpec(shape, lambda *_: (0,) * n)


def _norm_mod(x, g, shift, scale):
    y = x * lax.rsqrt(jnp.mean(x * x, axis=-1, keepdims=True) + RMS_EPS) * g
    return y * (1.0 + scale) + shift


def _mods_kernel(c_ref, w_ref, b_ref, o_ref):
    c = c_ref[...]
    s = c * jax.nn.sigmoid(c)
    o_ref[...] = _mm(s, w_ref[...], exact=True) + b_ref[...]


def _stage_mods(cs, w_ada, b_ada):
    rows, d = cs.shape
    n = w_ada.shape[1]
    tn = 1536
    return pl.pallas_call(
        _mods_kernel,
        grid=(n // tn,),
        in_specs=[_full((rows, d)),
                  pl.BlockSpec((d, tn), lambda j: (0, j)),
                  pl.BlockSpec((1, tn), lambda j: (0, j))],
        out_specs=pl.BlockSpec((rows, tn), lambda j: (0, j)),
        out_shape=jax.ShapeDtypeStruct((rows, n), F32),
        compiler_params=_params("arbitrary"),
        name="mods",
    )(cs, w_ada, b_ada)


def _rwkv_feats(p, mu_kk, mu_ka, r_k, w0, dlb, a0, ilb, glb, headsum):
    w = RWKV_WIDTH
    r = p[:, 0:w]
    k = p[:, w:2 * w]
    v = p[:, 2 * w:3 * w]
    o = 3 * w
    wd = (p[:, o:o + DECAY_LORA], p[:, o + DECAY_LORA:o + 2 * DECAY_LORA])
    o += 2 * DECAY_LORA
    ad = (p[:, o:o + ICLR_LORA], p[:, o + ICLR_LORA:o + 2 * ICLR_LORA])
    o += 2 * ICLR_LORA
    gd = p[:, o:o + GATE_LORA]

    kk = k * mu_kk
    kk = kk * lax.rsqrt(_mm(kk * kk, headsum) + 1e-12)
    g = _mm(jax.nn.sigmoid(gd), glb)
    ks, bs, lws = [], [], []
    ksum = None
    for d in range(2):
        z = w0[d:d + 1] + _mm(jnp.tanh(wd[d]), dlb[d])
        lws.append(-math.exp(-0.5) * jax.nn.sigmoid(z))
        ic = jax.nn.sigmoid(a0[d:d + 1] + _mm(ad[d], ilb[d]))
        kd = k * (1.0 + (ic - 1.0) * mu_ka)
        ks.append(kd)
        bs.append(kk * ic)
        ksum = kd if ksum is None else ksum + kd
    bonus = _mm(r * ksum * r_k, headsum) * v
    return r, v, -kk, g, bonus, ks, bs, lws


def _prep_kernel(xm_ref, xp_ref, xn_ref, sh_ref, sc_ref, n1_ref, w_ref,
                 mu_ref, kk_ref, ka_ref, rk_ref, w0_ref, dlb_ref, a0_ref, ilb_ref, glb_ref, hs_ref,
                 r_ref, v_ref, a_ref, g_ref, bon_ref, k_ref, b_ref, lw_ref, *, tm, grid_shift):
    project = lambda x: _mm(_norm_mod(x, n1_ref[...], sh_ref[0], sc_ref[0]), w_ref[...])
    lane = lax.broadcasted_iota(jnp.int32, (1, RWKV_COLS), 1)
    if grid_shift:
        ext = project(jnp.concatenate([xp_ref[0], xm_ref[0], xn_ref[0]], axis=0))
        main = ext[GRID_W:GRID_W + tm]
        t = pl.program_id(1) * tm + lax.broadcasted_iota(jnp.int32, (tm, 1), 0)
        col = t & (GRID_W - 1)
        row = t >> (GRID_W.bit_length() - 1)
        n_rows = pl.num_programs(1) * tm // GRID_W
        left = jnp.where(col > 0, ext[GRID_W - 1:GRID_W - 1 + tm], 0.0)
        right = jnp.where(col < GRID_W - 1, ext[GRID_W + 1:GRID_W + 1 + tm], 0.0)
        up = jnp.where(row > 0, ext[0:tm], 0.0)
        down = jnp.where(row < n_rows - 1, ext[2 * GRID_W:2 * GRID_W + tm], 0.0)
        cm = lane & 3
        shifted = jnp.where(cm == 0, left, jnp.where(cm == 1, right, jnp.where(cm == 2, up, down)))
    else:
        main = project(xm_ref[0])
        zero = jnp.zeros((1, RWKV_COLS), F32)
        prev = jnp.concatenate([zero, main[:tm - 1]], axis=0)
        nxt = jnp.concatenate([main[1:], zero], axis=0)
        shifted = jnp.where((lane & 1) == 0, prev, nxt)
    p = main + mu_ref[...] * (shifted - main)
    r, v, a, g, bonus, ks, bs, lws = _rwkv_feats(
        p, kk_ref[...], ka_ref[...], rk_ref[...], w0_ref[...], dlb_ref, a0_ref[...], ilb_ref,
        glb_ref[...], hs_ref[...])
    r_ref[0] = r.astype(BF16)
    v_ref[0] = v.astype(BF16)
    a_ref[0] = a.astype(BF16)
    g_ref[0] = g.astype(BF16)
    bon_ref[0] = bonus.astype(BF16)
    for d in range(2):
        k_ref[d, 0] = ks[d].astype(BF16)
        b_ref[d, 0] = bs[d].astype(BF16)
        lw_ref[d, 0] = lws[d]


def _stage_prep(x, shift, scale, g, w_bf16, prm, tm, grid_shift):
    b, t, c = x.shape
    w = RWKV_WIDTH
    mod = pl.BlockSpec((1, 1, c), lambda i, j: (i, 0, 0))
    hb = GRID_W if grid_shift else 8
    per = tm // hb
    last = t // hb - 1
    tok = pl.BlockSpec((1, tm, w), lambda i, j: (i, j, 0))
    tok2 = pl.BlockSpec((2, 1, tm, w), lambda i, j: (0, i, j, 0))
    in_specs = [
        pl.BlockSpec((1, tm, c), lambda i, j: (i, j, 0)),
        pl.BlockSpec((1, hb, c), lambda i, j: (i, jnp.maximum(j * per - 1, 0), 0)),
        pl.BlockSpec((1, hb, c), lambda i, j: (i, jnp.minimum((j + 1) * per, last), 0)),
        mod, mod, _full(g.shape), _full(w_bf16.shape),
    ] + [_full(a.shape) for a in prm]
    return pl.pallas_call(
        functools.partial(_prep_kernel, tm=tm, grid_shift=grid_shift),
        grid=(b, t // tm),
        in_specs=in_specs,
        out_specs=[tok] * 5 + [tok2] * 3,
        out_shape=[jax.ShapeDtypeStruct((b, t, w), BF16)] * 5
        + [jax.ShapeDtypeStruct((2, b, t, w), BF16)] * 2
        + [jax.ShapeDtypeStruct((2, b, t, w), F32)],
        compiler_params=_params("arbitrary", "arbitrary"),
        name="prep_lat" if grid_shift else "prep_ctx",
    )(x, x, x, shift, scale, g, w_bf16, *prm)


def _wkv_prepare(d, r, v, a, k, bb, lw):
    c = WKV_CHUNK
    n = HEAD_DIM
    ii = lax.broadcasted_iota(jnp.int32, (c, c), 0)
    jj = lax.broadcasted_iota(jnp.int32, (c, c), 1)
    incl = (jj <= ii) if d == 0 else (jj >= ii)
    lc = _mm_sel_x(incl.astype(BF16), lw)
    lx = lc - lw
    ltot = lc[c - 1:c] if d == 0 else lc[0:1]
    inv = jnp.exp(-lc)
    tail = jnp.exp(ltot - lc)
    etot = jnp.exp(ltot)
    eye_n = (lax.broadcasted_iota(jnp.int32, (n, n), 0)
             == lax.broadcasted_iota(jnp.int32, (n, n), 1)).astype(F32)
    scale = jnp.concatenate(
        [jnp.broadcast_to(jnp.sum(eye_n * etot[:, h * n:(h + 1) * n], axis=1, keepdims=True), (n, n))
         for h in range(RWKV_HEADS)], axis=1)
    bf = lambda x: x.astype(BF16)
    return (bf(a * jnp.exp(lx)), bf(r * jnp.exp(lc)), bf(k * inv), bf(bb * inv), bf(k * tail),
            bf(bb * tail), bf(v), scale)


def _wkv_consts():
    c = WKV_CHUNK
    n = HEAD_DIM
    gw = WKV_GROUP * n
    assert c == n and n & (n - 1) == 0
    ii = lax.broadcasted_iota(jnp.int32, (c, gw), 0)
    jj = lax.broadcasted_iota(jnp.int32, (c, gw), 1) & (c - 1)
    head_shift = n.bit_length() - 1
    same_head = (lax.broadcasted_iota(jnp.int32, (gw, gw), 0) >> head_shift
                 == lax.broadcasted_iota(jnp.int32, (gw, gw), 1) >> head_shift)

    def bd(x):
        x = x.astype(BF16)
        return jnp.where(same_head, jnp.concatenate([x] * WKV_GROUP, axis=0), jnp.zeros((), BF16))

    return ii, jj, bd


def _wkv_factor(chunks, emit):
    c = WKV_CHUNK
    gw = WKV_GROUP * HEAD_DIM
    n_groups = RWKV_WIDTH // gw
    ii, jj, bd = _wkv_consts()
    eye = (ii == jj).astype(F32)
    nt = ((1,), (1,))
    tn = ((0,), (0,))
    sl = [slice(h * HEAD_DIM, (h + 1) * HEAD_DIM) for h in range(WKV_GROUP)]

    prob = []
    for preps in chunks:
        for d, (at, rt, kt, bt, kh, bh, vb, _) in enumerate(preps):
            incl = (jj <= ii) if d == 0 else (jj >= ii)
            strict = (jj < ii) if d == 0 else (jj > ii)
            for gi in range(n_groups):
                gs = slice(gi * gw, (gi + 1) * gw)
                prob.append(dict(incl=incl, strict=strict, at=at[:, gs], rt=rt[:, gs],
                                 kt=kt[:, gs], bt=bt[:, gs], kh=kh[:, gs], bh=bh[:, gs],
                                 v=vb[:, gs]))
    for p in prob:
        lhs = jnp.concatenate([p["at"], p["rt"]], axis=0) if emit else p["at"]
        p["lhs_z"] = lhs
        gk = _mm(lhs, bd(p["kt"]), nt)
        gb = _mm(lhs, bd(p["bt"]), nt)
        p["a_ab"] = jnp.where(p["strict"], gb[:c], 0.0)
        lhs_v = jnp.where(p["strict"], gk[:c], 0.0)
        if emit:
            lhs_v = jnp.concatenate([lhs_v, jnp.where(p["incl"], gk[c:], 0.0)], axis=0)
            p["a_rb"] = jnp.where(p["incl"], gb[c:], 0.0).astype(BF16)
        p["lhs_v"] = lhs_v
    for p in prob:
        p["tinv"] = eye + p["a_ab"]
        p["x"] = _mm(p["a_ab"], bd(p["a_ab"]))
        p["av"] = _mm(p["lhs_v"], bd(p["v"]))
    for level in range(1, 6):
        for p in prob:
            if level < 5:
                both = _mm(jnp.concatenate([p["x"], p["tinv"]], axis=0), bd(p["x"]))
                p["x"] = both[:c]
                p["tinv"] = p["tinv"] + both[c:]
            else:
                p["tinv"] = (p["tinv"] + _mm(p["tinv"], bd(p["x"]))).astype(BF16)
    per_chunk = 2 * n_groups
    return [prob[j * per_chunk:(j + 1) * per_chunk] for j in range(len(chunks))]


def _wkv_apply(probs, chunks, states, emit):
    c = WKV_CHUNK
    n = HEAD_DIM
    gw = WKV_GROUP * n
    n_groups = RWKV_WIDTH // gw
    _, _, bd = _wkv_consts()
    tn = ((0,), (0,))
    sl = [slice(h * n, (h + 1) * n) for h in range(WKV_GROUP)]
    flat = []
    for prob, zs in zip(probs, states):
        zb = [z.astype(BF16) for z in zs]
        for i, p in enumerate(prob):
            d, gi = divmod(i, n_groups)
            p["z"] = zb[d][:, gi * gw:(gi + 1) * gw]
            flat.append(p)
    for p in flat:
        p["zv"] = _mm(p["lhs_z"], bd(p["z"])) + p["av"]
    for p in flat:
        p["u"] = _mm(p["tinv"], bd(p["zv"][:c])).astype(BF16)
    for p in flat:
        if emit:
            p["y"] = p["zv"][c:] + _mm(p["a_rb"], bd(p["u"]))
        p["z_new"] = [_mm(jnp.concatenate([p["kh"][:, s], p["bh"][:, s]], axis=0),
                          jnp.concatenate([p["v"][:, s], p["u"][:, s]], axis=0), tn) for s in sl]
    out = []
    for prob, preps, zs in zip(probs, chunks, states):
        res = []
        for d, (prep, z) in enumerate(zip(preps, zs)):
            mine = prob[d * n_groups:(d + 1) * n_groups]
            z_cat = jnp.concatenate([m for p in mine for m in p["z_new"]], axis=1) + prep[7] * z
            y = jnp.concatenate([p["y"] for p in mine], axis=1) if emit else None
            res.append((y, z_cat))
        out.append(res)
    return out


def _wkv_kernel(*refs, emit, has_init):
    refs = list(refs)
    ins = [[refs.pop(0) for _ in range(6)] for _ in range(2)]
    z0_ref = refs.pop(0) if has_init else None
    outs = [refs.pop(0) for _ in range(2 if emit else 1)]
    z_scr = refs.pop(0)
    n_seq = z_scr.shape[1]

    @pl.when(pl.program_id(0) == 0)
    def _():
        if has_init:
            z_scr[...] = z0_ref[...]
        else:
            z_scr[...] = jnp.zeros_like(z_scr)

    c = WKV_CHUNK
    per = ins[0][0].shape[1] // c

    def rows_of(d, j):
        sub = j if d == 0 else per - 1 - j
        return slice(sub * c, (sub + 1) * c)

    chunks = []
    for j in range(per):
        for i in range(n_seq):
            preps = []
            for d in range(2):
                rows = rows_of(d, j)
                r_ref, v_ref, a_ref, k_ref, b_ref, lw_ref = ins[d]
                preps.append(_wkv_prepare(d, r_ref[i, rows], v_ref[i, rows], a_ref[i, rows],
                                          k_ref[0, i, rows], b_ref[0, i, rows], lw_ref[0, i, rows]))
            chunks.append(preps)
    probs = _wkv_factor(chunks, emit)
    states = [(z_scr[0, i], z_scr[1, i]) for i in range(n_seq)]
    for j in range(per):
        now = slice(j * n_seq, (j + 1) * n_seq)
        res = _wkv_apply(probs[now], chunks[now], states, emit)
        states = [tuple(z_new for _, z_new in seq) for seq in res]
        if emit:
            for i, seq in enumerate(res):
                for d, (y, _) in enumerate(seq):
                    outs[d][i, rows_of(d, j)] = y
    for i in range(n_seq):
        for d in range(2):
            z_scr[d, i] = states[i][d]
    if not emit:
        outs[0][...] = z_scr[...]


def _stage_wkv(r, v, a, k, bb, lw, z0=None, emit=True):
    b, t, w = r.shape
    c = WKV_CHUNK * WKV_CHUNKS_PER_STEP
    nch = t // c
    n = HEAD_DIM
    pos = (lambda s: s, lambda s: nch - 1 - s)
    in_specs, args = [], []
    for d in range(2):
        tok = pl.BlockSpec((b, c, w), lambda s, d=d: (0, pos[d](s), 0))
        tok2 = pl.BlockSpec((1, b, c, w), lambda s, d=d: (d, 0, pos[d](s), 0))
        in_specs += [tok, tok, tok, tok2, tok2, tok2]
        args += [r, v, a, k, bb, lw]
    zspec = pl.BlockSpec((2, b, n, w), lambda s: (0, 0, 0, 0))
    if z0 is not None:
        in_specs.append(zspec)
        args.append(z0)
    if emit:
        out_specs = [pl.BlockSpec((b, c, w), lambda s, d=d: (0, pos[d](s), 0)) for d in range(2)]
        out_shape = [jax.ShapeDtypeStruct((b, t, w), F32)] * 2
    else:
        out_specs = [zspec]
        out_shape = [jax.ShapeDtypeStruct((2, b, n, w), F32)]
    return pl.pallas_call(
        functools.partial(_wkv_kernel, emit=emit, has_init=z0 is not None),
        grid=(nch,),
        in_specs=in_specs,
        out_specs=out_specs,
        out_shape=out_shape,
        scratch_shapes=[pltpu.VMEM((2, b, n, w), F32)],
        compiler_params=_params("arbitrary"),
        name="wkv_lat" if emit else "wkv_ctx",
    )(*args)


def _to_token_tiles(ref, val, rows):
    for cc in range(val.shape[1] // 128):
        ref[pl.ds(cc, rows, stride=8), :] = val[:, cc * 128:(cc + 1) * 128]


def _from_token_tiles(ref, rows):
    return jnp.concatenate([ref[pl.ds(cc, rows, stride=8), :] for cc in range(8)], axis=1)


def _route_tokens(h, rwt_ref, rb_ref, e_ref, w_ref, rank_ref, cnt_ref, carry, tm):
    logits = _mm_hi_lo(rwt_ref[...], h, dims=((1,), (1,))) + rb_ref[...]
    eio = lax.broadcasted_iota(jnp.int32, (N_EXPERTS, tm), 0)
    vals, sels, idxs = [], [], []
    for _ in range(TOP_K):
        m = jnp.max(logits, axis=0, keepdims=True)
        idx = jnp.min(jnp.where(logits == m, eio, N_EXPERTS), axis=0, keepdims=True)
        sel = eio == idx
        logits = jnp.where(sel, -jnp.inf, logits)
        vals.append(m)
        sels.append(sel)
        idxs.append(idx)
        e_ref[len(vals) - 1:len(vals), :] = idx
    ex = [jnp.exp(vk - vals[0]) for vk in vals]
    tot = ex[0] + ex[1] + ex[2] + ex[3]
    for kk in range(TOP_K):
        w_ref[kk:kk + 1, :] = ex[kk] / tot
    cnt = (sels[0] | sels[1] | sels[2] | sels[3]).astype(F32)
    ti = lax.broadcasted_iota(jnp.int32, (tm, tm), 0)
    tj = lax.broadcasted_iota(jnp.int32, (tm, tm), 1)
    before = _mm(cnt, (ti < tj).astype(F32))
    base = carry[...] + before
    ranks = []
    for kk in range(TOP_K):
        ranks.append(jnp.sum(jnp.where(sels[kk], base, 0.0), axis=0,
                             keepdims=True).astype(jnp.int32))
        rank_ref[kk:kk + 1, :] = ranks[kk]
    new = carry[...] + jnp.sum(cnt, axis=1, keepdims=True)
    carry[...] = new
    cnt_ref[...] = jnp.broadcast_to(new, cnt_ref.shape).astype(jnp.int32)
    return idxs, ranks


def _merge_kernel(x_ref, yf_ref, yb_ref, g_ref, bon_ref, sh_ref, sc_ref, ga_ref, sh2_ref, sc2_ref,
                  n1_ref, w2_ref, gnw_ref, gnb_ref, hs_ref, wor_ref, lnw_ref, lnb_ref, wsp_ref,
                  bsp_ref, wos_ref, wo_ref, n2_ref, rwt_ref, rb_ref,
                  xt_ref, e_ref, w_ref, rank_ref, cnt_ref, xs_ref,
                  carry, h2buf, zeros, dest_v, dest_s, cnt_s, ssem, csem, zsem, *, tm, cap):
    step = pl.program_id(0) * pl.num_programs(1) + pl.program_id(1)
    n_steps = pl.num_programs(0) * pl.num_programs(1)
    dump = N_EXPERTS * cap

    def dest_copy(which):
        return pltpu.make_async_copy(dest_v, dest_s.at[which], csem)

    def drain(which):
        for _ in range(TOP_K):
            pltpu.make_async_copy(h2buf.at[which], h2buf.at[which], ssem.at[which]).wait()

    @pl.when(step == 0)
    def _():
        carry[...] = jnp.zeros_like(carry)
        zeros[...] = jnp.zeros_like(zeros)
        h2buf[2] = jnp.zeros(h2buf.shape[1:], F32)
        h2buf[3] = jnp.zeros(h2buf.shape[1:], F32)
        dest_v[...] = dump + (lax.broadcasted_iota(jnp.int32, (TOP_K, tm), 0) * tm
                              + lax.broadcasted_iota(jnp.int32, (TOP_K, tm), 1))
        for which in range(2):
            dest_copy(which).start()
            dest_copy(which).wait()

    @pl.when(step > 1)
    def _():
        drain(step % 4)

    src = (step + 2) % 4
    for t in range(tm):
        for kk in range(TOP_K):
            _tile_copy(h2buf.at[src], t, xs_ref, dest_s[step % 2, kk, t], ssem.at[src]).start(
                priority=kk % 2)

    x = x_ref[0]
    h = _norm_mod(x, n1_ref[...], sh_ref[0], sc_ref[0])
    p2 = _mm(h, w2_ref[...])

    ps = p2[:, :2 * SGU_WIDTH]
    ge = 0.5 * ps * (1.0 + lax.erf(ps * (1.0 / math.sqrt(2.0))))
    u = ge[:, :SGU_WIDTH]
    z = ge[:, SGU_WIDTH:]
    mu = jnp.mean(z, axis=-1, keepdims=True)
    zc = z - mu
    var = jnp.mean(zc * zc, axis=-1, keepdims=True)
    z = zc * lax.rsqrt(var + LN_EPS) * lnw_ref[...] + lnb_ref[...]
    gw = SGU_WIDTH // SGU_GROUPS
    rows = []
    for c in range(tm // SGU_CHUNK):
        zc = z[c * SGU_CHUNK:(c + 1) * SGU_CHUNK]
        cols = [_mm(wsp_ref[gi], zc[:, gi * gw:(gi + 1) * gw]) for gi in range(SGU_GROUPS)]
        rows.append(jnp.concatenate(cols, axis=1) + bsp_ref[...])
    s = jnp.concatenate(rows, axis=0)
    y_b = _mm(u * s, wos_ref[...])

    y = yf_ref[0] + yb_ref[0]
    hs = hs_ref[...]
    ym = _mm(y, hs) * (1.0 / HEAD_DIM)
    yc = y - ym
    yv = _mm(yc * yc, hs) * (1.0 / HEAD_DIM)
    yn = yc * lax.rsqrt(yv + GN_EPS) * gnw_ref[...] + gnb_ref[...]
    y_a = _mm((yn + bon_ref[0]) * g_ref[0], wor_ref[...])

    gates = jax.nn.sigmoid(p2[:, 2 * SGU_WIDTH:])
    mix = gates[:, :D_MODEL] * y_a + gates[:, D_MODEL:] * y_b
    x1 = x + ga_ref[0] * _mm(mix, wo_ref[...])

    h2 = _norm_mod(x1, n2_ref[...], sh2_ref[0], sc2_ref[0])
    _to_token_tiles(h2buf.at[step % 4], h2, tm)
    _to_token_tiles(xt_ref, x1, tm)
    idxs, ranks = _route_tokens(h2, rwt_ref, rb_ref, e_ref, w_ref, rank_ref, cnt_ref, carry, tm)

    @pl.when(step > 0)
    def _():
        dest_copy((step - 1) % 2).wait()

    for kk in range(TOP_K):
        dest_v[kk:kk + 1, :] = idxs[kk] * cap + ranks[kk]
    dest_copy(step % 2).start()

    @pl.when(step == n_steps - 1)
    def _():
        dest_copy(step % 2).wait()
        for back in (1, 0):

            def flush(t, c, back=back):
                for kk in range(TOP_K):
                    _tile_copy(h2buf.at[(step - back) % 4], t, xs_ref,
                               dest_s[(step - back) % 2, kk, t],
                               ssem.at[(step - back) % 4]).start(priority=kk % 2)
                return c

            lax.fori_loop(0, tm, flush, 0)
        counts = pltpu.make_async_copy(cnt_ref, cnt_s, csem)
        counts.start()
        counts.wait()
        for e in range(N_EXPERTS):
            first = pl.multiple_of((e * cap + cnt_s[e, 0]) * 8, 8)
            pltpu.make_async_copy(zeros, xs_ref.at[pl.ds(first, zeros.shape[0])], zsem).start()
        for e in range(N_EXPERTS):
            pltpu.make_async_copy(zeros, xs_ref.at[pl.ds(0, zeros.shape[0])], zsem).wait()
        for which in range(4):
            drain(which)


def _stage_merge(x, yf, yb, g, bonus, mods, consts, tm):
    b, t, d = x.shape
    n = b * t
    w = RWKV_WIDTH
    per = t // tm
    cap = n + EXPERT_ROWS
    assert d == 8 * 128, "token-tile layout stores one (8, 128) tile per token"
    mod = pl.BlockSpec((1, 1, d), lambda i, j: (i, 0, 0))
    tok = pl.BlockSpec((1, tm, w), lambda i, j: (i, j, 0))
    tiles = pl.BlockSpec((tm * 8, 128), lambda i, j: (i * per + j, 0))
    lane = pl.BlockSpec((TOP_K, tm), lambda i, j: (0, i * per + j))
    return pl.pallas_call(
        functools.partial(_merge_kernel, tm=tm, cap=cap),
        grid=(b, per),
        in_specs=[pl.BlockSpec((1, tm, d), lambda i, j: (i, j, 0)), tok, tok, tok, tok]
        + [mod] * len(mods) + [_full(a.shape) for a in consts],
        out_specs=[tiles, lane, lane, lane, _full((N_EXPERTS, 128)),
                   pl.BlockSpec(memory_space=pl.ANY)],
        out_shape=[jax.ShapeDtypeStruct((n * 8, 128), F32),
                   jax.ShapeDtypeStruct((TOP_K, n), jnp.int32),
                   jax.ShapeDtypeStruct((TOP_K, n), F32),
                   jax.ShapeDtypeStruct((TOP_K, n), jnp.int32),
                   jax.ShapeDtypeStruct((N_EXPERTS, 128), jnp.int32),
                   jax.ShapeDtypeStruct(((N_EXPERTS * cap + TOP_K * tm) * 8, 128), F32)],
        scratch_shapes=[pltpu.VMEM((N_EXPERTS, 1), F32),
                        pltpu.VMEM((4, tm * 8, 128), F32),
                        pltpu.VMEM((EXPERT_ROWS * 8, 128), F32),
                        pltpu.VMEM((TOP_K, tm), jnp.int32),
                        pltpu.SMEM((2, TOP_K, tm), jnp.int32),
                        pltpu.SMEM((N_EXPERTS, 128), jnp.int32),
                        pltpu.SemaphoreType.DMA((4,)), pltpu.SemaphoreType.DMA(()),
                        pltpu.SemaphoreType.DMA(())],
        compiler_params=_params("arbitrary", "arbitrary"),
        name="merge",
    )(x, yf, yb, g, bonus, *mods, *consts), cap


def _expert_kernel(be_ref, nu_ref, nx_ref, xb_ref, x_ref, bg_ref, bu_ref, bd_ref, wg_hbm, wu_hbm, wd_hbm,
                   o_ref, stage, wg_s, wu_s, wd_s, sem):
    i = pl.program_id(0)
    e = be_ref[i]
    used = i < nu_ref[0]
    prev = be_ref[jnp.maximum(i - 1, 0)]

    def fetch(expert):
        return [pltpu.make_async_copy(w.at[expert], stage.at[j], sem.at[j])
                for j, w in enumerate((wg_hbm, wu_hbm, wd_hbm))]

    @pl.when(i == 0)
    def _():
        for cp in fetch(e):
            cp.start()

    @pl.when(used & ((i == 0) | (e != prev)))
    def _():
        for cp, w_s, j in zip(fetch(e), (wg_s, wu_s, wd_s), range(3)):
            cp.wait()
            w_s[...] = stage[j].astype(BF16)
        nxt = nx_ref[e]

        @pl.when(nxt >= 0)
        def _():
            for cp in fetch(nxt):
                cp.start()

    @pl.when(used)
    def _():
        x = _from_token_tiles(x_ref, EXPERT_ROWS)
        gate = _mm(x, wg_s[...]) + bg_ref[0]
        up = _mm(x, wu_s[...]) + bu_ref[0]
        gate = jnp.minimum(gate, SWIGLU_LIMIT)
        up = jnp.clip(up, -SWIGLU_LIMIT, SWIGLU_LIMIT)
        act = gate * jax.nn.sigmoid(SWIGLU_ALPHA * gate) * (up + 1.0)
        _to_token_tiles(o_ref, _mm(act, wd_s[...]) + bd_ref[0], EXPERT_ROWS)

    @pl.when(jnp.logical_not(used))
    def _():
        o_ref[...] = jnp.zeros_like(o_ref)


def _stage_experts(block_e, n_used, next_e, x_block, n_rows, xs, wg, bg, wu, bu, wd, bd):
    d, f = wg.shape[1:]
    assert d == f, "one staging buffer shape serves all three weight matrices"
    bm = EXPERT_ROWS
    bspec = lambda n_: pl.BlockSpec((1, 1, n_), lambda i, be, *_: (be[i], 0, 0))
    hbm = pl.BlockSpec(memory_space=pl.ANY)
    return pl.pallas_call(
        _expert_kernel,
        grid_spec=pltpu.PrefetchScalarGridSpec(
            num_scalar_prefetch=4,
            grid=(n_rows // bm,),
            in_specs=[pl.BlockSpec((bm * 8, 128), lambda i, be, nu, nx, xb: (xb[i], 0)),
                      bspec(f), bspec(f), bspec(d), hbm, hbm, hbm],
            out_specs=pl.BlockSpec((bm * 8, 128), lambda i, *_: (i, 0)),
            scratch_shapes=[pltpu.VMEM((3, d, f), F32), pltpu.VMEM((d, f), BF16),
                            pltpu.VMEM((d, f), BF16), pltpu.VMEM((f, d), BF16),
                            pltpu.SemaphoreType.DMA((3,))]),
        out_shape=jax.ShapeDtypeStruct((n_rows * 8, 128), F32),
        compiler_params=_params("arbitrary"),
        name="experts",
    )(block_e, n_used, next_e, x_block, xs, bg, bu, bd, wg, wu, wd)


def _tile_copy(src_ref, src_row, dst_ref, dst_row, sem):
    src = src_ref.at[pl.ds(pl.multiple_of(src_row * 8, 8), 8)]
    dst = dst_ref.at[pl.ds(pl.multiple_of(dst_row * 8, 8), 8)]
    return pltpu.make_async_copy(src, dst, sem)


def _combine_kernel(dest_ref, w_ref, xt_ref, ga_ref, g_ref, yb_ref, o_ref, buf0, buf1, res, sem,
                    *, tm, n_tok):
    i = pl.program_id(0)
    ga = ga_ref[0]
    bufs = (buf0, buf1)

    def issue(tile, slot, t):
        for kk in range(TOP_K):
            _tile_copy(yb_ref, dest_ref[kk * n_tok + tile * tm + t], bufs[slot].at[kk], t,
                       sem.at[slot]).start(priority=kk % 2)

    def combine(slot, t):
        rows = pl.ds(pl.multiple_of(t * 8, 8), 8)
        acc = bufs[slot][0, rows, :] * w_ref[i * tm + t]
        for kk in range(1, TOP_K):
            acc = acc + bufs[slot][kk, rows, :] * w_ref[kk * n_tok + i * tm + t]
        res[rows, :] = xt_ref[rows, :] + ga * acc

    def loop(body):
        lax.fori_loop(0, tm, lambda t, carry: (body(t), carry)[1], 0, unroll=16)

    @pl.when(i == 0)
    def _():
        loop(lambda t: issue(0, 0, t))

    for slot in range(2):
        @pl.when(i % 2 == slot)
        def _():
            for kk in range(TOP_K):
                pltpu.make_async_copy(bufs[slot].at[kk], bufs[slot].at[kk], sem.at[slot]).wait()

            @pl.when(i + 1 < pl.num_programs(0))
            def _():
                loop(lambda t: (issue(i + 1, 1 - slot, t), combine(slot, t)))

            @pl.when(i + 1 >= pl.num_programs(0))
            def _():
                loop(lambda t: combine(slot, t))

    x = _from_token_tiles(res, tm)
    o_ref[...] = x * lax.rsqrt(jnp.mean(x * x, axis=-1, keepdims=True) + RMS_EPS) * g_ref[...]


def _stage_combine(dest_flat, w_flat, x1t, ga_t, g_t, yb, tokens_per_batch, tm):
    n = x1t.shape[0] // 8
    d = 8 * 128
    per = tokens_per_batch // tm
    return pl.pallas_call(
        functools.partial(_combine_kernel, tm=tm, n_tok=n),
        grid_spec=pltpu.PrefetchScalarGridSpec(
            num_scalar_prefetch=2,
            grid=(n // tm,),
            in_specs=[pl.BlockSpec((tm * 8, 128), lambda i, *_: (i, 0)),
                      pl.BlockSpec((1, 8, 128), lambda i, *_: (i // per, 0, 0)),
                      pl.BlockSpec((1, d), lambda i, *_: (0, 0)),
                      pl.BlockSpec(memory_space=pl.ANY)],
            out_specs=pl.BlockSpec((tm, d), lambda i, *_: (i, 0)),
            scratch_shapes=[pltpu.VMEM((TOP_K, tm * 8, 128), F32),
                            pltpu.VMEM((TOP_K, tm * 8, 128), F32),
                            pltpu.VMEM((tm * 8, 128), F32),
                            pltpu.SemaphoreType.DMA((2,))]),
        out_shape=jax.ShapeDtypeStruct((n, d), F32),
        compiler_params=_params("arbitrary"),
        name="combine",
    )(dest_flat, w_flat, x1t, ga_t, g_t, yb)


def _rwkv_branch(x, ctx, mods_lat, mods_ctx, n1, w_rwkv, prm):
    r, v, a, _, _, k, bb, lw = _stage_prep(ctx, *mods_ctx, n1, w_rwkv, prm, tm=ctx.shape[1],
                                           grid_shift=False)
    (z_ctx,) = _stage_wkv(r, v, a, k, bb, lw, emit=False)
    r, v, a, g, bonus, k, bb, lw = _stage_prep(x, *mods_lat, n1, w_rwkv, prm, tm=512,
                                               grid_shift=True)
    y_f, y_b = _stage_wkv(r, v, a, k, bb, lw, z0=z_ctx, emit=True)
    return y_f, y_b, g, bonus


def _route(top_e, rank, counts, n_tok, cap):
    bm = EXPERT_ROWS
    n_rows = n_tok * TOP_K + N_EXPERTS * bm
    n_blocks = n_rows // bm
    padded = (counts + bm - 1) // bm * bm
    pad_end = jnp.cumsum(padded)
    pad_start = pad_end - padded
    experts = jnp.arange(N_EXPERTS, dtype=jnp.int32)
    start_of = jnp.sum(jnp.where(top_e[..., None] == experts, pad_start, 0), axis=-1)
    dest = (start_of + rank).astype(jnp.int32).reshape(-1)
    block_start = jnp.arange(n_blocks, dtype=jnp.int32) * bm
    n_used = (pad_end[-1] // bm).astype(jnp.int32).reshape(1)
    block_e = jnp.minimum(jnp.sum(pad_end[None, :] <= block_start[:, None], axis=1),
                          N_EXPERTS - 1).astype(jnp.int32)
    is_e = block_e[:, None] == experts[None, :]
    within = block_start - jnp.sum(jnp.where(is_e, pad_start, 0), axis=1)
    x_block = (block_e * (cap // bm) + within // bm).astype(jnp.int32)
    last_used = jnp.sum(jnp.where(jnp.arange(n_blocks) == n_used[0] - 1, x_block, 0))
    x_block = jnp.where(jnp.arange(n_blocks) < n_used[0], x_block, last_used).astype(jnp.int32)
    first_at = lax.cummin(jnp.where(padded > 0, experts, N_EXPERTS), axis=0, reverse=True)
    next_e = jnp.concatenate([first_at[1:], jnp.full((1,), N_EXPERTS, jnp.int32)])
    next_e = jnp.where(next_e < N_EXPERTS, next_e, -1).astype(jnp.int32)
    return dest, block_e, n_used, next_e, x_block, n_rows


def kernel(x, c, ctx, c_ctx, w_ada, b_ada, norm1_g, w_in, shift_mu, decay_w0, decay_lora_b,
           iclr_a0, iclr_lora_b, gate_lora_b, k_k, k_a, r_k, gn_w, gn_b, w_out_rwkv,
           sgu_ln_w, sgu_ln_b, sgu_w_spatial, sgu_b_spatial, w_out_sgu, w_o, norm2_g,
           router_w, router_b, exp_w_gate, exp_b_gate, exp_w_up, exp_b_up, exp_w_down,
           exp_b_down, final_norm_g):
    assert w_ada.shape[0] == 1, "single-layer problem"
    b, t, d = x.shape
    n_tok = b * t
    w = RWKV_WIDTH
    row = lambda a: a.reshape(1, -1)

    cs = jnp.zeros((8, d), F32).at[:b].set(c).at[b].set(c_ctx)
    mod = _stage_mods(cs, w_ada[0], row(b_ada[0]))
    sh1, sc1, ga1, sh2, sc2, ga2 = [m[:b, None, :] for m in jnp.split(mod, 6, axis=-1)]
    csh1, csc1 = [jnp.broadcast_to(m[b][None, None, :], (b, 1, d))
                  for m in jnp.split(mod, 6, axis=-1)[:2]]

    w_in_bf = w_in[0].astype(BF16)
    n1 = row(norm1_g[0])
    head_id = jnp.arange(w, dtype=jnp.int32) // HEAD_DIM
    headsum = (head_id[:, None] == head_id[None, :]).astype(BF16)
    prm = [row(shift_mu[0]), row(k_k[0]), row(k_a[0]), row(r_k[0]), decay_w0[0],
           decay_lora_b[0], iclr_a0[0], iclr_lora_b[0], gate_lora_b[0], headsum]
    y_f, y_b, g, bonus = _rwkv_branch(x, ctx, (sh1, sc1), (csh1, csc1), n1,
                                      w_in_bf[:, :RWKV_COLS], prm)

    bsp = jnp.repeat(sgu_b_spatial[0].T, SGU_WIDTH // SGU_GROUPS, axis=1)
    consts = [n1, w_in_bf[:, RWKV_COLS:], row(gn_w[0]), row(gn_b[0]), headsum,
              w_out_rwkv[0].astype(BF16), row(sgu_ln_w[0]), row(sgu_ln_b[0]),
              sgu_w_spatial[0].astype(BF16), bsp, w_out_sgu[0].astype(BF16),
              w_o[0].astype(BF16), row(norm2_g[0]), router_w[0].T, router_b[0].reshape(-1, 1)]
    (x1t, top_e, top_w, rank, counts, xs), cap = _stage_merge(
        x, y_f, y_b, g, bonus, (sh1, sc1, ga1, sh2, sc2), consts, tm=256)
    dest, block_e, n_used, next_e, x_block, n_rows = _route(top_e, rank, counts[:, 0], n_tok, cap)
    e3 = lambda a: a.reshape(N_EXPERTS, 1, -1)
    yb = _stage_experts(block_e, n_used, next_e, x_block, n_rows, xs, exp_w_gate[0],
                        e3(exp_b_gate[0]), exp_w_up[0], e3(exp_b_up[0]), exp_w_down[0],
                        e3(exp_b_down[0]))
    out = _stage_combine(dest, top_w.reshape(-1), x1t, ga2.reshape(b, 8, 128),
                         row(final_norm_g), yb, t, tm=256)
    return out.reshape(b, t, d)
```

```python
import functools
import math

import jax
import jax.numpy as jnp
from jax import lax
from jax.experimental import pallas as pl
from jax.experimental.pallas import tpu as pltpu

F32 = jnp.float32
BF16 = jnp.bfloat16
HIGHEST = lax.Precision.HIGHEST

D_MODEL = 1024
GRID_W = 64
RWKV_HEADS = 8
HEAD_DIM = 64
RWKV_WIDTH = RWKV_HEADS * HEAD_DIM
DECAY_LORA = 64
ICLR_LORA = 64
GATE_LORA = 128
RWKV_COLS = 3 * RWKV_WIDTH + 2 * DECAY_LORA + 2 * ICLR_LORA + GATE_LORA
SGU_WIDTH = 512
SGU_GROUPS = 8
SGU_CHUNK = 128
N_EXPERTS = 32
TOP_K = 4
SWIGLU_LIMIT = 7.0
SWIGLU_ALPHA = 1.702
RMS_EPS = 1e-6
LN_EPS = 1e-5
GN_EPS = 64e-5

WKV_CHUNK = 64
WKV_GROUP = 4
WKV_CHUNKS_PER_STEP = 1
EXPERT_ROWS = 512
VMEM_LIMIT = 48 * 1024 * 1024


def _params(*sem):
    return pltpu.CompilerParams(dimension_semantics=sem, vmem_limit_bytes=VMEM_LIMIT)


def _mm(a, b, dims=((1,), (0,)), exact=False):
    dn = (dims, ((), ()))
    if exact:
        return lax.dot_general(a, b, dn, precision=HIGHEST, preferred_element_type=F32)
    return lax.dot_general(a.astype(BF16), b.astype(BF16), dn, preferred_element_type=F32)


def _split3(x):
    hi = x.astype(BF16)
    r1 = x - hi.astype(F32)
    mid = r1.astype(BF16)
    lo = (r1 - mid.astype(F32)).astype(BF16)
    return hi, mid, lo


def _mm_hi_lo(a, b, dims):
    a_hi = a.astype(BF16)
    b_hi = b.astype(BF16)
    a_lo = a - a_hi.astype(F32)
    b_lo = b - b_hi.astype(F32)
    return _mm(a_hi, b_hi, dims) + _mm(a_hi, b_lo, dims) + _mm(a_lo, b_hi, dims)


def _mm_sel_x(sel, x):
    return sum(_mm(sel, p) for p in _split3(x))


def _full(shape):
    n = len(shape)
    return pl.BlockSpec(shape, lambda *_: (0,) * n)


def _norm_mod(x, g, shift, scale):
    y = x * lax.rsqrt(jnp.mean(x * x, axis=-1, keepdims=True) + RMS_EPS) * g
    return y * (1.0 + scale) + shift


def _mods_kernel(c_ref, w_ref, b_ref, o_ref):
    c = c_ref[...]
    s = c * jax.nn.sigmoid(c)
    o_ref[...] = _mm(s, w_ref[...], exact=True) + b_ref[...]


def _stage_mods(cs, w_ada, b_ada):
    rows, d = cs.shape
    n = w_ada.shape[1]
    tn = 1536
    return pl.pallas_call(
        _mods_kernel,
        grid=(n // tn,),
        in_specs=[_full((rows, d)),
                  pl.BlockSpec((d, tn), lambda j: (0, j)),
                  pl.BlockSpec((1, tn), lambda j: (0, j))],
        out_specs=pl.BlockSpec((rows, tn), lambda j: (0, j)),
        out_shape=jax.ShapeDtypeStruct((rows, n), F32),
        compiler_params=_params("arbitrary"),
        name="mods",
    )(cs, w_ada, b_ada)


def _rwkv_feats(p, mu_kk, mu_ka, r_k, w0, dlb, a0, ilb, glb, headsum):
    w = RWKV_WIDTH
    r = p[:, 0:w]
    k = p[:, w:2 * w]
    v = p[:, 2 * w:3 * w]
    o = 3 * w
    wd = (p[:, o:o + DECAY_LORA], p[:, o + DECAY_LORA:o + 2 * DECAY_LORA])
    o += 2 * DECAY_LORA
    ad = (p[:, o:o + ICLR_LORA], p[:, o + ICLR_LORA:o + 2 * ICLR_LORA])
    o += 2 * ICLR_LORA
    gd = p[:, o:o + GATE_LORA]

    kk = k * mu_kk
    kk = kk * lax.rsqrt(_mm(kk * kk, headsum) + 1e-12)
    g = _mm(jax.nn.sigmoid(gd), glb)
    ks, bs, lws = [], [], []
    ksum = None
    for d in range(2):
        z = w0[d:d + 1] + _mm(jnp.tanh(wd[d]), dlb[d])
        lws.append(-math.exp(-0.5) * jax.nn.sigmoid(z))
        ic = jax.nn.sigmoid(a0[d:d + 1] + _mm(ad[d], ilb[d]))
        kd = k * (1.0 + (ic - 1.0) * mu_ka)
        ks.append(kd)
        bs.append(kk * ic)
        ksum = kd if ksum is None else ksum + kd
    bonus = _mm(r * ksum * r_k, headsum) * v
    return r, v, -kk, g, bonus, ks, bs, lws


def _prep_kernel(xm_ref, xp_ref, xn_ref, sh_ref, sc_ref, n1_ref, w_ref,
                 mu_ref, kk_ref, ka_ref, rk_ref, w0_ref, dlb_ref, a0_ref, ilb_ref, glb_ref, hs_ref,
                 r_ref, v_ref, a_ref, g_ref, bon_ref, k_ref, b_ref, lw_ref, *, tm, grid_shift):
    project = lambda x: _mm(_norm_mod(x, n1_ref[...], sh_ref[0], sc_ref[0]), w_ref[...])
    lane = lax.broadcasted_iota(jnp.int32, (1, RWKV_COLS), 1)
    if grid_shift:
        ext = project(jnp.concatenate([xp_ref[0], xm_ref[0], xn_ref[0]], axis=0))
        main = ext[GRID_W:GRID_W + tm]
        t = pl.program_id(1) * tm + lax.broadcasted_iota(jnp.int32, (tm, 1), 0)
        col = t & (GRID_W - 1)
        row = t >> (GRID_W.bit_length() - 1)
        n_rows = pl.num_programs(1) * tm // GRID_W
        left = jnp.where(col > 0, ext[GRID_W - 1:GRID_W - 1 + tm], 0.0)
        right = jnp.where(col < GRID_W - 1, ext[GRID_W + 1:GRID_W + 1 + tm], 0.0)
        up = jnp.where(row > 0, ext[0:tm], 0.0)
        down = jnp.where(row < n_rows - 1, ext[2 * GRID_W:2 * GRID_W + tm], 0.0)
        cm = lane & 3
        shifted = jnp.where(cm == 0, left, jnp.where(cm == 1, right, jnp.where(cm == 2, up, down)))
    else:
        main = project(xm_ref[0])
        zero = jnp.zeros((1, RWKV_COLS), F32)
        prev = jnp.concatenate([zero, main[:tm - 1]], axis=0)
        nxt = jnp.concatenate([main[1:], zero], axis=0)
        shifted = jnp.where((lane & 1) == 0, prev, nxt)
    p = main + mu_ref[...] * (shifted - main)
    r, v, a, g, bonus, ks, bs, lws = _rwkv_feats(
        p, kk_ref[...], ka_ref[...], rk_ref[...], w0_ref[...], dlb_ref, a0_ref[...], ilb_ref,
        glb_ref[...], hs_ref[...])
    r_ref[0] = r.astype(BF16)
    v_ref[0] = v.astype(BF16)
    a_ref[0] = a.astype(BF16)
    g_ref[0] = g.astype(BF16)
    bon_ref[0] = bonus.astype(BF16)
    for d in range(2):
        k_ref[d, 0] = ks[d].astype(BF16)
        b_ref[d, 0] = bs[d].astype(BF16)
        lw_ref[d, 0] = lws[d]


def _stage_prep(x, shift, scale, g, w_bf16, prm, tm, grid_shift):
    b, t, c = x.shape
    w = RWKV_WIDTH
    mod = pl.BlockSpec((1, 1, c), lambda i, j: (i, 0, 0))
    hb = GRID_W if grid_shift else 8
    per = tm // hb
    last = t // hb - 1
    tok = pl.BlockSpec((1, tm, w), lambda i, j: (i, j, 0))
    tok2 = pl.BlockSpec((2, 1, tm, w), lambda i, j: (0, i, j, 0))
    in_specs = [
        pl.BlockSpec((1, tm, c), lambda i, j: (i, j, 0)),
        pl.BlockSpec((1, hb, c), lambda i, j: (i, jnp.maximum(j * per - 1, 0), 0)),
        pl.BlockSpec((1, hb, c), lambda i, j: (i, jnp.minimum((j + 1) * per, last), 0)),
        mod, mod, _full(g.shape), _full(w_bf16.shape),
    ] + [_full(a.shape) for a in prm]
    return pl.pallas_call(
        functools.partial(_prep_kernel, tm=tm, grid_shift=grid_shift),
        grid=(b, t // tm),
        in_specs=in_specs,
        out_specs=[tok] * 5 + [tok2] * 3,
        out_shape=[jax.ShapeDtypeStruct((b, t, w), BF16)] * 5
        + [jax.ShapeDtypeStruct((2, b, t, w), BF16)] * 2
        + [jax.ShapeDtypeStruct((2, b, t, w), F32)],
        compiler_params=_params("arbitrary", "arbitrary"),
        name="prep_lat" if grid_shift else "prep_ctx",
    )(x, x, x, shift, scale, g, w_bf16, *prm)


def _wkv_prepare(d, r, v, a, k, bb, lw):
    c = WKV_CHUNK
    n = HEAD_DIM
    ii = lax.broadcasted_iota(jnp.int32, (c, c), 0)
    jj = lax.broadcasted_iota(jnp.int32, (c, c), 1)
    incl = (jj <= ii) if d == 0 else (jj >= ii)
    lc = _mm_sel_x(incl.astype(BF16), lw)
    lx = lc - lw
    ltot = lc[c - 1:c] if d == 0 else lc[0:1]
    inv = jnp.exp(-lc)
    tail = jnp.exp(ltot - lc)
    etot = jnp.exp(ltot)
    eye_n = (lax.broadcasted_iota(jnp.int32, (n, n), 0)
             == lax.broadcasted_iota(jnp.int32, (n, n), 1)).astype(F32)
    scale = jnp.concatenate(
        [jnp.broadcast_to(jnp.sum(eye_n * etot[:, h * n:(h + 1) * n], axis=1, keepdims=True), (n, n))
         for h in range(RWKV_HEADS)], axis=1)
    bf = lambda x: x.astype(BF16)
    return (bf(a * jnp.exp(lx)), bf(r * jnp.exp(lc)), bf(k * inv), bf(bb * inv), bf(k * tail),
            bf(bb * tail), bf(v), scale)


def _wkv_consts():
    c = WKV_CHUNK
    n = HEAD_DIM
    gw = WKV_GROUP * n
    assert c == n and n & (n - 1) == 0
    ii = lax.broadcasted_iota(jnp.int32, (c, gw), 0)
    jj = lax.broadcasted_iota(jnp.int32, (c, gw), 1) & (c - 1)
    head_shift = n.bit_length() - 1
    same_head = (lax.broadcasted_iota(jnp.int32, (gw, gw), 0) >> head_shift
                 == lax.broadcasted_iota(jnp.int32, (gw, gw), 1) >> head_shift)

    def bd(x):
        x = x.astype(BF16)
        return jnp.where(same_head, jnp.concatenate([x] * WKV_GROUP, axis=0), jnp.zeros((), BF16))

    return ii, jj, bd


def _wkv_factor(chunks, emit):
    c = WKV_CHUNK
    gw = WKV_GROUP * HEAD_DIM
    n_groups = RWKV_WIDTH // gw
    ii, jj, bd = _wkv_consts()
    eye = (ii == jj).astype(F32)
    nt = ((1,), (1,))
    tn = ((0,), (0,))
    sl = [slice(h * HEAD_DIM, (h + 1) * HEAD_DIM) for h in range(WKV_GROUP)]

    prob = []
    for preps in chunks:
        for d, (at, rt, kt, bt, kh, bh, vb, _) in enumerate(preps):
            incl = (jj <= ii) if d == 0 else (jj >= ii)
            strict = (jj < ii) if d == 0 else (jj > ii)
            for gi in range(n_groups):
                gs = slice(gi * gw, (gi + 1) * gw)
                prob.append(dict(incl=incl, strict=strict, at=at[:, gs], rt=rt[:, gs],
                                 kt=kt[:, gs], bt=bt[:, gs], kh=kh[:, gs], bh=bh[:, gs],
                                 v=vb[:, gs]))
    for p in prob:
        lhs = jnp.concatenate([p["at"], p["rt"]], axis=0) if emit else p["at"]
        p["lhs_z"] = lhs
        gk = _mm(lhs, bd(p["kt"]), nt)
        gb = _mm(lhs, bd(p["bt"]), nt)
        p["a_ab"] = jnp.where(p["strict"], gb[:c], 0.0)
        lhs_v = jnp.where(p["strict"], gk[:c], 0.0)
        if emit:
            lhs_v = jnp.concatenate([lhs_v, jnp.where(p["incl"], gk[c:], 0.0)], axis=0)
            p["a_rb"] = jnp.where(p["incl"], gb[c:], 0.0).astype(BF16)
        p["lhs_v"] = lhs_v
    for p in prob:
        p["tinv"] = eye + p["a_ab"]
        p["x"] = _mm(p["a_ab"], bd(p["a_ab"]))
        p["av"] = _mm(p["lhs_v"], bd(p["v"]))
    for level in range(1, 6):
        for p in prob:
            if level < 5:
                both = _mm(jnp.concatenate([p["x"], p["tinv"]], axis=0), bd(p["x"]))
                p["x"] = both[:c]
                p["tinv"] = p["tinv"] + both[c:]
            else:
                p["tinv"] = (p["tinv"] + _mm(p["tinv"], bd(p["x"]))).astype(BF16)
    per_chunk = 2 * n_groups
    return [prob[j * per_chunk:(j + 1) * per_chunk] for j in range(len(chunks))]


def _wkv_apply(probs, chunks, states, emit):
    c = WKV_CHUNK
    n = HEAD_DIM
    gw = WKV_GROUP * n
    n_groups = RWKV_WIDTH // gw
    _, _, bd = _wkv_consts()
    tn = ((0,), (0,))
    sl = [slice(h * n, (h + 1) * n) for h in range(WKV_GROUP)]
    flat = []
    for prob, zs in zip(probs, states):
        zb = [z.astype(BF16) for z in zs]
        for i, p in enumerate(prob):
            d, gi = divmod(i, n_groups)
            p["z"] = zb[d][:, gi * gw:(gi + 1) * gw]
            flat.append(p)
    for p in flat:
        p["zv"] = _mm(p["lhs_z"], bd(p["z"])) + p["av"]
    for p in flat:
        p["u"] = _mm(p["tinv"], bd(p["zv"][:c])).astype(BF16)
    for p in flat:
        if emit:
            p["y"] = p["zv"][c:] + _mm(p["a_rb"], bd(p["u"]))
        p["z_new"] = [_mm(jnp.concatenate([p["kh"][:, s], p["bh"][:, s]], axis=0),
                          jnp.concatenate([p["v"][:, s], p["u"][:, s]], axis=0), tn) for s in sl]
    out = []
    for prob, preps, zs in zip(probs, chunks, states):
        res = []
        for d, (prep, z) in enumerate(zip(preps, zs)):
            mine = prob[d * n_groups:(d + 1) * n_groups]
            z_cat = jnp.concatenate([m for p in mine for m in p["z_new"]], axis=1) + prep[7] * z
            y = jnp.concatenate([p["y"] for p in mine], axis=1) if emit else None
            res.append((y, z_cat))
        out.append(res)
    return out


def _wkv_kernel(*refs, emit, has_init):
    refs = list(refs)
    ins = [[refs.pop(0) for _ in range(6)] for _ in range(2)]
    z0_ref = refs.pop(0) if has_init else None
    outs = [refs.pop(0) for _ in range(2 if emit else 1)]
    z_scr = refs.pop(0)
    n_seq = z_scr.shape[1]

    @pl.when(pl.program_id(0) == 0)
    def _():
        if has_init:
            z_scr[...] = z0_ref[...]
        else:
            z_scr[...] = jnp.zeros_like(z_scr)

    c = WKV_CHUNK
    per = ins[0][0].shape[1] // c

    def rows_of(d, j):
        sub = j if d == 0 else per - 1 - j
        return slice(sub * c, (sub + 1) * c)

    chunks = []
    for j in range(per):
        for i in range(n_seq):
            preps = []
            for d in range(2):
                rows = rows_of(d, j)
                r_ref, v_ref, a_ref, k_ref, b_ref, lw_ref = ins[d]
                preps.append(_wkv_prepare(d, r_ref[i, rows], v_ref[i, rows], a_ref[i, rows],
                                          k_ref[0, i, rows], b_ref[0, i, rows], lw_ref[0, i, rows]))
            chunks.append(preps)
    probs = _wkv_factor(chunks, emit)
    states = [(z_scr[0, i], z_scr[1, i]) for i in range(n_seq)]
    for j in range(per):
        now = slice(j * n_seq, (j + 1) * n_seq)
        res = _wkv_apply(probs[now], chunks[now], states, emit)
        states = [tuple(z_new for _, z_new in seq) for seq in res]
        if emit:
            for i, seq in enumerate(res):
                for d, (y, _) in enumerate(seq):
                    outs[d][i, rows_of(d, j)] = y
    for i in range(n_seq):
        for d in range(2):
            z_scr[d, i] = states[i][d]
    if not emit:
        outs[0][...] = z_scr[...]


def _stage_wkv(r, v, a, k, bb, lw, z0=None, emit=True):
    b, t, w = r.shape
    c = WKV_CHUNK * WKV_CHUNKS_PER_STEP
    nch = t // c
    n = HEAD_DIM
    pos = (lambda s: s, lambda s: nch - 1 - s)
    in_specs, args = [], []
    for d in range(2):
        tok = pl.BlockSpec((b, c, w), lambda s, d=d: (0, pos[d](s), 0))
        tok2 = pl.BlockSpec((1, b, c, w), lambda s, d=d: (d, 0, pos[d](s), 0))
        in_specs += [tok, tok, tok, tok2, tok2, tok2]
        args += [r, v, a, k, bb, lw]
    zspec = pl.BlockSpec((2, b, n, w), lambda s: (0, 0, 0, 0))
    if z0 is not None:
        in_specs.append(zspec)
        args.append(z0)
    if emit:
        out_specs = [pl.BlockSpec((b, c, w), lambda s, d=d: (0, pos[d](s), 0)) for d in range(2)]
        out_shape = [jax.ShapeDtypeStruct((b, t, w), F32)] * 2
    else:
        out_specs = [zspec]
        out_shape = [jax.ShapeDtypeStruct((2, b, n, w), F32)]
    return pl.pallas_call(
        functools.partial(_wkv_kernel, emit=emit, has_init=z0 is not None),
        grid=(nch,),
        in_specs=in_specs,
        out_specs=out_specs,
        out_shape=out_shape,
        scratch_shapes=[pltpu.VMEM((2, b, n, w), F32)],
        compiler_params=_params("arbitrary"),
        name="wkv_lat" if emit else "wkv_ctx",
    )(*args)


def _to_token_tiles(ref, val, rows):
    for cc in range(val.shape[1] // 128):
        ref[pl.ds(cc, rows, stride=8), :] = val[:, cc * 128:(cc + 1) * 128]


def _from_token_tiles(ref, rows):
    return jnp.concatenate([ref[pl.ds(cc, rows, stride=8), :] for cc in range(8)], axis=1)


def _route_tokens(h, rwt_ref, rb_ref, e_ref, w_ref, rank_ref, cnt_ref, carry, tm):
    logits = _mm_hi_lo(rwt_ref[...], h, dims=((1,), (1,))) + rb_ref[...]
    eio = lax.broadcasted_iota(jnp.int32, (N_EXPERTS, tm), 0)
    vals, sels, idxs = [], [], []
    for _ in range(TOP_K):
        m = jnp.max(logits, axis=0, keepdims=True)
        idx = jnp.min(jnp.where(logits == m, eio, N_EXPERTS), axis=0, keepdims=True)
        sel = eio == idx
        logits = jnp.where(sel, -jnp.inf, logits)
        vals.append(m)
        sels.append(sel)
        idxs.append(idx)
        e_ref[len(vals) - 1:len(vals), :] = idx
    ex = [jnp.exp(vk - vals[0]) for vk in vals]
    tot = ex[0] + ex[1] + ex[2] + ex[3]
    for kk in range(TOP_K):
        w_ref[kk:kk + 1, :] = ex[kk] / tot
    cnt = (sels[0] | sels[1] | sels[2] | sels[3]).astype(F32)
    ti = lax.broadcasted_iota(jnp.int32, (tm, tm), 0)
    tj = lax.broadcasted_iota(jnp.int32, (tm, tm), 1)
    before = _mm(cnt, (ti < tj).astype(F32))
    base = carry[...] + before
    ranks = []
    for kk in range(TOP_K):
        ranks.append(jnp.sum(jnp.where(sels[kk], base, 0.0), axis=0,
                             keepdims=True).astype(jnp.int32))
        rank_ref[kk:kk + 1, :] = ranks[kk]
    new = carry[...] + jnp.sum(cnt, axis=1, keepdims=True)
    carry[...] = new
    cnt_ref[...] = jnp.broadcast_to(new, cnt_ref.shape).astype(jnp.int32)
    return idxs, ranks


def _merge_kernel(x_ref, yf_ref, yb_ref, g_ref, bon_ref, sh_ref, sc_ref, ga_ref, sh2_ref, sc2_ref,
                  n1_ref, w2_ref, gnw_ref, gnb_ref, hs_ref, wor_ref, lnw_ref, lnb_ref, wsp_ref,
                  bsp_ref, wos_ref, wo_ref, n2_ref, rwt_ref, rb_ref,
                  xt_ref, e_ref, w_ref, rank_ref, cnt_ref, xs_ref,
                  carry, h2buf, zeros, dest_v, dest_s, cnt_s, ssem, csem, zsem, *, tm, cap):
    step = pl.program_id(0) * pl.num_programs(1) + pl.program_id(1)
    n_steps = pl.num_programs(0) * pl.num_programs(1)
    dump = N_EXPERTS * cap

    def dest_copy(which):
        return pltpu.make_async_copy(dest_v, dest_s.at[which], csem)

    def drain(which):
        for _ in range(TOP_K):
            pltpu.make_async_copy(h2buf.at[which], h2buf.at[which], ssem.at[which]).wait()

    @pl.when(step == 0)
    def _():
        carry[...] = jnp.zeros_like(carry)
        zeros[...] = jnp.zeros_like(zeros)
        h2buf[2] = jnp.zeros(h2buf.shape[1:], F32)
        h2buf[3] = jnp.zeros(h2buf.shape[1:], F32)
        for which in range(2):
            dest_v[...] = (dump + which * TOP_K * tm
                           + lax.broadcasted_iota(jnp.int32, (TOP_K, tm), 0) * tm
                           + lax.broadcasted_iota(jnp.int32, (TOP_K, tm), 1))
            dest_copy(which).start()
            dest_copy(which).wait()

    @pl.when(step > 1)
    def _():
        drain(step % 4)

    src = (step + 2) % 4
    for t in range(tm):
        for kk in range(TOP_K):
            _tile_copy(h2buf.at[src], t, xs_ref, dest_s[step % 2, kk, t], ssem.at[src]).start(
                priority=kk % 2)

    x = x_ref[0]
    h = _norm_mod(x, n1_ref[...], sh_ref[0], sc_ref[0])
    p2 = _mm(h, w2_ref[...])

    ps = p2[:, :2 * SGU_WIDTH]
    ge = 0.5 * ps * (1.0 + lax.erf(ps * (1.0 / math.sqrt(2.0))))
    u = ge[:, :SGU_WIDTH]
    z = ge[:, SGU_WIDTH:]
    mu = jnp.mean(z, axis=-1, keepdims=True)
    zc = z - mu
    var = jnp.mean(zc * zc, axis=-1, keepdims=True)
    z = zc * lax.rsqrt(var + LN_EPS) * lnw_ref[...] + lnb_ref[...]
    gw = SGU_WIDTH // SGU_GROUPS
    rows = []
    for c in range(tm // SGU_CHUNK):
        zc = z[c * SGU_CHUNK:(c + 1) * SGU_CHUNK]
        cols = [_mm(wsp_ref[gi], zc[:, gi * gw:(gi + 1) * gw]) for gi in range(SGU_GROUPS)]
        rows.append(jnp.concatenate(cols, axis=1) + bsp_ref[...])
    s = jnp.concatenate(rows, axis=0)
    y_b = _mm(u * s, wos_ref[...])

    y = yf_ref[0] + yb_ref[0]
    hs = hs_ref[...]
    ym = _mm(y, hs) * (1.0 / HEAD_DIM)
    yc = y - ym
    yv = _mm(yc * yc, hs) * (1.0 / HEAD_DIM)
    yn = yc * lax.rsqrt(yv + GN_EPS) * gnw_ref[...] + gnb_ref[...]
    y_a = _mm((yn + bon_ref[0]) * g_ref[0], wor_ref[...])

    gates = jax.nn.sigmoid(p2[:, 2 * SGU_WIDTH:])
    mix = gates[:, :D_MODEL] * y_a + gates[:, D_MODEL:] * y_b
    x1 = x + ga_ref[0] * _mm(mix, wo_ref[...])

    h2 = _norm_mod(x1, n2_ref[...], sh2_ref[0], sc2_ref[0])
    _to_token_tiles(h2buf.at[step % 4], h2, tm)
    _to_token_tiles(xt_ref, x1, tm)
    idxs, ranks = _route_tokens(h2, rwt_ref, rb_ref, e_ref, w_ref, rank_ref, cnt_ref, carry, tm)

    @pl.when(step > 0)
    def _():
        dest_copy((step - 1) % 2).wait()

    for kk in range(TOP_K):
        dest_v[kk:kk + 1, :] = idxs[kk] * cap + ranks[kk]
    dest_copy(step % 2).start()

    @pl.when(step == n_steps - 1)
    def _():
        dest_copy(step % 2).wait()
        for back in (1, 0):

            def flush(t, c, back=back):
                for kk in range(TOP_K):
                    _tile_copy(h2buf.at[(step - back) % 4], t, xs_ref,
                               dest_s[(step - back) % 2, kk, t],
                               ssem.at[(step - back) % 4]).start(priority=kk % 2)
                return c

            lax.fori_loop(0, tm, flush, 0)
        counts = pltpu.make_async_copy(cnt_ref, cnt_s, csem)
        counts.start()
        counts.wait()
        for e in range(N_EXPERTS):
            first = pl.multiple_of((e * cap + cnt_s[e, 0]) * 8, 8)
            pltpu.make_async_copy(zeros, xs_ref.at[pl.ds(first, zeros.shape[0])], zsem).start()
        for e in range(N_EXPERTS):
            pltpu.make_async_copy(zeros, xs_ref.at[pl.ds(0, zeros.shape[0])], zsem).wait()
        for which in range(4):
            drain(which)


def _stage_merge(x, yf, yb, g, bonus, mods, consts, tm):
    b, t, d = x.shape
    n = b * t
    w = RWKV_WIDTH
    per = t // tm
    cap = n + EXPERT_ROWS
    assert d == 8 * 128, "token-tile layout stores one (8, 128) tile per token"
    mod = pl.BlockSpec((1, 1, d), lambda i, j: (i, 0, 0))
    tok = pl.BlockSpec((1, tm, w), lambda i, j: (i, j, 0))
    tiles = pl.BlockSpec((tm * 8, 128), lambda i, j: (i * per + j, 0))
    lane = pl.BlockSpec((TOP_K, tm), lambda i, j: (0, i * per + j))
    return pl.pallas_call(
        functools.partial(_merge_kernel, tm=tm, cap=cap),
        grid=(b, per),
        in_specs=[pl.BlockSpec((1, tm, d), lambda i, j: (i, j, 0)), tok, tok, tok, tok]
        + [mod] * len(mods) + [_full(a.shape) for a in consts],
        out_specs=[tiles, lane, lane, lane, _full((N_EXPERTS, 128)),
                   pl.BlockSpec(memory_space=pl.ANY)],
        out_shape=[jax.ShapeDtypeStruct((n * 8, 128), F32),
                   jax.ShapeDtypeStruct((TOP_K, n), jnp.int32),
                   jax.ShapeDtypeStruct((TOP_K, n), F32),
                   jax.ShapeDtypeStruct((TOP_K, n), jnp.int32),
                   jax.ShapeDtypeStruct((N_EXPERTS, 128), jnp.int32),
                   jax.ShapeDtypeStruct(((N_EXPERTS * cap + 2 * TOP_K * tm) * 8, 128), F32)],
        scratch_shapes=[pltpu.VMEM((N_EXPERTS, 1), F32),
                        pltpu.VMEM((4, tm * 8, 128), F32),
                        pltpu.VMEM((EXPERT_ROWS * 8, 128), F32),
                        pltpu.VMEM((TOP_K, tm), jnp.int32),
                        pltpu.SMEM((2, TOP_K, tm), jnp.int32),
                        pltpu.SMEM((N_EXPERTS, 128), jnp.int32),
                        pltpu.SemaphoreType.DMA((4,)), pltpu.SemaphoreType.DMA(()),
                        pltpu.SemaphoreType.DMA(())],
        compiler_params=_params("arbitrary", "arbitrary"),
        name="merge",
    )(x, yf, yb, g, bonus, *mods, *consts), cap


def _expert_kernel(be_ref, nu_ref, nx_ref, xb_ref, x_ref, bg_ref, bu_ref, bd_ref, wg_hbm, wu_hbm, wd_hbm,
                   o_ref, stage, wg_s, wu_s, wd_s, sem):
    i = pl.program_id(0)
    e = be_ref[i]
    used = i < nu_ref[0]
    prev = be_ref[jnp.maximum(i - 1, 0)]

    def fetch(expert):
        return [pltpu.make_async_copy(w.at[expert], stage.at[j], sem.at[j])
                for j, w in enumerate((wg_hbm, wu_hbm, wd_hbm))]

    @pl.when(i == 0)
    def _():
        for cp in fetch(e):
            cp.start()

    @pl.when(used & ((i == 0) | (e != prev)))
    def _():
        for cp, w_s, j in zip(fetch(e), (wg_s, wu_s, wd_s), range(3)):
            cp.wait()
            w_s[...] = stage[j].astype(BF16)
        nxt = nx_ref[e]

        @pl.when(nxt >= 0)
        def _():
            for cp in fetch(nxt):
                cp.start()

    @pl.when(used)
    def _():
        x = _from_token_tiles(x_ref, EXPERT_ROWS)
        gate = _mm(x, wg_s[...]) + bg_ref[0]
        up = _mm(x, wu_s[...]) + bu_ref[0]
        gate = jnp.minimum(gate, SWIGLU_LIMIT)
        up = jnp.clip(up, -SWIGLU_LIMIT, SWIGLU_LIMIT)
        act = gate * jax.nn.sigmoid(SWIGLU_ALPHA * gate) * (up + 1.0)
        _to_token_tiles(o_ref, _mm(act, wd_s[...]) + bd_ref[0], EXPERT_ROWS)

    @pl.when(jnp.logical_not(used))
    def _():
        o_ref[...] = jnp.zeros_like(o_ref)


def _stage_experts(block_e, n_used, next_e, x_block, n_rows, xs, wg, bg, wu, bu, wd, bd):
    d, f = wg.shape[1:]
    assert d == f, "one staging buffer shape serves all three weight matrices"
    bm = EXPERT_ROWS
    bspec = lambda n_: pl.BlockSpec((1, 1, n_), lambda i, be, *_: (be[i], 0, 0))
    hbm = pl.BlockSpec(memory_space=pl.ANY)
    return pl.pallas_call(
        _expert_kernel,
        grid_spec=pltpu.PrefetchScalarGridSpec(
            num_scalar_prefetch=4,
            grid=(n_rows // bm,),
            in_specs=[pl.BlockSpec((bm * 8, 128), lambda i, be, nu, nx, xb: (xb[i], 0)),
                      bspec(f), bspec(f), bspec(d), hbm, hbm, hbm],
            out_specs=pl.BlockSpec((bm * 8, 128), lambda i, *_: (i, 0)),
            scratch_shapes=[pltpu.VMEM((3, d, f), F32), pltpu.VMEM((d, f), BF16),
                            pltpu.VMEM((d, f), BF16), pltpu.VMEM((f, d), BF16),
                            pltpu.SemaphoreType.DMA((3,))]),
        out_shape=jax.ShapeDtypeStruct((n_rows * 8, 128), F32),
        compiler_params=_params("arbitrary"),
        name="experts",
    )(block_e, n_used, next_e, x_block, xs, bg, bu, bd, wg, wu, wd)


def _tile_copy(src_ref, src_row, dst_ref, dst_row, sem):
    src = src_ref.at[pl.ds(pl.multiple_of(src_row * 8, 8), 8)]
    dst = dst_ref.at[pl.ds(pl.multiple_of(dst_row * 8, 8), 8)]
    return pltpu.make_async_copy(src, dst, sem)


def _combine_kernel(dest_ref, w_ref, xt_ref, ga_ref, g_ref, yb_ref, o_ref, buf0, buf1, res, sem,
                    *, tm, n_tok):
    i = pl.program_id(0)
    ga = ga_ref[0]
    bufs = (buf0, buf1)

    def issue(tile, slot, t):
        for kk in range(TOP_K):
            _tile_copy(yb_ref, dest_ref[kk * n_tok + tile * tm + t], bufs[slot].at[kk], t,
                       sem.at[slot]).start(priority=kk % 2)

    def combine(slot, t):
        rows = pl.ds(pl.multiple_of(t * 8, 8), 8)
        acc = bufs[slot][0, rows, :] * w_ref[i * tm + t]
        for kk in range(1, TOP_K):
            acc = acc + bufs[slot][kk, rows, :] * w_ref[kk * n_tok + i * tm + t]
        res[rows, :] = xt_ref[rows, :] + ga * acc

    def loop(body):
        for t in range(tm):
            body(t)

    @pl.when(i == 0)
    def _():
        loop(lambda t: issue(0, 0, t))

    for slot in range(2):
        @pl.when(i % 2 == slot)
        def _():
            for kk in range(TOP_K):
                pltpu.make_async_copy(bufs[slot].at[kk], bufs[slot].at[kk], sem.at[slot]).wait()

            @pl.when(i + 1 < pl.num_programs(0))
            def _():
                loop(lambda t: (issue(i + 1, 1 - slot, t), combine(slot, t)))

            @pl.when(i + 1 >= pl.num_programs(0))
            def _():
                loop(lambda t: combine(slot, t))

    x = _from_token_tiles(res, tm)
    o_ref[...] = x * lax.rsqrt(jnp.mean(x * x, axis=-1, keepdims=True) + RMS_EPS) * g_ref[...]


def _stage_combine(dest_flat, w_flat, x1t, ga_t, g_t, yb, tokens_per_batch, tm):
    n = x1t.shape[0] // 8
    d = 8 * 128
    per = tokens_per_batch // tm
    return pl.pallas_call(
        functools.partial(_combine_kernel, tm=tm, n_tok=n),
        grid_spec=pltpu.PrefetchScalarGridSpec(
            num_scalar_prefetch=2,
            grid=(n // tm,),
            in_specs=[pl.BlockSpec((tm * 8, 128), lambda i, *_: (i, 0)),
                      pl.BlockSpec((1, 8, 128), lambda i, *_: (i // per, 0, 0)),
                      pl.BlockSpec((1, d), lambda i, *_: (0, 0)),
                      pl.BlockSpec(memory_space=pl.ANY)],
            out_specs=pl.BlockSpec((tm, d), lambda i, *_: (i, 0)),
            scratch_shapes=[pltpu.VMEM((TOP_K, tm * 8, 128), F32),
                            pltpu.VMEM((TOP_K, tm * 8, 128), F32),
                            pltpu.VMEM((tm * 8, 128), F32),
                            pltpu.SemaphoreType.DMA((2,))]),
        out_shape=jax.ShapeDtypeStruct((n, d), F32),
        compiler_params=_params("arbitrary"),
        name="combine",
    )(dest_flat, w_flat, x1t, ga_t, g_t, yb)


def _rwkv_branch(x, ctx, mods_lat, mods_ctx, n1, w_rwkv, prm):
    r, v, a, _, _, k, bb, lw = _stage_prep(ctx, *mods_ctx, n1, w_rwkv, prm, tm=ctx.shape[1],
                                           grid_shift=False)
    (z_ctx,) = _stage_wkv(r, v, a, k, bb, lw, emit=False)
    r, v, a, g, bonus, k, bb, lw = _stage_prep(x, *mods_lat, n1, w_rwkv, prm, tm=512,
                                               grid_shift=True)
    y_f, y_b = _stage_wkv(r, v, a, k, bb, lw, z0=z_ctx, emit=True)
    return y_f, y_b, g, bonus


def _route(top_e, rank, counts, n_tok, cap):
    bm = EXPERT_ROWS
    n_rows = n_tok * TOP_K + N_EXPERTS * bm
    n_blocks = n_rows // bm
    padded = (counts + bm - 1) // bm * bm
    pad_end = jnp.cumsum(padded)
    pad_start = pad_end - padded
    experts = jnp.arange(N_EXPERTS, dtype=jnp.int32)
    start_of = jnp.sum(jnp.where(top_e[..., None] == experts, pad_start, 0), axis=-1)
    dest = (start_of + rank).astype(jnp.int32).reshape(-1)
    block_start = jnp.arange(n_blocks, dtype=jnp.int32) * bm
    n_used = (pad_end[-1] // bm).astype(jnp.int32).reshape(1)
    block_e = jnp.minimum(jnp.sum(pad_end[None, :] <= block_start[:, None], axis=1),
                          N_EXPERTS - 1).astype(jnp.int32)
    is_e = block_e[:, None] == experts[None, :]
    within = block_start - jnp.sum(jnp.where(is_e, pad_start, 0), axis=1)
    x_block = (block_e * (cap // bm) + within // bm).astype(jnp.int32)
    last_used = jnp.sum(jnp.where(jnp.arange(n_blocks) == n_used[0] - 1, x_block, 0))
    x_block = jnp.where(jnp.arange(n_blocks) < n_used[0], x_block, last_used).astype(jnp.int32)
    first_at = lax.cummin(jnp.where(padded > 0, experts, N_EXPERTS), axis=0, reverse=True)
    next_e = jnp.concatenate([first_at[1:], jnp.full((1,), N_EXPERTS, jnp.int32)])
    next_e = jnp.where(next_e < N_EXPERTS, next_e, -1).astype(jnp.int32)
    return dest, block_e, n_used, next_e, x_block, n_rows


def kernel(x, c, ctx, c_ctx, w_ada, b_ada, norm1_g, w_in, shift_mu, decay_w0, decay_lora_b,
           iclr_a0, iclr_lora_b, gate_lora_b, k_k, k_a, r_k, gn_w, gn_b, w_out_rwkv,
           sgu_ln_w, sgu_ln_b, sgu_w_spatial, sgu_b_spatial, w_out_sgu, w_o, norm2_g,
           router_w, router_b, exp_w_gate, exp_b_gate, exp_w_up, exp_b_up, exp_w_down,
           exp_b_down, final_norm_g):
    assert w_ada.shape[0] == 1, "single-layer problem"
    b, t, d = x.shape
    n_tok = b * t
    w = RWKV_WIDTH
    row = lambda a: a.reshape(1, -1)

    cs = jnp.zeros((8, d), F32).at[:b].set(c).at[b].set(c_ctx)
    mod = _stage_mods(cs, w_ada[0], row(b_ada[0]))
    sh1, sc1, ga1, sh2, sc2, ga2 = [m[:b, None, :] for m in jnp.split(mod, 6, axis=-1)]
    csh1, csc1 = [jnp.broadcast_to(m[b][None, None, :], (b, 1, d))
                  for m in jnp.split(mod, 6, axis=-1)[:2]]

    w_in_bf = w_in[0].astype(BF16)
    n1 = row(norm1_g[0])
    head_id = jnp.arange(w, dtype=jnp.int32) // HEAD_DIM
    headsum = (head_id[:, None] == head_id[None, :]).astype(BF16)
    prm = [row(shift_mu[0]), row(k_k[0]), row(k_a[0]), row(r_k[0]), decay_w0[0],
           decay_lora_b[0], iclr_a0[0], iclr_lora_b[0], gate_lora_b[0], headsum]
    y_f, y_b, g, bonus = _rwkv_branch(x, ctx, (sh1, sc1), (csh1, csc1), n1,
                                      w_in_bf[:, :RWKV_COLS], prm)

    bsp = jnp.repeat(sgu_b_spatial[0].T, SGU_WIDTH // SGU_GROUPS, axis=1)
    consts = [n1, w_in_bf[:, RWKV_COLS:], row(gn_w[0]), row(gn_b[0]), headsum,
              w_out_rwkv[0].astype(BF16), row(sgu_ln_w[0]), row(sgu_ln_b[0]),
              sgu_w_spatial[0].astype(BF16), bsp, w_out_sgu[0].astype(BF16),
              w_o[0].astype(BF16), row(norm2_g[0]), router_w[0].T, router_b[0].reshape(-1, 1)]
    (x1t, top_e, top_w, rank, counts, xs), cap = _stage_merge(
        x, y_f, y_b, g, bonus, (sh1, sc1, ga1, sh2, sc2), consts, tm=256)
    dest, block_e, n_used, next_e, x_block, n_rows = _route(top_e, rank, counts[:, 0], n_tok, cap)
    e3 = lambda a: a.reshape(N_EXPERTS, 1, -1)
    yb = _stage_experts(block_e, n_used, next_e, x_block, n_rows, xs, exp_w_gate[0],
                        e3(exp_b_gate[0]), exp_w_up[0], e3(exp_b_up[0]), exp_w_down[0],
                        e3(exp_b_down[0]))
    out = _stage_combine(dest, top_w.reshape(-1), x1t, ga2.reshape(b, 8, 128),
                         row(final_norm_g), yb, t, tm=256)
    return out.reshape(b, t, d)
```

```python
import functools
import math

import jax
import jax.numpy as jnp
from jax import lax
from jax.experimental import pallas as pl
from jax.experimental.pallas import tpu as pltpu

F32 = jnp.float32
BF16 = jnp.bfloat16
HIGHEST = lax.Precision.HIGHEST

D_MODEL = 1024
GRID_W = 64
RWKV_HEADS = 8
HEAD_DIM = 64
RWKV_WIDTH = RWKV_HEADS * HEAD_DIM
DECAY_LORA = 64
ICLR_LORA = 64
GATE_LORA = 128
RWKV_COLS = 3 * RWKV_WIDTH + 2 * DECAY_LORA + 2 * ICLR_LORA + GATE_LORA
SGU_WIDTH = 512
SGU_GROUPS = 8
SGU_CHUNK = 128
N_EXPERTS = 32
TOP_K = 4
SWIGLU_LIMIT = 7.0
SWIGLU_ALPHA = 1.702
RMS_EPS = 1e-6
LN_EPS = 1e-5
GN_EPS = 64e-5

WKV_CHUNK = 64
WKV_GROUP = 4
EXPERT_ROWS = 512
VMEM_LIMIT = 48 * 1024 * 1024


def _params(*sem):
    return pltpu.CompilerParams(dimension_semantics=sem, vmem_limit_bytes=VMEM_LIMIT)


def _mm(a, b, dims=((1,), (0,)), exact=False):
    dn = (dims, ((), ()))
    if exact:
        return lax.dot_general(a, b, dn, precision=HIGHEST, preferred_element_type=F32)
    return lax.dot_general(a.astype(BF16), b.astype(BF16), dn, preferred_element_type=F32)


def _split3(x):
    hi = x.astype(BF16)
    r1 = x - hi.astype(F32)
    mid = r1.astype(BF16)
    lo = (r1 - mid.astype(F32)).astype(BF16)
    return hi, mid, lo


def _mm_hi_lo(a, b, dims):
    a_hi = a.astype(BF16)
    b_hi = b.astype(BF16)
    a_lo = a - a_hi.astype(F32)
    b_lo = b - b_hi.astype(F32)
    return _mm(a_hi, b_hi, dims) + _mm(a_hi, b_lo, dims) + _mm(a_lo, b_hi, dims)


def _mm_sel_x(sel, x):
    return sum(_mm(sel, p) for p in _split3(x))


def _full(shape):
    n = len(shape)
    return pl.BlockSpec(shape, lambda *_: (0,) * n)


def _norm_mod(x, g, shift, scale):
    y = x * lax.rsqrt(jnp.mean(x * x, axis=-1, keepdims=True) + RMS_EPS) * g
    return y * (1.0 + scale) + shift


def _mods_kernel(c_ref, w_ref, b_ref, o_ref):
    c = c_ref[...]
    s = c * jax.nn.sigmoid(c)
    o_ref[...] = _mm(s, w_ref[...], exact=True) + b_ref[...]


def _stage_mods(cs, w_ada, b_ada):
    rows, d = cs.shape
    n = w_ada.shape[1]
    tn = 1536
    return pl.pallas_call(
        _mods_kernel,
        grid=(n // tn,),
        in_specs=[_full((rows, d)),
                  pl.BlockSpec((d, tn), lambda j: (0, j)),
                  pl.BlockSpec((1, tn), lambda j: (0, j))],
        out_specs=pl.BlockSpec((rows, tn), lambda j: (0, j)),
        out_shape=jax.ShapeDtypeStruct((rows, n), F32),
        compiler_params=_params("arbitrary"),
        name="mods",
    )(cs, w_ada, b_ada)


def _rwkv_feats(p, mu_kk, mu_ka, r_k, w0, dlb, a0, ilb, glb, headsum):
    w = RWKV_WIDTH
    r = p[:, 0:w]
    k = p[:, w:2 * w]
    v = p[:, 2 * w:3 * w]
    o = 3 * w
    wd = (p[:, o:o + DECAY_LORA], p[:, o + DECAY_LORA:o + 2 * DECAY_LORA])
    o += 2 * DECAY_LORA
    ad = (p[:, o:o + ICLR_LORA], p[:, o + ICLR_LORA:o + 2 * ICLR_LORA])
    o += 2 * ICLR_LORA
    gd = p[:, o:o + GATE_LORA]

    kk = k * mu_kk
    kk = kk * lax.rsqrt(_mm(kk * kk, headsum) + 1e-12)
    g = _mm(jax.nn.sigmoid(gd), glb)
    ks, bs, lws = [], [], []
    ksum = None
    for d in range(2):
        z = w0[d:d + 1] + _mm(jnp.tanh(wd[d]), dlb[d])
        lws.append(-math.exp(-0.5) * jax.nn.sigmoid(z))
        ic = jax.nn.sigmoid(a0[d:d + 1] + _mm(ad[d], ilb[d]))
        kd = k * (1.0 + (ic - 1.0) * mu_ka)
        ks.append(kd)
        bs.append(kk * ic)
        ksum = kd if ksum is None else ksum + kd
    bonus = _mm(r * ksum * r_k, headsum) * v
    return r, v, -kk, g, bonus, ks, bs, lws


def _prep_kernel(xm_ref, xp_ref, xn_ref, sh_ref, sc_ref, n1_ref, w_ref,
                 mu_ref, kk_ref, ka_ref, rk_ref, w0_ref, dlb_ref, a0_ref, ilb_ref, glb_ref, hs_ref,
                 r_ref, v_ref, a_ref, g_ref, bon_ref, k_ref, b_ref, lw_ref, *, tm, grid_shift):
    project = lambda x: _mm(_norm_mod(x, n1_ref[...], sh_ref[0], sc_ref[0]), w_ref[...])
    lane = lax.broadcasted_iota(jnp.int32, (1, RWKV_COLS), 1)
    if grid_shift:
        ext = project(jnp.concatenate([xp_ref[0], xm_ref[0], xn_ref[0]], axis=0))
        main = ext[GRID_W:GRID_W + tm]
        t = pl.program_id(1) * tm + lax.broadcasted_iota(jnp.int32, (tm, 1), 0)
        col = t & (GRID_W - 1)
        row = t >> (GRID_W.bit_length() - 1)
        n_rows = pl.num_programs(1) * tm // GRID_W
        left = jnp.where(col > 0, ext[GRID_W - 1:GRID_W - 1 + tm], 0.0)
        right = jnp.where(col < GRID_W - 1, ext[GRID_W + 1:GRID_W + 1 + tm], 0.0)
        up = jnp.where(row > 0, ext[0:tm], 0.0)
        down = jnp.where(row < n_rows - 1, ext[2 * GRID_W:2 * GRID_W + tm], 0.0)
        cm = lane & 3
        shifted = jnp.where(cm == 0, left, jnp.where(cm == 1, right, jnp.where(cm == 2, up, down)))
    else:
        main = project(xm_ref[0])
        zero = jnp.zeros((1, RWKV_COLS), F32)
        prev = jnp.concatenate([zero, main[:tm - 1]], axis=0)
        nxt = jnp.concatenate([main[1:], zero], axis=0)
        shifted = jnp.where((lane & 1) == 0, prev, nxt)
    p = main + mu_ref[...] * (shifted - main)
    r, v, a, g, bonus, ks, bs, lws = _rwkv_feats(
        p, kk_ref[...], ka_ref[...], rk_ref[...], w0_ref[...], dlb_ref, a0_ref[...], ilb_ref,
        glb_ref[...], hs_ref[...])
    r_ref[0] = r.astype(BF16)
    v_ref[0] = v.astype(BF16)
    a_ref[0] = a.astype(BF16)
    g_ref[0] = g.astype(BF16)
    bon_ref[0] = bonus.astype(BF16)
    for d in range(2):
        k_ref[d, 0] = ks[d].astype(BF16)
        b_ref[d, 0] = bs[d].astype(BF16)
        lw_ref[d, 0] = lws[d]


def _stage_prep(x, shift, scale, g, w_bf16, prm, tm, grid_shift):
    b, t, c = x.shape
    w = RWKV_WIDTH
    mod = pl.BlockSpec((1, 1, c), lambda i, j: (i, 0, 0))
    hb = GRID_W if grid_shift else 8
    per = tm // hb
    last = t // hb - 1
    tok = pl.BlockSpec((1, tm, w), lambda i, j: (i, j, 0))
    tok2 = pl.BlockSpec((2, 1, tm, w), lambda i, j: (0, i, j, 0))
    in_specs = [
        pl.BlockSpec((1, tm, c), lambda i, j: (i, j, 0)),
        pl.BlockSpec((1, hb, c), lambda i, j: (i, jnp.maximum(j * per - 1, 0), 0)),
        pl.BlockSpec((1, hb, c), lambda i, j: (i, jnp.minimum((j + 1) * per, last), 0)),
        mod, mod, _full(g.shape), _full(w_bf16.shape),
    ] + [_full(a.shape) for a in prm]
    return pl.pallas_call(
        functools.partial(_prep_kernel, tm=tm, grid_shift=grid_shift),
        grid=(b, t // tm),
        in_specs=in_specs,
        out_specs=[tok] * 5 + [tok2] * 3,
        out_shape=[jax.ShapeDtypeStruct((b, t, w), BF16)] * 5
        + [jax.ShapeDtypeStruct((2, b, t, w), BF16)] * 2
        + [jax.ShapeDtypeStruct((2, b, t, w), F32)],
        compiler_params=_params("arbitrary", "arbitrary"),
        name="prep_lat" if grid_shift else "prep_ctx",
    )(x, x, x, shift, scale, g, w_bf16, *prm)


def _wkv_prepare(d, r, v, a, k, bb, lw):
    c = WKV_CHUNK
    n = HEAD_DIM
    ii = lax.broadcasted_iota(jnp.int32, (c, c), 0)
    jj = lax.broadcasted_iota(jnp.int32, (c, c), 1)
    incl = (jj <= ii) if d == 0 else (jj >= ii)
    lc = _mm_sel_x(incl.astype(BF16), lw)
    lx = lc - lw
    ltot = lc[c - 1:c] if d == 0 else lc[0:1]
    inv = jnp.exp(-lc)
    tail = jnp.exp(ltot - lc)
    etot = jnp.exp(ltot)
    eye_n = (lax.broadcasted_iota(jnp.int32, (n, n), 0)
             == lax.broadcasted_iota(jnp.int32, (n, n), 1)).astype(F32)
    scale = jnp.concatenate(
        [jnp.broadcast_to(jnp.sum(eye_n * etot[:, h * n:(h + 1) * n], axis=1, keepdims=True), (n, n))
         for h in range(RWKV_HEADS)], axis=1)
    bf = lambda x: x.astype(BF16)
    return (bf(a * jnp.exp(lx)), bf(r * jnp.exp(lc)), bf(k * inv), bf(bb * inv), bf(k * tail),
            bf(bb * tail), bf(v), scale)


def _wkv_consts():
    c = WKV_CHUNK
    n = HEAD_DIM
    gw = WKV_GROUP * n
    assert c == n and n & (n - 1) == 0
    ii = lax.broadcasted_iota(jnp.int32, (c, gw), 0)
    jj = lax.broadcasted_iota(jnp.int32, (c, gw), 1) & (c - 1)
    head_shift = n.bit_length() - 1
    same_head = (lax.broadcasted_iota(jnp.int32, (gw, gw), 0) >> head_shift
                 == lax.broadcasted_iota(jnp.int32, (gw, gw), 1) >> head_shift)

    def bd(x):
        x = x.astype(BF16)
        return jnp.where(same_head, jnp.concatenate([x] * WKV_GROUP, axis=0), jnp.zeros((), BF16))

    return ii, jj, bd


def _wkv_factor(chunks, emit):
    c = WKV_CHUNK
    gw = WKV_GROUP * HEAD_DIM
    n_groups = RWKV_WIDTH // gw
    ii, jj, bd = _wkv_consts()
    eye = (ii == jj).astype(F32)
    nt = ((1,), (1,))
    tn = ((0,), (0,))
    sl = [slice(h * HEAD_DIM, (h + 1) * HEAD_DIM) for h in range(WKV_GROUP)]

    prob = []
    for preps in chunks:
        for d, (at, rt, kt, bt, kh, bh, vb, _) in enumerate(preps):
            incl = (jj <= ii) if d == 0 else (jj >= ii)
            strict = (jj < ii) if d == 0 else (jj > ii)
            for gi in range(n_groups):
                gs = slice(gi * gw, (gi + 1) * gw)
                prob.append(dict(incl=incl, strict=strict, at=at[:, gs], rt=rt[:, gs],
                                 kt=kt[:, gs], bt=bt[:, gs], kh=kh[:, gs], bh=bh[:, gs],
                                 v=vb[:, gs]))
    for p in prob:
        lhs = jnp.concatenate([p["at"], p["rt"]], axis=0) if emit else p["at"]
        p["lhs_z"] = lhs
        gk = _mm(lhs, bd(p["kt"]), nt)
        gb = _mm(lhs, bd(p["bt"]), nt)
        p["a_ab"] = jnp.where(p["strict"], gb[:c], 0.0)
        lhs_v = jnp.where(p["strict"], gk[:c], 0.0)
        if emit:
            lhs_v = jnp.concatenate([lhs_v, jnp.where(p["incl"], gk[c:], 0.0)], axis=0)
            p["a_rb"] = jnp.where(p["incl"], gb[c:], 0.0).astype(BF16)
        p["lhs_v"] = lhs_v
    for p in prob:
        p["tinv"] = eye + p["a_ab"]
        p["x"] = _mm(p["a_ab"], bd(p["a_ab"]))
        p["av"] = _mm(p["lhs_v"], bd(p["v"]))
    for level in range(1, 6):
        for p in prob:
            if level < 5:
                both = _mm(jnp.concatenate([p["x"], p["tinv"]], axis=0), bd(p["x"]))
                p["x"] = both[:c]
                p["tinv"] = p["tinv"] + both[c:]
            else:
                p["tinv"] = (p["tinv"] + _mm(p["tinv"], bd(p["x"]))).astype(BF16)
    per_chunk = 2 * n_groups
    return [prob[j * per_chunk:(j + 1) * per_chunk] for j in range(len(chunks))]


def _wkv_apply(probs, chunks, states, emit):
    c = WKV_CHUNK
    n = HEAD_DIM
    gw = WKV_GROUP * n
    n_groups = RWKV_WIDTH // gw
    _, _, bd = _wkv_consts()
    tn = ((0,), (0,))
    sl = [slice(h * n, (h + 1) * n) for h in range(WKV_GROUP)]
    flat = []
    for prob, zs in zip(probs, states):
        zb = [z.astype(BF16) for z in zs]
        for i, p in enumerate(prob):
            d, gi = divmod(i, n_groups)
            p["z"] = zb[d][:, gi * gw:(gi + 1) * gw]
            flat.append(p)
    for p in flat:
        p["zv"] = _mm(p["lhs_z"], bd(p["z"])) + p["av"]
    for p in flat:
        p["u"] = _mm(p["tinv"], bd(p["zv"][:c])).astype(BF16)
    for p in flat:
        if emit:
            p["y"] = p["zv"][c:] + _mm(p["a_rb"], bd(p["u"]))
        p["z_new"] = [_mm(jnp.concatenate([p["kh"][:, s], p["bh"][:, s]], axis=0),
                          jnp.concatenate([p["v"][:, s], p["u"][:, s]], axis=0), tn) for s in sl]
    out = []
    for prob, preps, zs in zip(probs, chunks, states):
        res = []
        for d, (prep, z) in enumerate(zip(preps, zs)):
            mine = prob[d * n_groups:(d + 1) * n_groups]
            z_cat = jnp.concatenate([m for p in mine for m in p["z_new"]], axis=1) + prep[7] * z
            y = jnp.concatenate([p["y"] for p in mine], axis=1) if emit else None
            res.append((y, z_cat))
        out.append(res)
    return out


def _wkv_kernel(*refs, emit, has_init):
    refs = list(refs)
    ins = [[refs.pop(0) for _ in range(6)] for _ in range(2)]
    z0_ref = refs.pop(0) if has_init else None
    outs = [refs.pop(0) for _ in range(2 if emit else 1)]
    z_scr = refs.pop(0)
    n_seq = z_scr.shape[1]

    @pl.when(pl.program_id(0) == 0)
    def _():
        if has_init:
            z_scr[...] = z0_ref[...]
        else:
            z_scr[...] = jnp.zeros_like(z_scr)

    chunks = []
    for i in range(n_seq):
        preps = []
        for d in range(2):
            r_ref, v_ref, a_ref, k_ref, b_ref, lw_ref = ins[d]
            preps.append(_wkv_prepare(d, r_ref[i], v_ref[i], a_ref[i], k_ref[0, i], b_ref[0, i],
                                      lw_ref[0, i]))
        chunks.append(preps)
    probs = _wkv_factor(chunks, emit)
    states = [(z_scr[0, i], z_scr[1, i]) for i in range(n_seq)]
    for i, res in enumerate(_wkv_apply(probs, chunks, states, emit)):
        for d, (y, z_new) in enumerate(res):
            z_scr[d, i] = z_new
            if emit:
                outs[d][i] = y
    if not emit:
        outs[0][...] = z_scr[...]


def _stage_wkv(r, v, a, k, bb, lw, z0=None, emit=True):
    b, t, w = r.shape
    c = WKV_CHUNK
    nch = t // c
    n = HEAD_DIM
    pos = (lambda s: s, lambda s: nch - 1 - s)
    in_specs, args = [], []
    for d in range(2):
        tok = pl.BlockSpec((b, c, w), lambda s, d=d: (0, pos[d](s), 0))
        tok2 = pl.BlockSpec((1, b, c, w), lambda s, d=d: (d, 0, pos[d](s), 0))
        in_specs += [tok, tok, tok, tok2, tok2, tok2]
        args += [r, v, a, k, bb, lw]
    zspec = pl.BlockSpec((2, b, n, w), lambda s: (0, 0, 0, 0))
    if z0 is not None:
        in_specs.append(zspec)
        args.append(z0)
    if emit:
        out_specs = [pl.BlockSpec((b, c, w), lambda s, d=d: (0, pos[d](s), 0)) for d in range(2)]
        out_shape = [jax.ShapeDtypeStruct((b, t, w), F32)] * 2
    else:
        out_specs = [zspec]
        out_shape = [jax.ShapeDtypeStruct((2, b, n, w), F32)]
    return pl.pallas_call(
        functools.partial(_wkv_kernel, emit=emit, has_init=z0 is not None),
        grid=(nch,),
        in_specs=in_specs,
        out_specs=out_specs,
        out_shape=out_shape,
        scratch_shapes=[pltpu.VMEM((2, b, n, w), F32)],
        compiler_params=_params("arbitrary"),
        name="wkv_lat" if emit else "wkv_ctx",
    )(*args)


def _to_token_tiles(ref, val, rows):
    for cc in range(val.shape[1] // 128):
        ref[pl.ds(cc, rows, stride=8), :] = val[:, cc * 128:(cc + 1) * 128]


def _from_token_tiles(ref, rows):
    return jnp.concatenate([ref[pl.ds(cc, rows, stride=8), :] for cc in range(8)], axis=1)


def _route_tokens(h, rwt_ref, rb_ref, e_ref, w_ref, rank_ref, cnt_ref, carry, tm):
    logits = _mm_hi_lo(rwt_ref[...], h, dims=((1,), (1,))) + rb_ref[...]
    eio = lax.broadcasted_iota(jnp.int32, (N_EXPERTS, tm), 0)
    vals, sels, idxs = [], [], []
    for _ in range(TOP_K):
        m = jnp.max(logits, axis=0, keepdims=True)
        idx = jnp.min(jnp.where(logits == m, eio, N_EXPERTS), axis=0, keepdims=True)
        sel = eio == idx
        logits = jnp.where(sel, -jnp.inf, logits)
        vals.append(m)
        sels.append(sel)
        idxs.append(idx)
        e_ref[len(vals) - 1:len(vals), :] = idx
    ex = [jnp.exp(vk - vals[0]) for vk in vals]
    tot = ex[0] + ex[1] + ex[2] + ex[3]
    for kk in range(TOP_K):
        w_ref[kk:kk + 1, :] = ex[kk] / tot
    cnt = (sels[0] | sels[1] | sels[2] | sels[3]).astype(F32)
    ti = lax.broadcasted_iota(jnp.int32, (tm, tm), 0)
    tj = lax.broadcasted_iota(jnp.int32, (tm, tm), 1)
    before = _mm(cnt, (ti < tj).astype(F32))
    base = carry[...] + before
    ranks = []
    for kk in range(TOP_K):
        ranks.append(jnp.sum(jnp.where(sels[kk], base, 0.0), axis=0,
                             keepdims=True).astype(jnp.int32))
        rank_ref[kk:kk + 1, :] = ranks[kk]
    new = carry[...] + jnp.sum(cnt, axis=1, keepdims=True)
    carry[...] = new
    cnt_ref[...] = jnp.broadcast_to(new, cnt_ref.shape).astype(jnp.int32)
    return idxs, ranks


def _merge_kernel(x_ref, yf_ref, yb_ref, g_ref, bon_ref, sh_ref, sc_ref, ga_ref, sh2_ref, sc2_ref,
                  n1_ref, w2_ref, gnw_ref, gnb_ref, hs_ref, wor_ref, lnw_ref, lnb_ref, wsp_ref,
                  bsp_ref, wos_ref, wo_ref, n2_ref, rwt_ref, rb_ref,
                  xt_ref, e_ref, w_ref, rank_ref, cnt_ref, xs_ref,
                  carry, h2buf, zeros, dest_v, dest_s, cnt_s, ssem, csem, zsem, *, tm, cap):
    step = pl.program_id(0) * pl.num_programs(1) + pl.program_id(1)
    n_steps = pl.num_programs(0) * pl.num_programs(1)
    dump = N_EXPERTS * cap

    def dest_copy(which):
        return pltpu.make_async_copy(dest_v, dest_s.at[which], csem)

    def drain(which):
        for _ in range(TOP_K):
            pltpu.make_async_copy(h2buf.at[which], h2buf.at[which], ssem.at[which]).wait()

    @pl.when(step == 0)
    def _():
        carry[...] = jnp.zeros_like(carry)
        zeros[...] = jnp.zeros_like(zeros)
        h2buf[2] = jnp.zeros(h2buf.shape[1:], F32)
        h2buf[3] = jnp.zeros(h2buf.shape[1:], F32)
        for which in range(2):
            dest_v[...] = (dump + which * TOP_K * tm
                           + lax.broadcasted_iota(jnp.int32, (TOP_K, tm), 0) * tm
                           + lax.broadcasted_iota(jnp.int32, (TOP_K, tm), 1))
            dest_copy(which).start()
            dest_copy(which).wait()

    @pl.when(step > 1)
    def _():
        drain(step % 4)

    src = (step + 2) % 4
    for t in range(tm):
        for kk in range(TOP_K):
            _tile_copy(h2buf.at[src], t, xs_ref, dest_s[step % 2, kk, t], ssem.at[src]).start(
                priority=kk % 2)

    x = x_ref[0]
    h = _norm_mod(x, n1_ref[...], sh_ref[0], sc_ref[0])
    p2 = _mm(h, w2_ref[...])

    ps = p2[:, :2 * SGU_WIDTH]
    ge = 0.5 * ps * (1.0 + lax.erf(ps * (1.0 / math.sqrt(2.0))))
    u = ge[:, :SGU_WIDTH]
    z = ge[:, SGU_WIDTH:]
    mu = jnp.mean(z, axis=-1, keepdims=True)
    zc = z - mu
    var = jnp.mean(zc * zc, axis=-1, keepdims=True)
    z = zc * lax.rsqrt(var + LN_EPS) * lnw_ref[...] + lnb_ref[...]
    gw = SGU_WIDTH // SGU_GROUPS
    rows = []
    for c in range(tm // SGU_CHUNK):
        zc = z[c * SGU_CHUNK:(c + 1) * SGU_CHUNK]
        cols = [_mm(wsp_ref[gi], zc[:, gi * gw:(gi + 1) * gw]) for gi in range(SGU_GROUPS)]
        rows.append(jnp.concatenate(cols, axis=1) + bsp_ref[...])
    s = jnp.concatenate(rows, axis=0)
    y_b = _mm(u * s, wos_ref[...])

    y = yf_ref[0] + yb_ref[0]
    hs = hs_ref[...]
    ym = _mm(y, hs) * (1.0 / HEAD_DIM)
    yc = y - ym
    yv = _mm(yc * yc, hs) * (1.0 / HEAD_DIM)
    yn = yc * lax.rsqrt(yv + GN_EPS) * gnw_ref[...] + gnb_ref[...]
    y_a = _mm((yn + bon_ref[0]) * g_ref[0], wor_ref[...])

    gates = jax.nn.sigmoid(p2[:, 2 * SGU_WIDTH:])
    mix = gates[:, :D_MODEL] * y_a + gates[:, D_MODEL:] * y_b
    x1 = x + ga_ref[0] * _mm(mix, wo_ref[...])

    h2 = _norm_mod(x1, n2_ref[...], sh2_ref[0], sc2_ref[0])
    _to_token_tiles(h2buf.at[step % 4], h2, tm)
    _to_token_tiles(xt_ref, x1, tm)
    idxs, ranks = _route_tokens(h2, rwt_ref, rb_ref, e_ref, w_ref, rank_ref, cnt_ref, carry, tm)

    @pl.when(step > 0)
    def _():
        dest_copy((step - 1) % 2).wait()

    for kk in range(TOP_K):
        dest_v[kk:kk + 1, :] = idxs[kk] * cap + ranks[kk]
    dest_copy(step % 2).start()

    @pl.when(step == n_steps - 1)
    def _():
        dest_copy(step % 2).wait()
        for back in (1, 0):

            def flush(t, c, back=back):
                for kk in range(TOP_K):
                    _tile_copy(h2buf.at[(step - back) % 4], t, xs_ref,
                               dest_s[(step - back) % 2, kk, t],
                               ssem.at[(step - back) % 4]).start(priority=kk % 2)
                return c

            lax.fori_loop(0, tm, flush, 0)
        counts = pltpu.make_async_copy(cnt_ref, cnt_s, csem)
        counts.start()
        counts.wait()
        for e in range(N_EXPERTS):
            first = pl.multiple_of((e * cap + cnt_s[e, 0]) * 8, 8)
            pltpu.make_async_copy(zeros, xs_ref.at[pl.ds(first, zeros.shape[0])], zsem).start()
        for e in range(N_EXPERTS):
            pltpu.make_async_copy(zeros, xs_ref.at[pl.ds(0, zeros.shape[0])], zsem).wait()
        for which in range(4):
            drain(which)


def _stage_merge(x, yf, yb, g, bonus, mods, consts, tm):
    b, t, d = x.shape
    n = b * t
    w = RWKV_WIDTH
    per = t // tm
    cap = n + EXPERT_ROWS
    assert d == 8 * 128, "token-tile layout stores one (8, 128) tile per token"
    mod = pl.BlockSpec((1, 1, d), lambda i, j: (i, 0, 0))
    tok = pl.BlockSpec((1, tm, w), lambda i, j: (i, j, 0))
    tiles = pl.BlockSpec((tm * 8, 128), lambda i, j: (i * per + j, 0))
    lane = pl.BlockSpec((TOP_K, tm), lambda i, j: (0, i * per + j))
    return pl.pallas_call(
        functools.partial(_merge_kernel, tm=tm, cap=cap),
        grid=(b, per),
        in_specs=[pl.BlockSpec((1, tm, d), lambda i, j: (i, j, 0)), tok, tok, tok, tok]
        + [mod] * len(mods) + [_full(a.shape) for a in consts],
        out_specs=[tiles, lane, lane, lane, _full((N_EXPERTS, 128)),
                   pl.BlockSpec(memory_space=pl.ANY)],
        out_shape=[jax.ShapeDtypeStruct((n * 8, 128), F32),
                   jax.ShapeDtypeStruct((TOP_K, n), jnp.int32),
                   jax.ShapeDtypeStruct((TOP_K, n), F32),
                   jax.ShapeDtypeStruct((TOP_K, n), jnp.int32),
                   jax.ShapeDtypeStruct((N_EXPERTS, 128), jnp.int32),
                   jax.ShapeDtypeStruct(((N_EXPERTS * cap + 2 * TOP_K * tm) * 8, 128), F32)],
        scratch_shapes=[pltpu.VMEM((N_EXPERTS, 1), F32),
                        pltpu.VMEM((4, tm * 8, 128), F32),
                        pltpu.VMEM((EXPERT_ROWS * 8, 128), F32),
                        pltpu.VMEM((TOP_K, tm), jnp.int32),
                        pltpu.SMEM((2, TOP_K, tm), jnp.int32),
                        pltpu.SMEM((N_EXPERTS, 128), jnp.int32),
                        pltpu.SemaphoreType.DMA((4,)), pltpu.SemaphoreType.DMA(()),
                        pltpu.SemaphoreType.DMA(())],
        compiler_params=_params("arbitrary", "arbitrary"),
        name="merge",
    )(x, yf, yb, g, bonus, *mods, *consts), cap


def _expert_kernel(be_ref, nu_ref, nx_ref, xb_ref, x_ref, bg_ref, bu_ref, bd_ref, wg_hbm, wu_hbm, wd_hbm,
                   o_ref, stage, wg_s, wu_s, wd_s, sem):
    i = pl.program_id(0)
    e = be_ref[i]
    used = i < nu_ref[0]
    prev = be_ref[jnp.maximum(i - 1, 0)]

    def fetch(expert):
        return [pltpu.make_async_copy(w.at[expert], stage.at[j], sem.at[j])
                for j, w in enumerate((wg_hbm, wu_hbm, wd_hbm))]

    @pl.when(i == 0)
    def _():
        for cp in fetch(e):
            cp.start()

    @pl.when(used & ((i == 0) | (e != prev)))
    def _():
        for cp, w_s, j in zip(fetch(e), (wg_s, wu_s, wd_s), range(3)):
            cp.wait()
            w_s[...] = stage[j].astype(BF16)
        nxt = nx_ref[e]

        @pl.when(nxt >= 0)
        def _():
            for cp in fetch(nxt):
                cp.start()

    @pl.when(used)
    def _():
        x = _from_token_tiles(x_ref, EXPERT_ROWS)
        gate = _mm(x, wg_s[...]) + bg_ref[0]
        up = _mm(x, wu_s[...]) + bu_ref[0]
        gate = jnp.minimum(gate, SWIGLU_LIMIT)
        up = jnp.clip(up, -SWIGLU_LIMIT, SWIGLU_LIMIT)
        act = gate * jax.nn.sigmoid(SWIGLU_ALPHA * gate) * (up + 1.0)
        _to_token_tiles(o_ref, _mm(act, wd_s[...]) + bd_ref[0], EXPERT_ROWS)

    @pl.when(jnp.logical_not(used))
    def _():
        o_ref[...] = jnp.zeros_like(o_ref)


def _stage_experts(block_e, n_used, next_e, x_block, n_rows, xs, wg, bg, wu, bu, wd, bd):
    d, f = wg.shape[1:]
    assert d == f, "one staging buffer shape serves all three weight matrices"
    bm = EXPERT_ROWS
    bspec = lambda n_: pl.BlockSpec((1, 1, n_), lambda i, be, *_: (be[i], 0, 0))
    hbm = pl.BlockSpec(memory_space=pl.ANY)
    return pl.pallas_call(
        _expert_kernel,
        grid_spec=pltpu.PrefetchScalarGridSpec(
            num_scalar_prefetch=4,
            grid=(n_rows // bm,),
            in_specs=[pl.BlockSpec((bm * 8, 128), lambda i, be, nu, nx, xb: (xb[i], 0)),
                      bspec(f), bspec(f), bspec(d), hbm, hbm, hbm],
            out_specs=pl.BlockSpec((bm * 8, 128), lambda i, *_: (i, 0)),
            scratch_shapes=[pltpu.VMEM((3, d, f), F32), pltpu.VMEM((d, f), BF16),
                            pltpu.VMEM((d, f), BF16), pltpu.VMEM((f, d), BF16),
                            pltpu.SemaphoreType.DMA((3,))]),
        out_shape=jax.ShapeDtypeStruct((n_rows * 8, 128), F32),
        compiler_params=_params("arbitrary"),
        name="experts",
    )(block_e, n_used, next_e, x_block, xs, bg, bu, bd, wg, wu, wd)


def _tile_copy(src_ref, src_row, dst_ref, dst_row, sem):
    src = src_ref.at[pl.ds(pl.multiple_of(src_row * 8, 8), 8)]
    dst = dst_ref.at[pl.ds(pl.multiple_of(dst_row * 8, 8), 8)]
    return pltpu.make_async_copy(src, dst, sem)


def _combine_kernel(dest_ref, w_ref, xt_ref, ga_ref, g_ref, yb_ref, o_ref, buf0, buf1, res, sem,
                    *, tm, n_tok):
    i = pl.program_id(0)
    ga = ga_ref[0]
    bufs = (buf0, buf1)

    def issue(tile, slot, t):
        for kk in range(TOP_K):
            _tile_copy(yb_ref, dest_ref[kk * n_tok + tile * tm + t], bufs[slot].at[kk], t,
                       sem.at[slot]).start(priority=kk % 2)

    def combine(slot, t):
        rows = pl.ds(pl.multiple_of(t * 8, 8), 8)
        acc = bufs[slot][0, rows, :] * w_ref[i * tm + t]
        for kk in range(1, TOP_K):
            acc = acc + bufs[slot][kk, rows, :] * w_ref[kk * n_tok + i * tm + t]
        res[rows, :] = xt_ref[rows, :] + ga * acc

    def loop(body):
        for t in range(tm):
            body(t)

    @pl.when(i == 0)
    def _():
        loop(lambda t: issue(0, 0, t))

    for slot in range(2):
        @pl.when(i % 2 == slot)
        def _():
            for kk in range(TOP_K):
                pltpu.make_async_copy(bufs[slot].at[kk], bufs[slot].at[kk], sem.at[slot]).wait()

            @pl.when(i + 1 < pl.num_programs(0))
            def _():
                loop(lambda t: (issue(i + 1, 1 - slot, t), combine(slot, t)))

            @pl.when(i + 1 >= pl.num_programs(0))
            def _():
                loop(lambda t: combine(slot, t))

    x = _from_token_tiles(res, tm)
    o_ref[...] = x * lax.rsqrt(jnp.mean(x * x, axis=-1, keepdims=True) + RMS_EPS) * g_ref[...]


def _stage_combine(dest_flat, w_flat, x1t, ga_t, g_t, yb, tokens_per_batch, tm):
    n = x1t.shape[0] // 8
    d = 8 * 128
    per = tokens_per_batch // tm
    return pl.pallas_call(
        functools.partial(_combine_kernel, tm=tm, n_tok=n),
        grid_spec=pltpu.PrefetchScalarGridSpec(
            num_scalar_prefetch=2,
            grid=(n // tm,),
            in_specs=[pl.BlockSpec((tm * 8, 128), lambda i, *_: (i, 0)),
                      pl.BlockSpec((1, 8, 128), lambda i, *_: (i // per, 0, 0)),
                      pl.BlockSpec((1, d), lambda i, *_: (0, 0)),
                      pl.BlockSpec(memory_space=pl.ANY)],
            out_specs=pl.BlockSpec((tm, d), lambda i, *_: (i, 0)),
            scratch_shapes=[pltpu.VMEM((TOP_K, tm * 8, 128), F32),
                            pltpu.VMEM((TOP_K, tm * 8, 128), F32),
                            pltpu.VMEM((tm * 8, 128), F32),
                            pltpu.SemaphoreType.DMA((2,))]),
        out_shape=jax.ShapeDtypeStruct((n, d), F32),
        compiler_params=_params("arbitrary"),
        name="combine",
    )(dest_flat, w_flat, x1t, ga_t, g_t, yb)


def _rwkv_branch(x, ctx, mods_lat, mods_ctx, n1, w_rwkv, prm):
    r, v, a, _, _, k, bb, lw = _stage_prep(ctx, *mods_ctx, n1, w_rwkv, prm, tm=ctx.shape[1],
                                           grid_shift=False)
    (z_ctx,) = _stage_wkv(r, v, a, k, bb, lw, emit=False)
    r, v, a, g, bonus, k, bb, lw = _stage_prep(x, *mods_lat, n1, w_rwkv, prm, tm=512,
                                               grid_shift=True)
    y_f, y_b = _stage_wkv(r, v, a, k, bb, lw, z0=z_ctx, emit=True)
    return y_f, y_b, g, bonus


def _route(top_e, rank, counts, n_tok, cap):
    bm = EXPERT_ROWS
    n_rows = n_tok * TOP_K + N_EXPERTS * bm
    n_blocks = n_rows // bm
    padded = (counts + bm - 1) // bm * bm
    pad_end = jnp.cumsum(padded)
    pad_start = pad_end - padded
    experts = jnp.arange(N_EXPERTS, dtype=jnp.int32)
    start_of = jnp.sum(jnp.where(top_e[..., None] == experts, pad_start, 0), axis=-1)
    dest = (start_of + rank).astype(jnp.int32).reshape(-1)
    block_start = jnp.arange(n_blocks, dtype=jnp.int32) * bm
    n_used = (pad_end[-1] // bm).astype(jnp.int32).reshape(1)
    block_e = jnp.minimum(jnp.sum(pad_end[None, :] <= block_start[:, None], axis=1),
                          N_EXPERTS - 1).astype(jnp.int32)
    is_e = block_e[:, None] == experts[None, :]
    within = block_start - jnp.sum(jnp.where(is_e, pad_start, 0), axis=1)
    x_block = (block_e * (cap // bm) + within // bm).astype(jnp.int32)
    last_used = jnp.sum(jnp.where(jnp.arange(n_blocks) == n_used[0] - 1, x_block, 0))
    x_block = jnp.where(jnp.arange(n_blocks) < n_used[0], x_block, last_used).astype(jnp.int32)
    first_at = lax.cummin(jnp.where(padded > 0, experts, N_EXPERTS), axis=0, reverse=True)
    next_e = jnp.concatenate([first_at[1:], jnp.full((1,), N_EXPERTS, jnp.int32)])
    next_e = jnp.where(next_e < N_EXPERTS, next_e, -1).astype(jnp.int32)
    return dest, block_e, n_used, next_e, x_block, n_rows


def kernel(x, c, ctx, c_ctx, w_ada, b_ada, norm1_g, w_in, shift_mu, decay_w0, decay_lora_b,
           iclr_a0, iclr_lora_b, gate_lora_b, k_k, k_a, r_k, gn_w, gn_b, w_out_rwkv,
           sgu_ln_w, sgu_ln_b, sgu_w_spatial, sgu_b_spatial, w_out_sgu, w_o, norm2_g,
           router_w, router_b, exp_w_gate, exp_b_gate, exp_w_up, exp_b_up, exp_w_down,
           exp_b_down, final_norm_g):
    assert w_ada.shape[0] == 1, "single-layer problem"
    b, t, d = x.shape
    n_tok = b * t
    w = RWKV_WIDTH
    row = lambda a: a.reshape(1, -1)

    cs = jnp.zeros((8, d), F32).at[:b].set(c).at[b].set(c_ctx)
    mod = _stage_mods(cs, w_ada[0], row(b_ada[0]))
    sh1, sc1, ga1, sh2, sc2, ga2 = [m[:b, None, :] for m in jnp.split(mod, 6, axis=-1)]
    csh1, csc1 = [jnp.broadcast_to(m[b][None, None, :], (b, 1, d))
                  for m in jnp.split(mod, 6, axis=-1)[:2]]

    w_in_bf = w_in[0].astype(BF16)
    n1 = row(norm1_g[0])
    head_id = jnp.arange(w, dtype=jnp.int32) // HEAD_DIM
    headsum = (head_id[:, None] == head_id[None, :]).astype(BF16)
    prm = [row(shift_mu[0]), row(k_k[0]), row(k_a[0]), row(r_k[0]), decay_w0[0],
           decay_lora_b[0], iclr_a0[0], iclr_lora_b[0], gate_lora_b[0], headsum]
    y_f, y_b, g, bonus = _rwkv_branch(x, ctx, (sh1, sc1), (csh1, csc1), n1,
                                      w_in_bf[:, :RWKV_COLS], prm)

    bsp = jnp.repeat(sgu_b_spatial[0].T, SGU_WIDTH // SGU_GROUPS, axis=1)
    consts = [n1, w_in_bf[:, RWKV_COLS:], row(gn_w[0]), row(gn_b[0]), headsum,
              w_out_rwkv[0].astype(BF16), row(sgu_ln_w[0]), row(sgu_ln_b[0]),
              sgu_w_spatial[0].astype(BF16), bsp, w_out_sgu[0].astype(BF16),
              w_o[0].astype(BF16), row(norm2_g[0]), router_w[0].T, router_b[0].reshape(-1, 1)]
    (x1t, top_e, top_w, rank, counts, xs), cap = _stage_merge(
        x, y_f, y_b, g, bonus, (sh1, sc1, ga1, sh2, sc2), consts, tm=512)
    dest, block_e, n_used, next_e, x_block, n_rows = _route(top_e, rank, counts[:, 0], n_tok, cap)
    e3 = lambda a: a.reshape(N_EXPERTS, 1, -1)
    yb = _stage_experts(block_e, n_used, next_e, x_block, n_rows, xs, exp_w_gate[0],
                        e3(exp_b_gate[0]), exp_w_up[0], e3(exp_b_up[0]), exp_w_down[0],
                        e3(exp_b_down[0]))
    out = _stage_combine(dest, top_w.reshape(-1), x1t, ga2.reshape(b, 8, 128),
                         row(final_norm_g), yb, t, tm=256)
    return out.reshape(b, t, d)
```

```python
import functools
import math

import jax
import jax.numpy as jnp
from jax import lax
from jax.experimental import pallas as pl
from jax.experimental.pallas import tpu as pltpu

F32 = jnp.float32
BF16 = jnp.bfloat16
HIGHEST = lax.Precision.HIGHEST

D_MODEL = 1024
GRID_W = 64
RWKV_HEADS = 8
HEAD_DIM = 64
RWKV_WIDTH = RWKV_HEADS * HEAD_DIM
DECAY_LORA = 64
ICLR_LORA = 64
GATE_LORA = 128
RWKV_COLS = 3 * RWKV_WIDTH + 2 * DECAY_LORA + 2 * ICLR_LORA + GATE_LORA
SGU_WIDTH = 512
SGU_GROUPS = 8
SGU_CHUNK = 128
N_EXPERTS = 32
TOP_K = 4
SWIGLU_LIMIT = 7.0
SWIGLU_ALPHA = 1.702
RMS_EPS = 1e-6
LN_EPS = 1e-5
GN_EPS = 64e-5

TILE_ROWS = 8
TILE_LANES = 128
VMEM_LIMIT = 48 * 1024 * 1024

WKV_CHUNK = 64
WKV_GROUP = 4
EXPERT_ROWS = 512
MODS_COLS = 1536
PREP_TOKENS = 512
MERGE_TOKENS = 512
COMBINE_TOKENS = 512
SCATTER_LAG = 2


def _params(*sem):
    return pltpu.CompilerParams(dimension_semantics=sem, vmem_limit_bytes=VMEM_LIMIT)


def _mm(a, b, dims=((1,), (0,)), exact=False):
    dn = (dims, ((), ()))
    if exact:
        return lax.dot_general(a, b, dn, precision=HIGHEST, preferred_element_type=F32)
    return lax.dot_general(a.astype(BF16), b.astype(BF16), dn, preferred_element_type=F32)


def _split3(x):
    hi = x.astype(BF16)
    r1 = x - hi.astype(F32)
    mid = r1.astype(BF16)
    lo = (r1 - mid.astype(F32)).astype(BF16)
    return hi, mid, lo


def _mm_hi_lo(a, b, dims):
    a_hi = a.astype(BF16)
    b_hi = b.astype(BF16)
    a_lo = a - a_hi.astype(F32)
    b_lo = b - b_hi.astype(F32)
    return _mm(a_hi, b_hi, dims) + _mm(a_hi, b_lo, dims) + _mm(a_lo, b_hi, dims)


def _mm_sel_x(sel, x):
    return sum(_mm(sel, p) for p in _split3(x))


def _full(shape):
    n = len(shape)
    return pl.BlockSpec(shape, lambda *_: (0,) * n)


def _norm_mod(x, g, shift, scale):
    y = x * lax.rsqrt(jnp.mean(x * x, axis=-1, keepdims=True) + RMS_EPS) * g
    return y * (1.0 + scale) + shift


def _mods_kernel(c_ref, w_ref, b_ref, o_ref):
    c = c_ref[...]
    s = c * jax.nn.sigmoid(c)
    o_ref[...] = _mm(s, w_ref[...], exact=True) + b_ref[...]


def _stage_mods(cs, w_ada, b_ada):
    rows, d = cs.shape
    n = w_ada.shape[1]
    tn = MODS_COLS
    return pl.pallas_call(
        _mods_kernel,
        grid=(n // tn,),
        in_specs=[_full((rows, d)),
                  pl.BlockSpec((d, tn), lambda j: (0, j)),
                  pl.BlockSpec((1, tn), lambda j: (0, j))],
        out_specs=pl.BlockSpec((rows, tn), lambda j: (0, j)),
        out_shape=jax.ShapeDtypeStruct((rows, n), F32),
        compiler_params=_params("arbitrary"),
        name="mods",
    )(cs, w_ada, b_ada)


def _rwkv_feats(p, mu_kk, mu_ka, r_k, w0, dlb, a0, ilb, glb, headsum):
    w = RWKV_WIDTH
    r = p[:, 0:w]
    k = p[:, w:2 * w]
    v = p[:, 2 * w:3 * w]
    o = 3 * w
    wd = (p[:, o:o + DECAY_LORA], p[:, o + DECAY_LORA:o + 2 * DECAY_LORA])
    o += 2 * DECAY_LORA
    ad = (p[:, o:o + ICLR_LORA], p[:, o + ICLR_LORA:o + 2 * ICLR_LORA])
    o += 2 * ICLR_LORA
    gd = p[:, o:o + GATE_LORA]

    kk = k * mu_kk
    kk = kk * lax.rsqrt(_mm(kk * kk, headsum) + 1e-12)
    g = _mm(jax.nn.sigmoid(gd), glb)
    ks, bs, lws = [], [], []
    ksum = None
    for d in range(2):
        z = w0[d:d + 1] + _mm(jnp.tanh(wd[d]), dlb[d])
        lws.append(-math.exp(-0.5) * jax.nn.sigmoid(z))
        ic = jax.nn.sigmoid(a0[d:d + 1] + _mm(ad[d], ilb[d]))
        kd = k * (1.0 + (ic - 1.0) * mu_ka)
        ks.append(kd)
        bs.append(kk * ic)
        ksum = kd if ksum is None else ksum + kd
    bonus = _mm(r * ksum * r_k, headsum) * v
    return r, v, -kk, g, bonus, ks, bs, lws


def _prep_kernel(xm_ref, xp_ref, xn_ref, sh_ref, sc_ref, n1_ref, w_ref,
                 mu_ref, kk_ref, ka_ref, rk_ref, w0_ref, dlb_ref, a0_ref, ilb_ref, glb_ref, hs_ref,
                 rva_ref, g_ref, bon_ref, kb_ref, lw_ref, *, tm, grid_shift):
    project = lambda x: _mm(_norm_mod(x, n1_ref[...], sh_ref[0], sc_ref[0]), w_ref[...])
    lane = lax.broadcasted_iota(jnp.int32, (1, RWKV_COLS), 1)
    if grid_shift:
        ext = project(jnp.concatenate([xp_ref[0], xm_ref[0], xn_ref[0]], axis=0))
        main = ext[GRID_W:GRID_W + tm]
        t = pl.program_id(1) * tm + lax.broadcasted_iota(jnp.int32, (tm, 1), 0)
        col = t & (GRID_W - 1)
        row = t >> (GRID_W.bit_length() - 1)
        n_rows = pl.num_programs(1) * tm // GRID_W
        left = jnp.where(col > 0, ext[GRID_W - 1:GRID_W - 1 + tm], 0.0)
        right = jnp.where(col < GRID_W - 1, ext[GRID_W + 1:GRID_W + 1 + tm], 0.0)
        up = jnp.where(row > 0, ext[0:tm], 0.0)
        down = jnp.where(row < n_rows - 1, ext[2 * GRID_W:2 * GRID_W + tm], 0.0)
        cm = lane & 3
        shifted = jnp.where(cm == 0, left, jnp.where(cm == 1, right, jnp.where(cm == 2, up, down)))
    else:
        main = project(xm_ref[0])
        zero = jnp.zeros((1, RWKV_COLS), F32)
        prev = jnp.concatenate([zero, main[:tm - 1]], axis=0)
        nxt = jnp.concatenate([main[1:], zero], axis=0)
        shifted = jnp.where((lane & 1) == 0, prev, nxt)
    p = main + mu_ref[...] * (shifted - main)
    r, v, a, g, bonus, ks, bs, lws = _rwkv_feats(
        p, kk_ref[...], ka_ref[...], rk_ref[...], w0_ref[...], dlb_ref, a0_ref[...], ilb_ref,
        glb_ref[...], hs_ref[...])
    w = RWKV_WIDTH
    for j, val in enumerate((r, v, a)):
        rva_ref[0, :, j * w:(j + 1) * w] = val.astype(BF16)
    g_ref[0] = g.astype(BF16)
    bon_ref[0] = bonus.astype(BF16)
    for d in range(2):
        kb_ref[d, 0, :, 0:w] = ks[d].astype(BF16)
        kb_ref[d, 0, :, w:2 * w] = bs[d].astype(BF16)
        lw_ref[d, 0] = lws[d]


def _stage_prep(x, shift, scale, g, w_bf16, prm, tm, grid_shift):
    b, t, c = x.shape
    w = RWKV_WIDTH
    mod = pl.BlockSpec((1, 1, c), lambda i, j: (i, 0, 0))
    hb = GRID_W if grid_shift else 8
    per = tm // hb
    last = t // hb - 1
    tok = lambda n_: pl.BlockSpec((1, tm, n_ * w), lambda i, j: (i, j, 0))
    tok2 = lambda n_: pl.BlockSpec((2, 1, tm, n_ * w), lambda i, j: (0, i, j, 0))
    in_specs = [
        pl.BlockSpec((1, tm, c), lambda i, j: (i, j, 0)),
        pl.BlockSpec((1, hb, c), lambda i, j: (i, jnp.maximum(j * per - 1, 0), 0)),
        pl.BlockSpec((1, hb, c), lambda i, j: (i, jnp.minimum((j + 1) * per, last), 0)),
        mod, mod, _full(g.shape), _full(w_bf16.shape),
    ] + [_full(a.shape) for a in prm]
    return pl.pallas_call(
        functools.partial(_prep_kernel, tm=tm, grid_shift=grid_shift),
        grid=(b, t // tm),
        in_specs=in_specs,
        out_specs=[tok(3), tok(1), tok(1), tok2(2), tok2(1)],
        out_shape=[jax.ShapeDtypeStruct((b, t, 3 * w), BF16),
                   jax.ShapeDtypeStruct((b, t, w), BF16),
                   jax.ShapeDtypeStruct((b, t, w), BF16),
                   jax.ShapeDtypeStruct((2, b, t, 2 * w), BF16),
                   jax.ShapeDtypeStruct((2, b, t, w), F32)],
        compiler_params=_params("arbitrary", "arbitrary"),
        name="prep_lat" if grid_shift else "prep_ctx",
    )(x, x, x, shift, scale, g, w_bf16, *prm)


def _wkv_prepare(d, r, v, a, k, bb, lw):
    c = WKV_CHUNK
    n = HEAD_DIM
    ii = lax.broadcasted_iota(jnp.int32, (c, c), 0)
    jj = lax.broadcasted_iota(jnp.int32, (c, c), 1)
    incl = (jj <= ii) if d == 0 else (jj >= ii)
    lc = _mm_sel_x(incl.astype(BF16), lw)
    lx = lc - lw
    ltot = lc[c - 1:c] if d == 0 else lc[0:1]
    inv = jnp.exp(-lc)
    tail = jnp.exp(ltot - lc)
    etot = jnp.exp(ltot)
    eye_n = (lax.broadcasted_iota(jnp.int32, (n, n), 0)
             == lax.broadcasted_iota(jnp.int32, (n, n), 1)).astype(F32)
    scale = jnp.concatenate(
        [jnp.broadcast_to(jnp.sum(eye_n * etot[:, h * n:(h + 1) * n], axis=1, keepdims=True), (n, n))
         for h in range(RWKV_HEADS)], axis=1)
    bf = lambda x: x.astype(BF16)
    return (bf(a * jnp.exp(lx)), bf(r * jnp.exp(lc)), bf(k * inv), bf(bb * inv), bf(k * tail),
            bf(bb * tail), bf(v), scale)


def _wkv_consts():
    c = WKV_CHUNK
    n = HEAD_DIM
    gw = WKV_GROUP * n
    assert c == n and n & (n - 1) == 0
    ii = lax.broadcasted_iota(jnp.int32, (c, gw), 0)
    jj = lax.broadcasted_iota(jnp.int32, (c, gw), 1) & (c - 1)
    head_shift = n.bit_length() - 1
    same_head = (lax.broadcasted_iota(jnp.int32, (gw, gw), 0) >> head_shift
                 == lax.broadcasted_iota(jnp.int32, (gw, gw), 1) >> head_shift)

    def bd(x):
        x = x.astype(BF16)
        return jnp.where(same_head, jnp.concatenate([x] * WKV_GROUP, axis=0), jnp.zeros((), BF16))

    return ii, jj, bd


def _wkv_factor(chunks, emit):
    c = WKV_CHUNK
    gw = WKV_GROUP * HEAD_DIM
    n_groups = RWKV_WIDTH // gw
    ii, jj, bd = _wkv_consts()
    eye = (ii == jj).astype(F32)
    nt = ((1,), (1,))
    tn = ((0,), (0,))
    sl = [slice(h * HEAD_DIM, (h + 1) * HEAD_DIM) for h in range(WKV_GROUP)]

    prob = []
    for preps in chunks:
        for d, (at, rt, kt, bt, kh, bh, vb, _) in enumerate(preps):
            incl = (jj <= ii) if d == 0 else (jj >= ii)
            strict = (jj < ii) if d == 0 else (jj > ii)
            for gi in range(n_groups):
                gs = slice(gi * gw, (gi + 1) * gw)
                prob.append(dict(incl=incl, strict=strict, at=at[:, gs], rt=rt[:, gs],
                                 kt=kt[:, gs], bt=bt[:, gs], kh=kh[:, gs], bh=bh[:, gs],
                                 v=vb[:, gs]))
    for p in prob:
        lhs = jnp.concatenate([p["at"], p["rt"]], axis=0) if emit else p["at"]
        p["lhs_z"] = lhs
        gk = _mm(lhs, bd(p["kt"]), nt)
        gb = _mm(lhs, bd(p["bt"]), nt)
        p["a_ab"] = jnp.where(p["strict"], gb[:c], 0.0)
        lhs_v = jnp.where(p["strict"], gk[:c], 0.0)
        if emit:
            lhs_v = jnp.concatenate([lhs_v, jnp.where(p["incl"], gk[c:], 0.0)], axis=0)
            p["a_rb"] = jnp.where(p["incl"], gb[c:], 0.0).astype(BF16)
        p["lhs_v"] = lhs_v
    for p in prob:
        p["tinv"] = eye + p["a_ab"]
        p["x"] = _mm(p["a_ab"], bd(p["a_ab"]))
        p["av"] = _mm(p["lhs_v"], bd(p["v"]))
    for level in range(1, 6):
        for p in prob:
            if level < 5:
                both = _mm(jnp.concatenate([p["x"], p["tinv"]], axis=0), bd(p["x"]))
                p["x"] = both[:c]
                p["tinv"] = p["tinv"] + both[c:]
            else:
                p["tinv"] = (p["tinv"] + _mm(p["tinv"], bd(p["x"]))).astype(BF16)
    per_chunk = 2 * n_groups
    return [prob[j * per_chunk:(j + 1) * per_chunk] for j in range(len(chunks))]


def _wkv_apply(probs, chunks, states, emit):
    c = WKV_CHUNK
    n = HEAD_DIM
    gw = WKV_GROUP * n
    n_groups = RWKV_WIDTH // gw
    _, _, bd = _wkv_consts()
    tn = ((0,), (0,))
    sl = [slice(h * n, (h + 1) * n) for h in range(WKV_GROUP)]
    flat = []
    for prob, zs in zip(probs, states):
        zb = [z.astype(BF16) for z in zs]
        for i, p in enumerate(prob):
            d, gi = divmod(i, n_groups)
            p["z"] = zb[d][:, gi * gw:(gi + 1) * gw]
            flat.append(p)
    for p in flat:
        p["zv"] = _mm(p["lhs_z"], bd(p["z"])) + p["av"]
    for p in flat:
        p["u"] = _mm(p["tinv"], bd(p["zv"][:c])).astype(BF16)
    for p in flat:
        if emit:
            p["y"] = p["zv"][c:] + _mm(p["a_rb"], bd(p["u"]))
        p["z_new"] = [_mm(jnp.concatenate([p["kh"][:, s], p["bh"][:, s]], axis=0),
                          jnp.concatenate([p["v"][:, s], p["u"][:, s]], axis=0), tn) for s in sl]
    out = []
    for prob, preps, zs in zip(probs, chunks, states):
        res = []
        for d, (prep, z) in enumerate(zip(preps, zs)):
            mine = prob[d * n_groups:(d + 1) * n_groups]
            z_cat = jnp.concatenate([m for p in mine for m in p["z_new"]], axis=1) + prep[7] * z
            y = jnp.concatenate([p["y"] for p in mine], axis=1) if emit else None
            res.append((y, z_cat))
        out.append(res)
    return out


def _wkv_kernel(*refs, emit, has_init):
    refs = list(refs)
    ins = [[refs.pop(0) for _ in range(3)] for _ in range(2)]
    z0_ref = refs.pop(0) if has_init else None
    outs = [refs.pop(0) for _ in range(2 if emit else 1)]
    z_scr = refs.pop(0)
    n_seq = z_scr.shape[1]

    @pl.when(pl.program_id(0) == 0)
    def _():
        if has_init:
            z_scr[...] = z0_ref[...]
        else:
            z_scr[...] = jnp.zeros_like(z_scr)

    chunks = []
    for i in range(n_seq):
        preps = []
        for d in range(2):
            rva_ref, kb_ref, lw_ref = ins[d]
            w = RWKV_WIDTH
            r, v, a = (rva_ref[i, :, j * w:(j + 1) * w] for j in range(3))
            preps.append(_wkv_prepare(d, r, v, a, kb_ref[0, i, :, 0:w], kb_ref[0, i, :, w:2 * w],
                                      lw_ref[0, i]))
        chunks.append(preps)
    probs = _wkv_factor(chunks, emit)
    states = [(z_scr[0, i], z_scr[1, i]) for i in range(n_seq)]
    for i, res in enumerate(_wkv_apply(probs, chunks, states, emit)):
        for d, (y, z_new) in enumerate(res):
            z_scr[d, i] = z_new
            if emit:
                outs[d][i] = y
    if not emit:
        outs[0][...] = z_scr[...]


def _stage_wkv(rva, kb, lw, z0=None, emit=True):
    b, t, w = lw.shape[1:]
    c = WKV_CHUNK
    nch = t // c
    n = HEAD_DIM
    pos = (lambda s: s, lambda s: nch - 1 - s)
    in_specs, args = [], []
    for d in range(2):
        in_specs += [pl.BlockSpec((b, c, 3 * w), lambda s, d=d: (0, pos[d](s), 0)),
                     pl.BlockSpec((1, b, c, 2 * w), lambda s, d=d: (d, 0, pos[d](s), 0)),
                     pl.BlockSpec((1, b, c, w), lambda s, d=d: (d, 0, pos[d](s), 0))]
        args += [rva, kb, lw]
    zspec = pl.BlockSpec((2, b, n, w), lambda s: (0, 0, 0, 0))
    if z0 is not None:
        in_specs.append(zspec)
        args.append(z0)
    if emit:
        out_specs = [pl.BlockSpec((b, c, w), lambda s, d=d: (0, pos[d](s), 0)) for d in range(2)]
        out_shape = [jax.ShapeDtypeStruct((b, t, w), F32)] * 2
    else:
        out_specs = [zspec]
        out_shape = [jax.ShapeDtypeStruct((2, b, n, w), F32)]
    return pl.pallas_call(
        functools.partial(_wkv_kernel, emit=emit, has_init=z0 is not None),
        grid=(nch,),
        in_specs=in_specs,
        out_specs=out_specs,
        out_shape=out_shape,
        scratch_shapes=[pltpu.VMEM((2, b, n, w), F32)],
        compiler_params=_params("arbitrary"),
        name="wkv_lat" if emit else "wkv_ctx",
    )(*args)


def _to_token_tiles(ref, val, rows):
    for cc in range(val.shape[1] // TILE_LANES):
        ref[pl.ds(cc, rows, stride=TILE_ROWS), :] = val[:, cc * TILE_LANES:(cc + 1) * TILE_LANES]


def _from_token_tiles(ref, rows):
    return jnp.concatenate([ref[pl.ds(cc, rows, stride=TILE_ROWS), :] for cc in range(TILE_ROWS)],
                           axis=1)


def _route_tokens(h, rwt_ref, rb_ref, e_ref, w_ref, rank_ref, cnt_ref, carry, tm):
    logits = _mm_hi_lo(rwt_ref[...], h, dims=((1,), (1,))) + rb_ref[...]
    eio = lax.broadcasted_iota(jnp.int32, (N_EXPERTS, tm), 0)
    vals, sels, idxs = [], [], []
    for _ in range(TOP_K):
        m = jnp.max(logits, axis=0, keepdims=True)
        idx = jnp.min(jnp.where(logits == m, eio, N_EXPERTS), axis=0, keepdims=True)
        sel = eio == idx
        logits = jnp.where(sel, -jnp.inf, logits)
        vals.append(m)
        sels.append(sel)
        idxs.append(idx)
        e_ref[len(vals) - 1:len(vals), :] = idx
    ex = [jnp.exp(vk - vals[0]) for vk in vals]
    tot = ex[0] + ex[1] + ex[2] + ex[3]
    for kk in range(TOP_K):
        w_ref[kk:kk + 1, :] = ex[kk] / tot
    cnt = (sels[0] | sels[1] | sels[2] | sels[3]).astype(F32)
    ti = lax.broadcasted_iota(jnp.int32, (tm, tm), 0)
    tj = lax.broadcasted_iota(jnp.int32, (tm, tm), 1)
    before = _mm(cnt, (ti < tj).astype(F32))
    base = carry[...] + before
    ranks = []
    for kk in range(TOP_K):
        ranks.append(jnp.sum(jnp.where(sels[kk], base, 0.0), axis=0,
                             keepdims=True).astype(jnp.int32))
        rank_ref[kk:kk + 1, :] = ranks[kk]
    new = carry[...] + jnp.sum(cnt, axis=1, keepdims=True)
    carry[...] = new
    cnt_ref[...] = jnp.broadcast_to(new, cnt_ref.shape).astype(jnp.int32)
    return idxs, ranks


def _merge_kernel(x_ref, yf_ref, yb_ref, g_ref, bon_ref, sh_ref, sc_ref, ga_ref, sh2_ref, sc2_ref,
                  n1_ref, w2_ref, gnw_ref, gnb_ref, hs_ref, wor_ref, lnw_ref, lnb_ref, wsp_ref,
                  bsp_ref, wos_ref, wo_ref, n2_ref, rwt_ref, rb_ref,
                  xt_ref, e_ref, w_ref, rank_ref, cnt_ref, xs_ref,
                  carry, h2buf, zeros, dest_v, dest_s, cnt_s, ssem, csem, zsem, *, tm, cap):
    step = pl.program_id(0) * pl.num_programs(1) + pl.program_id(1)
    n_steps = pl.num_programs(0) * pl.num_programs(1)
    dump = N_EXPERTS * cap
    n_buf = h2buf.shape[0]

    def dest_copy(which):
        return pltpu.make_async_copy(dest_v, dest_s.at[which], csem)

    def drain(which):
        for _ in range(TOP_K):
            pltpu.make_async_copy(h2buf.at[which], h2buf.at[which], ssem.at[which]).wait()

    @pl.when(step == 0)
    def _():
        carry[...] = jnp.zeros_like(carry)
        zeros[...] = jnp.zeros_like(zeros)
        for which in range(SCATTER_LAG):
            h2buf[SCATTER_LAG + which] = jnp.zeros(h2buf.shape[1:], F32)
            dest_v[...] = (dump + which * TOP_K * tm
                           + lax.broadcasted_iota(jnp.int32, (TOP_K, tm), 0) * tm
                           + lax.broadcasted_iota(jnp.int32, (TOP_K, tm), 1))
            dest_copy(which).start()
            dest_copy(which).wait()

    @pl.when(step >= SCATTER_LAG)
    def _():
        drain(step % n_buf)

    src = (step + SCATTER_LAG) % n_buf
    for t in range(tm):
        for kk in range(TOP_K):
            _tile_copy(h2buf.at[src], t, xs_ref, dest_s[step % SCATTER_LAG, kk, t], ssem.at[src]).start(
                priority=kk % 2)

    x = x_ref[0]
    h = _norm_mod(x, n1_ref[...], sh_ref[0], sc_ref[0])
    p2 = _mm(h, w2_ref[...])

    ps = p2[:, :2 * SGU_WIDTH]
    ge = 0.5 * ps * (1.0 + lax.erf(ps * (1.0 / math.sqrt(2.0))))
    u = ge[:, :SGU_WIDTH]
    z = ge[:, SGU_WIDTH:]
    mu = jnp.mean(z, axis=-1, keepdims=True)
    zc = z - mu
    var = jnp.mean(zc * zc, axis=-1, keepdims=True)
    z = zc * lax.rsqrt(var + LN_EPS) * lnw_ref[...] + lnb_ref[...]
    gw = SGU_WIDTH // SGU_GROUPS
    rows = []
    for c in range(tm // SGU_CHUNK):
        zc = z[c * SGU_CHUNK:(c + 1) * SGU_CHUNK]
        cols = [_mm(wsp_ref[gi], zc[:, gi * gw:(gi + 1) * gw]) for gi in range(SGU_GROUPS)]
        rows.append(jnp.concatenate(cols, axis=1) + bsp_ref[...])
    s = jnp.concatenate(rows, axis=0)
    y_b = _mm(u * s, wos_ref[...])

    y = yf_ref[0] + yb_ref[0]
    hs = hs_ref[...]
    ym = _mm(y, hs) * (1.0 / HEAD_DIM)
    yc = y - ym
    yv = _mm(yc * yc, hs) * (1.0 / HEAD_DIM)
    yn = yc * lax.rsqrt(yv + GN_EPS) * gnw_ref[...] + gnb_ref[...]
    y_a = _mm((yn + bon_ref[0]) * g_ref[0], wor_ref[...])

    gates = jax.nn.sigmoid(p2[:, 2 * SGU_WIDTH:])
    mix = gates[:, :D_MODEL] * y_a + gates[:, D_MODEL:] * y_b
    x1 = x + ga_ref[0] * _mm(mix, wo_ref[...])

    h2 = _norm_mod(x1, n2_ref[...], sh2_ref[0], sc2_ref[0])
    _to_token_tiles(h2buf.at[step % n_buf], h2, tm)
    _to_token_tiles(xt_ref, x1, tm)
    idxs, ranks = _route_tokens(h2, rwt_ref, rb_ref, e_ref, w_ref, rank_ref, cnt_ref, carry, tm)

    @pl.when(step > 0)
    def _():
        dest_copy((step - 1) % SCATTER_LAG).wait()

    for kk in range(TOP_K):
        dest_v[kk:kk + 1, :] = idxs[kk] * cap + ranks[kk]
    dest_copy(step % SCATTER_LAG).start()

    @pl.when(step == n_steps - 1)
    def _():
        dest_copy(step % SCATTER_LAG).wait()
        for back in reversed(range(SCATTER_LAG)):

            def flush(t, c, back=back):
                for kk in range(TOP_K):
                    _tile_copy(h2buf.at[(step - back) % n_buf], t, xs_ref,
                               dest_s[(step - back) % SCATTER_LAG, kk, t],
                               ssem.at[(step - back) % n_buf]).start(priority=kk % 2)
                return c

            lax.fori_loop(0, tm, flush, 0)
        counts = pltpu.make_async_copy(cnt_ref, cnt_s, csem)
        counts.start()
        counts.wait()
        for e in range(N_EXPERTS):
            first = pl.multiple_of((e * cap + cnt_s[e, 0]) * TILE_ROWS, TILE_ROWS)
            pltpu.make_async_copy(zeros, xs_ref.at[pl.ds(first, zeros.shape[0])], zsem).start()
        for e in range(N_EXPERTS):
            pltpu.make_async_copy(zeros, xs_ref.at[pl.ds(0, zeros.shape[0])], zsem).wait()
        for which in range(n_buf):
            drain(which)


def _stage_merge(x, yf, yb, g, bonus, mods, consts, tm):
    b, t, d = x.shape
    n = b * t
    w = RWKV_WIDTH
    per = t // tm
    cap = n + EXPERT_ROWS
    assert d == TILE_ROWS * TILE_LANES, "token-tile layout stores one vector tile per token"
    mod = pl.BlockSpec((1, 1, d), lambda i, j: (i, 0, 0))
    tok = pl.BlockSpec((1, tm, w), lambda i, j: (i, j, 0))
    tiles = pl.BlockSpec((tm * TILE_ROWS, TILE_LANES), lambda i, j: (i * per + j, 0))
    lane = pl.BlockSpec((TOP_K, tm), lambda i, j: (0, i * per + j))
    return pl.pallas_call(
        functools.partial(_merge_kernel, tm=tm, cap=cap),
        grid=(b, per),
        in_specs=[pl.BlockSpec((1, tm, d), lambda i, j: (i, j, 0)), tok, tok, tok, tok]
        + [mod] * len(mods) + [_full(a.shape) for a in consts],
        out_specs=[tiles, lane, lane, lane, _full((N_EXPERTS, TILE_LANES)),
                   pl.BlockSpec(memory_space=pl.ANY)],
        out_shape=[jax.ShapeDtypeStruct((n * TILE_ROWS, TILE_LANES), F32),
                   jax.ShapeDtypeStruct((TOP_K, n), jnp.int32),
                   jax.ShapeDtypeStruct((TOP_K, n), F32),
                   jax.ShapeDtypeStruct((TOP_K, n), jnp.int32),
                   jax.ShapeDtypeStruct((N_EXPERTS, TILE_LANES), jnp.int32),
                   jax.ShapeDtypeStruct(((N_EXPERTS * cap + SCATTER_LAG * TOP_K * tm) * TILE_ROWS,
                                         TILE_LANES), F32)],
        scratch_shapes=[pltpu.VMEM((N_EXPERTS, 1), F32),
                        pltpu.VMEM((2 * SCATTER_LAG, tm * TILE_ROWS, TILE_LANES), F32),
                        pltpu.VMEM((EXPERT_ROWS * TILE_ROWS, TILE_LANES), F32),
                        pltpu.VMEM((TOP_K, tm), jnp.int32),
                        pltpu.SMEM((SCATTER_LAG, TOP_K, tm), jnp.int32),
                        pltpu.SMEM((N_EXPERTS, TILE_LANES), jnp.int32),
                        pltpu.SemaphoreType.DMA((2 * SCATTER_LAG,)), pltpu.SemaphoreType.DMA(()),
                        pltpu.SemaphoreType.DMA(())],
        compiler_params=_params("arbitrary", "arbitrary"),
        name="merge",
    )(x, yf, yb, g, bonus, *mods, *consts), cap


def _expert_kernel(be_ref, nu_ref, nx_ref, xb_ref, nrow_ref, x_ref, bg_ref, bu_ref, bd_ref,
                   wg_hbm, wu_hbm, wd_hbm, o_ref, stage, wg_s, wu_s, wd_s, sem):
    i = pl.program_id(0)
    e = be_ref[i]
    used = i < nu_ref[0]
    prev = be_ref[jnp.maximum(i - 1, 0)]

    def fetch(expert):
        return [pltpu.make_async_copy(w.at[expert], stage.at[j], sem.at[j])
                for j, w in enumerate((wg_hbm, wu_hbm, wd_hbm))]

    @pl.when(i == 0)
    def _():
        for cp in fetch(e):
            cp.start()

    @pl.when(used & ((i == 0) | (e != prev)))
    def _():
        for cp, w_s, j in zip(fetch(e), (wg_s, wu_s, wd_s), range(3)):
            cp.wait()
            w_s[...] = stage[j].astype(BF16)
        nxt = nx_ref[e]

        @pl.when(nxt >= 0)
        def _():
            for cp in fetch(nxt):
                cp.start()

    def run(rows):
        x = _from_token_tiles(x_ref, rows)
        gate = _mm(x, wg_s[...]) + bg_ref[0]
        up = _mm(x, wu_s[...]) + bu_ref[0]
        gate = jnp.minimum(gate, SWIGLU_LIMIT)
        up = jnp.clip(up, -SWIGLU_LIMIT, SWIGLU_LIMIT)
        act = gate * jax.nn.sigmoid(SWIGLU_ALPHA * gate) * (up + 1.0)
        _to_token_tiles(o_ref, _mm(act, wd_s[...]) + bd_ref[0], rows)

    half = EXPERT_ROWS // 2
    wide = used & (nrow_ref[i] > half)

    @pl.when(wide)
    def _():
        run(EXPERT_ROWS)

    @pl.when(used & jnp.logical_not(wide))
    def _():
        run(half)
        o_ref[half * TILE_ROWS:, :] = jnp.zeros((half * TILE_ROWS, TILE_LANES), F32)

    @pl.when(jnp.logical_not(used))
    def _():
        o_ref[...] = jnp.zeros_like(o_ref)


def _stage_experts(block_e, n_used, next_e, x_block, block_rows, n_rows, xs, wg, bg, wu, bu, wd,
                   bd):
    d, f = wg.shape[1:]
    assert d == f, "one staging buffer shape serves all three weight matrices"
    bm = EXPERT_ROWS
    bspec = lambda n_: pl.BlockSpec((1, 1, n_), lambda i, be, *_: (be[i], 0, 0))
    hbm = pl.BlockSpec(memory_space=pl.ANY)
    return pl.pallas_call(
        _expert_kernel,
        grid_spec=pltpu.PrefetchScalarGridSpec(
            num_scalar_prefetch=5,
            grid=(n_rows // bm,),
            in_specs=[pl.BlockSpec((bm * TILE_ROWS, TILE_LANES), lambda i, be, nu, nx, xb, nr: (xb[i], 0)),
                      bspec(f), bspec(f), bspec(d), hbm, hbm, hbm],
            out_specs=pl.BlockSpec((bm * TILE_ROWS, TILE_LANES), lambda i, *_: (i, 0)),
            scratch_shapes=[pltpu.VMEM((3, d, f), F32), pltpu.VMEM((d, f), BF16),
                            pltpu.VMEM((d, f), BF16), pltpu.VMEM((f, d), BF16),
                            pltpu.SemaphoreType.DMA((3,))]),
        out_shape=jax.ShapeDtypeStruct((n_rows * TILE_ROWS, TILE_LANES), F32),
        compiler_params=_params("arbitrary"),
        name="experts",
    )(block_e, n_used, next_e, x_block, block_rows, xs, bg, bu, bd, wg, wu, wd)


def _tile_copy(src_ref, src_row, dst_ref, dst_row, sem):
    src = src_ref.at[pl.ds(pl.multiple_of(src_row * TILE_ROWS, TILE_ROWS), TILE_ROWS)]
    dst = dst_ref.at[pl.ds(pl.multiple_of(dst_row * TILE_ROWS, TILE_ROWS), TILE_ROWS)]
    return pltpu.make_async_copy(src, dst, sem)


def _combine_kernel(dest_ref, w_ref, xt_ref, ga_ref, g_ref, yb_ref, o_ref, buf0, buf1, res, sem,
                    *, tm, n_tok):
    i = pl.program_id(0)
    ga = ga_ref[0]
    bufs = (buf0, buf1)

    def issue(tile, slot, t):
        for kk in range(TOP_K):
            _tile_copy(yb_ref, dest_ref[kk * n_tok + tile * tm + t], bufs[slot].at[kk], t,
                       sem.at[slot]).start(priority=kk % 2)

    def combine(slot, t):
        rows = pl.ds(pl.multiple_of(t * TILE_ROWS, TILE_ROWS), TILE_ROWS)
        acc = bufs[slot][0, rows, :] * w_ref[i * tm + t]
        for kk in range(1, TOP_K):
            acc = acc + bufs[slot][kk, rows, :] * w_ref[kk * n_tok + i * tm + t]
        res[rows, :] = xt_ref[rows, :] + ga * acc

    def loop(body):
        for t in range(tm):
            body(t)

    @pl.when(i == 0)
    def _():
        loop(lambda t: issue(0, 0, t))

    for slot in range(2):
        @pl.when(i % 2 == slot)
        def _():
            for kk in range(TOP_K):
                pltpu.make_async_copy(bufs[slot].at[kk], bufs[slot].at[kk], sem.at[slot]).wait()

            @pl.when(i + 1 < pl.num_programs(0))
            def _():
                loop(lambda t: (issue(i + 1, 1 - slot, t), combine(slot, t)))

            @pl.when(i + 1 >= pl.num_programs(0))
            def _():
                loop(lambda t: combine(slot, t))

    x = _from_token_tiles(res, tm)
    o_ref[...] = x * lax.rsqrt(jnp.mean(x * x, axis=-1, keepdims=True) + RMS_EPS) * g_ref[...]


def _stage_combine(dest_flat, w_flat, x1t, ga_t, g_t, yb, tokens_per_batch, tm):
    n = x1t.shape[0] // 8
    d = TILE_ROWS * TILE_LANES
    per = tokens_per_batch // tm
    return pl.pallas_call(
        functools.partial(_combine_kernel, tm=tm, n_tok=n),
        grid_spec=pltpu.PrefetchScalarGridSpec(
            num_scalar_prefetch=2,
            grid=(n // tm,),
            in_specs=[pl.BlockSpec((tm * TILE_ROWS, TILE_LANES), lambda i, *_: (i, 0)),
                      pl.BlockSpec((1, TILE_ROWS, TILE_LANES), lambda i, *_: (i // per, 0, 0)),
                      pl.BlockSpec((1, d), lambda i, *_: (0, 0)),
                      pl.BlockSpec(memory_space=pl.ANY)],
            out_specs=pl.BlockSpec((tm, d), lambda i, *_: (i, 0)),
            scratch_shapes=[pltpu.VMEM((TOP_K, tm * TILE_ROWS, TILE_LANES), F32),
                            pltpu.VMEM((TOP_K, tm * TILE_ROWS, TILE_LANES), F32),
                            pltpu.VMEM((tm * TILE_ROWS, TILE_LANES), F32),
                            pltpu.SemaphoreType.DMA((2,))]),
        out_shape=jax.ShapeDtypeStruct((n, d), F32),
        compiler_params=_params("arbitrary"),
        name="combine",
    )(dest_flat, w_flat, x1t, ga_t, g_t, yb)


def _rwkv_branch(x, ctx, mods_lat, mods_ctx, n1, w_rwkv, prm):
    rva, _, _, kb, lw = _stage_prep(ctx, *mods_ctx, n1, w_rwkv, prm, tm=ctx.shape[1],
                                    grid_shift=False)
    (z_ctx,) = _stage_wkv(rva, kb, lw, emit=False)
    rva, g, bonus, kb, lw = _stage_prep(x, *mods_lat, n1, w_rwkv, prm, tm=PREP_TOKENS,
                                        grid_shift=True)
    y_f, y_b = _stage_wkv(rva, kb, lw, z0=z_ctx, emit=True)
    return y_f, y_b, g, bonus


def _route(top_e, rank, counts, n_tok, cap):
    bm = EXPERT_ROWS
    n_rows = n_tok * TOP_K + N_EXPERTS * bm
    n_blocks = n_rows // bm
    padded = (counts + bm - 1) // bm * bm
    pad_end = jnp.cumsum(padded)
    pad_start = pad_end - padded
    experts = jnp.arange(N_EXPERTS, dtype=jnp.int32)
    start_of = jnp.sum(jnp.where(top_e[..., None] == experts, pad_start, 0), axis=-1)
    dest = (start_of + rank).astype(jnp.int32).reshape(-1)
    block_start = jnp.arange(n_blocks, dtype=jnp.int32) * bm
    n_used = (pad_end[-1] // bm).astype(jnp.int32).reshape(1)
    block_e = jnp.minimum(jnp.sum(pad_end[None, :] <= block_start[:, None], axis=1),
                          N_EXPERTS - 1).astype(jnp.int32)
    is_e = block_e[:, None] == experts[None, :]
    within = block_start - jnp.sum(jnp.where(is_e, pad_start, 0), axis=1)
    x_block = (block_e * (cap // bm) + within // bm).astype(jnp.int32)
    last_used = jnp.sum(jnp.where(jnp.arange(n_blocks) == n_used[0] - 1, x_block, 0))
    x_block = jnp.where(jnp.arange(n_blocks) < n_used[0], x_block, last_used).astype(jnp.int32)
    first_at = lax.cummin(jnp.where(padded > 0, experts, N_EXPERTS), axis=0, reverse=True)
    next_e = jnp.concatenate([first_at[1:], jnp.full((1,), N_EXPERTS, jnp.int32)])
    next_e = jnp.where(next_e < N_EXPERTS, next_e, -1).astype(jnp.int32)
    row_end = jnp.sum(jnp.where(is_e, pad_start + counts, 0), axis=1)
    block_rows = jnp.clip(row_end - block_start, 0, bm).astype(jnp.int32)
    return dest, block_e, n_used, next_e, x_block, block_rows, n_rows


def kernel(x, c, ctx, c_ctx, w_ada, b_ada, norm1_g, w_in, shift_mu, decay_w0, decay_lora_b,
           iclr_a0, iclr_lora_b, gate_lora_b, k_k, k_a, r_k, gn_w, gn_b, w_out_rwkv,
           sgu_ln_w, sgu_ln_b, sgu_w_spatial, sgu_b_spatial, w_out_sgu, w_o, norm2_g,
           router_w, router_b, exp_w_gate, exp_b_gate, exp_w_up, exp_b_up, exp_w_down,
           exp_b_down, final_norm_g):
    assert w_ada.shape[0] == 1, "single-layer problem"
    b, t, d = x.shape
    n_tok = b * t
    w = RWKV_WIDTH
    row = lambda a: a.reshape(1, -1)

    cs = jnp.zeros((8, d), F32).at[:b].set(c).at[b].set(c_ctx)
    mod = _stage_mods(cs, w_ada[0], row(b_ada[0]))
    sh1, sc1, ga1, sh2, sc2, ga2 = [m[:b, None, :] for m in jnp.split(mod, 6, axis=-1)]
    csh1, csc1 = [jnp.broadcast_to(m[b][None, None, :], (b, 1, d))
                  for m in jnp.split(mod, 6, axis=-1)[:2]]

    w_in_bf = w_in[0].astype(BF16)
    n1 = row(norm1_g[0])
    head_id = jnp.arange(w, dtype=jnp.int32) // HEAD_DIM
    headsum = (head_id[:, None] == head_id[None, :]).astype(BF16)
    prm = [row(shift_mu[0]), row(k_k[0]), row(k_a[0]), row(r_k[0]), decay_w0[0],
           decay_lora_b[0], iclr_a0[0], iclr_lora_b[0], gate_lora_b[0], headsum]
    y_f, y_b, g, bonus = _rwkv_branch(x, ctx, (sh1, sc1), (csh1, csc1), n1,
                                      w_in_bf[:, :RWKV_COLS], prm)

    bsp = jnp.repeat(sgu_b_spatial[0].T, SGU_WIDTH // SGU_GROUPS, axis=1)
    consts = [n1, w_in_bf[:, RWKV_COLS:], row(gn_w[0]), row(gn_b[0]), headsum,
              w_out_rwkv[0].astype(BF16), row(sgu_ln_w[0]), row(sgu_ln_b[0]),
              sgu_w_spatial[0].astype(BF16), bsp, w_out_sgu[0].astype(BF16),
              w_o[0].astype(BF16), row(norm2_g[0]), router_w[0].T, router_b[0].reshape(-1, 1)]
    (x1t, top_e, top_w, rank, counts, xs), cap = _stage_merge(
        x, y_f, y_b, g, bonus, (sh1, sc1, ga1, sh2, sc2), consts, tm=MERGE_TOKENS)
    dest, block_e, n_used, next_e, x_block, block_rows, n_rows = _route(top_e, rank, counts[:, 0], n_tok, cap)
    e3 = lambda a: a.reshape(N_EXPERTS, 1, -1)
    yb = _stage_experts(block_e, n_used, next_e, x_block, block_rows, n_rows, xs, exp_w_gate[0],
                        e3(exp_b_gate[0]), exp_w_up[0], e3(exp_b_up[0]), exp_w_down[0],
                        e3(exp_b_down[0]))
    out = _stage_combine(dest, top_w.reshape(-1), x1t, ga2.reshape(b, TILE_ROWS, TILE_LANES),
                         row(final_norm_g), yb, t, tm=COMBINE_TOKENS)
    return out.reshape(b, t, d)
```

```python
import functools
import math

import jax
import jax.numpy as jnp
from jax import lax
from jax.experimental import pallas as pl
from jax.experimental.pallas import tpu as pltpu

F32 = jnp.float32
BF16 = jnp.bfloat16
HIGHEST = lax.Precision.HIGHEST

D_MODEL = 1024
GRID_W = 64
RWKV_HEADS = 8
HEAD_DIM = 64
RWKV_WIDTH = RWKV_HEADS * HEAD_DIM
DECAY_LORA = 64
ICLR_LORA = 64
GATE_LORA = 128
RWKV_COLS = 3 * RWKV_WIDTH + 2 * DECAY_LORA + 2 * ICLR_LORA + GATE_LORA
SGU_WIDTH = 512
SGU_GROUPS = 8
SGU_CHUNK = 128
N_EXPERTS = 32
TOP_K = 4
SWIGLU_LIMIT = 7.0
SWIGLU_ALPHA = 1.702
RMS_EPS = 1e-6
LN_EPS = 1e-5
GN_EPS = 64e-5

TILE_ROWS = 8
TILE_LANES = 128
VMEM_LIMIT = 48 * 1024 * 1024

WKV_CHUNK = 64
WKV_GROUP = 4
EXPERT_ROWS = 512
MODS_COLS = 1536
PREP_TOKENS = 512
MERGE_TOKENS = 512
COMBINE_TOKENS = 512
SCATTER_LAG = 2


def _params(*sem):
    return pltpu.CompilerParams(dimension_semantics=sem, vmem_limit_bytes=VMEM_LIMIT)


def _mm(a, b, dims=((1,), (0,)), exact=False):
    dn = (dims, ((), ()))
    if exact:
        return lax.dot_general(a, b, dn, precision=HIGHEST, preferred_element_type=F32)
    return lax.dot_general(a.astype(BF16), b.astype(BF16), dn, preferred_element_type=F32)


def _split3(x):
    hi = x.astype(BF16)
    r1 = x - hi.astype(F32)
    mid = r1.astype(BF16)
    lo = (r1 - mid.astype(F32)).astype(BF16)
    return hi, mid, lo


def _mm_hi_lo(a, b, dims):
    a_hi = a.astype(BF16)
    b_hi = b.astype(BF16)
    a_lo = a - a_hi.astype(F32)
    b_lo = b - b_hi.astype(F32)
    return _mm(a_hi, b_hi, dims) + _mm(a_hi, b_lo, dims) + _mm(a_lo, b_hi, dims)


def _mm_sel_x(sel, x):
    return sum(_mm(sel, p) for p in _split3(x))


def _full(shape):
    n = len(shape)
    return pl.BlockSpec(shape, lambda *_: (0,) * n)


def _norm_mod(x, g, shift, scale):
    y = x * lax.rsqrt(jnp.mean(x * x, axis=-1, keepdims=True) + RMS_EPS) * g
    return y * (1.0 + scale) + shift


def _mods_kernel(c_ref, w_ref, b_ref, o_ref):
    c = c_ref[...]
    s = c * jax.nn.sigmoid(c)
    o_ref[...] = _mm(s, w_ref[...], exact=True) + b_ref[...]


def _stage_mods(cs, w_ada, b_ada):
    rows, d = cs.shape
    n = w_ada.shape[1]
    tn = MODS_COLS
    return pl.pallas_call(
        _mods_kernel,
        grid=(n // tn,),
        in_specs=[_full((rows, d)),
                  pl.BlockSpec((d, tn), lambda j: (0, j)),
                  pl.BlockSpec((1, tn), lambda j: (0, j))],
        out_specs=pl.BlockSpec((rows, tn), lambda j: (0, j)),
        out_shape=jax.ShapeDtypeStruct((rows, n), F32),
        compiler_params=_params("arbitrary"),
        name="mods",
    )(cs, w_ada, b_ada)


def _rwkv_feats(p, mu_kk, mu_ka, r_k, w0, dlb, a0, ilb, glb, headsum):
    w = RWKV_WIDTH
    r = p[:, 0:w]
    k = p[:, w:2 * w]
    v = p[:, 2 * w:3 * w]
    o = 3 * w
    wd = (p[:, o:o + DECAY_LORA], p[:, o + DECAY_LORA:o + 2 * DECAY_LORA])
    o += 2 * DECAY_LORA
    ad = (p[:, o:o + ICLR_LORA], p[:, o + ICLR_LORA:o + 2 * ICLR_LORA])
    o += 2 * ICLR_LORA
    gd = p[:, o:o + GATE_LORA]

    kk = k * mu_kk
    kk = kk * lax.rsqrt(_mm(kk * kk, headsum) + 1e-12)
    g = _mm(jax.nn.sigmoid(gd), glb)
    ks, bs, lws = [], [], []
    ksum = None
    for d in range(2):
        z = w0[d:d + 1] + _mm(jnp.tanh(wd[d]), dlb[d])
        lws.append(-math.exp(-0.5) * jax.nn.sigmoid(z))
        ic = jax.nn.sigmoid(a0[d:d + 1] + _mm(ad[d], ilb[d]))
        kd = k * (1.0 + (ic - 1.0) * mu_ka)
        ks.append(kd)
        bs.append(kk * ic)
        ksum = kd if ksum is None else ksum + kd
    bonus = _mm(r * ksum * r_k, headsum) * v
    return r, v, -kk, g, bonus, ks, bs, lws


def _prep_kernel(xm_ref, xp_ref, xn_ref, sh_ref, sc_ref, n1_ref, w_ref,
                 mu_ref, kk_ref, ka_ref, rk_ref, w0_ref, dlb_ref, a0_ref, ilb_ref, glb_ref, hs_ref,
                 rva_ref, g_ref, bon_ref, kb_ref, lw_ref, *, tm, grid_shift):
    project = lambda x: _mm(_norm_mod(x, n1_ref[...], sh_ref[0], sc_ref[0]), w_ref[...])
    lane = lax.broadcasted_iota(jnp.int32, (1, RWKV_COLS), 1)
    if grid_shift:
        ext = project(jnp.concatenate([xp_ref[0], xm_ref[0], xn_ref[0]], axis=0))
        main = ext[GRID_W:GRID_W + tm]
        t = pl.program_id(1) * tm + lax.broadcasted_iota(jnp.int32, (tm, 1), 0)
        col = t & (GRID_W - 1)
        row = t >> (GRID_W.bit_length() - 1)
        n_rows = pl.num_programs(1) * tm // GRID_W
        left = jnp.where(col > 0, ext[GRID_W - 1:GRID_W - 1 + tm], 0.0)
        right = jnp.where(col < GRID_W - 1, ext[GRID_W + 1:GRID_W + 1 + tm], 0.0)
        up = jnp.where(row > 0, ext[0:tm], 0.0)
        down = jnp.where(row < n_rows - 1, ext[2 * GRID_W:2 * GRID_W + tm], 0.0)
        cm = lane & 3
        shifted = jnp.where(cm == 0, left, jnp.where(cm == 1, right, jnp.where(cm == 2, up, down)))
    else:
        main = project(xm_ref[0])
        zero = jnp.zeros((1, RWKV_COLS), F32)
        prev = jnp.concatenate([zero, main[:tm - 1]], axis=0)
        nxt = jnp.concatenate([main[1:], zero], axis=0)
        shifted = jnp.where((lane & 1) == 0, prev, nxt)
    p = main + mu_ref[...] * (shifted - main)
    r, v, a, g, bonus, ks, bs, lws = _rwkv_feats(
        p, kk_ref[...], ka_ref[...], rk_ref[...], w0_ref[...], dlb_ref, a0_ref[...], ilb_ref,
        glb_ref[...], hs_ref[...])
    w = RWKV_WIDTH
    for j, val in enumerate((r, v, a)):
        rva_ref[0, :, j * w:(j + 1) * w] = val.astype(BF16)
    g_ref[0] = g.astype(BF16)
    bon_ref[0] = bonus.astype(BF16)
    for d in range(2):
        kb_ref[d, 0, :, 0:w] = ks[d].astype(BF16)
        kb_ref[d, 0, :, w:2 * w] = bs[d].astype(BF16)
        lw_ref[d, 0] = lws[d]


def _stage_prep(x, shift, scale, g, w_bf16, prm, tm, grid_shift):
    b, t, c = x.shape
    w = RWKV_WIDTH
    mod = pl.BlockSpec((1, 1, c), lambda i, j: (i, 0, 0))
    hb = GRID_W if grid_shift else 8
    per = tm // hb
    last = t // hb - 1
    tok = lambda n_: pl.BlockSpec((1, tm, n_ * w), lambda i, j: (i, j, 0))
    tok2 = lambda n_: pl.BlockSpec((2, 1, tm, n_ * w), lambda i, j: (0, i, j, 0))
    in_specs = [
        pl.BlockSpec((1, tm, c), lambda i, j: (i, j, 0)),
        pl.BlockSpec((1, hb, c), lambda i, j: (i, jnp.maximum(j * per - 1, 0), 0)),
        pl.BlockSpec((1, hb, c), lambda i, j: (i, jnp.minimum((j + 1) * per, last), 0)),
        mod, mod, _full(g.shape), _full(w_bf16.shape),
    ] + [_full(a.shape) for a in prm]
    return pl.pallas_call(
        functools.partial(_prep_kernel, tm=tm, grid_shift=grid_shift),
        grid=(b, t // tm),
        in_specs=in_specs,
        out_specs=[tok(3), tok(1), tok(1), tok2(2), tok2(1)],
        out_shape=[jax.ShapeDtypeStruct((b, t, 3 * w), BF16),
                   jax.ShapeDtypeStruct((b, t, w), BF16),
                   jax.ShapeDtypeStruct((b, t, w), BF16),
                   jax.ShapeDtypeStruct((2, b, t, 2 * w), BF16),
                   jax.ShapeDtypeStruct((2, b, t, w), F32)],
        compiler_params=_params("arbitrary", "arbitrary"),
        name="prep_lat" if grid_shift else "prep_ctx",
    )(x, x, x, shift, scale, g, w_bf16, *prm)


def _wkv_prepare(d, r, v, a, k, bb, lw):
    c = WKV_CHUNK
    n = HEAD_DIM
    ii = lax.broadcasted_iota(jnp.int32, (c, c), 0)
    jj = lax.broadcasted_iota(jnp.int32, (c, c), 1)
    incl = (jj <= ii) if d == 0 else (jj >= ii)
    lc = _mm_sel_x(incl.astype(BF16), lw)
    lx = lc - lw
    ltot = lc[c - 1:c] if d == 0 else lc[0:1]
    inv = jnp.exp(-lc)
    tail = jnp.exp(ltot - lc)
    etot = jnp.exp(ltot)
    eye_n = (lax.broadcasted_iota(jnp.int32, (n, n), 0)
             == lax.broadcasted_iota(jnp.int32, (n, n), 1)).astype(F32)
    scale = jnp.concatenate(
        [jnp.broadcast_to(jnp.sum(eye_n * etot[:, h * n:(h + 1) * n], axis=1, keepdims=True), (n, n))
         for h in range(RWKV_HEADS)], axis=1)
    bf = lambda x: x.astype(BF16)
    return (bf(a * jnp.exp(lx)), bf(r * jnp.exp(lc)), bf(k * inv), bf(bb * inv), bf(k * tail),
            bf(bb * tail), bf(v), scale)


def _wkv_consts():
    c = WKV_CHUNK
    n = HEAD_DIM
    gw = WKV_GROUP * n
    assert c == n and n & (n - 1) == 0
    ii = lax.broadcasted_iota(jnp.int32, (c, gw), 0)
    jj = lax.broadcasted_iota(jnp.int32, (c, gw), 1) & (c - 1)
    head_shift = n.bit_length() - 1
    same_head = (lax.broadcasted_iota(jnp.int32, (gw, gw), 0) >> head_shift
                 == lax.broadcasted_iota(jnp.int32, (gw, gw), 1) >> head_shift)

    def bd(x):
        x = x.astype(BF16)
        return jnp.where(same_head, jnp.concatenate([x] * WKV_GROUP, axis=0), jnp.zeros((), BF16))

    return ii, jj, bd


def _wkv_factor(chunks, emit):
    c = WKV_CHUNK
    gw = WKV_GROUP * HEAD_DIM
    n_groups = RWKV_WIDTH // gw
    ii, jj, bd = _wkv_consts()
    eye = (ii == jj).astype(F32)
    nt = ((1,), (1,))
    tn = ((0,), (0,))
    sl = [slice(h * HEAD_DIM, (h + 1) * HEAD_DIM) for h in range(WKV_GROUP)]

    prob = []
    for preps in chunks:
        for d, (at, rt, kt, bt, kh, bh, vb, _) in enumerate(preps):
            incl = (jj <= ii) if d == 0 else (jj >= ii)
            strict = (jj < ii) if d == 0 else (jj > ii)
            for gi in range(n_groups):
                gs = slice(gi * gw, (gi + 1) * gw)
                prob.append(dict(incl=incl, strict=strict, at=at[:, gs], rt=rt[:, gs],
                                 kt=kt[:, gs], bt=bt[:, gs], kh=kh[:, gs], bh=bh[:, gs],
                                 v=vb[:, gs]))
    for p in prob:
        lhs = jnp.concatenate([p["at"], p["rt"]], axis=0) if emit else p["at"]
        p["lhs_z"] = lhs
        gk = _mm(lhs, bd(p["kt"]), nt)
        gb = _mm(lhs, bd(p["bt"]), nt)
        p["a_ab"] = jnp.where(p["strict"], gb[:c], 0.0)
        lhs_v = jnp.where(p["strict"], gk[:c], 0.0)
        if emit:
            lhs_v = jnp.concatenate([lhs_v, jnp.where(p["incl"], gk[c:], 0.0)], axis=0)
            p["a_rb"] = jnp.where(p["incl"], gb[c:], 0.0).astype(BF16)
        p["lhs_v"] = lhs_v
    for p in prob:
        p["tinv"] = eye + p["a_ab"]
        p["x"] = _mm(p["a_ab"], bd(p["a_ab"]))
        p["av"] = _mm(p["lhs_v"], bd(p["v"]))
    for level in range(1, 6):
        for p in prob:
            if level < 5:
                both = _mm(jnp.concatenate([p["x"], p["tinv"]], axis=0), bd(p["x"]))
                p["x"] = both[:c]
                p["tinv"] = p["tinv"] + both[c:]
            else:
                p["tinv"] = (p["tinv"] + _mm(p["tinv"], bd(p["x"]))).astype(BF16)
    per_chunk = 2 * n_groups
    return [prob[j * per_chunk:(j + 1) * per_chunk] for j in range(len(chunks))]


def _wkv_apply(probs, chunks, states, emit):
    c = WKV_CHUNK
    n = HEAD_DIM
    gw = WKV_GROUP * n
    n_groups = RWKV_WIDTH // gw
    _, _, bd = _wkv_consts()
    tn = ((0,), (0,))
    sl = [slice(h * n, (h + 1) * n) for h in range(WKV_GROUP)]
    flat = []
    for prob, zs in zip(probs, states):
        zb = [z.astype(BF16) for z in zs]
        for i, p in enumerate(prob):
            d, gi = divmod(i, n_groups)
            p["z"] = zb[d][:, gi * gw:(gi + 1) * gw]
            flat.append(p)
    for p in flat:
        p["zv"] = _mm(p["lhs_z"], bd(p["z"])) + p["av"]
    for p in flat:
        p["u"] = _mm(p["tinv"], bd(p["zv"][:c])).astype(BF16)
    for p in flat:
        if emit:
            p["y"] = p["zv"][c:] + _mm(p["a_rb"], bd(p["u"]))
        p["z_new"] = [_mm(jnp.concatenate([p["kh"][:, s], p["bh"][:, s]], axis=0),
                          jnp.concatenate([p["v"][:, s], p["u"][:, s]], axis=0), tn) for s in sl]
    out = []
    for prob, preps, zs in zip(probs, chunks, states):
        res = []
        for d, (prep, z) in enumerate(zip(preps, zs)):
            mine = prob[d * n_groups:(d + 1) * n_groups]
            z_cat = jnp.concatenate([m for p in mine for m in p["z_new"]], axis=1) + prep[7] * z
            y = jnp.concatenate([p["y"] for p in mine], axis=1) if emit else None
            res.append((y, z_cat))
        out.append(res)
    return out


def _wkv_kernel(*refs, emit, has_init):
    refs = list(refs)
    ins = [[refs.pop(0) for _ in range(3)] for _ in range(2)]
    z0_ref = refs.pop(0) if has_init else None
    outs = [refs.pop(0) for _ in range(2 if emit else 1)]
    z_scr = refs.pop(0)
    n_seq = z_scr.shape[1]

    @pl.when(pl.program_id(0) == 0)
    def _():
        if has_init:
            z_scr[...] = z0_ref[...]
        else:
            z_scr[...] = jnp.zeros_like(z_scr)

    chunks = []
    for i in range(n_seq):
        preps = []
        for d in range(2):
            rva_ref, kb_ref, lw_ref = ins[d]
            w = RWKV_WIDTH
            r, v, a = (rva_ref[i, :, j * w:(j + 1) * w] for j in range(3))
            preps.append(_wkv_prepare(d, r, v, a, kb_ref[0, i, :, 0:w], kb_ref[0, i, :, w:2 * w],
                                      lw_ref[0, i]))
        chunks.append(preps)
    probs = _wkv_factor(chunks, emit)
    states = [(z_scr[0, i], z_scr[1, i]) for i in range(n_seq)]
    for i, res in enumerate(_wkv_apply(probs, chunks, states, emit)):
        for d, (y, z_new) in enumerate(res):
            z_scr[d, i] = z_new
            if emit:
                outs[d][i] = y
    if not emit:
        outs[0][...] = z_scr[...]


def _stage_wkv(rva, kb, lw, z0=None, emit=True):
    b, t, w = lw.shape[1:]
    c = WKV_CHUNK
    nch = t // c
    n = HEAD_DIM
    pos = (lambda s: s, lambda s: nch - 1 - s)
    in_specs, args = [], []
    for d in range(2):
        in_specs += [pl.BlockSpec((b, c, 3 * w), lambda s, d=d: (0, pos[d](s), 0)),
                     pl.BlockSpec((1, b, c, 2 * w), lambda s, d=d: (d, 0, pos[d](s), 0)),
                     pl.BlockSpec((1, b, c, w), lambda s, d=d: (d, 0, pos[d](s), 0))]
        args += [rva, kb, lw]
    zspec = pl.BlockSpec((2, b, n, w), lambda s: (0, 0, 0, 0))
    if z0 is not None:
        in_specs.append(zspec)
        args.append(z0)
    if emit:
        out_specs = [pl.BlockSpec((b, c, w), lambda s, d=d: (0, pos[d](s), 0)) for d in range(2)]
        out_shape = [jax.ShapeDtypeStruct((b, t, w), F32)] * 2
    else:
        out_specs = [zspec]
        out_shape = [jax.ShapeDtypeStruct((2, b, n, w), F32)]
    return pl.pallas_call(
        functools.partial(_wkv_kernel, emit=emit, has_init=z0 is not None),
        grid=(nch,),
        in_specs=in_specs,
        out_specs=out_specs,
        out_shape=out_shape,
        scratch_shapes=[pltpu.VMEM((2, b, n, w), F32)],
        compiler_params=_params("arbitrary"),
        name="wkv_lat" if emit else "wkv_ctx",
    )(*args)


def _to_token_tiles(ref, val, rows):
    for cc in range(val.shape[1] // TILE_LANES):
        ref[pl.ds(cc, rows, stride=TILE_ROWS), :] = val[:, cc * TILE_LANES:(cc + 1) * TILE_LANES]


def _from_token_tiles(ref, rows):
    return jnp.concatenate([ref[pl.ds(cc, rows, stride=TILE_ROWS), :] for cc in range(TILE_ROWS)],
                           axis=1)


def _route_tokens(h, rwt_ref, rb_ref, e_ref, w_ref, rank_ref, cnt_ref, carry, tm):
    logits = _mm_hi_lo(rwt_ref[...], h, dims=((1,), (1,))) + rb_ref[...]
    eio = lax.broadcasted_iota(jnp.int32, (N_EXPERTS, tm), 0)
    vals, sels, idxs = [], [], []
    for _ in range(TOP_K):
        m = jnp.max(logits, axis=0, keepdims=True)
        idx = jnp.min(jnp.where(logits == m, eio, N_EXPERTS), axis=0, keepdims=True)
        sel = eio == idx
        logits = jnp.where(sel, -jnp.inf, logits)
        vals.append(m)
        sels.append(sel)
        idxs.append(idx)
        e_ref[len(vals) - 1:len(vals), :] = idx
    ex = [jnp.exp(vk - vals[0]) for vk in vals]
    tot = ex[0] + ex[1] + ex[2] + ex[3]
    for kk in range(TOP_K):
        w_ref[kk:kk + 1, :] = ex[kk] / tot
    cnt = (sels[0] | sels[1] | sels[2] | sels[3]).astype(F32)
    ti = lax.broadcasted_iota(jnp.int32, (tm, tm), 0)
    tj = lax.broadcasted_iota(jnp.int32, (tm, tm), 1)
    before = _mm(cnt, (ti < tj).astype(F32))
    base = carry[...] + before
    ranks = []
    for kk in range(TOP_K):
        ranks.append(jnp.sum(jnp.where(sels[kk], base, 0.0), axis=0,
                             keepdims=True).astype(jnp.int32))
        rank_ref[kk:kk + 1, :] = ranks[kk]
    new = carry[...] + jnp.sum(cnt, axis=1, keepdims=True)
    carry[...] = new
    cnt_ref[...] = jnp.broadcast_to(new, cnt_ref.shape).astype(jnp.int32)
    return idxs, ranks


def _merge_kernel(x_ref, yf_ref, yb_ref, g_ref, bon_ref, sh_ref, sc_ref, ga_ref, sh2_ref, sc2_ref,
                  n1_ref, w2_ref, gnw_ref, gnb_ref, hs_ref, wor_ref, lnw_ref, lnb_ref, wsp_ref,
                  bsp_ref, wos_ref, wo_ref, n2_ref, rwt_ref, rb_ref,
                  xt_ref, e_ref, w_ref, rank_ref, cnt_ref, xs_ref,
                  carry, h2buf, zeros, dest_v, dest_s, cnt_s, ssem, csem, zsem, *, tm, cap):
    step = pl.program_id(0) * pl.num_programs(1) + pl.program_id(1)
    n_steps = pl.num_programs(0) * pl.num_programs(1)
    dump = N_EXPERTS * cap
    n_buf = h2buf.shape[0] - 1

    def dest_copy(which):
        return pltpu.make_async_copy(dest_v, dest_s.at[which], csem)

    def drain(which):
        for _ in range(TOP_K):
            pltpu.make_async_copy(h2buf.at[which], h2buf.at[which], ssem.at[which]).wait()

    @pl.when(step == 0)
    def _():
        carry[...] = jnp.zeros_like(carry)
        zeros[...] = jnp.zeros_like(zeros)
        for which in range(SCATTER_LAG):
            h2buf[SCATTER_LAG + which] = jnp.zeros(h2buf.shape[1:], F32)
            dest_v[...] = (dump + which * TOP_K * tm
                           + lax.broadcasted_iota(jnp.int32, (TOP_K, tm), 0) * tm
                           + lax.broadcasted_iota(jnp.int32, (TOP_K, tm), 1))
            dest_copy(which).start()
            dest_copy(which).wait()

    @pl.when(step >= SCATTER_LAG)
    def _():
        drain(step % n_buf)

    src = (step + SCATTER_LAG) % n_buf
    n_groups = 4

    def after_copies(val, group):
        per = tm // n_groups
        for t in range(group * per, (group + 1) * per):
            for kk in range(TOP_K):
                _tile_copy(h2buf.at[src], t, xs_ref, dest_s[step % SCATTER_LAG, kk, t],
                           ssem.at[src]).start(priority=kk % 2)
        rows = slice(group * TILE_ROWS, (group + 1) * TILE_ROWS)
        h2buf[n_buf, rows, :] = jnp.zeros((TILE_ROWS, TILE_LANES), F32)
        return val + h2buf[n_buf, rows, :][0:1, 0:1]

    x = x_ref[0]
    h = after_copies(_norm_mod(x, n1_ref[...], sh_ref[0], sc_ref[0]), 0)
    p2 = _mm(h, w2_ref[...])

    ps = p2[:, :2 * SGU_WIDTH]
    ge = 0.5 * ps * (1.0 + lax.erf(ps * (1.0 / math.sqrt(2.0))))
    u = ge[:, :SGU_WIDTH]
    z = ge[:, SGU_WIDTH:]
    mu = jnp.mean(z, axis=-1, keepdims=True)
    zc = z - mu
    var = jnp.mean(zc * zc, axis=-1, keepdims=True)
    z = zc * lax.rsqrt(var + LN_EPS) * lnw_ref[...] + lnb_ref[...]
    gw = SGU_WIDTH // SGU_GROUPS
    rows = []
    for c in range(tm // SGU_CHUNK):
        zc = z[c * SGU_CHUNK:(c + 1) * SGU_CHUNK]
        cols = [_mm(wsp_ref[gi], zc[:, gi * gw:(gi + 1) * gw]) for gi in range(SGU_GROUPS)]
        rows.append(jnp.concatenate(cols, axis=1) + bsp_ref[...])
    s = jnp.concatenate(rows, axis=0)
    y_b = _mm(after_copies(u * s, 1), wos_ref[...])

    y = yf_ref[0] + yb_ref[0]
    hs = hs_ref[...]
    ym = _mm(y, hs) * (1.0 / HEAD_DIM)
    yc = y - ym
    yv = _mm(yc * yc, hs) * (1.0 / HEAD_DIM)
    yn = yc * lax.rsqrt(yv + GN_EPS) * gnw_ref[...] + gnb_ref[...]
    y_a = _mm(after_copies((yn + bon_ref[0]) * g_ref[0], 2), wor_ref[...])

    gates = jax.nn.sigmoid(p2[:, 2 * SGU_WIDTH:])
    mix = gates[:, :D_MODEL] * y_a + gates[:, D_MODEL:] * y_b
    x1 = x + ga_ref[0] * _mm(after_copies(mix, 3), wo_ref[...])

    h2 = _norm_mod(x1, n2_ref[...], sh2_ref[0], sc2_ref[0])
    _to_token_tiles(h2buf.at[step % n_buf], h2, tm)
    _to_token_tiles(xt_ref, x1, tm)
    idxs, ranks = _route_tokens(h2, rwt_ref, rb_ref, e_ref, w_ref, rank_ref, cnt_ref, carry, tm)

    @pl.when(step > 0)
    def _():
        dest_copy((step - 1) % SCATTER_LAG).wait()

    for kk in range(TOP_K):
        dest_v[kk:kk + 1, :] = idxs[kk] * cap + ranks[kk]
    dest_copy(step % SCATTER_LAG).start()

    @pl.when(step == n_steps - 1)
    def _():
        dest_copy(step % SCATTER_LAG).wait()
        for back in reversed(range(SCATTER_LAG)):

            def flush(t, c, back=back):
                for kk in range(TOP_K):
                    _tile_copy(h2buf.at[(step - back) % n_buf], t, xs_ref,
                               dest_s[(step - back) % SCATTER_LAG, kk, t],
                               ssem.at[(step - back) % n_buf]).start(priority=kk % 2)
                return c

            lax.fori_loop(0, tm, flush, 0)
        counts = pltpu.make_async_copy(cnt_ref, cnt_s, csem)
        counts.start()
        counts.wait()
        for e in range(N_EXPERTS):
            first = pl.multiple_of((e * cap + cnt_s[e, 0]) * TILE_ROWS, TILE_ROWS)
            pltpu.make_async_copy(zeros, xs_ref.at[pl.ds(first, zeros.shape[0])], zsem).start()
        for e in range(N_EXPERTS):
            pltpu.make_async_copy(zeros, xs_ref.at[pl.ds(0, zeros.shape[0])], zsem).wait()
        for which in range(n_buf):
            drain(which)


def _stage_merge(x, yf, yb, g, bonus, mods, consts, tm):
    b, t, d = x.shape
    n = b * t
    w = RWKV_WIDTH
    per = t // tm
    cap = n + EXPERT_ROWS
    assert d == TILE_ROWS * TILE_LANES, "token-tile layout stores one vector tile per token"
    mod = pl.BlockSpec((1, 1, d), lambda i, j: (i, 0, 0))
    tok = pl.BlockSpec((1, tm, w), lambda i, j: (i, j, 0))
    tiles = pl.BlockSpec((tm * TILE_ROWS, TILE_LANES), lambda i, j: (i * per + j, 0))
    lane = pl.BlockSpec((TOP_K, tm), lambda i, j: (0, i * per + j))
    return pl.pallas_call(
        functools.partial(_merge_kernel, tm=tm, cap=cap),
        grid=(b, per),
        in_specs=[pl.BlockSpec((1, tm, d), lambda i, j: (i, j, 0)), tok, tok, tok, tok]
        + [mod] * len(mods) + [_full(a.shape) for a in consts],
        out_specs=[tiles, lane, lane, lane, _full((N_EXPERTS, TILE_LANES)),
                   pl.BlockSpec(memory_space=pl.ANY)],
        out_shape=[jax.ShapeDtypeStruct((n * TILE_ROWS, TILE_LANES), F32),
                   jax.ShapeDtypeStruct((TOP_K, n), jnp.int32),
                   jax.ShapeDtypeStruct((TOP_K, n), F32),
                   jax.ShapeDtypeStruct((TOP_K, n), jnp.int32),
                   jax.ShapeDtypeStruct((N_EXPERTS, TILE_LANES), jnp.int32),
                   jax.ShapeDtypeStruct(((N_EXPERTS * cap + SCATTER_LAG * TOP_K * tm) * TILE_ROWS,
                                         TILE_LANES), F32)],
        scratch_shapes=[pltpu.VMEM((N_EXPERTS, 1), F32),
                        pltpu.VMEM((2 * SCATTER_LAG + 1, tm * TILE_ROWS, TILE_LANES), F32),
                        pltpu.VMEM((EXPERT_ROWS * TILE_ROWS, TILE_LANES), F32),
                        pltpu.VMEM((TOP_K, tm), jnp.int32),
                        pltpu.SMEM((SCATTER_LAG, TOP_K, tm), jnp.int32),
                        pltpu.SMEM((N_EXPERTS, TILE_LANES), jnp.int32),
                        pltpu.SemaphoreType.DMA((2 * SCATTER_LAG,)), pltpu.SemaphoreType.DMA(()),
                        pltpu.SemaphoreType.DMA(())],
        compiler_params=_params("arbitrary", "arbitrary"),
        name="merge",
    )(x, yf, yb, g, bonus, *mods, *consts), cap


def _expert_kernel(be_ref, nu_ref, nx_ref, xb_ref, nrow_ref, x_ref, bg_ref, bu_ref, bd_ref,
                   wg_hbm, wu_hbm, wd_hbm, o_ref, stage, wg_s, wu_s, wd_s, sem):
    i = pl.program_id(0)
    e = be_ref[i]
    used = i < nu_ref[0]
    prev = be_ref[jnp.maximum(i - 1, 0)]

    def fetch(expert):
        return [pltpu.make_async_copy(w.at[expert], stage.at[j], sem.at[j])
                for j, w in enumerate((wg_hbm, wu_hbm, wd_hbm))]

    @pl.when(i == 0)
    def _():
        for cp in fetch(e):
            cp.start()

    @pl.when(used & ((i == 0) | (e != prev)))
    def _():
        for cp, w_s, j in zip(fetch(e), (wg_s, wu_s, wd_s), range(3)):
            cp.wait()
            w_s[...] = stage[j].astype(BF16)
        nxt = nx_ref[e]

        @pl.when(nxt >= 0)
        def _():
            for cp in fetch(nxt):
                cp.start()

    def run(rows):
        x = _from_token_tiles(x_ref, rows)
        gate = _mm(x, wg_s[...]) + bg_ref[0]
        up = _mm(x, wu_s[...]) + bu_ref[0]
        gate = jnp.minimum(gate, SWIGLU_LIMIT)
        up = jnp.clip(up, -SWIGLU_LIMIT, SWIGLU_LIMIT)
        act = gate * jax.nn.sigmoid(SWIGLU_ALPHA * gate) * (up + 1.0)
        _to_token_tiles(o_ref, _mm(act, wd_s[...]) + bd_ref[0], rows)

    half = EXPERT_ROWS // 2
    wide = used & (nrow_ref[i] > half)

    @pl.when(wide)
    def _():
        run(EXPERT_ROWS)

    @pl.when(used & jnp.logical_not(wide))
    def _():
        run(half)
        o_ref[half * TILE_ROWS:, :] = jnp.zeros((half * TILE_ROWS, TILE_LANES), F32)

    @pl.when(jnp.logical_not(used))
    def _():
        o_ref[...] = jnp.zeros_like(o_ref)


def _stage_experts(block_e, n_used, next_e, x_block, block_rows, n_rows, xs, wg, bg, wu, bu, wd,
                   bd):
    d, f = wg.shape[1:]
    assert d == f, "one staging buffer shape serves all three weight matrices"
    bm = EXPERT_ROWS
    bspec = lambda n_: pl.BlockSpec((1, 1, n_), lambda i, be, *_: (be[i], 0, 0))
    hbm = pl.BlockSpec(memory_space=pl.ANY)
    return pl.pallas_call(
        _expert_kernel,
        grid_spec=pltpu.PrefetchScalarGridSpec(
            num_scalar_prefetch=5,
            grid=(n_rows // bm,),
            in_specs=[pl.BlockSpec((bm * TILE_ROWS, TILE_LANES), lambda i, be, nu, nx, xb, nr: (xb[i], 0)),
                      bspec(f), bspec(f), bspec(d), hbm, hbm, hbm],
            out_specs=pl.BlockSpec((bm * TILE_ROWS, TILE_LANES), lambda i, *_: (i, 0)),
            scratch_shapes=[pltpu.VMEM((3, d, f), F32), pltpu.VMEM((d, f), BF16),
                            pltpu.VMEM((d, f), BF16), pltpu.VMEM((f, d), BF16),
                            pltpu.SemaphoreType.DMA((3,))]),
        out_shape=jax.ShapeDtypeStruct((n_rows * TILE_ROWS, TILE_LANES), F32),
        compiler_params=_params("arbitrary"),
        name="experts",
    )(block_e, n_used, next_e, x_block, block_rows, xs, bg, bu, bd, wg, wu, wd)


def _tile_copy(src_ref, src_row, dst_ref, dst_row, sem):
    src = src_ref.at[pl.ds(pl.multiple_of(src_row * TILE_ROWS, TILE_ROWS), TILE_ROWS)]
    dst = dst_ref.at[pl.ds(pl.multiple_of(dst_row * TILE_ROWS, TILE_ROWS), TILE_ROWS)]
    return pltpu.make_async_copy(src, dst, sem)


def _combine_kernel(dest_ref, w_ref, xt_ref, ga_ref, g_ref, yb_ref, o_ref, buf0, buf1, res, sem,
                    *, tm, n_tok):
    i = pl.program_id(0)
    ga = ga_ref[0]
    bufs = (buf0, buf1)

    def issue(tile, slot, t):
        for kk in range(TOP_K):
            _tile_copy(yb_ref, dest_ref[kk * n_tok + tile * tm + t], bufs[slot].at[kk], t,
                       sem.at[slot]).start(priority=kk % 2)

    def combine(slot, t):
        rows = pl.ds(pl.multiple_of(t * TILE_ROWS, TILE_ROWS), TILE_ROWS)
        acc = bufs[slot][0, rows, :] * w_ref[i * tm + t]
        for kk in range(1, TOP_K):
            acc = acc + bufs[slot][kk, rows, :] * w_ref[kk * n_tok + i * tm + t]
        res[rows, :] = xt_ref[rows, :] + ga * acc

    def loop(body):
        for t in range(tm):
            body(t)

    @pl.when(i == 0)
    def _():
        loop(lambda t: issue(0, 0, t))

    for slot in range(2):
        @pl.when(i % 2 == slot)
        def _():
            for kk in range(TOP_K):
                pltpu.make_async_copy(bufs[slot].at[kk], bufs[slot].at[kk], sem.at[slot]).wait()

            @pl.when(i + 1 < pl.num_programs(0))
            def _():
                loop(lambda t: (issue(i + 1, 1 - slot, t), combine(slot, t)))

            @pl.when(i + 1 >= pl.num_programs(0))
            def _():
                loop(lambda t: combine(slot, t))

    x = _from_token_tiles(res, tm)
    o_ref[...] = x * lax.rsqrt(jnp.mean(x * x, axis=-1, keepdims=True) + RMS_EPS) * g_ref[...]


def _stage_combine(dest_flat, w_flat, x1t, ga_t, g_t, yb, tokens_per_batch, tm):
    n = x1t.shape[0] // 8
    d = TILE_ROWS * TILE_LANES
    per = tokens_per_batch // tm
    return pl.pallas_call(
        functools.partial(_combine_kernel, tm=tm, n_tok=n),
        grid_spec=pltpu.PrefetchScalarGridSpec(
            num_scalar_prefetch=2,
            grid=(n // tm,),
            in_specs=[pl.BlockSpec((tm * TILE_ROWS, TILE_LANES), lambda i, *_: (i, 0)),
                      pl.BlockSpec((1, TILE_ROWS, TILE_LANES), lambda i, *_: (i // per, 0, 0)),
                      pl.BlockSpec((1, d), lambda i, *_: (0, 0)),
                      pl.BlockSpec(memory_space=pl.ANY)],
            out_specs=pl.BlockSpec((tm, d), lambda i, *_: (i, 0)),
            scratch_shapes=[pltpu.VMEM((TOP_K, tm * TILE_ROWS, TILE_LANES), F32),
                            pltpu.VMEM((TOP_K, tm * TILE_ROWS, TILE_LANES), F32),
                            pltpu.VMEM((tm * TILE_ROWS, TILE_LANES), F32),
                            pltpu.SemaphoreType.DMA((2,))]),
        out_shape=jax.ShapeDtypeStruct((n, d), F32),
        compiler_params=_params("arbitrary"),
        name="combine",
    )(dest_flat, w_flat, x1t, ga_t, g_t, yb)


def _rwkv_branch(x, ctx, mods_lat, mods_ctx, n1, w_rwkv, prm):
    rva, _, _, kb, lw = _stage_prep(ctx, *mods_ctx, n1, w_rwkv, prm, tm=ctx.shape[1],
                                    grid_shift=False)
    (z_ctx,) = _stage_wkv(rva, kb, lw, emit=False)
    rva, g, bonus, kb, lw = _stage_prep(x, *mods_lat, n1, w_rwkv, prm, tm=PREP_TOKENS,
                                        grid_shift=True)
    y_f, y_b = _stage_wkv(rva, kb, lw, z0=z_ctx, emit=True)
    return y_f, y_b, g, bonus


def _route(top_e, rank, counts, n_tok, cap):
    bm = EXPERT_ROWS
    n_rows = n_tok * TOP_K + N_EXPERTS * bm
    n_blocks = n_rows // bm
    padded = (counts + bm - 1) // bm * bm
    pad_end = jnp.cumsum(padded)
    pad_start = pad_end - padded
    experts = jnp.arange(N_EXPERTS, dtype=jnp.int32)
    start_of = jnp.sum(jnp.where(top_e[..., None] == experts, pad_start, 0), axis=-1)
    dest = (start_of + rank).astype(jnp.int32).reshape(-1)
    block_start = jnp.arange(n_blocks, dtype=jnp.int32) * bm
    n_used = (pad_end[-1] // bm).astype(jnp.int32).reshape(1)
    block_e = jnp.minimum(jnp.sum(pad_end[None, :] <= block_start[:, None], axis=1),
                          N_EXPERTS - 1).astype(jnp.int32)
    is_e = block_e[:, None] == experts[None, :]
    within = block_start - jnp.sum(jnp.where(is_e, pad_start, 0), axis=1)
    x_block = (block_e * (cap // bm) + within // bm).astype(jnp.int32)
    last_used = jnp.sum(jnp.where(jnp.arange(n_blocks) == n_used[0] - 1, x_block, 0))
    x_block = jnp.where(jnp.arange(n_blocks) < n_used[0], x_block, last_used).astype(jnp.int32)
    first_at = lax.cummin(jnp.where(padded > 0, experts, N_EXPERTS), axis=0, reverse=True)
    next_e = jnp.concatenate([first_at[1:], jnp.full((1,), N_EXPERTS, jnp.int32)])
    next_e = jnp.where(next_e < N_EXPERTS, next_e, -1).astype(jnp.int32)
    row_end = jnp.sum(jnp.where(is_e, pad_start + counts, 0), axis=1)
    block_rows = jnp.clip(row_end - block_start, 0, bm).astype(jnp.int32)
    return dest, block_e, n_used, next_e, x_block, block_rows, n_rows


def kernel(x, c, ctx, c_ctx, w_ada, b_ada, norm1_g, w_in, shift_mu, decay_w0, decay_lora_b,
           iclr_a0, iclr_lora_b, gate_lora_b, k_k, k_a, r_k, gn_w, gn_b, w_out_rwkv,
           sgu_ln_w, sgu_ln_b, sgu_w_spatial, sgu_b_spatial, w_out_sgu, w_o, norm2_g,
           router_w, router_b, exp_w_gate, exp_b_gate, exp_w_up, exp_b_up, exp_w_down,
           exp_b_down, final_norm_g):
    assert w_ada.shape[0] == 1, "single-layer problem"
    b, t, d = x.shape
    n_tok = b * t
    w = RWKV_WIDTH
    row = lambda a: a.reshape(1, -1)

    cs = jnp.zeros((8, d), F32).at[:b].set(c).at[b].set(c_ctx)
    mod = _stage_mods(cs, w_ada[0], row(b_ada[0]))
    sh1, sc1, ga1, sh2, sc2, ga2 = [m[:b, None, :] for m in jnp.split(mod, 6, axis=-1)]
    csh1, csc1 = [jnp.broadcast_to(m[b][None, None, :], (b, 1, d))
                  for m in jnp.split(mod, 6, axis=-1)[:2]]

    w_in_bf = w_in[0].astype(BF16)
    n1 = row(norm1_g[0])
    head_id = jnp.arange(w, dtype=jnp.int32) // HEAD_DIM
    headsum = (head_id[:, None] == head_id[None, :]).astype(BF16)
    prm = [row(shift_mu[0]), row(k_k[0]), row(k_a[0]), row(r_k[0]), decay_w0[0],
           decay_lora_b[0], iclr_a0[0], iclr_lora_b[0], gate_lora_b[0], headsum]
    y_f, y_b, g, bonus = _rwkv_branch(x, ctx, (sh1, sc1), (csh1, csc1), n1,
                                      w_in_bf[:, :RWKV_COLS], prm)

    bsp = jnp.repeat(sgu_b_spatial[0].T, SGU_WIDTH // SGU_GROUPS, axis=1)
    consts = [n1, w_in_bf[:, RWKV_COLS:], row(gn_w[0]), row(gn_b[0]), headsum,
              w_out_rwkv[0].astype(BF16), row(sgu_ln_w[0]), row(sgu_ln_b[0]),
              sgu_w_spatial[0].astype(BF16), bsp, w_out_sgu[0].astype(BF16),
              w_o[0].astype(BF16), row(norm2_g[0]), router_w[0].T, router_b[0].reshape(-1, 1)]
    (x1t, top_e, top_w, rank, counts, xs), cap = _stage_merge(
        x, y_f, y_b, g, bonus, (sh1, sc1, ga1, sh2, sc2), consts, tm=MERGE_TOKENS)
    dest, block_e, n_used, next_e, x_block, block_rows, n_rows = _route(top_e, rank, counts[:, 0], n_tok, cap)
    e3 = lambda a: a.reshape(N_EXPERTS, 1, -1)
    yb = _stage_experts(block_e, n_used, next_e, x_block, block_rows, n_rows, xs, exp_w_gate[0],
                        e3(exp_b_gate[0]), exp_w_up[0], e3(exp_b_up[0]), exp_w_down[0],
                        e3(exp_b_down[0]))
    out = _stage_combine(dest, top_w.reshape(-1), x1t, ga2.reshape(b, TILE_ROWS, TILE_LANES),
                         row(final_norm_g), yb, t, tm=COMBINE_TOKENS)
    return out.reshape(b, t, d)
```

```python
import functools
import math

import jax
import jax.numpy as jnp
from jax import lax
from jax.experimental import pallas as pl
from jax.experimental.pallas import tpu as pltpu

F32 = jnp.float32
BF16 = jnp.bfloat16
HIGHEST = lax.Precision.HIGHEST

D_MODEL = 1024
GRID_W = 64
RWKV_HEADS = 8
HEAD_DIM = 64
RWKV_WIDTH = RWKV_HEADS * HEAD_DIM
DECAY_LORA = 64
ICLR_LORA = 64
GATE_LORA = 128
RWKV_COLS = 3 * RWKV_WIDTH + 2 * DECAY_LORA + 2 * ICLR_LORA + GATE_LORA
SGU_WIDTH = 512
SGU_GROUPS = 8
SGU_CHUNK = 128
N_EXPERTS = 32
TOP_K = 4
SWIGLU_LIMIT = 7.0
SWIGLU_ALPHA = 1.702
RMS_EPS = 1e-6
LN_EPS = 1e-5
GN_EPS = 64e-5

TILE_ROWS = 8
TILE_LANES = 128
VMEM_LIMIT = 48 * 1024 * 1024

WKV_CHUNK = 64
WKV_GROUP = 4
EXPERT_ROWS = 512
MODS_COLS = 1536
PREP_TOKENS = 512
MERGE_TOKENS = 512
COMBINE_TOKENS = 512
SCATTER_LAG = 2


def _params(*sem):
    return pltpu.CompilerParams(dimension_semantics=sem, vmem_limit_bytes=VMEM_LIMIT)


def _mm(a, b, dims=((1,), (0,)), exact=False):
    dn = (dims, ((), ()))
    if exact:
        return lax.dot_general(a, b, dn, precision=HIGHEST, preferred_element_type=F32)
    return lax.dot_general(a.astype(BF16), b.astype(BF16), dn, preferred_element_type=F32)


def _split3(x):
    hi = x.astype(BF16)
    r1 = x - hi.astype(F32)
    mid = r1.astype(BF16)
    lo = (r1 - mid.astype(F32)).astype(BF16)
    return hi, mid, lo


def _mm_hi_lo(a, b, dims):
    a_hi = a.astype(BF16)
    b_hi = b.astype(BF16)
    a_lo = a - a_hi.astype(F32)
    b_lo = b - b_hi.astype(F32)
    return _mm(a_hi, b_hi, dims) + _mm(a_hi, b_lo, dims) + _mm(a_lo, b_hi, dims)


def _mm_sel_x(sel, x):
    return sum(_mm(sel, p) for p in _split3(x))


def _full(shape):
    n = len(shape)
    return pl.BlockSpec(shape, lambda *_: (0,) * n)


def _norm_mod(x, g, shift, scale):
    y = x * lax.rsqrt(jnp.mean(x * x, axis=-1, keepdims=True) + RMS_EPS) * g
    return y * (1.0 + scale) + shift


def _mods_kernel(c_ref, w_ref, b_ref, o_ref):
    c = c_ref[...]
    s = c * jax.nn.sigmoid(c)
    o_ref[...] = _mm(s, w_ref[...], exact=True) + b_ref[...]


def _stage_mods(cs, w_ada, b_ada):
    rows, d = cs.shape
    n = w_ada.shape[1]
    tn = MODS_COLS
    return pl.pallas_call(
        _mods_kernel,
        grid=(n // tn,),
        in_specs=[_full((rows, d)),
                  pl.BlockSpec((d, tn), lambda j: (0, j)),
                  pl.BlockSpec((1, tn), lambda j: (0, j))],
        out_specs=pl.BlockSpec((rows, tn), lambda j: (0, j)),
        out_shape=jax.ShapeDtypeStruct((rows, n), F32),
        compiler_params=_params("arbitrary"),
        name="mods",
    )(cs, w_ada, b_ada)


def _rwkv_feats(p, mu_kk, mu_ka, r_k, w0, dlb, a0, ilb, glb, headsum):
    w = RWKV_WIDTH
    r = p[:, 0:w]
    k = p[:, w:2 * w]
    v = p[:, 2 * w:3 * w]
    o = 3 * w
    wd = (p[:, o:o + DECAY_LORA], p[:, o + DECAY_LORA:o + 2 * DECAY_LORA])
    o += 2 * DECAY_LORA
    ad = (p[:, o:o + ICLR_LORA], p[:, o + ICLR_LORA:o + 2 * ICLR_LORA])
    o += 2 * ICLR_LORA
    gd = p[:, o:o + GATE_LORA]

    kk = k * mu_kk
    kk = kk * lax.rsqrt(_mm(kk * kk, headsum) + 1e-12)
    g = _mm(jax.nn.sigmoid(gd), glb)
    ks, bs, lws = [], [], []
    ksum = None
    for d in range(2):
        z = w0[d:d + 1] + _mm(jnp.tanh(wd[d]), dlb[d])
        lws.append(-math.exp(-0.5) * jax.nn.sigmoid(z))
        ic = jax.nn.sigmoid(a0[d:d + 1] + _mm(ad[d], ilb[d]))
        kd = k * (1.0 + (ic - 1.0) * mu_ka)
        ks.append(kd)
        bs.append(kk * ic)
        ksum = kd if ksum is None else ksum + kd
    bonus = _mm(r * ksum * r_k, headsum) * v
    return r, v, -kk, g, bonus, ks, bs, lws


def _prep_kernel(xm_ref, xp_ref, xn_ref, sh_ref, sc_ref, n1_ref, w_ref,
                 mu_ref, kk_ref, ka_ref, rk_ref, w0_ref, dlb_ref, a0_ref, ilb_ref, glb_ref, hs_ref,
                 rva_ref, g_ref, bon_ref, kb_ref, lw_ref, *, tm, grid_shift):
    project = lambda x: _mm(_norm_mod(x, n1_ref[...], sh_ref[0], sc_ref[0]), w_ref[...])
    lane = lax.broadcasted_iota(jnp.int32, (1, RWKV_COLS), 1)
    if grid_shift:
        ext = project(jnp.concatenate([xp_ref[0], xm_ref[0], xn_ref[0]], axis=0))
        main = ext[GRID_W:GRID_W + tm]
        t = pl.program_id(1) * tm + lax.broadcasted_iota(jnp.int32, (tm, 1), 0)
        col = t & (GRID_W - 1)
        row = t >> (GRID_W.bit_length() - 1)
        n_rows = pl.num_programs(1) * tm // GRID_W
        left = jnp.where(col > 0, ext[GRID_W - 1:GRID_W - 1 + tm], 0.0)
        right = jnp.where(col < GRID_W - 1, ext[GRID_W + 1:GRID_W + 1 + tm], 0.0)
        up = jnp.where(row > 0, ext[0:tm], 0.0)
        down = jnp.where(row < n_rows - 1, ext[2 * GRID_W:2 * GRID_W + tm], 0.0)
        cm = lane & 3
        shifted = jnp.where(cm == 0, left, jnp.where(cm == 1, right, jnp.where(cm == 2, up, down)))
    else:
        main = project(xm_ref[0])
        zero = jnp.zeros((1, RWKV_COLS), F32)
        prev = jnp.concatenate([zero, main[:tm - 1]], axis=0)
        nxt = jnp.concatenate([main[1:], zero], axis=0)
        shifted = jnp.where((lane & 1) == 0, prev, nxt)
    p = main + mu_ref[...] * (shifted - main)
    r, v, a, g, bonus, ks, bs, lws = _rwkv_feats(
        p, kk_ref[...], ka_ref[...], rk_ref[...], w0_ref[...], dlb_ref, a0_ref[...], ilb_ref,
        glb_ref[...], hs_ref[...])
    w = RWKV_WIDTH
    for j, val in enumerate((r, v, a)):
        rva_ref[0, :, j * w:(j + 1) * w] = val.astype(BF16)
    g_ref[0] = g.astype(BF16)
    bon_ref[0] = bonus.astype(BF16)
    for d in range(2):
        kb_ref[d, 0, :, 0:w] = ks[d].astype(BF16)
        kb_ref[d, 0, :, w:2 * w] = bs[d].astype(BF16)
        lw_ref[d, 0] = lws[d]


def _stage_prep(x, shift, scale, g, w_bf16, prm, tm, grid_shift):
    b, t, c = x.shape
    w = RWKV_WIDTH
    mod = pl.BlockSpec((1, 1, c), lambda i, j: (i, 0, 0))
    hb = GRID_W if grid_shift else 8
    per = tm // hb
    last = t // hb - 1
    tok = lambda n_: pl.BlockSpec((1, tm, n_ * w), lambda i, j: (i, j, 0))
    tok2 = lambda n_: pl.BlockSpec((2, 1, tm, n_ * w), lambda i, j: (0, i, j, 0))
    in_specs = [
        pl.BlockSpec((1, tm, c), lambda i, j: (i, j, 0)),
        pl.BlockSpec((1, hb, c), lambda i, j: (i, jnp.maximum(j * per - 1, 0), 0)),
        pl.BlockSpec((1, hb, c), lambda i, j: (i, jnp.minimum((j + 1) * per, last), 0)),
        mod, mod, _full(g.shape), _full(w_bf16.shape),
    ] + [_full(a.shape) for a in prm]
    return pl.pallas_call(
        functools.partial(_prep_kernel, tm=tm, grid_shift=grid_shift),
        grid=(b, t // tm),
        in_specs=in_specs,
        out_specs=[tok(3), tok(1), tok(1), tok2(2), tok2(1)],
        out_shape=[jax.ShapeDtypeStruct((b, t, 3 * w), BF16),
                   jax.ShapeDtypeStruct((b, t, w), BF16),
                   jax.ShapeDtypeStruct((b, t, w), BF16),
                   jax.ShapeDtypeStruct((2, b, t, 2 * w), BF16),
                   jax.ShapeDtypeStruct((2, b, t, w), F32)],
        compiler_params=_params("arbitrary", "arbitrary"),
        name="prep_lat" if grid_shift else "prep_ctx",
    )(x, x, x, shift, scale, g, w_bf16, *prm)


def _wkv_prepare(d, r, v, a, k, bb, lw):
    c = WKV_CHUNK
    n = HEAD_DIM
    ii = lax.broadcasted_iota(jnp.int32, (c, c), 0)
    jj = lax.broadcasted_iota(jnp.int32, (c, c), 1)
    incl = (jj <= ii) if d == 0 else (jj >= ii)
    lc = _mm_sel_x(incl.astype(BF16), lw)
    lx = lc - lw
    ltot = lc[c - 1:c] if d == 0 else lc[0:1]
    inv = jnp.exp(-lc)
    tail = jnp.exp(ltot - lc)
    etot = jnp.exp(ltot)
    eye_n = (lax.broadcasted_iota(jnp.int32, (n, n), 0)
             == lax.broadcasted_iota(jnp.int32, (n, n), 1)).astype(F32)
    scale = jnp.concatenate(
        [jnp.broadcast_to(jnp.sum(eye_n * etot[:, h * n:(h + 1) * n], axis=1, keepdims=True), (n, n))
         for h in range(RWKV_HEADS)], axis=1)
    bf = lambda x: x.astype(BF16)
    return (bf(a * jnp.exp(lx)), bf(r * jnp.exp(lc)), bf(k * inv), bf(bb * inv), bf(k * tail),
            bf(bb * tail), bf(v), scale)


def _wkv_consts():
    c = WKV_CHUNK
    n = HEAD_DIM
    gw = WKV_GROUP * n
    assert c == n and n & (n - 1) == 0
    ii = lax.broadcasted_iota(jnp.int32, (c, gw), 0)
    jj = lax.broadcasted_iota(jnp.int32, (c, gw), 1) & (c - 1)
    head_shift = n.bit_length() - 1
    same_head = (lax.broadcasted_iota(jnp.int32, (gw, gw), 0) >> head_shift
                 == lax.broadcasted_iota(jnp.int32, (gw, gw), 1) >> head_shift)

    def bd(x):
        x = x.astype(BF16)
        return jnp.where(same_head, jnp.concatenate([x] * WKV_GROUP, axis=0), jnp.zeros((), BF16))

    return ii, jj, bd


def _wkv_factor(chunks, emit, between=lambda: None):
    c = WKV_CHUNK
    gw = WKV_GROUP * HEAD_DIM
    n_groups = RWKV_WIDTH // gw
    ii, jj, bd = _wkv_consts()
    eye = (ii == jj).astype(F32)
    nt = ((1,), (1,))
    tn = ((0,), (0,))
    sl = [slice(h * HEAD_DIM, (h + 1) * HEAD_DIM) for h in range(WKV_GROUP)]

    prob = []
    for preps in chunks:
        for d, (at, rt, kt, bt, kh, bh, vb, _) in enumerate(preps):
            incl = (jj <= ii) if d == 0 else (jj >= ii)
            strict = (jj < ii) if d == 0 else (jj > ii)
            for gi in range(n_groups):
                gs = slice(gi * gw, (gi + 1) * gw)
                prob.append(dict(incl=incl, strict=strict, at=at[:, gs], rt=rt[:, gs],
                                 kt=kt[:, gs], bt=bt[:, gs], kh=kh[:, gs], bh=bh[:, gs],
                                 v=vb[:, gs]))
    for p in prob:
        lhs = jnp.concatenate([p["at"], p["rt"]], axis=0) if emit else p["at"]
        p["lhs_z"] = lhs
        gk = _mm(lhs, bd(p["kt"]), nt)
        gb = _mm(lhs, bd(p["bt"]), nt)
        p["a_ab"] = jnp.where(p["strict"], gb[:c], 0.0)
        lhs_v = jnp.where(p["strict"], gk[:c], 0.0)
        if emit:
            lhs_v = jnp.concatenate([lhs_v, jnp.where(p["incl"], gk[c:], 0.0)], axis=0)
            p["a_rb"] = jnp.where(p["incl"], gb[c:], 0.0).astype(BF16)
        p["lhs_v"] = lhs_v
    between()
    for p in prob:
        p["tinv"] = eye + p["a_ab"]
        p["x"] = _mm(p["a_ab"], bd(p["a_ab"]))
        p["av"] = _mm(p["lhs_v"], bd(p["v"]))
    between()
    for level in range(1, 6):
        for p in prob:
            if level < 5:
                both = _mm(jnp.concatenate([p["x"], p["tinv"]], axis=0), bd(p["x"]))
                p["x"] = both[:c]
                p["tinv"] = p["tinv"] + both[c:]
            else:
                p["tinv"] = (p["tinv"] + _mm(p["tinv"], bd(p["x"]))).astype(BF16)
        between()
    per_chunk = 2 * n_groups
    return [prob[j * per_chunk:(j + 1) * per_chunk] for j in range(len(chunks))]


def _wkv_apply(probs, chunks, states, emit):
    c = WKV_CHUNK
    n = HEAD_DIM
    gw = WKV_GROUP * n
    n_groups = RWKV_WIDTH // gw
    _, _, bd = _wkv_consts()
    tn = ((0,), (0,))
    sl = [slice(h * n, (h + 1) * n) for h in range(WKV_GROUP)]
    flat = []
    for prob, zs in zip(probs, states):
        zb = [z.astype(BF16) for z in zs]
        for i, p in enumerate(prob):
            d, gi = divmod(i, n_groups)
            p["z"] = zb[d][:, gi * gw:(gi + 1) * gw]
            flat.append(p)
    for p in flat:
        p["zv"] = _mm(p["lhs_z"], bd(p["z"])) + p["av"]
    for p in flat:
        p["u"] = _mm(p["tinv"], bd(p["zv"][:c])).astype(BF16)
    for p in flat:
        if emit:
            p["y"] = p["zv"][c:] + _mm(p["a_rb"], bd(p["u"]))
        p["z_new"] = [_mm(jnp.concatenate([p["kh"][:, s], p["bh"][:, s]], axis=0),
                          jnp.concatenate([p["v"][:, s], p["u"][:, s]], axis=0), tn) for s in sl]
    out = []
    for prob, preps, zs in zip(probs, chunks, states):
        res = []
        for d, (prep, z) in enumerate(zip(preps, zs)):
            mine = prob[d * n_groups:(d + 1) * n_groups]
            z_cat = jnp.concatenate([m for p in mine for m in p["z_new"]], axis=1) + prep[7] * z
            y = jnp.concatenate([p["y"] for p in mine], axis=1) if emit else None
            res.append((y, z_cat))
        out.append(res)
    return out


def _wkv_kernel(*refs, emit, has_init):
    refs = list(refs)
    ins = [[refs.pop(0) for _ in range(3)] for _ in range(2)]
    z0_ref = refs.pop(0) if has_init else None
    outs = [refs.pop(0) for _ in range(2 if emit else 1)]
    z_scr = refs.pop(0)
    n_seq = z_scr.shape[1]

    @pl.when(pl.program_id(0) == 0)
    def _():
        if has_init:
            z_scr[...] = z0_ref[...]
        else:
            z_scr[...] = jnp.zeros_like(z_scr)

    def prepare(i, d):
        rva_ref, kb_ref, lw_ref = ins[d]
        w = RWKV_WIDTH
        r, v, a = (rva_ref[i, :, j * w:(j + 1) * w] for j in range(3))
        return _wkv_prepare(d, r, v, a, kb_ref[0, i, :, 0:w], kb_ref[0, i, :, w:2 * w],
                            lw_ref[0, i])

    half = n_seq // 2
    chunks = [[prepare(i, d) for d in range(2)] for i in range(half)]
    late = {}
    todo = [(i, d) for i in range(half, n_seq) for d in range(2)]

    def prepare_one():
        if todo:
            i, d = todo.pop(0)
            late[(i, d)] = prepare(i, d)

    probs = _wkv_factor(chunks, emit, between=prepare_one)
    while todo:
        prepare_one()
    rest = [[late[(i, d)] for d in range(2)] for i in range(half, n_seq)]
    probs += _wkv_factor(rest, emit)
    chunks += rest
    states = [(z_scr[0, i], z_scr[1, i]) for i in range(n_seq)]
    for i, res in enumerate(_wkv_apply(probs, chunks, states, emit)):
        for d, (y, z_new) in enumerate(res):
            z_scr[d, i] = z_new
            if emit:
                outs[d][i] = y
    if not emit:
        outs[0][...] = z_scr[...]


def _stage_wkv(rva, kb, lw, z0=None, emit=True):
    b, t, w = lw.shape[1:]
    c = WKV_CHUNK
    nch = t // c
    n = HEAD_DIM
    pos = (lambda s: s, lambda s: nch - 1 - s)
    in_specs, args = [], []
    for d in range(2):
        in_specs += [pl.BlockSpec((b, c, 3 * w), lambda s, d=d: (0, pos[d](s), 0)),
                     pl.BlockSpec((1, b, c, 2 * w), lambda s, d=d: (d, 0, pos[d](s), 0)),
                     pl.BlockSpec((1, b, c, w), lambda s, d=d: (d, 0, pos[d](s), 0))]
        args += [rva, kb, lw]
    zspec = pl.BlockSpec((2, b, n, w), lambda s: (0, 0, 0, 0))
    if z0 is not None:
        in_specs.append(zspec)
        args.append(z0)
    if emit:
        out_specs = [pl.BlockSpec((b, c, w), lambda s, d=d: (0, pos[d](s), 0)) for d in range(2)]
        out_shape = [jax.ShapeDtypeStruct((b, t, w), F32)] * 2
    else:
        out_specs = [zspec]
        out_shape = [jax.ShapeDtypeStruct((2, b, n, w), F32)]
    return pl.pallas_call(
        functools.partial(_wkv_kernel, emit=emit, has_init=z0 is not None),
        grid=(nch,),
        in_specs=in_specs,
        out_specs=out_specs,
        out_shape=out_shape,
        scratch_shapes=[pltpu.VMEM((2, b, n, w), F32)],
        compiler_params=_params("arbitrary"),
        name="wkv_lat" if emit else "wkv_ctx",
    )(*args)


def _to_token_tiles(ref, val, rows):
    for cc in range(val.shape[1] // TILE_LANES):
        ref[pl.ds(cc, rows, stride=TILE_ROWS), :] = val[:, cc * TILE_LANES:(cc + 1) * TILE_LANES]


def _from_token_tiles(ref, rows):
    return jnp.concatenate([ref[pl.ds(cc, rows, stride=TILE_ROWS), :] for cc in range(TILE_ROWS)],
                           axis=1)


def _route_tokens(h, rwt_ref, rb_ref, e_ref, w_ref, rank_ref, cnt_ref, carry, tm):
    logits = _mm_hi_lo(rwt_ref[...], h, dims=((1,), (1,))) + rb_ref[...]
    eio = lax.broadcasted_iota(jnp.int32, (N_EXPERTS, tm), 0)
    vals, sels, idxs = [], [], []
    for _ in range(TOP_K):
        m = jnp.max(logits, axis=0, keepdims=True)
        idx = jnp.min(jnp.where(logits == m, eio, N_EXPERTS), axis=0, keepdims=True)
        sel = eio == idx
        logits = jnp.where(sel, -jnp.inf, logits)
        vals.append(m)
        sels.append(sel)
        idxs.append(idx)
        e_ref[len(vals) - 1:len(vals), :] = idx
    ex = [jnp.exp(vk - vals[0]) for vk in vals]
    tot = ex[0] + ex[1] + ex[2] + ex[3]
    for kk in range(TOP_K):
        w_ref[kk:kk + 1, :] = ex[kk] / tot
    cnt = (sels[0] | sels[1] | sels[2] | sels[3]).astype(F32)
    ti = lax.broadcasted_iota(jnp.int32, (tm, tm), 0)
    tj = lax.broadcasted_iota(jnp.int32, (tm, tm), 1)
    before = _mm(cnt, (ti < tj).astype(F32))
    base = carry[...] + before
    ranks = []
    for kk in range(TOP_K):
        ranks.append(jnp.sum(jnp.where(sels[kk], base, 0.0), axis=0,
                             keepdims=True).astype(jnp.int32))
        rank_ref[kk:kk + 1, :] = ranks[kk]
    new = carry[...] + jnp.sum(cnt, axis=1, keepdims=True)
    carry[...] = new
    cnt_ref[...] = jnp.broadcast_to(new, cnt_ref.shape).astype(jnp.int32)
    return idxs, ranks


def _merge_kernel(x_ref, yf_ref, yb_ref, g_ref, bon_ref, sh_ref, sc_ref, ga_ref, sh2_ref, sc2_ref,
                  n1_ref, w2_ref, gnw_ref, gnb_ref, hs_ref, wor_ref, lnw_ref, lnb_ref, wsp_ref,
                  bsp_ref, wos_ref, wo_ref, n2_ref, rwt_ref, rb_ref,
                  xt_ref, e_ref, w_ref, rank_ref, cnt_ref, xs_ref,
                  carry, h2buf, zeros, dest_v, dest_s, cnt_s, ssem, csem, zsem, *, tm, cap):
    step = pl.program_id(0) * pl.num_programs(1) + pl.program_id(1)
    n_steps = pl.num_programs(0) * pl.num_programs(1)
    dump = N_EXPERTS * cap
    n_buf = h2buf.shape[0]

    def dest_copy(which):
        return pltpu.make_async_copy(dest_v, dest_s.at[which], csem)

    def drain(which):
        for _ in range(TOP_K):
            pltpu.make_async_copy(h2buf.at[which], h2buf.at[which], ssem.at[which]).wait()

    @pl.when(step == 0)
    def _():
        carry[...] = jnp.zeros_like(carry)
        zeros[...] = jnp.zeros_like(zeros)
        for which in range(SCATTER_LAG):
            h2buf[SCATTER_LAG + which] = jnp.zeros(h2buf.shape[1:], F32)
            dest_v[...] = (dump + which * TOP_K * tm
                           + lax.broadcasted_iota(jnp.int32, (TOP_K, tm), 0) * tm
                           + lax.broadcasted_iota(jnp.int32, (TOP_K, tm), 1))
            dest_copy(which).start()
            dest_copy(which).wait()

    @pl.when(step >= SCATTER_LAG)
    def _():
        drain(step % n_buf)

    src = (step + SCATTER_LAG) % n_buf
    for t in range(tm):
        for kk in range(TOP_K):
            _tile_copy(h2buf.at[src], t, xs_ref, dest_s[step % SCATTER_LAG, kk, t], ssem.at[src]).start(
                priority=kk % 2)

    x = x_ref[0]
    h = _norm_mod(x, n1_ref[...], sh_ref[0], sc_ref[0])
    p2 = _mm(h, w2_ref[...])

    ps = p2[:, :2 * SGU_WIDTH]
    ge = 0.5 * ps * (1.0 + lax.erf(ps * (1.0 / math.sqrt(2.0))))
    u = ge[:, :SGU_WIDTH]
    z = ge[:, SGU_WIDTH:]
    mu = jnp.mean(z, axis=-1, keepdims=True)
    zc = z - mu
    var = jnp.mean(zc * zc, axis=-1, keepdims=True)
    z = zc * lax.rsqrt(var + LN_EPS) * lnw_ref[...] + lnb_ref[...]
    gw = SGU_WIDTH // SGU_GROUPS
    rows = []
    for c in range(tm // SGU_CHUNK):
        zc = z[c * SGU_CHUNK:(c + 1) * SGU_CHUNK]
        cols = [_mm(wsp_ref[gi], zc[:, gi * gw:(gi + 1) * gw]) for gi in range(SGU_GROUPS)]
        rows.append(jnp.concatenate(cols, axis=1) + bsp_ref[...])
    s = jnp.concatenate(rows, axis=0)
    y_b = _mm(u * s, wos_ref[...])

    y = yf_ref[0] + yb_ref[0]
    hs = hs_ref[...]
    ym = _mm(y, hs) * (1.0 / HEAD_DIM)
    yc = y - ym
    yv = _mm(yc * yc, hs) * (1.0 / HEAD_DIM)
    yn = yc * lax.rsqrt(yv + GN_EPS) * gnw_ref[...] + gnb_ref[...]
    y_a = _mm((yn + bon_ref[0]) * g_ref[0], wor_ref[...])

    gates = jax.nn.sigmoid(p2[:, 2 * SGU_WIDTH:])
    mix = gates[:, :D_MODEL] * y_a + gates[:, D_MODEL:] * y_b
    x1 = x + ga_ref[0] * _mm(mix, wo_ref[...])

    h2 = _norm_mod(x1, n2_ref[...], sh2_ref[0], sc2_ref[0])
    _to_token_tiles(h2buf.at[step % n_buf], h2, tm)
    _to_token_tiles(xt_ref, x1, tm)
    idxs, ranks = _route_tokens(h2, rwt_ref, rb_ref, e_ref, w_ref, rank_ref, cnt_ref, carry, tm)

    @pl.when(step > 0)
    def _():
        dest_copy((step - 1) % SCATTER_LAG).wait()

    for kk in range(TOP_K):
        dest_v[kk:kk + 1, :] = idxs[kk] * cap + ranks[kk]
    dest_copy(step % SCATTER_LAG).start()

    @pl.when(step == n_steps - 1)
    def _():
        dest_copy(step % SCATTER_LAG).wait()
        for back in reversed(range(SCATTER_LAG)):

            def flush(t, c, back=back):
                for kk in range(TOP_K):
                    _tile_copy(h2buf.at[(step - back) % n_buf], t, xs_ref,
                               dest_s[(step - back) % SCATTER_LAG, kk, t],
                               ssem.at[(step - back) % n_buf]).start(priority=kk % 2)
                return c

            lax.fori_loop(0, tm, flush, 0)
        counts = pltpu.make_async_copy(cnt_ref, cnt_s, csem)
        counts.start()
        counts.wait()
        for e in range(N_EXPERTS):
            first = pl.multiple_of((e * cap + cnt_s[e, 0]) * TILE_ROWS, TILE_ROWS)
            pltpu.make_async_copy(zeros, xs_ref.at[pl.ds(first, zeros.shape[0])], zsem).start()
        for e in range(N_EXPERTS):
            pltpu.make_async_copy(zeros, xs_ref.at[pl.ds(0, zeros.shape[0])], zsem).wait()
        for which in range(n_buf):
            drain(which)


def _stage_merge(x, yf, yb, g, bonus, mods, consts, tm):
    b, t, d = x.shape
    n = b * t
    w = RWKV_WIDTH
    per = t // tm
    cap = n + EXPERT_ROWS
    assert d == TILE_ROWS * TILE_LANES, "token-tile layout stores one vector tile per token"
    mod = pl.BlockSpec((1, 1, d), lambda i, j: (i, 0, 0))
    tok = pl.BlockSpec((1, tm, w), lambda i, j: (i, j, 0))
    tiles = pl.BlockSpec((tm * TILE_ROWS, TILE_LANES), lambda i, j: (i * per + j, 0))
    lane = pl.BlockSpec((TOP_K, tm), lambda i, j: (0, i * per + j))
    return pl.pallas_call(
        functools.partial(_merge_kernel, tm=tm, cap=cap),
        grid=(b, per),
        in_specs=[pl.BlockSpec((1, tm, d), lambda i, j: (i, j, 0)), tok, tok, tok, tok]
        + [mod] * len(mods) + [_full(a.shape) for a in consts],
        out_specs=[tiles, lane, lane, lane, _full((N_EXPERTS, TILE_LANES)),
                   pl.BlockSpec(memory_space=pl.ANY)],
        out_shape=[jax.ShapeDtypeStruct((n * TILE_ROWS, TILE_LANES), F32),
                   jax.ShapeDtypeStruct((TOP_K, n), jnp.int32),
                   jax.ShapeDtypeStruct((TOP_K, n), F32),
                   jax.ShapeDtypeStruct((TOP_K, n), jnp.int32),
                   jax.ShapeDtypeStruct((N_EXPERTS, TILE_LANES), jnp.int32),
                   jax.ShapeDtypeStruct(((N_EXPERTS * cap + SCATTER_LAG * TOP_K * tm) * TILE_ROWS,
                                         TILE_LANES), F32)],
        scratch_shapes=[pltpu.VMEM((N_EXPERTS, 1), F32),
                        pltpu.VMEM((2 * SCATTER_LAG, tm * TILE_ROWS, TILE_LANES), F32),
                        pltpu.VMEM((EXPERT_ROWS * TILE_ROWS, TILE_LANES), F32),
                        pltpu.VMEM((TOP_K, tm), jnp.int32),
                        pltpu.SMEM((SCATTER_LAG, TOP_K, tm), jnp.int32),
                        pltpu.SMEM((N_EXPERTS, TILE_LANES), jnp.int32),
                        pltpu.SemaphoreType.DMA((2 * SCATTER_LAG,)), pltpu.SemaphoreType.DMA(()),
                        pltpu.SemaphoreType.DMA(())],
        compiler_params=_params("arbitrary", "arbitrary"),
        name="merge",
    )(x, yf, yb, g, bonus, *mods, *consts), cap


def _expert_kernel(be_ref, nu_ref, nx_ref, xb_ref, nrow_ref, x_ref, bg_ref, bu_ref, bd_ref,
                   wg_hbm, wu_hbm, wd_hbm, o_ref, stage, wg_s, wu_s, wd_s, sem):
    i = pl.program_id(0)
    e = be_ref[i]
    used = i < nu_ref[0]
    prev = be_ref[jnp.maximum(i - 1, 0)]

    def fetch(expert):
        return [pltpu.make_async_copy(w.at[expert], stage.at[j], sem.at[j])
                for j, w in enumerate((wg_hbm, wu_hbm, wd_hbm))]

    @pl.when(i == 0)
    def _():
        for cp in fetch(e):
            cp.start()

    @pl.when(used & ((i == 0) | (e != prev)))
    def _():
        for cp, w_s, j in zip(fetch(e), (wg_s, wu_s, wd_s), range(3)):
            cp.wait()
            w_s[...] = stage[j].astype(BF16)
        nxt = nx_ref[e]

        @pl.when(nxt >= 0)
        def _():
            for cp in fetch(nxt):
                cp.start()

    def run(rows):
        x = _from_token_tiles(x_ref, rows)
        gate = _mm(x, wg_s[...]) + bg_ref[0]
        up = _mm(x, wu_s[...]) + bu_ref[0]
        gate = jnp.minimum(gate, SWIGLU_LIMIT)
        up = jnp.clip(up, -SWIGLU_LIMIT, SWIGLU_LIMIT)
        act = gate * jax.nn.sigmoid(SWIGLU_ALPHA * gate) * (up + 1.0)
        _to_token_tiles(o_ref, _mm(act, wd_s[...]) + bd_ref[0], rows)

    half = EXPERT_ROWS // 2
    wide = used & (nrow_ref[i] > half)

    @pl.when(wide)
    def _():
        run(EXPERT_ROWS)

    @pl.when(used & jnp.logical_not(wide))
    def _():
        run(half)
        o_ref[half * TILE_ROWS:, :] = jnp.zeros((half * TILE_ROWS, TILE_LANES), F32)

    @pl.when(jnp.logical_not(used))
    def _():
        o_ref[...] = jnp.zeros_like(o_ref)


def _stage_experts(block_e, n_used, next_e, x_block, block_rows, n_rows, xs, wg, bg, wu, bu, wd,
                   bd):
    d, f = wg.shape[1:]
    assert d == f, "one staging buffer shape serves all three weight matrices"
    bm = EXPERT_ROWS
    bspec = lambda n_: pl.BlockSpec((1, 1, n_), lambda i, be, *_: (be[i], 0, 0))
    hbm = pl.BlockSpec(memory_space=pl.ANY)
    return pl.pallas_call(
        _expert_kernel,
        grid_spec=pltpu.PrefetchScalarGridSpec(
            num_scalar_prefetch=5,
            grid=(n_rows // bm,),
            in_specs=[pl.BlockSpec((bm * TILE_ROWS, TILE_LANES), lambda i, be, nu, nx, xb, nr: (xb[i], 0)),
                      bspec(f), bspec(f), bspec(d), hbm, hbm, hbm],
            out_specs=pl.BlockSpec((bm * TILE_ROWS, TILE_LANES), lambda i, *_: (i, 0)),
            scratch_shapes=[pltpu.VMEM((3, d, f), F32), pltpu.VMEM((d, f), BF16),
                            pltpu.VMEM((d, f), BF16), pltpu.VMEM((f, d), BF16),
                            pltpu.SemaphoreType.DMA((3,))]),
        out_shape=jax.ShapeDtypeStruct((n_rows * TILE_ROWS, TILE_LANES), F32),
        compiler_params=_params("arbitrary"),
        name="experts",
    )(block_e, n_used, next_e, x_block, block_rows, xs, bg, bu, bd, wg, wu, wd)


def _tile_copy(src_ref, src_row, dst_ref, dst_row, sem):
    src = src_ref.at[pl.ds(pl.multiple_of(src_row * TILE_ROWS, TILE_ROWS), TILE_ROWS)]
    dst = dst_ref.at[pl.ds(pl.multiple_of(dst_row * TILE_ROWS, TILE_ROWS), TILE_ROWS)]
    return pltpu.make_async_copy(src, dst, sem)


def _combine_kernel(dest_ref, w_ref, xt_ref, ga_ref, g_ref, yb_ref, o_ref, buf0, buf1, res, sem,
                    *, tm, n_tok):
    i = pl.program_id(0)
    ga = ga_ref[0]
    bufs = (buf0, buf1)

    def issue(tile, slot, t):
        for kk in range(TOP_K):
            _tile_copy(yb_ref, dest_ref[kk * n_tok + tile * tm + t], bufs[slot].at[kk], t,
                       sem.at[slot]).start(priority=kk % 2)

    def combine(slot, t):
        rows = pl.ds(pl.multiple_of(t * TILE_ROWS, TILE_ROWS), TILE_ROWS)
        acc = bufs[slot][0, rows, :] * w_ref[i * tm + t]
        for kk in range(1, TOP_K):
            acc = acc + bufs[slot][kk, rows, :] * w_ref[kk * n_tok + i * tm + t]
        res[rows, :] = xt_ref[rows, :] + ga * acc

    def loop(body):
        for t in range(tm):
            body(t)

    @pl.when(i == 0)
    def _():
        loop(lambda t: issue(0, 0, t))

    for slot in range(2):
        @pl.when(i % 2 == slot)
        def _():
            for kk in range(TOP_K):
                pltpu.make_async_copy(bufs[slot].at[kk], bufs[slot].at[kk], sem.at[slot]).wait()

            @pl.when(i + 1 < pl.num_programs(0))
            def _():
                loop(lambda t: (issue(i + 1, 1 - slot, t), combine(slot, t)))

            @pl.when(i + 1 >= pl.num_programs(0))
            def _():
                loop(lambda t: combine(slot, t))

    x = _from_token_tiles(res, tm)
    o_ref[...] = x * lax.rsqrt(jnp.mean(x * x, axis=-1, keepdims=True) + RMS_EPS) * g_ref[...]


def _stage_combine(dest_flat, w_flat, x1t, ga_t, g_t, yb, tokens_per_batch, tm):
    n = x1t.shape[0] // 8
    d = TILE_ROWS * TILE_LANES
    per = tokens_per_batch // tm
    return pl.pallas_call(
        functools.partial(_combine_kernel, tm=tm, n_tok=n),
        grid_spec=pltpu.PrefetchScalarGridSpec(
            num_scalar_prefetch=2,
            grid=(n // tm,),
            in_specs=[pl.BlockSpec((tm * TILE_ROWS, TILE_LANES), lambda i, *_: (i, 0)),
                      pl.BlockSpec((1, TILE_ROWS, TILE_LANES), lambda i, *_: (i // per, 0, 0)),
                      pl.BlockSpec((1, d), lambda i, *_: (0, 0)),
                      pl.BlockSpec(memory_space=pl.ANY)],
            out_specs=pl.BlockSpec((tm, d), lambda i, *_: (i, 0)),
            scratch_shapes=[pltpu.VMEM((TOP_K, tm * TILE_ROWS, TILE_LANES), F32),
                            pltpu.VMEM((TOP_K, tm * TILE_ROWS, TILE_LANES), F32),
                            pltpu.VMEM((tm * TILE_ROWS, TILE_LANES), F32),
                            pltpu.SemaphoreType.DMA((2,))]),
        out_shape=jax.ShapeDtypeStruct((n, d), F32),
        compiler_params=_params("arbitrary"),
        name="combine",
    )(dest_flat, w_flat, x1t, ga_t, g_t, yb)


def _rwkv_branch(x, ctx, mods_lat, mods_ctx, n1, w_rwkv, prm):
    rva, _, _, kb, lw = _stage_prep(ctx, *mods_ctx, n1, w_rwkv, prm, tm=ctx.shape[1],
                                    grid_shift=False)
    (z_ctx,) = _stage_wkv(rva, kb, lw, emit=False)
    rva, g, bonus, kb, lw = _stage_prep(x, *mods_lat, n1, w_rwkv, prm, tm=PREP_TOKENS,
                                        grid_shift=True)
    y_f, y_b = _stage_wkv(rva, kb, lw, z0=z_ctx, emit=True)
    return y_f, y_b, g, bonus


def _route(top_e, rank, counts, n_tok, cap):
    bm = EXPERT_ROWS
    n_rows = n_tok * TOP_K + N_EXPERTS * bm
    n_blocks = n_rows // bm
    padded = (counts + bm - 1) // bm * bm
    pad_end = jnp.cumsum(padded)
    pad_start = pad_end - padded
    experts = jnp.arange(N_EXPERTS, dtype=jnp.int32)
    start_of = jnp.sum(jnp.where(top_e[..., None] == experts, pad_start, 0), axis=-1)
    dest = (start_of + rank).astype(jnp.int32).reshape(-1)
    block_start = jnp.arange(n_blocks, dtype=jnp.int32) * bm
    n_used = (pad_end[-1] // bm).astype(jnp.int32).reshape(1)
    block_e = jnp.minimum(jnp.sum(pad_end[None, :] <= block_start[:, None], axis=1),
                          N_EXPERTS - 1).astype(jnp.int32)
    is_e = block_e[:, None] == experts[None, :]
    within = block_start - jnp.sum(jnp.where(is_e, pad_start, 0), axis=1)
    x_block = (block_e * (cap // bm) + within // bm).astype(jnp.int32)
    last_used = jnp.sum(jnp.where(jnp.arange(n_blocks) == n_used[0] - 1, x_block, 0))
    x_block = jnp.where(jnp.arange(n_blocks) < n_used[0], x_block, last_used).astype(jnp.int32)
    first_at = lax.cummin(jnp.where(padded > 0, experts, N_EXPERTS), axis=0, reverse=True)
    next_e = jnp.concatenate([first_at[1:], jnp.full((1,), N_EXPERTS, jnp.int32)])
    next_e = jnp.where(next_e < N_EXPERTS, next_e, -1).astype(jnp.int32)
    row_end = jnp.sum(jnp.where(is_e, pad_start + counts, 0), axis=1)
    block_rows = jnp.clip(row_end - block_start, 0, bm).astype(jnp.int32)
    return dest, block_e, n_used, next_e, x_block, block_rows, n_rows


def kernel(x, c, ctx, c_ctx, w_ada, b_ada, norm1_g, w_in, shift_mu, decay_w0, decay_lora_b,
           iclr_a0, iclr_lora_b, gate_lora_b, k_k, k_a, r_k, gn_w, gn_b, w_out_rwkv,
           sgu_ln_w, sgu_ln_b, sgu_w_spatial, sgu_b_spatial, w_out_sgu, w_o, norm2_g,
           router_w, router_b, exp_w_gate, exp_b_gate, exp_w_up, exp_b_up, exp_w_down,
           exp_b_down, final_norm_g):
    assert w_ada.shape[0] == 1, "single-layer problem"
    b, t, d = x.shape
    n_tok = b * t
    w = RWKV_WIDTH
    row = lambda a: a.reshape(1, -1)

    cs = jnp.zeros((8, d), F32).at[:b].set(c).at[b].set(c_ctx)
    mod = _stage_mods(cs, w_ada[0], row(b_ada[0]))
    sh1, sc1, ga1, sh2, sc2, ga2 = [m[:b, None, :] for m in jnp.split(mod, 6, axis=-1)]
    csh1, csc1 = [jnp.broadcast_to(m[b][None, None, :], (b, 1, d))
                  for m in jnp.split(mod, 6, axis=-1)[:2]]

    w_in_bf = w_in[0].astype(BF16)
    n1 = row(norm1_g[0])
    head_id = jnp.arange(w, dtype=jnp.int32) // HEAD_DIM
    headsum = (head_id[:, None] == head_id[None, :]).astype(BF16)
    prm = [row(shift_mu[0]), row(k_k[0]), row(k_a[0]), row(r_k[0]), decay_w0[0],
           decay_lora_b[0], iclr_a0[0], iclr_lora_b[0], gate_lora_b[0], headsum]
    y_f, y_b, g, bonus = _rwkv_branch(x, ctx, (sh1, sc1), (csh1, csc1), n1,
                                      w_in_bf[:, :RWKV_COLS], prm)

    bsp = jnp.repeat(sgu_b_spatial[0].T, SGU_WIDTH // SGU_GROUPS, axis=1)
    consts = [n1, w_in_bf[:, RWKV_COLS:], row(gn_w[0]), row(gn_b[0]), headsum,
              w_out_rwkv[0].astype(BF16), row(sgu_ln_w[0]), row(sgu_ln_b[0]),
              sgu_w_spatial[0].astype(BF16), bsp, w_out_sgu[0].astype(BF16),
              w_o[0].astype(BF16), row(norm2_g[0]), router_w[0].T, router_b[0].reshape(-1, 1)]
    (x1t, top_e, top_w, rank, counts, xs), cap = _stage_merge(
        x, y_f, y_b, g, bonus, (sh1, sc1, ga1, sh2, sc2), consts, tm=MERGE_TOKENS)
    dest, block_e, n_used, next_e, x_block, block_rows, n_rows = _route(top_e, rank, counts[:, 0], n_tok, cap)
    e3 = lambda a: a.reshape(N_EXPERTS, 1, -1)
    yb = _stage_experts(block_e, n_used, next_e, x_block, block_rows, n_rows, xs, exp_w_gate[0],
                        e3(exp_b_gate[0]), exp_w_up[0], e3(exp_b_up[0]), exp_w_down[0],
                        e3(exp_b_down[0]))
    out = _stage_combine(dest, top_w.reshape(-1), x1t, ga2.reshape(b, TILE_ROWS, TILE_LANES),
                         row(final_norm_g), yb, t, tm=COMBINE_TOKENS)
    return out.reshape(b, t, d)
```

```python
import functools
import math

import jax
import jax.numpy as jnp
from jax import lax
from jax.experimental import pallas as pl
from jax.experimental.pallas import tpu as pltpu

F32 = jnp.float32
BF16 = jnp.bfloat16
HIGHEST = lax.Precision.HIGHEST

D_MODEL = 1024
GRID_W = 64
RWKV_HEADS = 8
HEAD_DIM = 64
RWKV_WIDTH = RWKV_HEADS * HEAD_DIM
DECAY_LORA = 64
ICLR_LORA = 64
GATE_LORA = 128
RWKV_COLS = 3 * RWKV_WIDTH + 2 * DECAY_LORA + 2 * ICLR_LORA + GATE_LORA
SGU_WIDTH = 512
SGU_GROUPS = 8
SGU_CHUNK = 128
N_EXPERTS = 32
TOP_K = 4
SWIGLU_LIMIT = 7.0
SWIGLU_ALPHA = 1.702
RMS_EPS = 1e-6
LN_EPS = 1e-5
GN_EPS = 64e-5

TILE_ROWS = 8
TILE_LANES = 128
VMEM_LIMIT = 48 * 1024 * 1024

WKV_CHUNK = 64
WKV_GROUP = 4
EXPERT_ROWS = 512
MODS_COLS = 1536
PREP_TOKENS = 512
MERGE_TOKENS = 512
COMBINE_TOKENS = 512
SCATTER_LAG = 2


def _params(*sem):
    return pltpu.CompilerParams(dimension_semantics=sem, vmem_limit_bytes=VMEM_LIMIT)


def _mm(a, b, dims=((1,), (0,)), exact=False):
    dn = (dims, ((), ()))
    if exact:
        return lax.dot_general(a, b, dn, precision=HIGHEST, preferred_element_type=F32)
    return lax.dot_general(a.astype(BF16), b.astype(BF16), dn, preferred_element_type=F32)


def _split3(x):
    hi = x.astype(BF16)
    r1 = x - hi.astype(F32)
    mid = r1.astype(BF16)
    lo = (r1 - mid.astype(F32)).astype(BF16)
    return hi, mid, lo


def _mm_hi_lo(a, b, dims):
    a_hi = a.astype(BF16)
    b_hi = b.astype(BF16)
    a_lo = a - a_hi.astype(F32)
    b_lo = b - b_hi.astype(F32)
    return _mm(a_hi, b_hi, dims) + _mm(a_hi, b_lo, dims) + _mm(a_lo, b_hi, dims)


def _mm_sel_x(sel, x):
    return sum(_mm(sel, p) for p in _split3(x))


def _full(shape):
    n = len(shape)
    return pl.BlockSpec(shape, lambda *_: (0,) * n)


def _norm_mod(x, g, shift, scale):
    y = x * lax.rsqrt(jnp.mean(x * x, axis=-1, keepdims=True) + RMS_EPS) * g
    return y * (1.0 + scale) + shift


def _mods_kernel(c_ref, w_ref, b_ref, o_ref):
    c = c_ref[...]
    s = c * jax.nn.sigmoid(c)
    o_ref[...] = _mm(s, w_ref[...], exact=True) + b_ref[...]


def _stage_mods(cs, w_ada, b_ada):
    rows, d = cs.shape
    n = w_ada.shape[1]
    tn = MODS_COLS
    return pl.pallas_call(
        _mods_kernel,
        grid=(n // tn,),
        in_specs=[_full((rows, d)),
                  pl.BlockSpec((d, tn), lambda j: (0, j)),
                  pl.BlockSpec((1, tn), lambda j: (0, j))],
        out_specs=pl.BlockSpec((rows, tn), lambda j: (0, j)),
        out_shape=jax.ShapeDtypeStruct((rows, n), F32),
        compiler_params=_params("arbitrary"),
        name="mods",
    )(cs, w_ada, b_ada)


def _rwkv_feats(p, mu_kk, mu_ka, r_k, w0, dlb, a0, ilb, glb, headsum):
    w = RWKV_WIDTH
    r = p[:, 0:w]
    k = p[:, w:2 * w]
    v = p[:, 2 * w:3 * w]
    o = 3 * w
    wd = (p[:, o:o + DECAY_LORA], p[:, o + DECAY_LORA:o + 2 * DECAY_LORA])
    o += 2 * DECAY_LORA
    ad = (p[:, o:o + ICLR_LORA], p[:, o + ICLR_LORA:o + 2 * ICLR_LORA])
    o += 2 * ICLR_LORA
    gd = p[:, o:o + GATE_LORA]

    kk = k * mu_kk
    kk = kk * lax.rsqrt(_mm(kk * kk, headsum) + 1e-12)
    g = _mm(jax.nn.sigmoid(gd), glb)
    ks, bs, lws = [], [], []
    ksum = None
    for d in range(2):
        z = w0[d:d + 1] + _mm(jnp.tanh(wd[d]), dlb[d])
        lws.append(-math.exp(-0.5) * jax.nn.sigmoid(z))
        ic = jax.nn.sigmoid(a0[d:d + 1] + _mm(ad[d], ilb[d]))
        kd = k * (1.0 + (ic - 1.0) * mu_ka)
        ks.append(kd)
        bs.append(kk * ic)
        ksum = kd if ksum is None else ksum + kd
    bonus = _mm(r * ksum * r_k, headsum) * v
    return r, v, -kk, g, bonus, ks, bs, lws


def _prep_kernel(xm_ref, xp_ref, xn_ref, sh_ref, sc_ref, n1_ref, w_ref,
                 mu_ref, kk_ref, ka_ref, rk_ref, w0_ref, dlb_ref, a0_ref, ilb_ref, glb_ref, hs_ref,
                 rva_ref, g_ref, bon_ref, kb_ref, lw_ref, *, tm, grid_shift):
    project = lambda x: _mm(_norm_mod(x, n1_ref[...], sh_ref[0], sc_ref[0]), w_ref[...])
    lane = lax.broadcasted_iota(jnp.int32, (1, RWKV_COLS), 1)
    if grid_shift:
        ext = project(jnp.concatenate([xp_ref[0], xm_ref[0], xn_ref[0]], axis=0))
        main = ext[GRID_W:GRID_W + tm]
        t = pl.program_id(1) * tm + lax.broadcasted_iota(jnp.int32, (tm, 1), 0)
        col = t & (GRID_W - 1)
        row = t >> (GRID_W.bit_length() - 1)
        n_rows = pl.num_programs(1) * tm // GRID_W
        left = jnp.where(col > 0, ext[GRID_W - 1:GRID_W - 1 + tm], 0.0)
        right = jnp.where(col < GRID_W - 1, ext[GRID_W + 1:GRID_W + 1 + tm], 0.0)
        up = jnp.where(row > 0, ext[0:tm], 0.0)
        down = jnp.where(row < n_rows - 1, ext[2 * GRID_W:2 * GRID_W + tm], 0.0)
        cm = lane & 3
        shifted = jnp.where(cm == 0, left, jnp.where(cm == 1, right, jnp.where(cm == 2, up, down)))
    else:
        main = project(xm_ref[0])
        zero = jnp.zeros((1, RWKV_COLS), F32)
        prev = jnp.concatenate([zero, main[:tm - 1]], axis=0)
        nxt = jnp.concatenate([main[1:], zero], axis=0)
        shifted = jnp.where((lane & 1) == 0, prev, nxt)
    p = main + mu_ref[...] * (shifted - main)
    r, v, a, g, bonus, ks, bs, lws = _rwkv_feats(
        p, kk_ref[...], ka_ref[...], rk_ref[...], w0_ref[...], dlb_ref, a0_ref[...], ilb_ref,
        glb_ref[...], hs_ref[...])
    w = RWKV_WIDTH
    for j, val in enumerate((r, v, a)):
        rva_ref[0, :, j * w:(j + 1) * w] = val.astype(BF16)
    g_ref[0] = g.astype(BF16)
    bon_ref[0] = bonus.astype(BF16)
    for d in range(2):
        kb_ref[d, 0, :, 0:w] = ks[d].astype(BF16)
        kb_ref[d, 0, :, w:2 * w] = bs[d].astype(BF16)
        lw_ref[d, 0] = lws[d]


def _stage_prep(x, shift, scale, g, w_bf16, prm, tm, grid_shift):
    b, t, c = x.shape
    w = RWKV_WIDTH
    mod = pl.BlockSpec((1, 1, c), lambda i, j: (i, 0, 0))
    hb = GRID_W if grid_shift else 8
    per = tm // hb
    last = t // hb - 1
    tok = lambda n_: pl.BlockSpec((1, tm, n_ * w), lambda i, j: (i, j, 0))
    tok2 = lambda n_: pl.BlockSpec((2, 1, tm, n_ * w), lambda i, j: (0, i, j, 0))
    in_specs = [
        pl.BlockSpec((1, tm, c), lambda i, j: (i, j, 0)),
        pl.BlockSpec((1, hb, c), lambda i, j: (i, jnp.maximum(j * per - 1, 0), 0)),
        pl.BlockSpec((1, hb, c), lambda i, j: (i, jnp.minimum((j + 1) * per, last), 0)),
        mod, mod, _full(g.shape), _full(w_bf16.shape),
    ] + [_full(a.shape) for a in prm]
    return pl.pallas_call(
        functools.partial(_prep_kernel, tm=tm, grid_shift=grid_shift),
        grid=(b, t // tm),
        in_specs=in_specs,
        out_specs=[tok(3), tok(1), tok(1), tok2(2), tok2(1)],
        out_shape=[jax.ShapeDtypeStruct((b, t, 3 * w), BF16),
                   jax.ShapeDtypeStruct((b, t, w), BF16),
                   jax.ShapeDtypeStruct((b, t, w), BF16),
                   jax.ShapeDtypeStruct((2, b, t, 2 * w), BF16),
                   jax.ShapeDtypeStruct((2, b, t, w), F32)],
        compiler_params=_params("arbitrary", "arbitrary"),
        name="prep_lat" if grid_shift else "prep_ctx",
    )(x, x, x, shift, scale, g, w_bf16, *prm)


def _wkv_prepare(d, r, v, a, k, bb, lw):
    c = WKV_CHUNK
    n = HEAD_DIM
    ii = lax.broadcasted_iota(jnp.int32, (c, c), 0)
    jj = lax.broadcasted_iota(jnp.int32, (c, c), 1)
    incl = (jj <= ii) if d == 0 else (jj >= ii)
    lc = _mm_sel_x(incl.astype(BF16), lw)
    lx = lc - lw
    ltot = lc[c - 1:c] if d == 0 else lc[0:1]
    inv = jnp.exp(-lc)
    tail = jnp.exp(ltot - lc)
    etot = jnp.exp(ltot)
    eye_n = (lax.broadcasted_iota(jnp.int32, (n, n), 0)
             == lax.broadcasted_iota(jnp.int32, (n, n), 1)).astype(F32)
    scale = jnp.concatenate(
        [jnp.broadcast_to(jnp.sum(eye_n * etot[:, h * n:(h + 1) * n], axis=1, keepdims=True), (n, n))
         for h in range(RWKV_HEADS)], axis=1)
    bf = lambda x: x.astype(BF16)
    return (bf(a * jnp.exp(lx)), bf(r * jnp.exp(lc)), bf(k * inv), bf(bb * inv), bf(k * tail),
            bf(bb * tail), bf(v), scale)


def _wkv_consts():
    c = WKV_CHUNK
    n = HEAD_DIM
    gw = WKV_GROUP * n
    assert c == n and n & (n - 1) == 0
    ii = lax.broadcasted_iota(jnp.int32, (c, gw), 0)
    jj = lax.broadcasted_iota(jnp.int32, (c, gw), 1) & (c - 1)
    head_shift = n.bit_length() - 1
    same_head = (lax.broadcasted_iota(jnp.int32, (gw, gw), 0) >> head_shift
                 == lax.broadcasted_iota(jnp.int32, (gw, gw), 1) >> head_shift)

    def bd(x):
        x = x.astype(BF16)
        return jnp.where(same_head, jnp.concatenate([x] * WKV_GROUP, axis=0), jnp.zeros((), BF16))

    return ii, jj, bd


def _wkv_factor(chunks, emit, between=lambda: None):
    c = WKV_CHUNK
    gw = WKV_GROUP * HEAD_DIM
    n_groups = RWKV_WIDTH // gw
    ii, jj, bd = _wkv_consts()
    eye = (ii == jj).astype(F32)
    nt = ((1,), (1,))
    tn = ((0,), (0,))
    sl = [slice(h * HEAD_DIM, (h + 1) * HEAD_DIM) for h in range(WKV_GROUP)]

    prob = []
    for preps in chunks:
        for d, (at, rt, kt, bt, kh, bh, vb, _) in enumerate(preps):
            incl = (jj <= ii) if d == 0 else (jj >= ii)
            strict = (jj < ii) if d == 0 else (jj > ii)
            for gi in range(n_groups):
                gs = slice(gi * gw, (gi + 1) * gw)
                prob.append(dict(incl=incl, strict=strict, at=at[:, gs], rt=rt[:, gs],
                                 kt=kt[:, gs], bt=bt[:, gs], kh=kh[:, gs], bh=bh[:, gs],
                                 v=vb[:, gs]))
    for p in prob:
        lhs = jnp.concatenate([p["at"], p["rt"]], axis=0) if emit else p["at"]
        p["lhs_z"] = lhs
        gk = _mm(lhs, bd(p["kt"]), nt)
        gb = _mm(lhs, bd(p["bt"]), nt)
        p["a_ab"] = jnp.where(p["strict"], gb[:c], 0.0)
        lhs_v = jnp.where(p["strict"], gk[:c], 0.0)
        if emit:
            lhs_v = jnp.concatenate([lhs_v, jnp.where(p["incl"], gk[c:], 0.0)], axis=0)
            p["a_rb"] = jnp.where(p["incl"], gb[c:], 0.0).astype(BF16)
        p["lhs_v"] = lhs_v
    for p in prob:
        p["tinv"] = eye + p["a_ab"]
        p["x"] = _mm(p["a_ab"], bd(p["a_ab"]))
        p["av"] = _mm(p["lhs_v"], bd(p["v"]))
    for level in range(1, 6):
        for p in prob:
            if level < 5:
                both = _mm(jnp.concatenate([p["x"], p["tinv"]], axis=0), bd(p["x"]))
                p["x"] = both[:c]
                p["tinv"] = p["tinv"] + both[c:]
            else:
                p["tinv"] = (p["tinv"] + _mm(p["tinv"], bd(p["x"]))).astype(BF16)
        between()
    per_chunk = 2 * n_groups
    return [prob[j * per_chunk:(j + 1) * per_chunk] for j in range(len(chunks))]


def _wkv_apply(probs, chunks, states, emit):
    c = WKV_CHUNK
    n = HEAD_DIM
    gw = WKV_GROUP * n
    n_groups = RWKV_WIDTH // gw
    _, _, bd = _wkv_consts()
    tn = ((0,), (0,))
    sl = [slice(h * n, (h + 1) * n) for h in range(WKV_GROUP)]
    flat = []
    for prob, zs in zip(probs, states):
        zb = [z.astype(BF16) for z in zs]
        for i, p in enumerate(prob):
            d, gi = divmod(i, n_groups)
            p["z"] = zb[d][:, gi * gw:(gi + 1) * gw]
            flat.append(p)
    for p in flat:
        p["zv"] = _mm(p["lhs_z"], bd(p["z"])) + p["av"]
    for p in flat:
        p["u"] = _mm(p["tinv"], bd(p["zv"][:c])).astype(BF16)
    for p in flat:
        if emit:
            p["y"] = p["zv"][c:] + _mm(p["a_rb"], bd(p["u"]))
        p["z_new"] = [_mm(jnp.concatenate([p["kh"][:, s], p["bh"][:, s]], axis=0),
                          jnp.concatenate([p["v"][:, s], p["u"][:, s]], axis=0), tn) for s in sl]
    out = []
    for prob, preps, zs in zip(probs, chunks, states):
        res = []
        for d, (prep, z) in enumerate(zip(preps, zs)):
            mine = prob[d * n_groups:(d + 1) * n_groups]
            z_cat = jnp.concatenate([m for p in mine for m in p["z_new"]], axis=1) + prep[7] * z
            y = jnp.concatenate([p["y"] for p in mine], axis=1) if emit else None
            res.append((y, z_cat))
        out.append(res)
    return out


def _wkv_kernel(*refs, emit, has_init):
    refs = list(refs)
    ins = [[refs.pop(0) for _ in range(3)] for _ in range(2)]
    z0_ref = refs.pop(0) if has_init else None
    outs = [refs.pop(0) for _ in range(2 if emit else 1)]
    z_scr = refs.pop(0)
    n_seq = z_scr.shape[1]

    @pl.when(pl.program_id(0) == 0)
    def _():
        if has_init:
            z_scr[...] = z0_ref[...]
        else:
            z_scr[...] = jnp.zeros_like(z_scr)

    def prepare(i, d):
        rva_ref, kb_ref, lw_ref = ins[d]
        w = RWKV_WIDTH
        r, v, a = (rva_ref[i, :, j * w:(j + 1) * w] for j in range(3))
        return _wkv_prepare(d, r, v, a, kb_ref[0, i, :, 0:w], kb_ref[0, i, :, w:2 * w],
                            lw_ref[0, i])

    half = n_seq // 2
    chunks = [[prepare(i, d) for d in range(2)] for i in range(half)]
    late = {}
    todo = [(i, d) for i in range(half, n_seq) for d in range(2)]

    def prepare_one():
        if todo:
            i, d = todo.pop(0)
            late[(i, d)] = prepare(i, d)

    probs = _wkv_factor(chunks, emit, between=prepare_one)
    while todo:
        prepare_one()
    rest = [[late[(i, d)] for d in range(2)] for i in range(half, n_seq)]
    probs += _wkv_factor(rest, emit)
    chunks += rest
    states = [(z_scr[0, i], z_scr[1, i]) for i in range(n_seq)]
    for i, res in enumerate(_wkv_apply(probs, chunks, states, emit)):
        for d, (y, z_new) in enumerate(res):
            z_scr[d, i] = z_new
            if emit:
                outs[d][i] = y
    if not emit:
        outs[0][...] = z_scr[...]


def _stage_wkv(rva, kb, lw, z0=None, emit=True):
    b, t, w = lw.shape[1:]
    c = WKV_CHUNK
    nch = t // c
    n = HEAD_DIM
    pos = (lambda s: s, lambda s: nch - 1 - s)
    in_specs, args = [], []
    for d in range(2):
        in_specs += [pl.BlockSpec((b, c, 3 * w), lambda s, d=d: (0, pos[d](s), 0)),
                     pl.BlockSpec((1, b, c, 2 * w), lambda s, d=d: (d, 0, pos[d](s), 0)),
                     pl.BlockSpec((1, b, c, w), lambda s, d=d: (d, 0, pos[d](s), 0))]
        args += [rva, kb, lw]
    zspec = pl.BlockSpec((2, b, n, w), lambda s: (0, 0, 0, 0))
    if z0 is not None:
        in_specs.append(zspec)
        args.append(z0)
    if emit:
        out_specs = [pl.BlockSpec((b, c, w), lambda s, d=d: (0, pos[d](s), 0)) for d in range(2)]
        out_shape = [jax.ShapeDtypeStruct((b, t, w), F32)] * 2
    else:
        out_specs = [zspec]
        out_shape = [jax.ShapeDtypeStruct((2, b, n, w), F32)]
    return pl.pallas_call(
        functools.partial(_wkv_kernel, emit=emit, has_init=z0 is not None),
        grid=(nch,),
        in_specs=in_specs,
        out_specs=out_specs,
        out_shape=out_shape,
        scratch_shapes=[pltpu.VMEM((2, b, n, w), F32)],
        compiler_params=_params("arbitrary"),
        name="wkv_lat" if emit else "wkv_ctx",
    )(*args)


def _to_token_tiles(ref, val, rows):
    for cc in range(val.shape[1] // TILE_LANES):
        ref[pl.ds(cc, rows, stride=TILE_ROWS), :] = val[:, cc * TILE_LANES:(cc + 1) * TILE_LANES]


def _from_token_tiles(ref, rows):
    return jnp.concatenate([ref[pl.ds(cc, rows, stride=TILE_ROWS), :] for cc in range(TILE_ROWS)],
                           axis=1)


def _route_tokens(h, rwt_ref, rb_ref, e_ref, w_ref, rank_ref, cnt_ref, carry, tm):
    logits = _mm_hi_lo(rwt_ref[...], h, dims=((1,), (1,))) + rb_ref[...]
    eio = lax.broadcasted_iota(jnp.int32, (N_EXPERTS, tm), 0)
    vals, sels, idxs = [], [], []
    for _ in range(TOP_K):
        m = jnp.max(logits, axis=0, keepdims=True)
        idx = jnp.min(jnp.where(logits == m, eio, N_EXPERTS), axis=0, keepdims=True)
        sel = eio == idx
        logits = jnp.where(sel, -jnp.inf, logits)
        vals.append(m)
        sels.append(sel)
        idxs.append(idx)
        e_ref[len(vals) - 1:len(vals), :] = idx
    ex = [jnp.exp(vk - vals[0]) for vk in vals]
    tot = ex[0] + ex[1] + ex[2] + ex[3]
    for kk in range(TOP_K):
        w_ref[kk:kk + 1, :] = ex[kk] / tot
    cnt = (sels[0] | sels[1] | sels[2] | sels[3]).astype(F32)
    ti = lax.broadcasted_iota(jnp.int32, (tm, tm), 0)
    tj = lax.broadcasted_iota(jnp.int32, (tm, tm), 1)
    before = _mm(cnt, (ti < tj).astype(F32))
    base = carry[...] + before
    ranks = []
    for kk in range(TOP_K):
        ranks.append(jnp.sum(jnp.where(sels[kk], base, 0.0), axis=0,
                             keepdims=True).astype(jnp.int32))
        rank_ref[kk:kk + 1, :] = ranks[kk]
    new = carry[...] + jnp.sum(cnt, axis=1, keepdims=True)
    carry[...] = new
    cnt_ref[...] = jnp.broadcast_to(new, cnt_ref.shape).astype(jnp.int32)
    return idxs, ranks


def _merge_kernel(x_ref, yf_ref, yb_ref, g_ref, bon_ref, sh_ref, sc_ref, ga_ref, sh2_ref, sc2_ref,
                  n1_ref, w2_ref, gnw_ref, gnb_ref, hs_ref, wor_ref, lnw_ref, lnb_ref, wsp_ref,
                  bsp_ref, wos_ref, wo_ref, n2_ref, rwt_ref, rb_ref,
                  xt_ref, e_ref, w_ref, rank_ref, cnt_ref, xs_ref,
                  carry, h2buf, zeros, dest_v, dest_s, cnt_s, ssem, csem, zsem, *, tm, cap):
    step = pl.program_id(0) * pl.num_programs(1) + pl.program_id(1)
    n_steps = pl.num_programs(0) * pl.num_programs(1)
    dump = N_EXPERTS * cap
    n_buf = h2buf.shape[0]

    def dest_copy(which):
        return pltpu.make_async_copy(dest_v, dest_s.at[which], csem)

    def drain(which):
        for _ in range(TOP_K):
            pltpu.make_async_copy(h2buf.at[which], h2buf.at[which], ssem.at[which]).wait()

    @pl.when(step == 0)
    def _():
        carry[...] = jnp.zeros_like(carry)
        zeros[...] = jnp.zeros_like(zeros)
        for which in range(SCATTER_LAG):
            h2buf[SCATTER_LAG + which] = jnp.zeros(h2buf.shape[1:], F32)
            dest_v[...] = (dump + which * TOP_K * tm
                           + lax.broadcasted_iota(jnp.int32, (TOP_K, tm), 0) * tm
                           + lax.broadcasted_iota(jnp.int32, (TOP_K, tm), 1))
            dest_copy(which).start()
            dest_copy(which).wait()

    @pl.when(step >= SCATTER_LAG)
    def _():
        drain(step % n_buf)

    src = (step + SCATTER_LAG) % n_buf
    for t in range(tm):
        for kk in range(TOP_K):
            _tile_copy(h2buf.at[src], t, xs_ref, dest_s[step % SCATTER_LAG, kk, t], ssem.at[src]).start(
                priority=kk % 2)

    x = x_ref[0]
    h = _norm_mod(x, n1_ref[...], sh_ref[0], sc_ref[0])
    p2 = _mm(h, w2_ref[...])

    ps = p2[:, :2 * SGU_WIDTH]
    ge = 0.5 * ps * (1.0 + lax.erf(ps * (1.0 / math.sqrt(2.0))))
    u = ge[:, :SGU_WIDTH]
    z = ge[:, SGU_WIDTH:]
    mu = jnp.mean(z, axis=-1, keepdims=True)
    zc = z - mu
    var = jnp.mean(zc * zc, axis=-1, keepdims=True)
    z = zc * lax.rsqrt(var + LN_EPS) * lnw_ref[...] + lnb_ref[...]
    gw = SGU_WIDTH // SGU_GROUPS
    rows = []
    for c in range(tm // SGU_CHUNK):
        zc = z[c * SGU_CHUNK:(c + 1) * SGU_CHUNK]
        cols = [_mm(wsp_ref[gi], zc[:, gi * gw:(gi + 1) * gw]) for gi in range(SGU_GROUPS)]
        rows.append(jnp.concatenate(cols, axis=1) + bsp_ref[...])
    s = jnp.concatenate(rows, axis=0)
    y_b = _mm(u * s, wos_ref[...])

    y = yf_ref[0] + yb_ref[0]
    hs = hs_ref[...]
    ym = _mm(y, hs) * (1.0 / HEAD_DIM)
    yc = y - ym
    yv = _mm(yc * yc, hs) * (1.0 / HEAD_DIM)
    yn = yc * lax.rsqrt(yv + GN_EPS) * gnw_ref[...] + gnb_ref[...]
    y_a = _mm((yn + bon_ref[0]) * g_ref[0], wor_ref[...])

    gates = jax.nn.sigmoid(p2[:, 2 * SGU_WIDTH:])
    mix = gates[:, :D_MODEL] * y_a + gates[:, D_MODEL:] * y_b
    x1 = x + ga_ref[0] * _mm(mix, wo_ref[...])

    h2 = _norm_mod(x1, n2_ref[...], sh2_ref[0], sc2_ref[0])
    _to_token_tiles(h2buf.at[step % n_buf], h2, tm)
    _to_token_tiles(xt_ref, x1, tm)
    idxs, ranks = _route_tokens(h2, rwt_ref, rb_ref, e_ref, w_ref, rank_ref, cnt_ref, carry, tm)

    @pl.when(step > 0)
    def _():
        dest_copy((step - 1) % SCATTER_LAG).wait()

    for kk in range(TOP_K):
        dest_v[kk:kk + 1, :] = idxs[kk] * cap + ranks[kk]
    dest_copy(step % SCATTER_LAG).start()

    @pl.when(step == n_steps - 1)
    def _():
        dest_copy(step % SCATTER_LAG).wait()
        for back in reversed(range(SCATTER_LAG)):

            def flush(t, c, back=back):
                for kk in range(TOP_K):
                    _tile_copy(h2buf.at[(step - back) % n_buf], t, xs_ref,
                               dest_s[(step - back) % SCATTER_LAG, kk, t],
                               ssem.at[(step - back) % n_buf]).start(priority=kk % 2)
                return c

            lax.fori_loop(0, tm, flush, 0)
        counts = pltpu.make_async_copy(cnt_ref, cnt_s, csem)
        counts.start()
        counts.wait()
        for e in range(N_EXPERTS):
            first = pl.multiple_of((e * cap + cnt_s[e, 0]) * TILE_ROWS, TILE_ROWS)
            pltpu.make_async_copy(zeros, xs_ref.at[pl.ds(first, zeros.shape[0])], zsem).start()
        for e in range(N_EXPERTS):
            pltpu.make_async_copy(zeros, xs_ref.at[pl.ds(0, zeros.shape[0])], zsem).wait()
        for which in range(n_buf):
            drain(which)


def _stage_merge(x, yf, yb, g, bonus, mods, consts, tm):
    b, t, d = x.shape
    n = b * t
    w = RWKV_WIDTH
    per = t // tm
    cap = n + EXPERT_ROWS
    assert d == TILE_ROWS * TILE_LANES, "token-tile layout stores one vector tile per token"
    mod = pl.BlockSpec((1, 1, d), lambda i, j: (i, 0, 0))
    tok = pl.BlockSpec((1, tm, w), lambda i, j: (i, j, 0))
    tiles = pl.BlockSpec((tm * TILE_ROWS, TILE_LANES), lambda i, j: (i * per + j, 0))
    lane = pl.BlockSpec((TOP_K, tm), lambda i, j: (0, i * per + j))
    return pl.pallas_call(
        functools.partial(_merge_kernel, tm=tm, cap=cap),
        grid=(b, per),
        in_specs=[pl.BlockSpec((1, tm, d), lambda i, j: (i, j, 0)), tok, tok, tok, tok]
        + [mod] * len(mods) + [_full(a.shape) for a in consts],
        out_specs=[tiles, lane, lane, lane, _full((N_EXPERTS, TILE_LANES)),
                   pl.BlockSpec(memory_space=pl.ANY)],
        out_shape=[jax.ShapeDtypeStruct((n * TILE_ROWS, TILE_LANES), F32),
                   jax.ShapeDtypeStruct((TOP_K, n), jnp.int32),
                   jax.ShapeDtypeStruct((TOP_K, n), F32),
                   jax.ShapeDtypeStruct((TOP_K, n), jnp.int32),
                   jax.ShapeDtypeStruct((N_EXPERTS, TILE_LANES), jnp.int32),
                   jax.ShapeDtypeStruct(((N_EXPERTS * cap + SCATTER_LAG * TOP_K * tm) * TILE_ROWS,
                                         TILE_LANES), F32)],
        scratch_shapes=[pltpu.VMEM((N_EXPERTS, 1), F32),
                        pltpu.VMEM((2 * SCATTER_LAG, tm * TILE_ROWS, TILE_LANES), F32),
                        pltpu.VMEM((EXPERT_ROWS * TILE_ROWS, TILE_LANES), F32),
                        pltpu.VMEM((TOP_K, tm), jnp.int32),
                        pltpu.SMEM((SCATTER_LAG, TOP_K, tm), jnp.int32),
                        pltpu.SMEM((N_EXPERTS, TILE_LANES), jnp.int32),
                        pltpu.SemaphoreType.DMA((2 * SCATTER_LAG,)), pltpu.SemaphoreType.DMA(()),
                        pltpu.SemaphoreType.DMA(())],
        compiler_params=_params("arbitrary", "arbitrary"),
        name="merge",
    )(x, yf, yb, g, bonus, *mods, *consts), cap


def _expert_kernel(be_ref, nu_ref, nx_ref, xb_ref, nrow_ref, x_ref, bg_ref, bu_ref, bd_ref,
                   wg_hbm, wu_hbm, wd_hbm, o_ref, stage, wg_s, wu_s, wd_s, sem):
    i = pl.program_id(0)
    e = be_ref[i]
    used = i < nu_ref[0]
    prev = be_ref[jnp.maximum(i - 1, 0)]

    def fetch(expert):
        return [pltpu.make_async_copy(w.at[expert], stage.at[j], sem.at[j])
                for j, w in enumerate((wg_hbm, wu_hbm, wd_hbm))]

    @pl.when(i == 0)
    def _():
        for cp in fetch(e):
            cp.start()

    @pl.when(used & ((i == 0) | (e != prev)))
    def _():
        for cp, w_s, j in zip(fetch(e), (wg_s, wu_s, wd_s), range(3)):
            cp.wait()
            w_s[...] = stage[j].astype(BF16)
        nxt = nx_ref[e]

        @pl.when(nxt >= 0)
        def _():
            for cp in fetch(nxt):
                cp.start()

    def run(rows):
        x = _from_token_tiles(x_ref, rows)
        gate = _mm(x, wg_s[...]) + bg_ref[0]
        up = _mm(x, wu_s[...]) + bu_ref[0]
        gate = jnp.minimum(gate, SWIGLU_LIMIT)
        up = jnp.clip(up, -SWIGLU_LIMIT, SWIGLU_LIMIT)
        act = gate * jax.nn.sigmoid(SWIGLU_ALPHA * gate) * (up + 1.0)
        _to_token_tiles(o_ref, _mm(act, wd_s[...]) + bd_ref[0], rows)

    half = EXPERT_ROWS // 2
    wide = used & (nrow_ref[i] > half)

    @pl.when(wide)
    def _():
        run(EXPERT_ROWS)

    @pl.when(used & jnp.logical_not(wide))
    def _():
        run(half)
        o_ref[half * TILE_ROWS:, :] = jnp.zeros((half * TILE_ROWS, TILE_LANES), F32)

    @pl.when(jnp.logical_not(used))
    def _():
        o_ref[...] = jnp.zeros_like(o_ref)


def _stage_experts(block_e, n_used, next_e, x_block, block_rows, n_rows, xs, wg, bg, wu, bu, wd,
                   bd):
    d, f = wg.shape[1:]
    assert d == f, "one staging buffer shape serves all three weight matrices"
    bm = EXPERT_ROWS
    bspec = lambda n_: pl.BlockSpec((1, 1, n_), lambda i, be, *_: (be[i], 0, 0))
    hbm = pl.BlockSpec(memory_space=pl.ANY)
    return pl.pallas_call(
        _expert_kernel,
        grid_spec=pltpu.PrefetchScalarGridSpec(
            num_scalar_prefetch=5,
            grid=(n_rows // bm,),
            in_specs=[pl.BlockSpec((bm * TILE_ROWS, TILE_LANES), lambda i, be, nu, nx, xb, nr: (xb[i], 0)),
                      bspec(f), bspec(f), bspec(d), hbm, hbm, hbm],
            out_specs=pl.BlockSpec((bm * TILE_ROWS, TILE_LANES), lambda i, *_: (i, 0)),
            scratch_shapes=[pltpu.VMEM((3, d, f), F32), pltpu.VMEM((d, f), BF16),
                            pltpu.VMEM((d, f), BF16), pltpu.VMEM((f, d), BF16),
                            pltpu.SemaphoreType.DMA((3,))]),
        out_shape=jax.ShapeDtypeStruct((n_rows * TILE_ROWS, TILE_LANES), F32),
        compiler_params=_params("arbitrary"),
        name="experts",
    )(block_e, n_used, next_e, x_block, block_rows, xs, bg, bu, bd, wg, wu, wd)


def _tile_copy(src_ref, src_row, dst_ref, dst_row, sem):
    src = src_ref.at[pl.ds(pl.multiple_of(src_row * TILE_ROWS, TILE_ROWS), TILE_ROWS)]
    dst = dst_ref.at[pl.ds(pl.multiple_of(dst_row * TILE_ROWS, TILE_ROWS), TILE_ROWS)]
    return pltpu.make_async_copy(src, dst, sem)


def _combine_kernel(dest_ref, w_ref, xt_ref, ga_ref, g_ref, yb_ref, o_ref, buf0, buf1, res, sem,
                    *, tm, n_tok):
    i = pl.program_id(0)
    ga = ga_ref[0]
    bufs = (buf0, buf1)

    def issue(tile, slot, t):
        for kk in range(TOP_K):
            _tile_copy(yb_ref, dest_ref[kk * n_tok + tile * tm + t], bufs[slot].at[kk], t,
                       sem.at[slot]).start(priority=kk % 2)

    def combine(slot, t):
        rows = pl.ds(pl.multiple_of(t * TILE_ROWS, TILE_ROWS), TILE_ROWS)
        acc = bufs[slot][0, rows, :] * w_ref[i * tm + t]
        for kk in range(1, TOP_K):
            acc = acc + bufs[slot][kk, rows, :] * w_ref[kk * n_tok + i * tm + t]
        res[rows, :] = xt_ref[rows, :] + ga * acc

    def loop(body):
        for t in range(tm):
            body(t)

    @pl.when(i == 0)
    def _():
        loop(lambda t: issue(0, 0, t))

    for slot in range(2):
        @pl.when(i % 2 == slot)
        def _():
            for kk in range(TOP_K):
                pltpu.make_async_copy(bufs[slot].at[kk], bufs[slot].at[kk], sem.at[slot]).wait()

            @pl.when(i + 1 < pl.num_programs(0))
            def _():
                loop(lambda t: (issue(i + 1, 1 - slot, t), combine(slot, t)))

            @pl.when(i + 1 >= pl.num_programs(0))
            def _():
                loop(lambda t: combine(slot, t))

    x = _from_token_tiles(res, tm)
    o_ref[...] = x * lax.rsqrt(jnp.mean(x * x, axis=-1, keepdims=True) + RMS_EPS) * g_ref[...]


def _stage_combine(dest_flat, w_flat, x1t, ga_t, g_t, yb, tokens_per_batch, tm):
    n = x1t.shape[0] // 8
    d = TILE_ROWS * TILE_LANES
    per = tokens_per_batch // tm
    return pl.pallas_call(
        functools.partial(_combine_kernel, tm=tm, n_tok=n),
        grid_spec=pltpu.PrefetchScalarGridSpec(
            num_scalar_prefetch=2,
            grid=(n // tm,),
            in_specs=[pl.BlockSpec((tm * TILE_ROWS, TILE_LANES), lambda i, *_: (i, 0)),
                      pl.BlockSpec((1, TILE_ROWS, TILE_LANES), lambda i, *_: (i // per, 0, 0)),
                      pl.BlockSpec((1, d), lambda i, *_: (0, 0)),
                      pl.BlockSpec(memory_space=pl.ANY)],
            out_specs=pl.BlockSpec((tm, d), lambda i, *_: (i, 0)),
            scratch_shapes=[pltpu.VMEM((TOP_K, tm * TILE_ROWS, TILE_LANES), F32),
                            pltpu.VMEM((TOP_K, tm * TILE_ROWS, TILE_LANES), F32),
                            pltpu.VMEM((tm * TILE_ROWS, TILE_LANES), F32),
                            pltpu.SemaphoreType.DMA((2,))]),
        out_shape=jax.ShapeDtypeStruct((n, d), F32),
        compiler_params=_params("arbitrary"),
        name="combine",
    )(dest_flat, w_flat, x1t, ga_t, g_t, yb)


def _rwkv_branch(x, ctx, mods_lat, mods_ctx, n1, w_rwkv, prm):
    rva, _, _, kb, lw = _stage_prep(ctx, *mods_ctx, n1, w_rwkv, prm, tm=ctx.shape[1],
                                    grid_shift=False)
    (z_ctx,) = _stage_wkv(rva, kb, lw, emit=False)
    rva, g, bonus, kb, lw = _stage_prep(x, *mods_lat, n1, w_rwkv, prm, tm=PREP_TOKENS,
                                        grid_shift=True)
    y_f, y_b = _stage_wkv(rva, kb, lw, z0=z_ctx, emit=True)
    return y_f, y_b, g, bonus


def _route(top_e, rank, counts, n_tok, cap):
    bm = EXPERT_ROWS
    n_rows = n_tok * TOP_K + N_EXPERTS * bm
    n_blocks = n_rows // bm
    padded = (counts + bm - 1) // bm * bm
    pad_end = jnp.cumsum(padded)
    pad_start = pad_end - padded
    experts = jnp.arange(N_EXPERTS, dtype=jnp.int32)
    start_of = jnp.sum(jnp.where(top_e[..., None] == experts, pad_start, 0), axis=-1)
    dest = (start_of + rank).astype(jnp.int32).reshape(-1)
    block_start = jnp.arange(n_blocks, dtype=jnp.int32) * bm
    n_used = (pad_end[-1] // bm).astype(jnp.int32).reshape(1)
    block_e = jnp.minimum(jnp.sum(pad_end[None, :] <= block_start[:, None], axis=1),
                          N_EXPERTS - 1).astype(jnp.int32)
    is_e = block_e[:, None] == experts[None, :]
    within = block_start - jnp.sum(jnp.where(is_e, pad_start, 0), axis=1)
    x_block = (block_e * (cap // bm) + within // bm).astype(jnp.int32)
    last_used = jnp.sum(jnp.where(jnp.arange(n_blocks) == n_used[0] - 1, x_block, 0))
    x_block = jnp.where(jnp.arange(n_blocks) < n_used[0], x_block, last_used).astype(jnp.int32)
    first_at = lax.cummin(jnp.where(padded > 0, experts, N_EXPERTS), axis=0, reverse=True)
    next_e = jnp.concatenate([first_at[1:], jnp.full((1,), N_EXPERTS, jnp.int32)])
    next_e = jnp.where(next_e < N_EXPERTS, next_e, -1).astype(jnp.int32)
    row_end = jnp.sum(jnp.where(is_e, pad_start + counts, 0), axis=1)
    block_rows = jnp.clip(row_end - block_start, 0, bm).astype(jnp.int32)
    return dest, block_e, n_used, next_e, x_block, block_rows, n_rows


def kernel(x, c, ctx, c_ctx, w_ada, b_ada, norm1_g, w_in, shift_mu, decay_w0, decay_lora_b,
           iclr_a0, iclr_lora_b, gate_lora_b, k_k, k_a, r_k, gn_w, gn_b, w_out_rwkv,
           sgu_ln_w, sgu_ln_b, sgu_w_spatial, sgu_b_spatial, w_out_sgu, w_o, norm2_g,
           router_w, router_b, exp_w_gate, exp_b_gate, exp_w_up, exp_b_up, exp_w_down,
           exp_b_down, final_norm_g):
    assert w_ada.shape[0] == 1, "single-layer problem"
    b, t, d = x.shape
    n_tok = b * t
    w = RWKV_WIDTH
    row = lambda a: a.reshape(1, -1)

    cs = jnp.zeros((8, d), F32).at[:b].set(c).at[b].set(c_ctx)
    mod = _stage_mods(cs, w_ada[0], row(b_ada[0]))
    sh1, sc1, ga1, sh2, sc2, ga2 = [m[:b, None, :] for m in jnp.split(mod, 6, axis=-1)]
    csh1, csc1 = [jnp.broadcast_to(m[b][None, None, :], (b, 1, d))
                  for m in jnp.split(mod, 6, axis=-1)[:2]]

    w_in_bf = w_in[0].astype(BF16)
    n1 = row(norm1_g[0])
    head_id = jnp.arange(w, dtype=jnp.int32) // HEAD_DIM
    headsum = (head_id[:, None] == head_id[None, :]).astype(BF16)
    prm = [row(shift_mu[0]), row(k_k[0]), row(k_a[0]), row(r_k[0]), decay_w0[0],
           decay_lora_b[0], iclr_a0[0], iclr_lora_b[0], gate_lora_b[0], headsum]
    y_f, y_b, g, bonus = _rwkv_branch(x, ctx, (sh1, sc1), (csh1, csc1), n1,
                                      w_in_bf[:, :RWKV_COLS], prm)

    bsp = jnp.repeat(sgu_b_spatial[0].T, SGU_WIDTH // SGU_GROUPS, axis=1)
    consts = [n1, w_in_bf[:, RWKV_COLS:], row(gn_w[0]), row(gn_b[0]), headsum,
              w_out_rwkv[0].astype(BF16), row(sgu_ln_w[0]), row(sgu_ln_b[0]),
              sgu_w_spatial[0].astype(BF16), bsp, w_out_sgu[0].astype(BF16),
              w_o[0].astype(BF16), row(norm2_g[0]), router_w[0].T, router_b[0].reshape(-1, 1)]
    (x1t, top_e, top_w, rank, counts, xs), cap = _stage_merge(
        x, y_f, y_b, g, bonus, (sh1, sc1, ga1, sh2, sc2), consts, tm=MERGE_TOKENS)
    dest, block_e, n_used, next_e, x_block, block_rows, n_rows = _route(top_e, rank, counts[:, 0], n_tok, cap)
    e3 = lambda a: a.reshape(N_EXPERTS, 1, -1)
    yb = _stage_experts(block_e, n_used, next_e, x_block, block_rows, n_rows, xs, exp_w_gate[0],
                        e3(exp_b_gate[0]), exp_w_up[0], e3(exp_b_up[0]), exp_w_down[0],
                        e3(exp_b_down[0]))
    out = _stage_combine(dest, top_w.reshape(-1), x1t, ga2.reshape(b, TILE_ROWS, TILE_LANES),
                         row(final_norm_g), yb, t, tm=COMBINE_TOKENS)
    return out.reshape(b, t, d)
```
